```python
import math
import jax, jax.numpy as jnp
from jax import lax
import numpy as np

D_MODEL = 1024
BATCH = 16
SEQ = 256
DEPTH = 4
DEC_BATCH = 4
DEC_SEQ = 2048
PAST_LEN = 512

GRID_W = 64
N_EVEN = (DEPTH + 1) // 2
N_ODD = DEPTH // 2
N_MOD = 6
EPS = 1e-6

SSD_HEADDIM = 64
SSD_HEADS = D_MODEL // SSD_HEADDIM
D_SSD = SSD_HEADS * SSD_HEADDIM
SSD_GROUPS = 4
SSD_STATE = 128
SSD_CHUNK = 128
CONV_W = 5
CONV_CH = D_SSD + 2 * SSD_GROUPS * SSD_STATE
D_SGU = D_MODEL
SGU_GROUPS = 4
SGU_GDIM = D_SGU // SGU_GROUPS
SGU_CHUNK = 128
IN_COLS = D_SSD + CONV_CH + 2 * SSD_HEADS + 2 * D_SGU
MLA_HEADS = 8
Q_RANK = 384
KV_RANK = 256
QK_NOPE = 128
QK_ROPE = 64
V_HEAD = 128
QK_DIM = QK_NOPE + QK_ROPE
ROPE_AXIS = QK_ROPE // 2
ROPE_THETA = 10000.0
Q_BLOCK = 128
D_FF = 2816
N_EXPERTS = 8
TOP_K = 2
MOE_BLOCK = 128

kernel_name = 'hybrid_ssd_sgu_mla_dit_step'


def rms_norm(x):
    xf = x.astype(jnp.float32)
    return (xf * lax.rsqrt(jnp.mean(xf * xf, axis=-1, keepdims=True) + EPS)).astype(x.dtype)


def modulate(x, shift, scale):
    return rms_norm(x) * (1 + scale) + shift


def decay_matrix(cs):
    t = cs.shape[-1]
    lower = jnp.tril(jnp.ones((t, t), dtype=bool))
    return jnp.exp(jnp.where(lower, cs[..., :, None] - cs[..., None, :], -jnp.inf))


def ssd_scan(x, dt, a, bm, cm, init_state):
    f32 = jnp.float32
    b, l, h, p = x.shape
    nc = l // SSD_CHUNK
    hpg = h // SSD_GROUPS
    xc = (x.astype(f32) * dt[..., None]).reshape(b, nc, SSD_CHUNK, h, p)
    ac = jnp.moveaxis((dt * a).reshape(b, nc, SSD_CHUNK, h), 3, 1)
    a_cs = jnp.cumsum(ac, axis=-1)
    bc = bm.astype(f32).reshape(b, nc, SSD_CHUNK, SSD_GROUPS, SSD_STATE)
    cc = cm.astype(f32).reshape(b, nc, SSD_CHUNK, SSD_GROUPS, SSD_STATE)
    cb = jnp.repeat(jnp.einsum('bclgn,bcsgn->bcgls', cc, bc), hpg, axis=2)
    y_diag = jnp.einsum('bchls,bhcls,bcshp->bclhp', cb, decay_matrix(a_cs), xc)
    b_h = jnp.repeat(bc, hpg, axis=3)
    c_h = jnp.repeat(cc, hpg, axis=3)
    decay_to_end = jnp.exp(a_cs[..., -1:] - a_cs)
    chunk_states = jnp.einsum('bclhn,bhcl,bclhp->bchpn', b_h, decay_to_end, xc)
    chunk_states = jnp.concatenate([init_state.astype(f32)[:, None], chunk_states], axis=1)
    chunk_cs = jnp.cumsum(jnp.pad(a_cs[..., -1], ((0, 0), (0, 0), (1, 0))), axis=-1)
    states = jnp.einsum('bhzc,bchpn->bzhpn', decay_matrix(chunk_cs), chunk_states)
    y_off = jnp.einsum('bclhn,bchpn,bhcl->bclhp', c_h, states[:, :-1], jnp.exp(a_cs))
    y = (y_diag + y_off).reshape(b, l, h, p).astype(x.dtype)
    return y, states[:, -1].astype(init_state.dtype)


def centred_dwconv(x, w, bias):
    k = w.shape[0]
    pad = k // 2
    l = x.shape[1]
    xp = jnp.pad(x, ((0, 0), (pad, pad), (0, 0)))
    return sum(xp[:, i:i + l] * w[i] for i in range(k)) + bias


def spatial_gating(u, v, sgu_norm, w_sp, b_sp):
    b, l, _ = v.shape
    nc = l // SGU_CHUNK
    vb = (rms_norm(v) * sgu_norm).reshape(b, nc, SGU_CHUNK, SGU_GROUPS, SGU_GDIM)
    sv = jnp.einsum('gts,bcsgd->bctgd', w_sp, vb) + b_sp.T[:, :, None]
    return u * sv.reshape(b, l, D_SGU)


def ssd_sgu_mixer(h, init_state, w_in, conv_w, conv_b, dt_bias, a_log, d_skip, ssd_norm,
                  sgu_norm, w_sp, b_sp, w_out):
    b, l, _ = h.shape
    i1 = D_SSD
    i2 = i1 + CONV_CH
    i3 = i2 + 2 * SSD_HEADS
    i4 = i3 + D_SGU
    z, xbc, dt_raw, u, v = jnp.split(h @ w_in, [i1, i2, i3, i4], axis=-1)
    xbc = jax.nn.silu(centred_dwconv(xbc, conv_w, conv_b))
    xs, bm, cm = jnp.split(xbc, [D_SSD, D_SSD + SSD_GROUPS * SSD_STATE], axis=-1)
    xs = xs.reshape(b, l, SSD_HEADS, SSD_HEADDIM)
    bm = bm.reshape(b, l, SSD_GROUPS, SSD_STATE)
    cm = cm.reshape(b, l, SSD_GROUPS, SSD_STATE)
    dt = jax.nn.softplus(dt_raw.astype(jnp.float32).reshape(b, l, 2, SSD_HEADS) + dt_bias.astype(jnp.float32))
    a = -jnp.exp(a_log.astype(jnp.float32))
    y_f, s_f = ssd_scan(xs, dt[:, :, 0], a[0], bm, cm, init_state[:, 0])
    y_b, s_b = ssd_scan(xs[:, ::-1], dt[:, ::-1, 1], a[1], bm[:, ::-1], cm[:, ::-1], init_state[:, 1])
    y = y_f + y_b[:, ::-1] + xs * d_skip[:, None]
    y = rms_norm(y.reshape(b, l, D_SSD) * jax.nn.silu(z)) * ssd_norm
    y_sgu = spatial_gating(u, v, sgu_norm, w_sp, b_sp)
    out = jnp.concatenate([y, y_sgu], axis=-1) @ w_out
    return out, jnp.stack([s_f, s_b], axis=1)


def rope_rotate(x, ang):
    m = ang.shape[-1]
    cos = jnp.cos(ang)[:, None, :].astype(x.dtype)
    sin = jnp.sin(ang)[:, None, :].astype(x.dtype)
    x1, x2 = x[..., :m], x[..., m:]
    return jnp.concatenate([x1 * cos - x2 * sin, x1 * sin + x2 * cos], axis=-1)


def axial_rope(x):
    l = x.shape[1]
    rows = l // GRID_W
    row = jnp.repeat(jnp.arange(rows), GRID_W).astype(jnp.float32)
    col = jnp.tile(jnp.arange(GRID_W), rows).astype(jnp.float32)
    inv = ROPE_THETA ** (-jnp.arange(0, ROPE_AXIS, 2, dtype=jnp.float32) / ROPE_AXIS)
    xr = x[..., QK_NOPE:QK_NOPE + ROPE_AXIS]
    xc = x[..., QK_NOPE + ROPE_AXIS:]
    return jnp.concatenate([x[..., :QK_NOPE], rope_rotate(xr, row[:, None] * inv),
                            rope_rotate(xc, col[:, None] * inv)], axis=-1)


def mla_queries(h, w_dq, q_a_norm, w_uq, q_norm):
    b, l, _ = h.shape
    q = (rms_norm(h @ w_dq) * q_a_norm) @ w_uq
    return rms_norm(q.reshape(b, l, MLA_HEADS, QK_DIM)) * q_norm


def mla_compress_kv(h, w_dkv, kv_a_norm):
    kv_a = h @ w_dkv
    return rms_norm(kv_a[..., :KV_RANK]) * kv_a_norm, kv_a[..., KV_RANK:]


def mla_expand_kv(c_kv, k_pe, w_ukv, k_norm):
    b, l, _ = c_kv.shape
    kv = (c_kv @ w_ukv).reshape(b, l, MLA_HEADS, QK_NOPE + V_HEAD)
    k = jnp.concatenate([kv[..., :QK_NOPE],
                         jnp.broadcast_to(k_pe[:, :, None, :], (b, l, MLA_HEADS, QK_ROPE))], axis=-1)
    return rms_norm(k) * k_norm, kv[..., QK_NOPE:]


def attention(q, k, v):
    b, lq, h, dh = q.shape
    nb = lq // Q_BLOCK
    qb = jnp.moveaxis(q.reshape(b, nb, Q_BLOCK, h, dh), 1, 0)
    scale = dh ** -0.5

    def one_block(qi):
        s = jnp.einsum('bqhd,bkhd->bhqk', qi, k, preferred_element_type=jnp.float32) * scale
        p = jax.nn.softmax(s, axis=-1).astype(v.dtype)
        return jnp.einsum('bhqk,bkhd->bqhd', p, v)

    o = lax.map(one_block, qb)
    return jnp.moveaxis(o, 0, 1).reshape(b, lq, h, v.shape[-1])


def mla_output(o, w_o):
    b, l = o.shape[:2]
    return o.reshape(b, l, MLA_HEADS * V_HEAD) @ w_o


def swiglu(h, wg, wu, wd):
    return (jax.nn.silu(h @ wg) * (h @ wu)) @ wd


def moe_swiglu(h, w_router, wg, wu, wd):
    shp = h.shape
    t = h.reshape(-1, shp[-1])
    n_tok = t.shape[0]
    logits = jnp.einsum('td,de->te', t, w_router, preferred_element_type=jnp.float32)
    top_logit, top_e = lax.top_k(logits, TOP_K)
    gates = jax.nn.softmax(top_logit, axis=-1)
    n_assign = n_tok * TOP_K
    flat_e = top_e.reshape(-1)
    order = jnp.argsort(flat_e)
    e_sorted = flat_e[order]
    tok_sorted = order // TOP_K
    gate_sorted = gates.reshape(-1)[order]
    counts = jnp.bincount(flat_e, length=N_EXPERTS)
    padded = (counts + MOE_BLOCK - 1) // MOE_BLOCK * MOE_BLOCK
    pad_end = jnp.cumsum(padded)
    pad_start = pad_end - padded
    start = jnp.cumsum(counts) - counts
    dest = pad_start[e_sorted] + (jnp.arange(n_assign) - start[e_sorted])
    n_rows = -(-n_assign // MOE_BLOCK) * MOE_BLOCK + N_EXPERTS * MOE_BLOCK
    n_blocks = n_rows // MOE_BLOCK
    buf = jnp.zeros((n_rows, shp[-1]), t.dtype).at[dest].set(t[tok_sorted])
    block_e = jnp.minimum(jnp.searchsorted(pad_end, jnp.arange(n_blocks) * MOE_BLOCK, side='right'),
                          N_EXPERTS - 1)

    def expert_block(args):
        xb, e = args
        return (jax.nn.silu(xb @ wg[e]) * (xb @ wu[e])) @ wd[e]

    out = lax.map(expert_block, (buf.reshape(n_blocks, MOE_BLOCK, shp[-1]), block_e)).reshape(n_rows, shp[-1])
    y = jnp.zeros_like(t).at[tok_sorted].add(out[dest] * gate_sorted[:, None].astype(t.dtype))
    return y.reshape(shp)


def setup_inputs(seed: int = 0) -> dict:
    key = jax.random.key(seed)
    ks = iter(jax.random.split(key, 48))
    f32 = jnp.float32

    def nrm(shape, scale):
        return jax.random.normal(next(ks), shape, f32) * scale

    def gain(shape):
        return 1.0 + nrm(shape, 0.02)

    dt0 = jnp.exp(jax.random.uniform(next(ks), (N_EVEN, 2, SSD_HEADS), f32,
                                     minval=math.log(1e-3), maxval=math.log(1e-1)))
    dt_bias = dt0 + jnp.log(-jnp.expm1(-dt0))
    a_log = jnp.log(jax.random.uniform(next(ks), (N_EVEN, 2, SSD_HEADS), f32, minval=1.0, maxval=16.0))
    return {
        'x_prompt': nrm((BATCH, SEQ, D_MODEL), 1.0),
        'x_sample': nrm((DEC_BATCH, DEC_SEQ, D_MODEL), 1.0),
        'c': nrm((DEC_BATCH, D_MODEL), 1.0),
        'state_ssm': nrm((DEC_BATCH, N_EVEN, 2, SSD_HEADS, SSD_HEADDIM, SSD_STATE), 0.5),
        'cache_ckv': nrm((DEC_BATCH, N_ODD, PAST_LEN, KV_RANK), 1.0),
        'cache_kpe': nrm((DEC_BATCH, N_ODD, PAST_LEN, QK_ROPE), 1.0),
        'c_ctx': nrm((D_MODEL,), 1.0),
        'w_mod': nrm((DEPTH, D_MODEL, N_MOD * D_MODEL), 0.5 * D_MODEL ** -0.5),
        'b_mod': nrm((DEPTH, N_MOD * D_MODEL), 0.02),
        'w_in': nrm((N_EVEN, D_MODEL, IN_COLS), D_MODEL ** -0.5),
        'conv_w': nrm((N_EVEN, CONV_W, CONV_CH), CONV_W ** -0.5),
        'conv_b': nrm((N_EVEN, CONV_CH), 0.02),
        'dt_bias': dt_bias,
        'a_log': a_log,
        'd_skip': gain((N_EVEN, SSD_HEADS)),
        'ssd_norm': gain((N_EVEN, D_SSD)),
        'sgu_norm': gain((N_EVEN, D_SGU)),
        'w_sp': nrm((N_EVEN, SGU_GROUPS, SGU_CHUNK, SGU_CHUNK), SGU_CHUNK ** -0.5),
        'b_sp': gain((N_EVEN, SGU_GROUPS, SGU_CHUNK)),
        'w_out': nrm((N_EVEN, D_SSD + D_SGU, D_MODEL), (D_SSD + D_SGU) ** -0.5),
        'ffn_w_gate': nrm((N_EVEN, D_MODEL, D_FF), D_MODEL ** -0.5),
        'ffn_w_up': nrm((N_EVEN, D_MODEL, D_FF), D_MODEL ** -0.5),
        'ffn_w_down': nrm((N_EVEN, D_FF, D_MODEL), D_FF ** -0.5),
        'w_dq': nrm((N_ODD, D_MODEL, Q_RANK), D_MODEL ** -0.5),
        'q_a_norm': gain((N_ODD, Q_RANK)),
        'w_uq': nrm((N_ODD, Q_RANK, MLA_HEADS * QK_DIM), Q_RANK ** -0.5),
        'w_dkv': nrm((N_ODD, D_MODEL, KV_RANK + QK_ROPE), D_MODEL ** -0.5),
        'kv_a_norm': gain((N_ODD, KV_RANK)),
        'w_ukv': nrm((N_ODD, KV_RANK, MLA_HEADS * (QK_NOPE + V_HEAD)), KV_RANK ** -0.5),
        'q_norm': gain((N_ODD, QK_DIM)),
        'k_norm': gain((N_ODD, QK_DIM)),
        'w_o': nrm((N_ODD, MLA_HEADS * V_HEAD, D_MODEL), (MLA_HEADS * V_HEAD) ** -0.5),
        'router': nrm((N_ODD, D_MODEL, N_EXPERTS), D_MODEL ** -0.5),
        'moe_w_gate': nrm((N_ODD, N_EXPERTS, D_MODEL, D_FF), D_MODEL ** -0.5),
        'moe_w_up': nrm((N_ODD, N_EXPERTS, D_MODEL, D_FF), D_MODEL ** -0.5),
        'moe_w_down': nrm((N_ODD, N_EXPERTS, D_FF, D_MODEL), D_FF ** -0.5),
    }


def reference(x_prompt, x_sample, c, state_ssm, cache_ckv, cache_kpe, c_ctx, w_mod, b_mod,
              w_in, conv_w, conv_b, dt_bias, a_log, d_skip, ssd_norm, sgu_norm, w_sp, b_sp, w_out,
              ffn_w_gate, ffn_w_up, ffn_w_down,
              w_dq, q_a_norm, w_uq, w_dkv, kv_a_norm, w_ukv, q_norm, k_norm, w_o,
              router, moe_w_gate, moe_w_up, moe_w_down):
    xp, xs = x_prompt, x_sample
    silu_ctx = jax.nn.silu(c_ctx)
    silu_c = jax.nn.silu(c)
    new_ssm, new_ckv, new_kpe = [], [], []
    for i in range(DEPTH):
        j = i // 2
        mp = jnp.split(silu_ctx @ w_mod[i] + b_mod[i], N_MOD, axis=-1)
        ms = jnp.split((silu_c @ w_mod[i] + b_mod[i])[:, None, :], N_MOD, axis=-1)
        hp = modulate(xp, mp[0], mp[1])
        hs = modulate(xs, ms[0], ms[1])
        if i % 2 == 0:
            mix = (w_in[j], conv_w[j], conv_b[j], dt_bias[j], a_log[j], d_skip[j], ssd_norm[j],
                   sgu_norm[j], w_sp[j], b_sp[j], w_out[j])
            zero_state = jnp.zeros((xp.shape[0], 2, SSD_HEADS, SSD_HEADDIM, SSD_STATE), xp.dtype)
            op, ctx_state = ssd_sgu_mixer(hp, zero_state, *mix)
            os_, _ = ssd_sgu_mixer(hs, state_ssm[:, j], *mix)
            new_ssm.append(ctx_state)
        else:
            qp = mla_queries(hp, w_dq[j], q_a_norm[j], w_uq[j], q_norm[j])
            ckv_p, kpe_p = mla_compress_kv(hp, w_dkv[j], kv_a_norm[j])
            kp, vp = mla_expand_kv(ckv_p, kpe_p, w_ukv[j], k_norm[j])
            op = mla_output(attention(qp, kp, vp), w_o[j])
            new_ckv.append(ckv_p)
            new_kpe.append(kpe_p)
            qs = axial_rope(mla_queries(hs, w_dq[j], q_a_norm[j], w_uq[j], q_norm[j]))
            ckv_s, kpe_s = mla_compress_kv(hs, w_dkv[j], kv_a_norm[j])
            ks_, vs_ = mla_expand_kv(ckv_s, kpe_s, w_ukv[j], k_norm[j])
            kc, vc = mla_expand_kv(cache_ckv[:, j], cache_kpe[:, j], w_ukv[j], k_norm[j])
            os_ = mla_output(attention(qs, jnp.concatenate([kc, axial_rope(ks_)], axis=1),
                                       jnp.concatenate([vc, vs_], axis=1)), w_o[j])
        xp = xp + mp[2] * op
        xs = xs + ms[2] * os_
        hp = modulate(xp, mp[3], mp[4])
        hs = modulate(xs, ms[3], ms[4])
        if i % 2 == 0:
            fp = swiglu(hp, ffn_w_gate[j], ffn_w_up[j], ffn_w_down[j])
            fs = swiglu(hs, ffn_w_gate[j], ffn_w_up[j], ffn_w_down[j])
        else:
            fp = moe_swiglu(hp, router[j], moe_w_gate[j], moe_w_up[j], moe_w_down[j])
            fs = moe_swiglu(hs, router[j], moe_w_gate[j], moe_w_up[j], moe_w_down[j])
        xp = xp + mp[5] * fp
        xs = xs + ms[5] * fs
    new_state_ssm = jnp.stack(new_ssm, axis=1)
    new_cache_ckv = jnp.stack(new_ckv, axis=1)
    new_cache_kpe = jnp.stack(new_kpe, axis=1)
    return (xp, xs, new_state_ssm, new_cache_ckv, new_cache_kpe)
```

```python
import functools
import math

import jax
import jax.numpy as jnp
from jax import lax
from jax.experimental import pallas as pl
from jax.experimental.pallas import tpu as pltpu

F32 = jnp.float32
BF16 = jnp.bfloat16

D_MODEL = 1024
BATCH = 16
SEQ = 256
DEPTH = 4
DEC_BATCH = 4
DEC_SEQ = 2048
PAST_LEN = 512
GRID_W = 64
N_MOD = 6
EPS = 1e-6

SSD_HEADDIM = 64
SSD_HEADS = 16
D_SSD = 1024
SSD_GROUPS = 4
SSD_STATE = 128
CHUNK = 128
CONV_W = 5
CONV_CH = D_SSD + 2 * SSD_GROUPS * SSD_STATE
D_SGU = 1024
SGU_GROUPS = 4
SGU_GDIM = D_SGU // SGU_GROUPS

MLA_HEADS = 8
Q_RANK = 384
KV_RANK = 256
QK_NOPE = 128
QK_ROPE = 64
V_HEAD = 128
QK_DIM = QK_NOPE + QK_ROPE
ROPE_AXIS = QK_ROPE // 2
ROPE_THETA = 10000.0
HEAD_PAD = 256

D_FF = 2816
N_EXPERTS = 8
TOP_K = 2

N_PROMPT = BATCH * SEQ
N_SAMPLE = DEC_BATCH * DEC_SEQ
N_TOK = N_PROMPT + N_SAMPLE
N_SEG = 1 + DEC_BATCH
SEG_PAD = 8

LANES = 128
VMEM_LIMIT = 56 * 1024 * 1024

TM_ROWS = 256
TM_FFN = 512
FF_CHUNK = 256
TM_MOE = 512
TQ = 256


def _cparams(*sem):
    return pltpu.CompilerParams(dimension_semantics=sem, vmem_limit_bytes=VMEM_LIMIT)


def _dot(a, b):
    return jnp.dot(a, b, preferred_element_type=F32)


def _dot_nt(a, b):
    return lax.dot_general(a, b, (((1,), (1,)), ((), ())), preferred_element_type=F32)


def _split(x):
    hi = x.astype(BF16)
    lo = (x - hi.astype(F32)).astype(BF16)
    return hi, lo


def _dot3(a, b):
    ah, al = _split(a)
    bh, bl = _split(b)
    return _dot(ah, bh) + _dot(ah, bl) + _dot(al, bh)


def _silu(x):
    return x / (1.0 + jnp.exp(-x))


def _rms(x):
    return x * lax.rsqrt(jnp.mean(x * x, axis=-1, keepdims=True) + EPS)


def _modulated(x_ref, mod_ref, first):
    shift = mod_ref[0, first:first + 1, :]
    scale = mod_ref[0, first + 1:first + 2, :]
    return _rms(x_ref[...]) * (1.0 + scale) + shift


def _seg_map(tm):
    def index_map(i, *_):
        r = i * tm
        return (jnp.where(r < N_PROMPT, 0, 1 + (r - N_PROMPT) // DEC_SEQ), 0, 0)
    return index_map


def _row_spec(tm, width):
    return pl.BlockSpec((tm, width), lambda i, *_: (i, 0))


def _const_spec(shape):
    zeros = (0,) * len(shape)
    return pl.BlockSpec(shape, lambda i, *_: zeros)


def _mod_spec(tm):
    return pl.BlockSpec((1, N_MOD, D_MODEL), _seg_map(tm))


def _mod_kernel(c_ref, w_ref, b_ref, o_ref):
    o_ref[0] = _dot3(_silu(c_ref[...]), w_ref[0]) + b_ref[0]


def _modulation_tables(cond, w_mod, b_mod):
    tn = 1536
    return pl.pallas_call(
        _mod_kernel,
        grid=(DEPTH, N_MOD * D_MODEL // tn),
        in_specs=[pl.BlockSpec((SEG_PAD, D_MODEL), lambda l, j: (0, 0)),
                  pl.BlockSpec((1, D_MODEL, tn), lambda l, j: (l, 0, j)),
                  pl.BlockSpec((1, 1, tn), lambda l, j: (l, 0, j))],
        out_specs=pl.BlockSpec((1, SEG_PAD, tn), lambda l, j: (l, 0, j)),
        out_shape=jax.ShapeDtypeStruct((DEPTH, SEG_PAD, N_MOD * D_MODEL), F32),
        compiler_params=_cparams("parallel", "parallel"),
        name="modulation",
    )(cond, w_mod, b_mod.reshape(DEPTH, 1, N_MOD * D_MODEL))


def _inproj_kernel(x_ref, mod_ref, wz, wx, wu, wv, wdt, z_ref, xbc_ref, u_ref, v_ref, dt_ref):
    h = _modulated(x_ref, mod_ref, 0).astype(BF16)
    z_ref[...] = _dot(h, wz[...]).astype(BF16)
    xbc_ref[...] = _dot(h, wx[...]).astype(BF16)
    u_ref[...] = _dot(h, wu[...]).astype(BF16)
    v_ref[...] = _dot(h, wv[...]).astype(BF16)
    dt_ref[...] = _dot(h, wdt[...])


def _in_projection(x, mod, wz, wx, wu, wv, wdt):
    tm = TM_ROWS
    widths = (D_SSD, CONV_CH, D_SGU, D_SGU, LANES)
    dtypes = (BF16, BF16, BF16, BF16, F32)
    return pl.pallas_call(
        _inproj_kernel,
        grid=(N_TOK // tm,),
        in_specs=[_row_spec(tm, D_MODEL), _mod_spec(tm)] + [_const_spec((D_MODEL, w)) for w in widths],
        out_specs=[_row_spec(tm, w) for w in widths],
        out_shape=[jax.ShapeDtypeStruct((N_TOK, w), dt) for w, dt in zip(widths, dtypes)],
        compiler_params=_cparams("parallel"),
        name="in_projection",
    )(x, mod, wz, wx, wu, wv, wdt)


def _ssd_kernel(has_init, n_chunks, *refs):
    if has_init:
        (xbc_ref, dt_ref, init_ref, convw_ref, convb_ref, dtb_ref, alog_ref, dskip_ref,
         tri_ref, ef_ref, eb_ref, y_ref, fin_ref, cv_ref, dtv_ref, cs_ref, st_ref) = refs
    else:
        (xbc_ref, dt_ref, convw_ref, convb_ref, dtb_ref, alog_ref, dskip_ref,
         tri_ref, ef_ref, eb_ref, y_ref, fin_ref, cv_ref, dtv_ref, cs_ref, st_ref) = refs
        init_ref = None
    seq = n_chunks * CHUNK
    halo = 16
    lane = lax.broadcasted_iota(jnp.int32, (1, LANES), 1)
    a_neg = -jnp.exp(alog_ref[...])
    tri = tri_ref[...]
    rows_i = lax.broadcasted_iota(jnp.int32, (CHUNK, CHUNK), 0)
    cols_i = lax.broadcasted_iota(jnp.int32, (CHUNK, CHUNK), 1)
    lower = rows_i >= cols_i
    upper = cols_i >= rows_i
    even_head = jnp.bitwise_and(lax.broadcasted_iota(jnp.int32, (CHUNK, D_SSD), 1), LANES - 1) < SSD_HEADDIM

    def row0(c):
        return pl.multiple_of(c * CHUNK, CHUNK)

    def local_body(c, carry):
        r0 = row0(c)
        prev0 = pl.multiple_of(jnp.maximum(r0 - halo, 0), halo)
        next0 = pl.multiple_of(jnp.minimum(r0 + CHUNK, seq - halo), halo)
        keep_prev = (c > 0).astype(F32)
        keep_next = (c < n_chunks - 1).astype(F32)
        for cb in range(CONV_CH // 256):
            cols = slice(cb * 256, (cb + 1) * 256)
            win = jnp.concatenate([
                xbc_ref[pl.ds(prev0, halo), cols].astype(F32) * keep_prev,
                xbc_ref[pl.ds(r0, CHUNK), cols].astype(F32),
                xbc_ref[pl.ds(next0, halo), cols].astype(F32) * keep_next], axis=0)
            acc = jnp.zeros((CHUNK, 256), F32) + convb_ref[:, cols]
            for k in range(CONV_W):
                shift = (CONV_W // 2 - k) % (CHUNK + 2 * halo)
                rolled = win if shift == 0 else pltpu.roll(win, shift, 0)
                acc = acc + rolled[halo:halo + CHUNK, :] * convw_ref[k:k + 1, cols]
            cv_ref[pl.ds(r0, CHUNK), cols] = _silu(acc).astype(BF16)
        raw = dt_ref[pl.ds(r0, CHUNK), :] + dtb_ref[...]
        dt = jnp.where(lane < 2 * SSD_HEADS, jnp.maximum(raw, 0.0) + jnp.log(1.0 + jnp.exp(-jnp.abs(raw))), 0.0)
        ac = dt * a_neg
        hi = ac.astype(BF16)
        rest = ac - hi.astype(F32)
        mid = rest.astype(BF16)
        lo = (rest - mid.astype(F32)).astype(BF16)
        dtv_ref[pl.ds(r0, CHUNK), :] = dt
        cs_ref[pl.ds(r0, CHUNK), :] = _dot(tri, hi) + _dot(tri, mid) + _dot(tri, lo)
        return carry

    lax.fori_loop(0, n_chunks, local_body, 0)

    def expand(q, e_ref):
        hi, lo = _split(q)
        return _dot(hi, e_ref[...]) + _dot(lo, e_ref[...])

    def chunk_pass(c, forward):
        r0 = row0(c)
        rows = pl.ds(r0, CHUNK)
        e_ref = ef_ref if forward else eb_ref
        off = 0 if forward else SSD_HEADS
        dt = dtv_ref[rows, :]
        cs = cs_ref[rows, :]
        total = cs_ref[pl.ds(r0 + CHUNK - 1, 1), :]
        if forward:
            pos = cs
            to_edge = jnp.exp(total - cs)
            from_edge = jnp.exp(cs)
            mask = lower
        else:
            pos = cs - dt * a_neg
            to_edge = jnp.exp(pos)
            from_edge = jnp.exp(total - pos)
            mask = upper
        pos_t = pos.T
        x = cv_ref[rows, 0:D_SSD]
        dt_e = expand(dt, e_ref)
        xc = x.astype(F32) * dt_e
        xc_sub = (jnp.where(even_head, xc, 0.0).astype(BF16), jnp.where(even_head, 0.0, xc).astype(BF16))
        xd_b = (xc * expand(to_edge, e_ref)).astype(BF16)
        from_e = expand(from_edge, e_ref)
        tot_e = expand(jnp.exp(jnp.broadcast_to(total, (8, LANES))), e_ref)[0:1, :]
        st = st_ref[0 if forward else 1]
        st_b = st.astype(BF16)
        y_parts = []
        st_parts = []
        for g in range(SSD_GROUPS):
            bm = cv_ref[rows, D_SSD + g * SSD_STATE:D_SSD + (g + 1) * SSD_STATE]
            cm = cv_ref[rows, D_SSD + (SSD_GROUPS + g) * SSD_STATE:D_SSD + (SSD_GROUPS + g + 1) * SSD_STATE]
            bm_t = bm.astype(F32).T.astype(BF16)
            cb = _dot(cm, bm_t)
            gcols = slice(g * 256, (g + 1) * 256)
            y_off = _dot(cm, st_b[:, gcols])
            st_parts.append(_dot(bm_t, xd_b[:, gcols]))
            diag = []
            for pair in range(2):
                pcols = slice(g * 256 + pair * LANES, g * 256 + (pair + 1) * LANES)
                acc = None
                for sub in range(2):
                    h = g * 4 + pair * 2 + sub
                    col = pos[:, off + h:off + h + 1]
                    row = pos_t[off + h:off + h + 1, :]
                    diff = (col - row) if forward else (row - col)
                    m = (jnp.where(mask, jnp.exp(diff), 0.0) * cb).astype(BF16)
                    part = _dot(m, xc_sub[sub][:, pcols])
                    acc = part if acc is None else acc + part
                diag.append(acc)
            y_parts.append(jnp.concatenate(diag, axis=1) + y_off * from_e[:, gcols])
        y_new = jnp.concatenate(y_parts, axis=1)
        st_ref[0 if forward else 1] = st * tot_e + jnp.concatenate(st_parts, axis=1)
        if forward:
            y_ref[rows, :] = y_new + x.astype(F32) * dskip_ref[...]
        else:
            y_ref[rows, :] = y_ref[rows, :] + y_new

    if has_init:
        st_ref[0] = init_ref[0, 0].T
        st_ref[1] = init_ref[0, 1].T
    else:
        st_ref[...] = jnp.zeros_like(st_ref)

    def fwd_body(c, carry):
        chunk_pass(c, True)
        return carry

    def bwd_body(i, carry):
        chunk_pass(n_chunks - 1 - i, False)
        return carry

    lax.fori_loop(0, n_chunks, fwd_body, 0)
    lax.fori_loop(0, n_chunks, bwd_body, 0)
    fin_ref[0, 0] = st_ref[0].T
    fin_ref[0, 1] = st_ref[1].T


def _ssd_scan(xbc, dt, init, consts, n_seq, seq, row_block0):
    n_chunks = seq // CHUNK
    has_init = init is not None
    once = pl.Buffered(1)
    seq_spec = lambda w: pl.BlockSpec((seq, w), lambda b: (row_block0 + b, 0), pipeline_mode=once)
    in_specs = [seq_spec(CONV_CH), seq_spec(LANES)]
    args = [xbc, dt]
    if has_init:
        in_specs.append(pl.BlockSpec((1, 2, D_SSD, SSD_STATE), lambda b: (b, 0, 0, 0), pipeline_mode=once))
        args.append(init)
    for cst in consts:
        in_specs.append(pl.BlockSpec(cst.shape, lambda b, nd=cst.ndim: (0,) * nd))
        args.append(cst)
    return pl.pallas_call(
        functools.partial(_ssd_kernel, has_init, n_chunks),
        grid=(n_seq,),
        in_specs=in_specs,
        out_specs=[pl.BlockSpec((seq, D_SSD), lambda b: (b, 0)),
                   pl.BlockSpec((1, 2, D_SSD, SSD_STATE), lambda b: (b, 0, 0, 0))],
        out_shape=[jax.ShapeDtypeStruct((n_seq * seq, D_SSD), F32),
                   jax.ShapeDtypeStruct((n_seq, 2, D_SSD, SSD_STATE), F32)],
        scratch_shapes=[pltpu.VMEM((seq, CONV_CH), BF16),
                        pltpu.VMEM((seq, LANES), F32),
                        pltpu.VMEM((seq, LANES), F32),
                        pltpu.VMEM((2, SSD_STATE, D_SSD), F32)],
        compiler_params=_cparams("parallel"),
        name="ssd_scan_%d" % seq,
    )(*args)


def _mixer_out_kernel(x_ref, mod_ref, y_ref, z_ref, u_ref, v_ref, ssdn_ref, sgun_ref, wsp_ref, bsp_ref,
                      wo_ref, o_ref):
    tm = x_ref.shape[0]
    gated = y_ref[...] * _silu(z_ref[...].astype(F32))
    a = (_rms(gated) * ssdn_ref[...]).astype(BF16)
    vb = (_rms(v_ref[...].astype(F32)) * sgun_ref[...]).astype(BF16)
    chunks = []
    for k in range(tm // CHUNK):
        rows = slice(k * CHUNK, (k + 1) * CHUNK)
        groups = [_dot(wsp_ref[g], vb[rows, g * SGU_GDIM:(g + 1) * SGU_GDIM]) for g in range(SGU_GROUPS)]
        chunks.append(jnp.concatenate(groups, axis=1) + bsp_ref[...])
    s = (u_ref[...].astype(F32) * jnp.concatenate(chunks, axis=0)).astype(BF16)
    out = _dot(a, wo_ref[0:D_SSD, :]) + _dot(s, wo_ref[D_SSD:D_SSD + D_SGU, :])
    o_ref[...] = x_ref[...] + mod_ref[0, 2:3, :] * out


def _mixer_output(x, mod, y, z, u, v, ssd_norm, sgu_norm, w_sp, b_sp_e, w_out):
    tm = TM_ROWS
    return pl.pallas_call(
        _mixer_out_kernel,
        grid=(N_TOK // tm,),
        in_specs=[_row_spec(tm, D_MODEL), _mod_spec(tm), _row_spec(tm, D_SSD), _row_spec(tm, D_SSD),
                  _row_spec(tm, D_SGU), _row_spec(tm, D_SGU), _const_spec((1, D_SSD)), _const_spec((1, D_SGU)),
                  _const_spec((SGU_GROUPS, CHUNK, CHUNK)), _const_spec((CHUNK, D_SGU)),
                  _const_spec((D_SSD + D_SGU, D_MODEL))],
        out_specs=_row_spec(tm, D_MODEL),
        out_shape=jax.ShapeDtypeStruct((N_TOK, D_MODEL), F32),
        compiler_params=_cparams("parallel"),
        name="mixer_output",
    )(x, mod, y, z, u, v, ssd_norm, sgu_norm, w_sp, b_sp_e, w_out)


def _ffn_kernel(x_ref, mod_ref, wg_ref, wu_ref, wd_ref, o_ref):
    h = _modulated(x_ref, mod_ref, 3).astype(BF16)
    acc = jnp.zeros(o_ref.shape, F32)
    for f in range(D_FF // FF_CHUNK):
        cols = slice(f * FF_CHUNK, (f + 1) * FF_CHUNK)
        act = (_silu(_dot(h, wg_ref[:, cols])) * _dot(h, wu_ref[:, cols])).astype(BF16)
        acc = acc + _dot(act, wd_ref[cols, :])
    o_ref[...] = x_ref[...] + mod_ref[0, 5:6, :] * acc


def _dense_ffn(x, mod, wg, wu, wd):
    tm = TM_FFN
    once = pl.Buffered(1)
    return pl.pallas_call(
        _ffn_kernel,
        grid=(N_TOK // tm,),
        in_specs=[_row_spec(tm, D_MODEL), _mod_spec(tm),
                  pl.BlockSpec((D_MODEL, D_FF), lambda i: (0, 0), pipeline_mode=once),
                  pl.BlockSpec((D_MODEL, D_FF), lambda i: (0, 0), pipeline_mode=once),
                  pl.BlockSpec((D_FF, D_MODEL), lambda i: (0, 0), pipeline_mode=once)],
        out_specs=_row_spec(tm, D_MODEL),
        out_shape=jax.ShapeDtypeStruct((N_TOK, D_MODEL), F32),
        compiler_params=_cparams("parallel"),
        name="dense_ffn",
    )(x, mod, wg, wu, wd)


def _rope(block, cos, sin_up, sin_dn):
    half = ROPE_AXIS // 2
    return block * cos + pltpu.roll(block, half, 1) * sin_up + pltpu.roll(block, LANES - half, 1) * sin_dn


def _expand_keys(ckv_b, kpe, wuk_ref, wuv_ref, kn_ref, cos, sin_up, sin_dn, k_ref, v_ref):
    kn_nope = kn_ref[:, 0:QK_NOPE]
    kn_rope = kn_ref[:, QK_NOPE:QK_NOPE + LANES]
    v_ref[...] = _dot(ckv_b, wuv_ref[...]).astype(BF16)
    k_nope = _dot(ckv_b, wuk_ref[...])
    pe_ss = jnp.sum(kpe * kpe, axis=-1, keepdims=True)
    pe_rot = _rope(kpe * kn_rope, cos, sin_up, sin_dn)
    for h in range(MLA_HEADS):
        kh = k_nope[:, h * QK_NOPE:(h + 1) * QK_NOPE]
        r = lax.rsqrt((jnp.sum(kh * kh, axis=-1, keepdims=True) + pe_ss) * (1.0 / QK_DIM) + EPS)
        k_ref[:, h * HEAD_PAD:h * HEAD_PAD + QK_NOPE] = (kh * r * kn_nope).astype(BF16)
        k_ref[:, h * HEAD_PAD + QK_NOPE:(h + 1) * HEAD_PAD] = (pe_rot * r).astype(BF16)


def _mla_proj_kernel(x_ref, mod_ref, cos_ref, sup_ref, sdn_ref, wdq_ref, wdkv_ref, qan_ref, kvan_ref,
                     wuq_ref, wuk_ref, wuv_ref, qn_ref, kn_ref, q_ref, k_ref, v_ref, ckv_ref, kpe_ref):
    h = _modulated(x_ref, mod_ref, 0).astype(BF16)
    cos, sin_up, sin_dn = cos_ref[...], sup_ref[...], sdn_ref[...]
    qa = (_rms(_dot(h, wdq_ref[...])) * qan_ref[...]).astype(BF16)
    q = _dot(qa, wuq_ref[...])
    qn_nope = qn_ref[:, 0:QK_NOPE]
    qn_rope = qn_ref[:, QK_NOPE:QK_NOPE + LANES]
    scale = QK_DIM ** -0.5
    for hd in range(MLA_HEADS):
        qh = q[:, hd * HEAD_PAD:hd * HEAD_PAD + QK_NOPE]
        qr = q[:, hd * HEAD_PAD + QK_NOPE:(hd + 1) * HEAD_PAD]
        ss = jnp.sum(qh * qh, axis=-1, keepdims=True) + jnp.sum(qr * qr, axis=-1, keepdims=True)
        r = lax.rsqrt(ss * (1.0 / QK_DIM) + EPS) * scale
        q_ref[:, hd * HEAD_PAD:hd * HEAD_PAD + QK_NOPE] = (qh * r * qn_nope).astype(BF16)
        q_ref[:, hd * HEAD_PAD + QK_NOPE:(hd + 1) * HEAD_PAD] = (
            _rope(qr * qn_rope, cos, sin_up, sin_dn) * r).astype(BF16)
    kva = _dot(h, wdkv_ref[...])
    ckv = _rms(kva[:, 0:KV_RANK]) * kvan_ref[...]
    kpe = kva[:, KV_RANK:KV_RANK + LANES]
    ckv_ref[...] = ckv
    kpe_ref[...] = kpe
    _expand_keys(ckv.astype(BF16), kpe, wuk_ref, wuv_ref, kn_ref, cos, sin_up, sin_dn, k_ref, v_ref)


def _mla_projection(x, mod, rope_tabs, wdq, wdkv, qan, kvan, wuq, wuk, wuv, qn, kn):
    tm = TM_ROWS
    hw = MLA_HEADS * HEAD_PAD
    out_w = (hw, hw, MLA_HEADS * V_HEAD, KV_RANK, LANES)
    out_dt = (BF16, BF16, BF16, F32, F32)
    consts = (wdq, wdkv, qan, kvan, wuq, wuk, wuv, qn, kn)
    return pl.pallas_call(
        _mla_proj_kernel,
        grid=(N_TOK // tm,),
        in_specs=[_row_spec(tm, D_MODEL), _mod_spec(tm)] + [_row_spec(tm, LANES)] * 3
                 + [_const_spec(cst.shape) for cst in consts],
        out_specs=[_row_spec(tm, w) for w in out_w],
        out_shape=[jax.ShapeDtypeStruct((N_TOK, w), dt) for w, dt in zip(out_w, out_dt)],
        compiler_params=_cparams("parallel"),
        name="mla_projection",
    )(x, mod, *rope_tabs, *consts)


def _cache_kv_kernel(ckv_ref, kpe_ref, wuk_ref, wuv_ref, kn_ref, k_ref, v_ref):
    ones = jnp.ones((1, LANES), F32)
    zeros = jnp.zeros((1, LANES), F32)
    _expand_keys(ckv_ref[...].astype(BF16), kpe_ref[...], wuk_ref, wuv_ref, kn_ref, ones, zeros, zeros,
                 k_ref, v_ref)


def _cache_keys(ckv, kpe, wuk, wuv, kn):
    rows = ckv.shape[0]
    tm = TM_ROWS
    consts = (wuk, wuv, kn)
    return pl.pallas_call(
        _cache_kv_kernel,
        grid=(rows // tm,),
        in_specs=[_row_spec(tm, KV_RANK), _row_spec(tm, LANES)] + [_const_spec(cst.shape) for cst in consts],
        out_specs=[_row_spec(tm, MLA_HEADS * HEAD_PAD), _row_spec(tm, MLA_HEADS * V_HEAD)],
        out_shape=[jax.ShapeDtypeStruct((rows, MLA_HEADS * HEAD_PAD), BF16),
                   jax.ShapeDtypeStruct((rows, MLA_HEADS * V_HEAD), BF16)],
        compiler_params=_cparams("parallel"),
        name="cache_keys",
    )(ckv, kpe, *consts)


def _attn_kernel(has_cache, *refs):
    if has_cache:
        q_ref, k_ref, v_ref, kc_ref, vc_ref, o_ref = refs
    else:
        q_ref, k_ref, v_ref, o_ref = refs
    q = q_ref[...]
    s = _dot_nt(q, k_ref[...])
    m = jnp.max(s, axis=-1, keepdims=True)
    if has_cache:
        sc = _dot_nt(q, kc_ref[...])
        m = jnp.maximum(m, jnp.max(sc, axis=-1, keepdims=True))
        pc = jnp.exp(sc - m)
    p = jnp.exp(s - m)
    den = jnp.sum(p, axis=-1, keepdims=True)
    num = _dot(p.astype(BF16), v_ref[...])
    if has_cache:
        den = den + jnp.sum(pc, axis=-1, keepdims=True)
        num = num + _dot(pc.astype(BF16), vc_ref[...])
    o_ref[...] = (num / den).astype(BF16)


def _attention(q, k, v, kc, vc, n_seq, seq, row_block0):
    has_cache = kc is not None
    tq = min(TQ, seq)
    n_q = seq // tq
    in_specs = [pl.BlockSpec((tq, HEAD_PAD), lambda b, h, i: ((row_block0 + b) * n_q + i, h)),
                pl.BlockSpec((seq, HEAD_PAD), lambda b, h, i: (row_block0 + b, h)),
                pl.BlockSpec((seq, V_HEAD), lambda b, h, i: (row_block0 + b, h))]
    args = [q, k, v]
    if has_cache:
        past = kc.shape[0] // n_seq
        in_specs += [pl.BlockSpec((past, HEAD_PAD), lambda b, h, i: (b, h)),
                     pl.BlockSpec((past, V_HEAD), lambda b, h, i: (b, h))]
        args += [kc, vc]
    return pl.pallas_call(
        functools.partial(_attn_kernel, has_cache),
        grid=(n_seq, MLA_HEADS, n_q),
        in_specs=in_specs,
        out_specs=pl.BlockSpec((tq, V_HEAD), lambda b, h, i: (b * n_q + i, h)),
        out_shape=jax.ShapeDtypeStruct((n_seq * seq, MLA_HEADS * V_HEAD), BF16),
        compiler_params=_cparams("parallel", "parallel", "arbitrary"),
        name="attention_%d" % seq,
    )(*args)


def _residual_proj_kernel(gate_row, x_ref, mod_ref, a_ref, w_ref, o_ref):
    o_ref[...] = x_ref[...] + mod_ref[0, gate_row:gate_row + 1, :] * _dot(a_ref[...], w_ref[...])


def _residual_projection(x, mod, a, w, gate_row):
    tm = TM_FFN
    k = a.shape[1]
    return pl.pallas_call(
        functools.partial(_residual_proj_kernel, gate_row),
        grid=(N_TOK // tm,),
        in_specs=[_row_spec(tm, D_MODEL), _mod_spec(tm), _row_spec(tm, k), _const_spec((k, D_MODEL))],
        out_specs=_row_spec(tm, D_MODEL),
        out_shape=jax.ShapeDtypeStruct((N_TOK, D_MODEL), F32),
        compiler_params=_cparams("parallel"),
        name="residual_projection",
    )(x, mod, a, w)


def _router_kernel(x_ref, mod_ref, wr_ref, h_ref, meta_ref):
    h = _modulated(x_ref, mod_ref, 3)
    h_ref[...] = h.astype(BF16)
    lane = lax.broadcasted_iota(jnp.int32, (x_ref.shape[0], LANES), 1)
    lane_f = lane.astype(F32)
    logits = jnp.where(lane < N_EXPERTS, _dot3(h, wr_ref[...]), -jnp.inf)
    m1 = jnp.max(logits, axis=-1, keepdims=True)
    e1 = jnp.min(jnp.where(logits == m1, lane_f, float(LANES)), axis=-1, keepdims=True)
    rest = jnp.where(lane_f == e1, -jnp.inf, logits)
    m2 = jnp.max(rest, axis=-1, keepdims=True)
    e2 = jnp.min(jnp.where(rest == m2, lane_f, float(LANES)), axis=-1, keepdims=True)
    t = jnp.exp(m2 - m1)
    g1 = 1.0 / (1.0 + t)
    meta_ref[...] = jnp.where(lane == 0, e1, jnp.where(lane == 1, e2, jnp.where(lane == 2, g1, 1.0 - g1)))


def _router(x, mod, wr):
    tm = TM_ROWS
    return pl.pallas_call(
        _router_kernel,
        grid=(N_TOK // tm,),
        in_specs=[_row_spec(tm, D_MODEL), _mod_spec(tm), _const_spec((D_MODEL, LANES))],
        out_specs=[_row_spec(tm, D_MODEL), _row_spec(tm, LANES)],
        out_shape=[jax.ShapeDtypeStruct((N_TOK, D_MODEL), BF16), jax.ShapeDtypeStruct((N_TOK, LANES), F32)],
        compiler_params=_cparams("parallel"),
        name="router",
    )(x, mod, wr)


def _experts_kernel(be_ref, nu_ref, rows_ref, wg_ref, wu_ref, wd_ref, o_ref, acc_ref):
    i = pl.program_id(0)
    f = pl.program_id(1)

    @pl.when(i < nu_ref[0])
    def _():
        @pl.when(f == 0)
        def _():
            acc_ref[...] = jnp.zeros_like(acc_ref)

        rows = rows_ref[...]
        act = (_silu(_dot(rows, wg_ref[0].astype(BF16))) * _dot(rows, wu_ref[0].astype(BF16))).astype(BF16)
        acc_ref[...] += _dot(act, wd_ref[0].astype(BF16))

    @pl.when(f == pl.num_programs(1) - 1)
    def _():
        o_ref[...] = acc_ref[...]


def _expert_ffn(rows, block_expert, n_used, wg, wu, wd):
    n_rows = rows.shape[0]
    tm = TM_MOE
    tf = FF_CHUNK
    n_f = D_FF // tf

    def f_idx(i, f, nu):
        return jnp.where(i < nu[0], f, n_f - 1)

    grid_spec = pltpu.PrefetchScalarGridSpec(
        num_scalar_prefetch=2,
        grid=(n_rows // tm, n_f),
        in_specs=[pl.BlockSpec((tm, D_MODEL), lambda i, f, be, nu: (i, 0)),
                  pl.BlockSpec((1, D_MODEL, tf), lambda i, f, be, nu: (be[i], 0, f_idx(i, f, nu))),
                  pl.BlockSpec((1, D_MODEL, tf), lambda i, f, be, nu: (be[i], 0, f_idx(i, f, nu))),
                  pl.BlockSpec((1, tf, D_MODEL), lambda i, f, be, nu: (be[i], f_idx(i, f, nu), 0))],
        out_specs=pl.BlockSpec((tm, D_MODEL), lambda i, f, be, nu: (i, 0)),
        scratch_shapes=[pltpu.VMEM((tm, D_MODEL), F32)])
    return pl.pallas_call(
        _experts_kernel,
        grid_spec=grid_spec,
        out_shape=jax.ShapeDtypeStruct((n_rows, D_MODEL), F32),
        compiler_params=_cparams("arbitrary", "arbitrary"),
        name="expert_ffn",
    )(block_expert, n_used, rows, wg, wu, wd)


def _moe(x, mod, wr, wg, wu, wd):
    h, meta = _router(x, mod, wr)
    experts = meta[:, 0:TOP_K].astype(jnp.int32)
    gates = meta[:, TOP_K:2 * TOP_K]
    tm = TM_MOE
    n_assign = N_TOK * TOP_K
    flat_e = experts.reshape(-1)
    onehot = (flat_e[:, None] == jnp.arange(N_EXPERTS, dtype=jnp.int32)[None, :]).astype(jnp.int32)
    before = jnp.cumsum(onehot, axis=0) - onehot
    rank = jnp.sum(before * onehot, axis=1)
    counts = jnp.sum(onehot, axis=0)
    padded = (counts + tm - 1) // tm * tm
    pad_end = jnp.cumsum(padded)
    dest = (pad_end - padded)[flat_e] + rank
    n_rows = n_assign + N_EXPERTS * tm
    n_blocks = n_rows // tm
    row_token = jnp.zeros((n_rows,), jnp.int32).at[dest].set(jnp.arange(n_assign, dtype=jnp.int32) // TOP_K)
    n_used = (pad_end[-1] // tm).astype(jnp.int32).reshape(1)
    block_start = jnp.minimum(jnp.arange(n_blocks, dtype=jnp.int32), n_used[0] - 1) * tm
    block_expert = jnp.minimum(jnp.searchsorted(pad_end, block_start, side='right'),
                               N_EXPERTS - 1).astype(jnp.int32)
    out = _expert_ffn(h[row_token], block_expert, n_used, wg, wu, wd)
    picked = out[dest].reshape(N_TOK, TOP_K, D_MODEL)
    y = picked[:, 0] * gates[:, 0:1] + picked[:, 1] * gates[:, 1:2]
    gate = mod[:, 5]
    seg = jnp.concatenate([jnp.zeros((N_PROMPT,), jnp.int32),
                           1 + jnp.arange(N_SAMPLE, dtype=jnp.int32) // DEC_SEQ])
    return x + gate[seg] * y


def _rope_tables():
    half = ROPE_AXIS // 2
    pos = jnp.arange(DEC_SEQ)
    row = (pos // GRID_W).astype(F32)
    col = (pos % GRID_W).astype(F32)
    inv = ROPE_THETA ** (-jnp.arange(0, ROPE_AXIS, 2, dtype=F32) / ROPE_AXIS)
    ang_r = row[:, None] * inv
    ang_c = col[:, None] * inv
    z16 = jnp.zeros((DEC_SEQ, half), F32)
    pad = jnp.zeros((DEC_SEQ, LANES - QK_ROPE), F32)
    cos = jnp.concatenate([jnp.cos(ang_r), jnp.cos(ang_r), jnp.cos(ang_c), jnp.cos(ang_c), pad + 1.0], axis=1)
    sin_up = jnp.concatenate([z16, jnp.sin(ang_r), z16, jnp.sin(ang_c), pad], axis=1)
    sin_dn = jnp.concatenate([-jnp.sin(ang_r), z16, -jnp.sin(ang_c), z16, pad], axis=1)
    prompt = jnp.zeros((N_PROMPT, LANES), F32)
    tile = lambda t: jnp.tile(t, (DEC_BATCH, 1))
    return (jnp.concatenate([prompt + 1.0, tile(cos)], axis=0),
            jnp.concatenate([prompt, tile(sin_up)], axis=0),
            jnp.concatenate([prompt, tile(sin_dn)], axis=0))


def _pad_lanes(a, width):
    return jnp.pad(a, [(0, 0)] * (a.ndim - 1) + [(0, width - a.shape[-1])])


def _ssd_constants():
    r = jnp.arange(CHUNK)
    tri = (r[:, None] >= r[None, :]).astype(BF16)
    head_of = jnp.arange(D_SSD) // SSD_HEADDIM
    e_f = (r[:, None] == head_of[None, :]).astype(BF16)
    e_b = (r[:, None] == head_of[None, :] + SSD_HEADS).astype(BF16)
    return tri, e_f, e_b


def kernel(x_prompt, x_sample, c, state_ssm, cache_ckv, cache_kpe, c_ctx, w_mod, b_mod, w_in, conv_w, conv_b, dt_bias, a_log, d_skip, ssd_norm, sgu_norm, w_sp, b_sp, w_out, ffn_w_gate, ffn_w_up, ffn_w_down, w_dq, q_a_norm, w_uq, w_dkv, kv_a_norm, w_ukv, q_norm, k_norm, w_o, router, moe_w_gate, moe_w_up, moe_w_down):
    x = jnp.concatenate([x_prompt.reshape(N_PROMPT, D_MODEL), x_sample.reshape(N_SAMPLE, D_MODEL)], axis=0)
    cond = jnp.concatenate([c_ctx[None, :], c, jnp.zeros((SEG_PAD - N_SEG, D_MODEL), F32)], axis=0)
    mods = _modulation_tables(cond, w_mod, b_mod).reshape(DEPTH, SEG_PAD, N_MOD, D_MODEL)
    rope_tabs = _rope_tables()
    tri, e_f, e_b = _ssd_constants()
    i1 = D_SSD
    i2 = i1 + CONV_CH
    i3 = i2 + 2 * SSD_HEADS
    i4 = i3 + D_SGU
    new_ssm, new_ckv, new_kpe = [], [], []
    for i in range(DEPTH):
        j = i // 2
        mod = mods[i]
        if i % 2 == 0:
            w = w_in[j]
            wz, wx, wu, wv = (w[:, 0:i1].astype(BF16), w[:, i1:i2].astype(BF16),
                              w[:, i3:i4].astype(BF16), w[:, i4:].astype(BF16))
            wdt = _pad_lanes(w[:, i2:i3], LANES).astype(BF16)
            z, xbc, u, v, dt = _in_projection(x, mod, wz, wx, wu, wv, wdt)
            consts = (conv_w[j], conv_b[j][None, :], _pad_lanes(dt_bias[j].reshape(1, -1), LANES),
                      _pad_lanes(a_log[j].reshape(1, -1), LANES),
                      jnp.repeat(d_skip[j], SSD_HEADDIM)[None, :], tri, e_f, e_b)
            y_p, st_p = _ssd_scan(xbc, dt, None, consts, BATCH, SEQ, 0)
            init = state_ssm[:, j].reshape(DEC_BATCH, 2, D_SSD, SSD_STATE)
            y_s, _ = _ssd_scan(xbc, dt, init, consts, DEC_BATCH, DEC_SEQ, N_PROMPT // DEC_SEQ)
            new_ssm.append(st_p.reshape(BATCH, 2, SSD_HEADS, SSD_HEADDIM, SSD_STATE))
            y = jnp.concatenate([y_p, y_s], axis=0)
            b_sp_e = jnp.repeat(b_sp[j].T, SGU_GDIM, axis=1)
            x = _mixer_output(x, mod, y, z, u, v, ssd_norm[j][None, :], sgu_norm[j][None, :],
                              w_sp[j].astype(BF16), b_sp_e, w_out[j].astype(BF16))
            x = _dense_ffn(x, mod, ffn_w_gate[j].astype(BF16), ffn_w_up[j].astype(BF16),
                           ffn_w_down[j].astype(BF16))
        else:
            wuq = _pad_lanes(w_uq[j].reshape(Q_RANK, MLA_HEADS, QK_DIM), HEAD_PAD).reshape(Q_RANK, -1)
            wdkv = _pad_lanes(w_dkv[j], KV_RANK + LANES)
            wukv = w_ukv[j].reshape(KV_RANK, MLA_HEADS, QK_NOPE + V_HEAD)
            wuk = wukv[:, :, :QK_NOPE].reshape(KV_RANK, -1).astype(BF16)
            wuv = wukv[:, :, QK_NOPE:].reshape(KV_RANK, -1).astype(BF16)
            qn = _pad_lanes(q_norm[j][None, :], HEAD_PAD)
            kn = _pad_lanes(k_norm[j][None, :], HEAD_PAD)
            q, k, v, ckv, kpe = _mla_projection(
                x, mod, rope_tabs, w_dq[j].astype(BF16), wdkv.astype(BF16), q_a_norm[j][None, :],
                kv_a_norm[j][None, :], wuq.astype(BF16), wuk, wuv, qn, kn)
            new_ckv.append(ckv[:N_PROMPT].reshape(BATCH, SEQ, KV_RANK))
            new_kpe.append(kpe[:N_PROMPT, :QK_ROPE].reshape(BATCH, SEQ, QK_ROPE))
            kc, vc = _cache_keys(cache_ckv[:, j].reshape(DEC_BATCH * PAST_LEN, KV_RANK),
                                 _pad_lanes(cache_kpe[:, j].reshape(DEC_BATCH * PAST_LEN, QK_ROPE), LANES),
                                 wuk, wuv, kn)
            o_p = _attention(q, k, v, None, None, BATCH, SEQ, 0)
            o_s = _attention(q, k, v, kc, vc, DEC_BATCH, DEC_SEQ, N_PROMPT // DEC_SEQ)
            x = _residual_projection(x, mod, jnp.concatenate([o_p, o_s], axis=0), w_o[j].astype(BF16), 2)
            x = _moe(x, mod, _pad_lanes(router[j], LANES), moe_w_gate[j], moe_w_up[j], moe_w_down[j])
    return (x[:N_PROMPT].reshape(BATCH, SEQ, D_MODEL),
            x[N_PROMPT:].reshape(DEC_BATCH, DEC_SEQ, D_MODEL),
            jnp.stack(new_ssm, axis=1),
            jnp.stack(new_ckv, axis=1),
            jnp.stack(new_kpe, axis=1))
```

```python
import functools
import math

import jax
import jax.numpy as jnp
from jax import lax
from jax.experimental import pallas as pl
from jax.experimental.pallas import tpu as pltpu

F32 = jnp.float32
BF16 = jnp.bfloat16

D_MODEL = 1024
BATCH = 16
SEQ = 256
DEPTH = 4
DEC_BATCH = 4
DEC_SEQ = 2048
PAST_LEN = 512
GRID_W = 64
N_MOD = 6
EPS = 1e-6

SSD_HEADDIM = 64
SSD_HEADS = 16
D_SSD = 1024
SSD_GROUPS = 4
SSD_STATE = 128
CHUNK = 128
CONV_W = 5
CONV_CH = D_SSD + 2 * SSD_GROUPS * SSD_STATE
D_SGU = 1024
SGU_GROUPS = 4
SGU_GDIM = D_SGU // SGU_GROUPS

MLA_HEADS = 8
Q_RANK = 384
KV_RANK = 256
QK_NOPE = 128
QK_ROPE = 64
V_HEAD = 128
QK_DIM = QK_NOPE + QK_ROPE
ROPE_AXIS = QK_ROPE // 2
ROPE_THETA = 10000.0
HEAD_PAD = 256

D_FF = 2816
N_EXPERTS = 8
TOP_K = 2

N_PROMPT = BATCH * SEQ
N_SAMPLE = DEC_BATCH * DEC_SEQ
N_TOK = N_PROMPT + N_SAMPLE
N_SEG = 1 + DEC_BATCH
SEG_PAD = 8

LANES = 128
VMEM_LIMIT = 56 * 1024 * 1024

TM_ROWS = 256
TM_FFN = 512
FF_CHUNK = 256
TM_MOE = 1024
TQ = 256


def _cparams(*sem):
    return pltpu.CompilerParams(dimension_semantics=sem, vmem_limit_bytes=VMEM_LIMIT)


def _dot(a, b):
    return jnp.dot(a, b, preferred_element_type=F32)


def _dot_nt(a, b):
    return lax.dot_general(a, b, (((1,), (1,)), ((), ())), preferred_element_type=F32)


def _split(x):
    hi = x.astype(BF16)
    lo = (x - hi.astype(F32)).astype(BF16)
    return hi, lo


def _dot3(a, b):
    ah, al = _split(a)
    bh, bl = _split(b)
    return _dot(ah, bh) + _dot(ah, bl) + _dot(al, bh)


def _silu(x):
    return x / (1.0 + jnp.exp(-x))


def _rms(x):
    return x * lax.rsqrt(jnp.mean(x * x, axis=-1, keepdims=True) + EPS)


def _modulated(x_ref, mod_ref, first):
    shift = mod_ref[0, first:first + 1, :]
    scale = mod_ref[0, first + 1:first + 2, :]
    return _rms(x_ref[...]) * (1.0 + scale) + shift


def _seg_map(tm):
    def index_map(i, *_):
        r = i * tm
        return (jnp.where(r < N_PROMPT, 0, 1 + (r - N_PROMPT) // DEC_SEQ), 0, 0)
    return index_map


def _row_spec(tm, width):
    return pl.BlockSpec((tm, width), lambda i, *_: (i, 0))


def _const_spec(shape):
    zeros = (0,) * len(shape)
    return pl.BlockSpec(shape, lambda i, *_: zeros)


def _mod_spec(tm):
    return pl.BlockSpec((1, N_MOD, D_MODEL), _seg_map(tm))


def _mod_kernel(c_ref, w_ref, b_ref, o_ref):
    o_ref[0] = _dot3(_silu(c_ref[...]), w_ref[0]) + b_ref[0]


def _modulation_tables(cond, w_mod, b_mod):
    tn = 1536
    return pl.pallas_call(
        _mod_kernel,
        grid=(DEPTH, N_MOD * D_MODEL // tn),
        in_specs=[pl.BlockSpec((SEG_PAD, D_MODEL), lambda l, j: (0, 0)),
                  pl.BlockSpec((1, D_MODEL, tn), lambda l, j: (l, 0, j)),
                  pl.BlockSpec((1, 1, tn), lambda l, j: (l, 0, j))],
        out_specs=pl.BlockSpec((1, SEG_PAD, tn), lambda l, j: (l, 0, j)),
        out_shape=jax.ShapeDtypeStruct((DEPTH, SEG_PAD, N_MOD * D_MODEL), F32),
        compiler_params=_cparams("parallel", "parallel"),
        name="modulation",
    )(cond, w_mod, b_mod.reshape(DEPTH, 1, N_MOD * D_MODEL))


def _inproj_kernel(x_ref, mod_ref, wz, wx, wu, wv, wdt, z_ref, xbc_ref, u_ref, v_ref, dt_ref):
    h = _modulated(x_ref, mod_ref, 0).astype(BF16)
    z_ref[...] = _dot(h, wz[...]).astype(BF16)
    xbc_ref[...] = _dot(h, wx[...]).astype(BF16)
    u_ref[...] = _dot(h, wu[...]).astype(BF16)
    v_ref[...] = _dot(h, wv[...]).astype(BF16)
    dt_ref[...] = _dot(h, wdt[...])


def _in_projection(x, mod, wz, wx, wu, wv, wdt):
    tm = TM_ROWS
    widths = (D_SSD, CONV_CH, D_SGU, D_SGU, LANES)
    dtypes = (BF16, BF16, BF16, BF16, F32)
    return pl.pallas_call(
        _inproj_kernel,
        grid=(N_TOK // tm,),
        in_specs=[_row_spec(tm, D_MODEL), _mod_spec(tm)] + [_const_spec((D_MODEL, w)) for w in widths],
        out_specs=[_row_spec(tm, w) for w in widths],
        out_shape=[jax.ShapeDtypeStruct((N_TOK, w), dt) for w, dt in zip(widths, dtypes)],
        compiler_params=_cparams("parallel"),
        name="in_projection",
    )(x, mod, wz, wx, wu, wv, wdt)


def _ssd_kernel(has_init, has_prev, n_chunks, *refs):
    refs = list(refs)
    xbc_ref, dt_ref = refs[0:2]
    del refs[0:2]
    init_ref = refs.pop(0) if has_init else None
    convw_ref, convb_ref, dtb_ref, alog_ref, dskip_ref, tri_ref, ef_ref, eb_ref = refs[0:8]
    del refs[0:8]
    if has_prev:
        refs.pop(0)
    y_ref, fin_ref, cv_ref, dtv_ref, cs_ref, st_ref = refs
    seq = n_chunks * CHUNK
    halo = 16
    lane = lax.broadcasted_iota(jnp.int32, (1, LANES), 1)
    a_neg = -jnp.exp(alog_ref[...])
    tri = tri_ref[...]
    rows_i = lax.broadcasted_iota(jnp.int32, (CHUNK, CHUNK), 0)
    cols_i = lax.broadcasted_iota(jnp.int32, (CHUNK, CHUNK), 1)
    lower = rows_i >= cols_i
    upper = cols_i >= rows_i
    even_head = jnp.bitwise_and(lax.broadcasted_iota(jnp.int32, (CHUNK, D_SSD), 1), LANES - 1) < SSD_HEADDIM

    def row0(c):
        return pl.multiple_of(c * CHUNK, CHUNK)

    def local_body(c, carry):
        r0 = row0(c)
        prev0 = pl.multiple_of(jnp.maximum(r0 - halo, 0), halo)
        next0 = pl.multiple_of(jnp.minimum(r0 + CHUNK, seq - halo), halo)
        keep_prev = (c > 0).astype(F32)
        keep_next = (c < n_chunks - 1).astype(F32)
        for cb in range(CONV_CH // 256):
            cols = slice(cb * 256, (cb + 1) * 256)
            win = jnp.concatenate([
                xbc_ref[pl.ds(prev0, halo), cols].astype(F32) * keep_prev,
                xbc_ref[pl.ds(r0, CHUNK), cols].astype(F32),
                xbc_ref[pl.ds(next0, halo), cols].astype(F32) * keep_next], axis=0)
            acc = jnp.zeros((CHUNK, 256), F32) + convb_ref[:, cols]
            for k in range(CONV_W):
                shift = (CONV_W // 2 - k) % (CHUNK + 2 * halo)
                rolled = win if shift == 0 else pltpu.roll(win, shift, 0)
                acc = acc + rolled[halo:halo + CHUNK, :] * convw_ref[k:k + 1, cols]
            cv_ref[pl.ds(r0, CHUNK), cols] = _silu(acc).astype(BF16)
        raw = dt_ref[pl.ds(r0, CHUNK), :] + dtb_ref[...]
        dt = jnp.where(lane < 2 * SSD_HEADS, jnp.maximum(raw, 0.0) + jnp.log(1.0 + jnp.exp(-jnp.abs(raw))), 0.0)
        ac = dt * a_neg
        hi = ac.astype(BF16)
        rest = ac - hi.astype(F32)
        mid = rest.astype(BF16)
        lo = (rest - mid.astype(F32)).astype(BF16)
        dtv_ref[pl.ds(r0, CHUNK), :] = dt
        cs_ref[pl.ds(r0, CHUNK), :] = _dot(tri, hi) + _dot(tri, mid) + _dot(tri, lo)
        return carry

    lax.fori_loop(0, n_chunks, local_body, 0)

    def expand(q, e_ref):
        hi, lo = _split(q)
        return _dot(hi, e_ref[...]) + _dot(lo, e_ref[...])

    def chunk_pass(c, forward):
        r0 = row0(c)
        rows = pl.ds(r0, CHUNK)
        e_ref = ef_ref if forward else eb_ref
        off = 0 if forward else SSD_HEADS
        dt = dtv_ref[rows, :]
        cs = cs_ref[rows, :]
        total = cs_ref[pl.ds(r0 + CHUNK - 1, 1), :]
        if forward:
            pos = cs
            to_edge = jnp.exp(total - cs)
            from_edge = jnp.exp(cs)
            mask = lower
        else:
            pos = cs - dt * a_neg
            to_edge = jnp.exp(pos)
            from_edge = jnp.exp(total - pos)
            mask = upper
        pos_t = pos.T
        x = cv_ref[rows, 0:D_SSD]
        dt_e = expand(dt, e_ref)
        xc = x.astype(F32) * dt_e
        xc_sub = (jnp.where(even_head, xc, 0.0).astype(BF16), jnp.where(even_head, 0.0, xc).astype(BF16))
        xd_b = (xc * expand(to_edge, e_ref)).astype(BF16)
        from_e = expand(from_edge, e_ref)
        tot_e = expand(jnp.exp(jnp.broadcast_to(total, (8, LANES))), e_ref)[0:1, :]
        st = st_ref[0 if forward else 1]
        st_b = st.astype(BF16)
        y_parts = []
        st_parts = []
        for g in range(SSD_GROUPS):
            bm = cv_ref[rows, D_SSD + g * SSD_STATE:D_SSD + (g + 1) * SSD_STATE]
            cm = cv_ref[rows, D_SSD + (SSD_GROUPS + g) * SSD_STATE:D_SSD + (SSD_GROUPS + g + 1) * SSD_STATE]
            bm_t = bm.astype(F32).T.astype(BF16)
            cb = _dot(cm, bm_t)
            gcols = slice(g * 256, (g + 1) * 256)
            y_off = _dot(cm, st_b[:, gcols])
            st_parts.append(_dot(bm_t, xd_b[:, gcols]))
            diag = []
            for pair in range(2):
                pcols = slice(g * 256 + pair * LANES, g * 256 + (pair + 1) * LANES)
                acc = None
                for sub in range(2):
                    h = g * 4 + pair * 2 + sub
                    col = pos[:, off + h:off + h + 1]
                    row = pos_t[off + h:off + h + 1, :]
                    diff = (col - row) if forward else (row - col)
                    m = (jnp.where(mask, jnp.exp(diff), 0.0) * cb).astype(BF16)
                    part = _dot(m, xc_sub[sub][:, pcols])
                    acc = part if acc is None else acc + part
                diag.append(acc)
            y_parts.append(jnp.concatenate(diag, axis=1) + y_off * from_e[:, gcols])
        y_new = jnp.concatenate(y_parts, axis=1)
        st_ref[0 if forward else 1] = st * tot_e + jnp.concatenate(st_parts, axis=1)
        if forward:
            y_ref[rows, :] = y_new + x.astype(F32) * dskip_ref[...]
        else:
            y_ref[rows, :] = y_ref[rows, :] + y_new

    if has_init:
        st_ref[0] = init_ref[0, 0].T
        st_ref[1] = init_ref[0, 1].T
    else:
        st_ref[...] = jnp.zeros_like(st_ref)

    def fwd_body(c, carry):
        chunk_pass(c, True)
        return carry

    def bwd_body(i, carry):
        chunk_pass(n_chunks - 1 - i, False)
        return carry

    lax.fori_loop(0, n_chunks, fwd_body, 0)
    lax.fori_loop(0, n_chunks, bwd_body, 0)
    fin_ref[0, 0] = st_ref[0].T
    fin_ref[0, 1] = st_ref[1].T


def _ssd_scan(xbc, dt, init, consts, n_seq, seq, row_block0, y_prev=None):
    n_chunks = seq // CHUNK
    has_init = init is not None
    has_prev = y_prev is not None
    once = pl.Buffered(1)
    seq_spec = lambda w: pl.BlockSpec((seq, w), lambda b: (row_block0 + b, 0), pipeline_mode=once)
    in_specs = [seq_spec(CONV_CH), seq_spec(LANES)]
    args = [xbc, dt]
    if has_init:
        in_specs.append(pl.BlockSpec((1, 2, D_SSD, SSD_STATE), lambda b: (b, 0, 0, 0), pipeline_mode=once))
        args.append(init)
    for cst in consts:
        in_specs.append(pl.BlockSpec(cst.shape, lambda b, nd=cst.ndim: (0,) * nd))
        args.append(cst)
    aliases = {}
    if has_prev:
        aliases[len(args)] = 0
        in_specs.append(pl.BlockSpec(memory_space=pl.ANY))
        args.append(y_prev)
    return pl.pallas_call(
        functools.partial(_ssd_kernel, has_init, has_prev, n_chunks),
        grid=(n_seq,),
        in_specs=in_specs,
        out_specs=[pl.BlockSpec((seq, D_SSD), lambda b: (row_block0 + b, 0)),
                   pl.BlockSpec((1, 2, D_SSD, SSD_STATE), lambda b: (b, 0, 0, 0))],
        out_shape=[jax.ShapeDtypeStruct((N_TOK, D_SSD), F32),
                   jax.ShapeDtypeStruct((n_seq, 2, D_SSD, SSD_STATE), F32)],
        input_output_aliases=aliases,
        scratch_shapes=[pltpu.VMEM((seq, CONV_CH), BF16),
                        pltpu.VMEM((seq, LANES), F32),
                        pltpu.VMEM((seq, LANES), F32),
                        pltpu.VMEM((2, SSD_STATE, D_SSD), F32)],
        compiler_params=_cparams("parallel"),
        name="ssd_scan_%d" % seq,
    )(*args)


def _mixer_out_kernel(x_ref, mod_ref, y_ref, z_ref, u_ref, v_ref, ssdn_ref, sgun_ref, wsp_ref, bsp_ref,
                      wo_ref, o_ref):
    tm = x_ref.shape[0]
    gated = y_ref[...] * _silu(z_ref[...].astype(F32))
    a = (_rms(gated) * ssdn_ref[...]).astype(BF16)
    vb = (_rms(v_ref[...].astype(F32)) * sgun_ref[...]).astype(BF16)
    chunks = []
    for k in range(tm // CHUNK):
        rows = slice(k * CHUNK, (k + 1) * CHUNK)
        groups = [_dot(wsp_ref[g], vb[rows, g * SGU_GDIM:(g + 1) * SGU_GDIM]) for g in range(SGU_GROUPS)]
        chunks.append(jnp.concatenate(groups, axis=1) + bsp_ref[...])
    s = (u_ref[...].astype(F32) * jnp.concatenate(chunks, axis=0)).astype(BF16)
    out = _dot(a, wo_ref[0:D_SSD, :]) + _dot(s, wo_ref[D_SSD:D_SSD + D_SGU, :])
    o_ref[...] = x_ref[...] + mod_ref[0, 2:3, :] * out


def _mixer_output(x, mod, y, z, u, v, ssd_norm, sgu_norm, w_sp, b_sp_e, w_out):
    tm = TM_ROWS
    return pl.pallas_call(
        _mixer_out_kernel,
        grid=(N_TOK // tm,),
        in_specs=[_row_spec(tm, D_MODEL), _mod_spec(tm), _row_spec(tm, D_SSD), _row_spec(tm, D_SSD),
                  _row_spec(tm, D_SGU), _row_spec(tm, D_SGU), _const_spec((1, D_SSD)), _const_spec((1, D_SGU)),
                  _const_spec((SGU_GROUPS, CHUNK, CHUNK)), _const_spec((CHUNK, D_SGU)),
                  _const_spec((D_SSD + D_SGU, D_MODEL))],
        out_specs=_row_spec(tm, D_MODEL),
        out_shape=jax.ShapeDtypeStruct((N_TOK, D_MODEL), F32),
        compiler_params=_cparams("parallel"),
        name="mixer_output",
    )(x, mod, y, z, u, v, ssd_norm, sgu_norm, w_sp, b_sp_e, w_out)


def _ffn_kernel(x_ref, mod_ref, wg_ref, wu_ref, wd_ref, o_ref):
    h = _modulated(x_ref, mod_ref, 3).astype(BF16)
    acc = jnp.zeros(o_ref.shape, F32)
    for f in range(D_FF // FF_CHUNK):
        cols = slice(f * FF_CHUNK, (f + 1) * FF_CHUNK)
        act = (_silu(_dot(h, wg_ref[:, cols])) * _dot(h, wu_ref[:, cols])).astype(BF16)
        acc = acc + _dot(act, wd_ref[cols, :])
    o_ref[...] = x_ref[...] + mod_ref[0, 5:6, :] * acc


def _dense_ffn(x, mod, wg, wu, wd):
    tm = TM_FFN
    once = pl.Buffered(1)
    return pl.pallas_call(
        _ffn_kernel,
        grid=(N_TOK // tm,),
        in_specs=[_row_spec(tm, D_MODEL), _mod_spec(tm),
                  pl.BlockSpec((D_MODEL, D_FF), lambda i: (0, 0), pipeline_mode=once),
                  pl.BlockSpec((D_MODEL, D_FF), lambda i: (0, 0), pipeline_mode=once),
                  pl.BlockSpec((D_FF, D_MODEL), lambda i: (0, 0), pipeline_mode=once)],
        out_specs=_row_spec(tm, D_MODEL),
        out_shape=jax.ShapeDtypeStruct((N_TOK, D_MODEL), F32),
        compiler_params=_cparams("parallel"),
        name="dense_ffn",
    )(x, mod, wg, wu, wd)


def _rope(block, cos, sin_up, sin_dn):
    half = ROPE_AXIS // 2
    return block * cos + pltpu.roll(block, half, 1) * sin_up + pltpu.roll(block, LANES - half, 1) * sin_dn


def _expand_keys(ckv_b, kpe, wuk_ref, wuv_ref, kn_ref, cos, sin_up, sin_dn, k_ref, v_ref):
    kn_nope = kn_ref[:, 0:QK_NOPE]
    kn_rope = kn_ref[:, QK_NOPE:QK_NOPE + LANES]
    v_ref[...] = _dot(ckv_b, wuv_ref[...]).astype(BF16)
    k_nope = _dot(ckv_b, wuk_ref[...])
    pe_ss = jnp.sum(kpe * kpe, axis=-1, keepdims=True)
    pe_rot = _rope(kpe * kn_rope, cos, sin_up, sin_dn)
    for h in range(MLA_HEADS):
        kh = k_nope[:, h * QK_NOPE:(h + 1) * QK_NOPE]
        r = lax.rsqrt((jnp.sum(kh * kh, axis=-1, keepdims=True) + pe_ss) * (1.0 / QK_DIM) + EPS)
        k_ref[:, h * HEAD_PAD:h * HEAD_PAD + QK_NOPE] = (kh * r * kn_nope).astype(BF16)
        k_ref[:, h * HEAD_PAD + QK_NOPE:(h + 1) * HEAD_PAD] = (pe_rot * r).astype(BF16)


def _mla_proj_kernel(x_ref, mod_ref, cos_ref, sup_ref, sdn_ref, wdq_ref, wdkv_ref, qan_ref, kvan_ref,
                     wuq_ref, wuk_ref, wuv_ref, qn_ref, kn_ref, q_ref, k_ref, v_ref, ckv_ref, kpe_ref):
    h = _modulated(x_ref, mod_ref, 0).astype(BF16)
    cos, sin_up, sin_dn = cos_ref[...], sup_ref[...], sdn_ref[...]
    qa = (_rms(_dot(h, wdq_ref[...])) * qan_ref[...]).astype(BF16)
    q = _dot(qa, wuq_ref[...])
    qn_nope = qn_ref[:, 0:QK_NOPE]
    qn_rope = qn_ref[:, QK_NOPE:QK_NOPE + LANES]
    scale = QK_DIM ** -0.5
    for hd in range(MLA_HEADS):
        qh = q[:, hd * HEAD_PAD:hd * HEAD_PAD + QK_NOPE]
        qr = q[:, hd * HEAD_PAD + QK_NOPE:(hd + 1) * HEAD_PAD]
        ss = jnp.sum(qh * qh, axis=-1, keepdims=True) + jnp.sum(qr * qr, axis=-1, keepdims=True)
        r = lax.rsqrt(ss * (1.0 / QK_DIM) + EPS) * scale
        q_ref[:, hd * HEAD_PAD:hd * HEAD_PAD + QK_NOPE] = (qh * r * qn_nope).astype(BF16)
        q_ref[:, hd * HEAD_PAD + QK_NOPE:(hd + 1) * HEAD_PAD] = (
            _rope(qr * qn_rope, cos, sin_up, sin_dn) * r).astype(BF16)
    kva = _dot(h, wdkv_ref[...])
    ckv = _rms(kva[:, 0:KV_RANK]) * kvan_ref[...]
    kpe = kva[:, KV_RANK:KV_RANK + LANES]
    ckv_ref[...] = ckv
    kpe_ref[...] = kpe
    _expand_keys(ckv.astype(BF16), kpe, wuk_ref, wuv_ref, kn_ref, cos, sin_up, sin_dn, k_ref, v_ref)


def _mla_projection(x, mod, rope_tabs, wdq, wdkv, qan, kvan, wuq, wuk, wuv, qn, kn):
    tm = TM_ROWS
    hw = MLA_HEADS * HEAD_PAD
    out_w = (hw, hw, MLA_HEADS * V_HEAD, KV_RANK, LANES)
    out_dt = (BF16, BF16, BF16, F32, F32)
    consts = (wdq, wdkv, qan, kvan, wuq, wuk, wuv, qn, kn)
    return pl.pallas_call(
        _mla_proj_kernel,
        grid=(N_TOK // tm,),
        in_specs=[_row_spec(tm, D_MODEL), _mod_spec(tm)] + [_row_spec(tm, LANES)] * 3
                 + [_const_spec(cst.shape) for cst in consts],
        out_specs=[_row_spec(tm, w) for w in out_w],
        out_shape=[jax.ShapeDtypeStruct((N_TOK, w), dt) for w, dt in zip(out_w, out_dt)],
        compiler_params=_cparams("parallel"),
        name="mla_projection",
    )(x, mod, *rope_tabs, *consts)


def _cache_kv_kernel(ckv_ref, kpe_ref, wuk_ref, wuv_ref, kn_ref, k_ref, v_ref):
    ones = jnp.ones((1, LANES), F32)
    zeros = jnp.zeros((1, LANES), F32)
    _expand_keys(ckv_ref[...].astype(BF16), kpe_ref[...], wuk_ref, wuv_ref, kn_ref, ones, zeros, zeros,
                 k_ref, v_ref)


def _cache_keys(ckv, kpe, wuk, wuv, kn):
    rows = ckv.shape[0]
    tm = TM_ROWS
    consts = (wuk, wuv, kn)
    return pl.pallas_call(
        _cache_kv_kernel,
        grid=(rows // tm,),
        in_specs=[_row_spec(tm, KV_RANK), _row_spec(tm, LANES)] + [_const_spec(cst.shape) for cst in consts],
        out_specs=[_row_spec(tm, MLA_HEADS * HEAD_PAD), _row_spec(tm, MLA_HEADS * V_HEAD)],
        out_shape=[jax.ShapeDtypeStruct((rows, MLA_HEADS * HEAD_PAD), BF16),
                   jax.ShapeDtypeStruct((rows, MLA_HEADS * V_HEAD), BF16)],
        compiler_params=_cparams("parallel"),
        name="cache_keys",
    )(ckv, kpe, *consts)


def _attn_kernel(has_cache, *refs):
    if has_cache:
        q_ref, k_ref, v_ref, kc_ref, vc_ref, _, o_ref = refs
    else:
        q_ref, k_ref, v_ref, o_ref = refs
    q = q_ref[...]
    s = _dot_nt(q, k_ref[...])
    m = jnp.max(s, axis=-1, keepdims=True)
    if has_cache:
        sc = _dot_nt(q, kc_ref[...])
        m = jnp.maximum(m, jnp.max(sc, axis=-1, keepdims=True))
        pc = jnp.exp(sc - m)
    p = jnp.exp(s - m)
    den = jnp.sum(p, axis=-1, keepdims=True)
    num = _dot(p.astype(BF16), v_ref[...])
    if has_cache:
        den = den + jnp.sum(pc, axis=-1, keepdims=True)
        num = num + _dot(pc.astype(BF16), vc_ref[...])
    o_ref[...] = (num / den).astype(BF16)


def _attention(q, k, v, kc, vc, o_prev, n_seq, seq, row_block0):
    has_cache = kc is not None
    tq = min(TQ, seq)
    n_q = seq // tq
    in_specs = [pl.BlockSpec((tq, HEAD_PAD), lambda b, h, i: ((row_block0 + b) * n_q + i, h)),
                pl.BlockSpec((seq, HEAD_PAD), lambda b, h, i: (row_block0 + b, h)),
                pl.BlockSpec((seq, V_HEAD), lambda b, h, i: (row_block0 + b, h))]
    args = [q, k, v]
    if has_cache:
        past = kc.shape[0] // n_seq
        in_specs += [pl.BlockSpec((past, HEAD_PAD), lambda b, h, i: (b, h)),
                     pl.BlockSpec((past, V_HEAD), lambda b, h, i: (b, h))]
        in_specs.append(pl.BlockSpec(memory_space=pl.ANY))
        args += [kc, vc, o_prev]
    return pl.pallas_call(
        functools.partial(_attn_kernel, has_cache),
        grid=(n_seq, MLA_HEADS, n_q),
        in_specs=in_specs,
        out_specs=pl.BlockSpec((tq, V_HEAD), lambda b, h, i: ((row_block0 + b) * n_q + i, h)),
        out_shape=jax.ShapeDtypeStruct((N_TOK, MLA_HEADS * V_HEAD), BF16),
        input_output_aliases={5: 0} if has_cache else {},
        compiler_params=_cparams("parallel", "parallel", "arbitrary"),
        name="attention_%d" % seq,
    )(*args)


def _residual_proj_kernel(gate_row, x_ref, mod_ref, a_ref, w_ref, o_ref):
    o_ref[...] = x_ref[...] + mod_ref[0, gate_row:gate_row + 1, :] * _dot(a_ref[...], w_ref[...])


def _residual_projection(x, mod, a, w, gate_row):
    tm = TM_FFN
    k = a.shape[1]
    return pl.pallas_call(
        functools.partial(_residual_proj_kernel, gate_row),
        grid=(N_TOK // tm,),
        in_specs=[_row_spec(tm, D_MODEL), _mod_spec(tm), _row_spec(tm, k), _const_spec((k, D_MODEL))],
        out_specs=_row_spec(tm, D_MODEL),
        out_shape=jax.ShapeDtypeStruct((N_TOK, D_MODEL), F32),
        compiler_params=_cparams("parallel"),
        name="residual_projection",
    )(x, mod, a, w)


def _router_kernel(x_ref, mod_ref, wr_ref, tri_ref, h_ref, meta_ref, cnt_ref, run_ref):
    @pl.when(pl.program_id(0) == 0)
    def _():
        run_ref[...] = jnp.zeros_like(run_ref)

    h = _modulated(x_ref, mod_ref, 3)
    h_ref[...] = h.astype(BF16)
    lane = lax.broadcasted_iota(jnp.int32, (x_ref.shape[0], LANES), 1)
    lane_f = lane.astype(F32)
    logits = jnp.where(lane < N_EXPERTS, _dot3(h, wr_ref[...]), -jnp.inf)
    m1 = jnp.max(logits, axis=-1, keepdims=True)
    e1 = jnp.min(jnp.where(logits == m1, lane_f, float(LANES)), axis=-1, keepdims=True)
    rest = jnp.where(lane_f == e1, -jnp.inf, logits)
    m2 = jnp.max(rest, axis=-1, keepdims=True)
    e2 = jnp.min(jnp.where(rest == m2, lane_f, float(LANES)), axis=-1, keepdims=True)
    t = jnp.exp(m2 - m1)
    g1 = 1.0 / (1.0 + t)
    pick1 = lane_f == e1
    pick2 = lane_f == e2
    onehot = jnp.where(pick1, 1.0, jnp.where(pick2, 1.0, 0.0))
    before = run_ref[...] + _dot(tri_ref[...], onehot.astype(BF16)) - onehot
    rank1 = jnp.sum(jnp.where(pick1, before, 0.0), axis=-1, keepdims=True)
    rank2 = jnp.sum(jnp.where(pick2, before, 0.0), axis=-1, keepdims=True)
    run_ref[...] += jnp.sum(onehot, axis=0, keepdims=True)
    cnt_ref[...] = jnp.broadcast_to(run_ref[...], cnt_ref.shape)
    cols = (e1, e2, g1, 1.0 - g1, rank1, rank2)
    meta = jnp.zeros(meta_ref.shape, F32)
    for idx, val in enumerate(cols):
        meta = jnp.where(lane == idx, val, meta)
    meta_ref[...] = meta


META_EXPERT, META_GATE, META_RANK = 0, 2, 4


def _router(x, mod, wr):
    tm = TM_ROWS
    r = jnp.arange(tm)
    tri = (r[:, None] >= r[None, :]).astype(BF16)
    return pl.pallas_call(
        _router_kernel,
        grid=(N_TOK // tm,),
        in_specs=[_row_spec(tm, D_MODEL), _mod_spec(tm), _const_spec((D_MODEL, LANES)), _const_spec((tm, tm))],
        out_specs=[_row_spec(tm, D_MODEL), _row_spec(tm, LANES), _const_spec((SEG_PAD, LANES))],
        out_shape=[jax.ShapeDtypeStruct((N_TOK, D_MODEL), BF16), jax.ShapeDtypeStruct((N_TOK, LANES), F32),
                   jax.ShapeDtypeStruct((SEG_PAD, LANES), F32)],
        scratch_shapes=[pltpu.VMEM((1, LANES), F32)],
        compiler_params=_cparams("arbitrary"),
        name="router",
    )(x, mod, wr, tri)


def _experts_kernel(be_ref, nu_ref, rows_ref, wg_ref, wu_ref, wd_ref, o_ref, acc_ref):
    i = pl.program_id(0)
    f = pl.program_id(1)

    @pl.when(i < nu_ref[0])
    def _():
        @pl.when(f == 0)
        def _():
            acc_ref[...] = jnp.zeros_like(acc_ref)

        rows = rows_ref[...]
        act = (_silu(_dot(rows, wg_ref[0, 0].astype(BF16))) * _dot(rows, wu_ref[0, 0].astype(BF16))).astype(BF16)
        acc_ref[...] += _dot(act, wd_ref[0, 0].astype(BF16))

    @pl.when(f == pl.num_programs(1) - 1)
    def _():
        o_ref[...] = acc_ref[...].astype(BF16)


def _expert_ffn(rows, block_expert, n_used, wg, wu, wd, layer):
    n_rows = rows.shape[0]
    tm = TM_MOE
    tf = FF_CHUNK
    n_f = D_FF // tf

    def f_idx(i, f, nu):
        return jnp.where(i < nu[0], f, n_f - 1)

    grid_spec = pltpu.PrefetchScalarGridSpec(
        num_scalar_prefetch=2,
        grid=(n_rows // tm, n_f),
        in_specs=[pl.BlockSpec((tm, D_MODEL), lambda i, f, be, nu: (i, 0)),
                  pl.BlockSpec((1, 1, D_MODEL, tf), lambda i, f, be, nu: (layer, be[i], 0, f_idx(i, f, nu))),
                  pl.BlockSpec((1, 1, D_MODEL, tf), lambda i, f, be, nu: (layer, be[i], 0, f_idx(i, f, nu))),
                  pl.BlockSpec((1, 1, tf, D_MODEL), lambda i, f, be, nu: (layer, be[i], f_idx(i, f, nu), 0))],
        out_specs=pl.BlockSpec((tm, D_MODEL), lambda i, f, be, nu: (i, 0)),
        scratch_shapes=[pltpu.VMEM((tm, D_MODEL), F32)])
    return pl.pallas_call(
        _experts_kernel,
        grid_spec=grid_spec,
        out_shape=jax.ShapeDtypeStruct((n_rows, D_MODEL), BF16),
        compiler_params=_cparams("arbitrary", "arbitrary"),
        name="expert_ffn",
    )(block_expert, n_used, rows, wg, wu, wd)


def _combine_kernel(x_ref, mod_ref, meta_ref, a_ref, b_ref, o_ref):
    g1 = meta_ref[:, META_GATE:META_GATE + 1]
    g2 = meta_ref[:, META_GATE + 1:META_GATE + 2]
    y = a_ref[...].astype(F32) * g1 + b_ref[...].astype(F32) * g2
    o_ref[...] = x_ref[...] + mod_ref[0, 5:6, :] * y


def _moe_combine(x, mod, meta, a, b):
    tm = TM_FFN
    return pl.pallas_call(
        _combine_kernel,
        grid=(N_TOK // tm,),
        in_specs=[_row_spec(tm, D_MODEL), _mod_spec(tm), _row_spec(tm, LANES), _row_spec(tm, D_MODEL),
                  _row_spec(tm, D_MODEL)],
        out_specs=_row_spec(tm, D_MODEL),
        out_shape=jax.ShapeDtypeStruct((N_TOK, D_MODEL), F32),
        compiler_params=_cparams("parallel"),
        name="moe_combine",
    )(x, mod, meta, a, b)


def _moe(x, mod, wr, wg, wu, wd, layer):
    h, meta, counts = _router(x, mod, wr)
    tm = TM_MOE
    n_assign = N_TOK * TOP_K
    experts = meta[:, META_EXPERT:META_EXPERT + TOP_K].astype(jnp.int32)
    rank = meta[:, META_RANK:META_RANK + TOP_K].astype(jnp.int32)
    padded = (counts[0, :N_EXPERTS].astype(jnp.int32) + tm - 1) // tm * tm
    pad_end = jnp.cumsum(padded)
    pad_start = pad_end - padded
    onehot = experts[:, :, None] == jnp.arange(N_EXPERTS, dtype=jnp.int32)
    dest = jnp.sum(jnp.where(onehot, pad_start, 0), axis=-1) + rank
    n_rows = n_assign + N_EXPERTS * tm
    n_blocks = n_rows // tm
    row_token = jnp.zeros((n_rows,), jnp.int32).at[dest.reshape(-1)].set(
        jnp.arange(n_assign, dtype=jnp.int32) // TOP_K)
    n_used = (pad_end[-1] // tm).astype(jnp.int32).reshape(1)
    block_start = jnp.minimum(jnp.arange(n_blocks, dtype=jnp.int32), n_used[0] - 1) * tm
    block_expert = jnp.minimum(jnp.sum((pad_end[None, :] <= block_start[:, None]).astype(jnp.int32), axis=1),
                               N_EXPERTS - 1)
    out = _expert_ffn(h[row_token], block_expert, n_used, wg, wu, wd, layer)
    return _moe_combine(x, mod, meta, out[dest[:, 0]], out[dest[:, 1]])


def _rope_tables():
    half = ROPE_AXIS // 2
    pos = jnp.arange(DEC_SEQ)
    row = (pos // GRID_W).astype(F32)
    col = (pos % GRID_W).astype(F32)
    inv = ROPE_THETA ** (-jnp.arange(0, ROPE_AXIS, 2, dtype=F32) / ROPE_AXIS)
    ang_r = row[:, None] * inv
    ang_c = col[:, None] * inv
    z16 = jnp.zeros((DEC_SEQ, half), F32)
    pad = jnp.zeros((DEC_SEQ, LANES - QK_ROPE), F32)
    cos = jnp.concatenate([jnp.cos(ang_r), jnp.cos(ang_r), jnp.cos(ang_c), jnp.cos(ang_c), pad + 1.0], axis=1)
    sin_up = jnp.concatenate([z16, jnp.sin(ang_r), z16, jnp.sin(ang_c), pad], axis=1)
    sin_dn = jnp.concatenate([-jnp.sin(ang_r), z16, -jnp.sin(ang_c), z16, pad], axis=1)
    prompt = jnp.zeros((N_PROMPT, LANES), F32)
    tile = lambda t: jnp.tile(t, (DEC_BATCH, 1))
    return (jnp.concatenate([prompt + 1.0, tile(cos)], axis=0),
            jnp.concatenate([prompt, tile(sin_up)], axis=0),
            jnp.concatenate([prompt, tile(sin_dn)], axis=0))


def _pad_lanes(a, width):
    return jnp.pad(a, [(0, 0)] * (a.ndim - 1) + [(0, width - a.shape[-1])])


def _ssd_constants():
    r = jnp.arange(CHUNK)
    tri = (r[:, None] >= r[None, :]).astype(BF16)
    head_of = jnp.arange(D_SSD) // SSD_HEADDIM
    e_f = (r[:, None] == head_of[None, :]).astype(BF16)
    e_b = (r[:, None] == head_of[None, :] + SSD_HEADS).astype(BF16)
    return tri, e_f, e_b


def kernel(x_prompt, x_sample, c, state_ssm, cache_ckv, cache_kpe, c_ctx, w_mod, b_mod, w_in, conv_w, conv_b, dt_bias, a_log, d_skip, ssd_norm, sgu_norm, w_sp, b_sp, w_out, ffn_w_gate, ffn_w_up, ffn_w_down, w_dq, q_a_norm, w_uq, w_dkv, kv_a_norm, w_ukv, q_norm, k_norm, w_o, router, moe_w_gate, moe_w_up, moe_w_down):
    x = jnp.concatenate([x_prompt.reshape(N_PROMPT, D_MODEL), x_sample.reshape(N_SAMPLE, D_MODEL)], axis=0)
    cond = jnp.concatenate([c_ctx[None, :], c, jnp.zeros((SEG_PAD - N_SEG, D_MODEL), F32)], axis=0)
    mods = _modulation_tables(cond, w_mod, b_mod).reshape(DEPTH, SEG_PAD, N_MOD, D_MODEL)
    rope_tabs = _rope_tables()
    tri, e_f, e_b = _ssd_constants()
    i1 = D_SSD
    i2 = i1 + CONV_CH
    i3 = i2 + 2 * SSD_HEADS
    i4 = i3 + D_SGU
    new_ssm, new_ckv, new_kpe = [], [], []
    for i in range(DEPTH):
        j = i // 2
        mod = mods[i]
        if i % 2 == 0:
            w = w_in[j]
            wz, wx, wu, wv = (w[:, 0:i1].astype(BF16), w[:, i1:i2].astype(BF16),
                              w[:, i3:i4].astype(BF16), w[:, i4:].astype(BF16))
            wdt = _pad_lanes(w[:, i2:i3], LANES).astype(BF16)
            z, xbc, u, v, dt = _in_projection(x, mod, wz, wx, wu, wv, wdt)
            consts = (conv_w[j], conv_b[j][None, :], _pad_lanes(dt_bias[j].reshape(1, -1), LANES),
                      _pad_lanes(a_log[j].reshape(1, -1), LANES),
                      jnp.repeat(d_skip[j], SSD_HEADDIM)[None, :], tri, e_f, e_b)
            y, st_p = _ssd_scan(xbc, dt, None, consts, BATCH, SEQ, 0)
            init = state_ssm[:, j].reshape(DEC_BATCH, 2, D_SSD, SSD_STATE)
            y, _ = _ssd_scan(xbc, dt, init, consts, DEC_BATCH, DEC_SEQ, N_PROMPT // DEC_SEQ, y_prev=y)
            new_ssm.append(st_p.reshape(BATCH, 2, SSD_HEADS, SSD_HEADDIM, SSD_STATE))
            b_sp_e = jnp.repeat(b_sp[j].T, SGU_GDIM, axis=1)
            x = _mixer_output(x, mod, y, z, u, v, ssd_norm[j][None, :], sgu_norm[j][None, :],
                              w_sp[j].astype(BF16), b_sp_e, w_out[j].astype(BF16))
            x = _dense_ffn(x, mod, ffn_w_gate[j].astype(BF16), ffn_w_up[j].astype(BF16),
                           ffn_w_down[j].astype(BF16))
        else:
            wuq = _pad_lanes(w_uq[j].reshape(Q_RANK, MLA_HEADS, QK_DIM), HEAD_PAD).reshape(Q_RANK, -1)
            wdkv = _pad_lanes(w_dkv[j], KV_RANK + LANES)
            wukv = w_ukv[j].reshape(KV_RANK, MLA_HEADS, QK_NOPE + V_HEAD)
            wuk = wukv[:, :, :QK_NOPE].reshape(KV_RANK, -1).astype(BF16)
            wuv = wukv[:, :, QK_NOPE:].reshape(KV_RANK, -1).astype(BF16)
            qn = _pad_lanes(q_norm[j][None, :], HEAD_PAD)
            kn = _pad_lanes(k_norm[j][None, :], HEAD_PAD)
            q, k, v, ckv, kpe = _mla_projection(
                x, mod, rope_tabs, w_dq[j].astype(BF16), wdkv.astype(BF16), q_a_norm[j][None, :],
                kv_a_norm[j][None, :], wuq.astype(BF16), wuk, wuv, qn, kn)
            new_ckv.append(ckv[:N_PROMPT].reshape(BATCH, SEQ, KV_RANK))
            new_kpe.append(kpe[:N_PROMPT, :QK_ROPE].reshape(BATCH, SEQ, QK_ROPE))
            kc, vc = _cache_keys(cache_ckv[:, j].reshape(DEC_BATCH * PAST_LEN, KV_RANK),
                                 _pad_lanes(cache_kpe[:, j].reshape(DEC_BATCH * PAST_LEN, QK_ROPE), LANES),
                                 wuk, wuv, kn)
            o = _attention(q, k, v, None, None, None, BATCH, SEQ, 0)
            o = _attention(q, k, v, kc, vc, o, DEC_BATCH, DEC_SEQ, N_PROMPT // DEC_SEQ)
            x = _residual_projection(x, mod, o, w_o[j].astype(BF16), 2)
            x = _moe(x, mod, _pad_lanes(router[j], LANES), moe_w_gate, moe_w_up, moe_w_down, j)
    return (x[:N_PROMPT].reshape(BATCH, SEQ, D_MODEL),
            x[N_PROMPT:].reshape(DEC_BATCH, DEC_SEQ, D_MODEL),
            jnp.stack(new_ssm, axis=1),
            jnp.stack(new_ckv, axis=1),
            jnp.stack(new_kpe, axis=1))
```

```python
import functools
import math

import jax
import jax.numpy as jnp
from jax import lax
from jax.experimental import pallas as pl
from jax.experimental.pallas import tpu as pltpu

F32 = jnp.float32
BF16 = jnp.bfloat16

D_MODEL = 1024
BATCH = 16
SEQ = 256
DEPTH = 4
DEC_BATCH = 4
DEC_SEQ = 2048
PAST_LEN = 512
GRID_W = 64
N_MOD = 6
EPS = 1e-6

SSD_HEADDIM = 64
SSD_HEADS = 16
D_SSD = 1024
SSD_GROUPS = 4
SSD_STATE = 128
CHUNK = 128
CONV_W = 5
CONV_CH = D_SSD + 2 * SSD_GROUPS * SSD_STATE
D_SGU = 1024
SGU_GROUPS = 4
SGU_GDIM = D_SGU // SGU_GROUPS

MLA_HEADS = 8
Q_RANK = 384
KV_RANK = 256
QK_NOPE = 128
QK_ROPE = 64
V_HEAD = 128
QK_DIM = QK_NOPE + QK_ROPE
ROPE_AXIS = QK_ROPE // 2
ROPE_THETA = 10000.0
HEAD_PAD = 256

D_FF = 2816
N_EXPERTS = 8
TOP_K = 2

N_PROMPT = BATCH * SEQ
N_SAMPLE = DEC_BATCH * DEC_SEQ
N_TOK = N_PROMPT + N_SAMPLE
N_SEG = 1 + DEC_BATCH
SEG_PAD = 8

LANES = 128
VMEM_LIMIT = 56 * 1024 * 1024

TM_ROWS = 256
TM_FFN = 512
FF_CHUNK = 256
TM_MOE = 1024
TQ = 256


def _cparams(*sem):
    return pltpu.CompilerParams(dimension_semantics=sem, vmem_limit_bytes=VMEM_LIMIT)


def _dot(a, b):
    return jnp.dot(a, b, preferred_element_type=F32)


def _dot_nt(a, b):
    return lax.dot_general(a, b, (((1,), (1,)), ((), ())), preferred_element_type=F32)


def _split(x):
    hi = x.astype(BF16)
    lo = (x - hi.astype(F32)).astype(BF16)
    return hi, lo


def _dot3(a, b):
    ah, al = _split(a)
    bh, bl = _split(b)
    return _dot(ah, bh) + _dot(ah, bl) + _dot(al, bh)


def _silu(x):
    return x / (1.0 + jnp.exp(-x))


def _rms(x):
    return x * lax.rsqrt(jnp.mean(x * x, axis=-1, keepdims=True) + EPS)


def _modulated(x_ref, mod_ref, first):
    shift = mod_ref[0, first:first + 1, :]
    scale = mod_ref[0, first + 1:first + 2, :]
    return _rms(x_ref[...]) * (1.0 + scale) + shift


def _seg_map(tm):
    def index_map(i, *_):
        r = i * tm
        return (jnp.where(r < N_PROMPT, 0, 1 + (r - N_PROMPT) // DEC_SEQ), 0, 0)
    return index_map


def _row_spec(tm, width):
    return pl.BlockSpec((tm, width), lambda i, *_: (i, 0))


def _const_spec(shape):
    zeros = (0,) * len(shape)
    return pl.BlockSpec(shape, lambda i, *_: zeros)


def _mod_spec(tm):
    return pl.BlockSpec((1, N_MOD, D_MODEL), _seg_map(tm))


def _mod_kernel(c_ref, w_ref, b_ref, o_ref):
    o_ref[0] = _dot3(_silu(c_ref[...]), w_ref[0]) + b_ref[0]


def _modulation_tables(cond, w_mod, b_mod):
    tn = 1536
    return pl.pallas_call(
        _mod_kernel,
        grid=(DEPTH, N_MOD * D_MODEL // tn),
        in_specs=[pl.BlockSpec((SEG_PAD, D_MODEL), lambda l, j: (0, 0)),
                  pl.BlockSpec((1, D_MODEL, tn), lambda l, j: (l, 0, j)),
                  pl.BlockSpec((1, 1, tn), lambda l, j: (l, 0, j))],
        out_specs=pl.BlockSpec((1, SEG_PAD, tn), lambda l, j: (l, 0, j)),
        out_shape=jax.ShapeDtypeStruct((DEPTH, SEG_PAD, N_MOD * D_MODEL), F32),
        compiler_params=_cparams("parallel", "parallel"),
        name="modulation",
    )(cond, w_mod, b_mod.reshape(DEPTH, 1, N_MOD * D_MODEL))


def _inproj_kernel(x_ref, mod_ref, wz, wx, wu, wv, wdt, z_ref, xbc_ref, u_ref, v_ref, dt_ref):
    h = _modulated(x_ref, mod_ref, 0).astype(BF16)
    z_ref[...] = _dot(h, wz[...]).astype(BF16)
    xbc_ref[...] = _dot(h, wx[...]).astype(BF16)
    u_ref[...] = _dot(h, wu[...]).astype(BF16)
    v_ref[...] = _dot(h, wv[...]).astype(BF16)
    dt_ref[...] = _dot(h, wdt[...])


def _in_projection(x, mod, wz, wx, wu, wv, wdt):
    tm = TM_ROWS
    widths = (D_SSD, CONV_CH, D_SGU, D_SGU, LANES)
    dtypes = (BF16, BF16, BF16, BF16, F32)
    return pl.pallas_call(
        _inproj_kernel,
        grid=(N_TOK // tm,),
        in_specs=[_row_spec(tm, D_MODEL), _mod_spec(tm)] + [_const_spec((D_MODEL, w)) for w in widths],
        out_specs=[_row_spec(tm, w) for w in widths],
        out_shape=[jax.ShapeDtypeStruct((N_TOK, w), dt) for w, dt in zip(widths, dtypes)],
        compiler_params=_cparams("parallel"),
        name="in_projection",
    )(x, mod, wz, wx, wu, wv, wdt)


def _ssd_kernel(has_init, has_prev, n_chunks, *refs):
    refs = list(refs)
    xbc_ref, dt_ref = refs[0:2]
    del refs[0:2]
    init_ref = refs.pop(0) if has_init else None
    convw_ref, convb_ref, dtb_ref, alog_ref, dskip_ref, tri_ref, ef_ref, eb_ref = refs[0:8]
    del refs[0:8]
    if has_prev:
        refs.pop(0)
    y_ref, fin_ref, cv_ref, dtv_ref, cs_ref, st_ref = refs
    seq = n_chunks * CHUNK
    halo = 16
    lane = lax.broadcasted_iota(jnp.int32, (1, LANES), 1)
    a_neg = -jnp.exp(alog_ref[...])
    tri = tri_ref[...]
    rows_i = lax.broadcasted_iota(jnp.int32, (CHUNK, CHUNK), 0)
    cols_i = lax.broadcasted_iota(jnp.int32, (CHUNK, CHUNK), 1)
    lower = rows_i >= cols_i
    upper = cols_i >= rows_i
    even_head = jnp.bitwise_and(lax.broadcasted_iota(jnp.int32, (CHUNK, D_SSD), 1), LANES - 1) < SSD_HEADDIM

    def row0(c):
        return pl.multiple_of(c * CHUNK, CHUNK)

    def local_body(c, carry):
        r0 = row0(c)
        prev0 = pl.multiple_of(jnp.maximum(r0 - halo, 0), halo)
        next0 = pl.multiple_of(jnp.minimum(r0 + CHUNK, seq - halo), halo)
        keep_prev = jnp.where(c > 0, 1.0, 0.0)
        keep_next = jnp.where(c < n_chunks - 1, 1.0, 0.0)
        for cb in range(CONV_CH // 256):
            cols = slice(cb * 256, (cb + 1) * 256)
            win = jnp.concatenate([
                xbc_ref[pl.ds(prev0, halo), cols].astype(F32) * keep_prev,
                xbc_ref[pl.ds(r0, CHUNK), cols].astype(F32),
                xbc_ref[pl.ds(next0, halo), cols].astype(F32) * keep_next], axis=0)
            acc = jnp.zeros((CHUNK, 256), F32) + convb_ref[:, cols]
            for k in range(CONV_W):
                shift = (CONV_W // 2 - k) % (CHUNK + 2 * halo)
                rolled = win if shift == 0 else pltpu.roll(win, shift, 0)
                acc = acc + rolled[halo:halo + CHUNK, :] * convw_ref[k:k + 1, cols]
            cv_ref[pl.ds(r0, CHUNK), cols] = _silu(acc).astype(BF16)
        raw = dt_ref[pl.ds(r0, CHUNK), :] + dtb_ref[...]
        dt = jnp.where(lane < 2 * SSD_HEADS, jnp.maximum(raw, 0.0) + jnp.log(1.0 + jnp.exp(-jnp.abs(raw))), 0.0)
        ac = dt * a_neg
        hi = ac.astype(BF16)
        rest = ac - hi.astype(F32)
        mid = rest.astype(BF16)
        lo = (rest - mid.astype(F32)).astype(BF16)
        dtv_ref[pl.ds(r0, CHUNK), :] = dt
        cs_ref[pl.ds(r0, CHUNK), :] = _dot(tri, hi) + _dot(tri, mid) + _dot(tri, lo)
        return carry

    lax.fori_loop(0, n_chunks, local_body, 0)

    def expand(q, e_ref):
        hi, lo = _split(q)
        return _dot(hi, e_ref[...]) + _dot(lo, e_ref[...])

    def chunk_pass(c, forward):
        r0 = row0(c)
        rows = pl.ds(r0, CHUNK)
        e_ref = ef_ref if forward else eb_ref
        off = 0 if forward else SSD_HEADS
        dt = dtv_ref[rows, :]
        cs = cs_ref[rows, :]
        total = cs_ref[pl.ds(r0 + CHUNK - 1, 1), :]
        if forward:
            pos = cs
            to_edge = jnp.exp(total - cs)
            from_edge = jnp.exp(cs)
            mask = lower
        else:
            pos = cs - dt * a_neg
            to_edge = jnp.exp(pos)
            from_edge = jnp.exp(total - pos)
            mask = upper
        pos_t = pos.T
        x = cv_ref[rows, 0:D_SSD]
        dt_e = expand(dt, e_ref)
        xc = x.astype(F32) * dt_e
        xc_sub = (jnp.where(even_head, xc, 0.0).astype(BF16), jnp.where(even_head, 0.0, xc).astype(BF16))
        xd_b = (xc * expand(to_edge, e_ref)).astype(BF16)
        from_e = expand(from_edge, e_ref)
        tot_e = expand(jnp.exp(jnp.broadcast_to(total, (8, LANES))), e_ref)[0:1, :]
        st = st_ref[0 if forward else 1]
        st_b = st.astype(BF16)
        y_parts = []
        st_parts = []
        for g in range(SSD_GROUPS):
            bm = cv_ref[rows, D_SSD + g * SSD_STATE:D_SSD + (g + 1) * SSD_STATE]
            cm = cv_ref[rows, D_SSD + (SSD_GROUPS + g) * SSD_STATE:D_SSD + (SSD_GROUPS + g + 1) * SSD_STATE]
            bm_t = bm.astype(F32).T.astype(BF16)
            cb = _dot(cm, bm_t)
            gcols = slice(g * 256, (g + 1) * 256)
            y_off = _dot(cm, st_b[:, gcols])
            st_parts.append(_dot(bm_t, xd_b[:, gcols]))
            diag = []
            for pair in range(2):
                pcols = slice(g * 256 + pair * LANES, g * 256 + (pair + 1) * LANES)
                acc = None
                for sub in range(2):
                    h = g * 4 + pair * 2 + sub
                    col = pos[:, off + h:off + h + 1]
                    row = pos_t[off + h:off + h + 1, :]
                    diff = (col - row) if forward else (row - col)
                    m = (jnp.where(mask, jnp.exp(diff), 0.0) * cb).astype(BF16)
                    part = _dot(m, xc_sub[sub][:, pcols])
                    acc = part if acc is None else acc + part
                diag.append(acc)
            y_parts.append(jnp.concatenate(diag, axis=1) + y_off * from_e[:, gcols])
        y_new = jnp.concatenate(y_parts, axis=1)
        st_ref[0 if forward else 1] = st * tot_e + jnp.concatenate(st_parts, axis=1)
        if forward:
            y_ref[rows, :] = y_new + x.astype(F32) * dskip_ref[...]
        else:
            y_ref[rows, :] = y_ref[rows, :] + y_new

    if has_init:
        st_ref[0] = init_ref[0, 0].T
        st_ref[1] = init_ref[0, 1].T
    else:
        st_ref[...] = jnp.zeros_like(st_ref)

    def fwd_body(c, carry):
        chunk_pass(c, True)
        return carry

    def bwd_body(i, carry):
        chunk_pass(n_chunks - 1 - i, False)
        return carry

    lax.fori_loop(0, n_chunks, fwd_body, 0)
    lax.fori_loop(0, n_chunks, bwd_body, 0)
    fin_ref[0, 0] = st_ref[0].T
    fin_ref[0, 1] = st_ref[1].T


def _ssd_scan(xbc, dt, init, consts, n_seq, seq, row_block0, y_prev=None):
    n_chunks = seq // CHUNK
    has_init = init is not None
    has_prev = y_prev is not None
    once = pl.Buffered(1)
    seq_spec = lambda w: pl.BlockSpec((seq, w), lambda b: (row_block0 + b, 0), pipeline_mode=once)
    in_specs = [seq_spec(CONV_CH), seq_spec(LANES)]
    args = [xbc, dt]
    if has_init:
        in_specs.append(pl.BlockSpec((1, 2, D_SSD, SSD_STATE), lambda b: (b, 0, 0, 0), pipeline_mode=once))
        args.append(init)
    for cst in consts:
        in_specs.append(pl.BlockSpec(cst.shape, lambda b, nd=cst.ndim: (0,) * nd))
        args.append(cst)
    aliases = {}
    if has_prev:
        aliases[len(args)] = 0
        in_specs.append(pl.BlockSpec(memory_space=pl.ANY))
        args.append(y_prev)
    return pl.pallas_call(
        functools.partial(_ssd_kernel, has_init, has_prev, n_chunks),
        grid=(n_seq,),
        in_specs=in_specs,
        out_specs=[pl.BlockSpec((seq, D_SSD), lambda b: (row_block0 + b, 0)),
                   pl.BlockSpec((1, 2, D_SSD, SSD_STATE), lambda b: (b, 0, 0, 0))],
        out_shape=[jax.ShapeDtypeStruct((N_TOK, D_SSD), F32),
                   jax.ShapeDtypeStruct((n_seq, 2, D_SSD, SSD_STATE), F32)],
        input_output_aliases=aliases,
        scratch_shapes=[pltpu.VMEM((seq, CONV_CH), BF16),
                        pltpu.VMEM((seq, LANES), F32),
                        pltpu.VMEM((seq, LANES), F32),
                        pltpu.VMEM((2, SSD_STATE, D_SSD), F32)],
        compiler_params=_cparams("parallel"),
        name="ssd_scan_%d" % seq,
    )(*args)


def _mixer_out_kernel(x_ref, mod_ref, y_ref, z_ref, u_ref, v_ref, ssdn_ref, sgun_ref, wsp_ref, bsp_ref,
                      wo_ref, o_ref):
    tm = x_ref.shape[0]
    gated = y_ref[...] * _silu(z_ref[...].astype(F32))
    a = (_rms(gated) * ssdn_ref[...]).astype(BF16)
    vb = (_rms(v_ref[...].astype(F32)) * sgun_ref[...]).astype(BF16)
    chunks = []
    for k in range(tm // CHUNK):
        rows = slice(k * CHUNK, (k + 1) * CHUNK)
        groups = [_dot(wsp_ref[g], vb[rows, g * SGU_GDIM:(g + 1) * SGU_GDIM]) for g in range(SGU_GROUPS)]
        chunks.append(jnp.concatenate(groups, axis=1) + bsp_ref[...])
    s = (u_ref[...].astype(F32) * jnp.concatenate(chunks, axis=0)).astype(BF16)
    out = _dot(a, wo_ref[0:D_SSD, :]) + _dot(s, wo_ref[D_SSD:D_SSD + D_SGU, :])
    o_ref[...] = x_ref[...] + mod_ref[0, 2:3, :] * out


def _mixer_output(x, mod, y, z, u, v, ssd_norm, sgu_norm, w_sp, b_sp_e, w_out):
    tm = TM_ROWS
    return pl.pallas_call(
        _mixer_out_kernel,
        grid=(N_TOK // tm,),
        in_specs=[_row_spec(tm, D_MODEL), _mod_spec(tm), _row_spec(tm, D_SSD), _row_spec(tm, D_SSD),
                  _row_spec(tm, D_SGU), _row_spec(tm, D_SGU), _const_spec((1, D_SSD)), _const_spec((1, D_SGU)),
                  _const_spec((SGU_GROUPS, CHUNK, CHUNK)), _const_spec((CHUNK, D_SGU)),
                  _const_spec((D_SSD + D_SGU, D_MODEL))],
        out_specs=_row_spec(tm, D_MODEL),
        out_shape=jax.ShapeDtypeStruct((N_TOK, D_MODEL), F32),
        compiler_params=_cparams("parallel"),
        name="mixer_output",
    )(x, mod, y, z, u, v, ssd_norm, sgu_norm, w_sp, b_sp_e, w_out)


def _ffn_kernel(x_ref, mod_ref, wg_ref, wu_ref, wd_ref, o_ref):
    h = _modulated(x_ref, mod_ref, 3).astype(BF16)
    acc = jnp.zeros(o_ref.shape, F32)
    for f in range(D_FF // FF_CHUNK):
        cols = slice(f * FF_CHUNK, (f + 1) * FF_CHUNK)
        act = (_silu(_dot(h, wg_ref[:, cols])) * _dot(h, wu_ref[:, cols])).astype(BF16)
        acc = acc + _dot(act, wd_ref[cols, :])
    o_ref[...] = x_ref[...] + mod_ref[0, 5:6, :] * acc


def _dense_ffn(x, mod, wg, wu, wd):
    tm = TM_FFN
    once = pl.Buffered(1)
    return pl.pallas_call(
        _ffn_kernel,
        grid=(N_TOK // tm,),
        in_specs=[_row_spec(tm, D_MODEL), _mod_spec(tm),
                  pl.BlockSpec((D_MODEL, D_FF), lambda i: (0, 0), pipeline_mode=once),
                  pl.BlockSpec((D_MODEL, D_FF), lambda i: (0, 0), pipeline_mode=once),
                  pl.BlockSpec((D_FF, D_MODEL), lambda i: (0, 0), pipeline_mode=once)],
        out_specs=_row_spec(tm, D_MODEL),
        out_shape=jax.ShapeDtypeStruct((N_TOK, D_MODEL), F32),
        compiler_params=_cparams("parallel"),
        name="dense_ffn",
    )(x, mod, wg, wu, wd)


HEAD_PROJ = 3 * LANES


def _pair_sums(sq_a, sq_b, ones_ref):
    return _dot(jnp.concatenate([sq_a, sq_b], axis=1).astype(BF16), ones_ref[...])


def _expand_keys(ckv_b, pe_sq, pe_rot, wuk_ref, wuv_ref, kn_ref, ones_ref, k_ref, v_ref):
    kn_nope = kn_ref[:, 0:QK_NOPE]
    v_ref[...] = _dot(ckv_b, wuv_ref[...]).astype(BF16)
    k_nope = _dot(ckv_b, wuk_ref[...])
    for pair in range(MLA_HEADS // 2):
        kh = [k_nope[:, h * QK_NOPE:(h + 1) * QK_NOPE] for h in (2 * pair, 2 * pair + 1)]
        ss = _pair_sums(kh[0] * kh[0] + pe_sq, kh[1] * kh[1] + pe_sq, ones_ref)
        r = lax.rsqrt(ss * (1.0 / QK_DIM) + EPS)
        for idx in range(2):
            h = 2 * pair + idx
            rh = r[:, idx * LANES:(idx + 1) * LANES]
            k_ref[:, h * HEAD_PAD:h * HEAD_PAD + QK_NOPE] = (kh[idx] * rh * kn_nope).astype(BF16)
            k_ref[:, h * HEAD_PAD + QK_NOPE:(h + 1) * HEAD_PAD] = (pe_rot * rh).astype(BF16)


def _mla_proj_kernel(x_ref, mod_ref, cos_ref, sin_ref, wdq_ref, wdkv_ref, qan_ref, kvan_ref,
                     wuq_ref, wuk_ref, wuv_ref, qn_ref, kn_ref, ones_ref, q_ref, k_ref, v_ref, ckv_ref, kpe_ref):
    h = _modulated(x_ref, mod_ref, 0).astype(BF16)
    cos, sin = cos_ref[...], sin_ref[...]
    qa = (_rms(_dot(h, wdq_ref[...])) * qan_ref[...]).astype(BF16)
    q = _dot(qa, wuq_ref[...])
    qn_nope = qn_ref[:, 0:LANES]
    qn_rope = qn_ref[:, LANES:2 * LANES] * cos
    qn_part = qn_ref[:, 2 * LANES:3 * LANES] * sin
    scale = QK_DIM ** -0.5
    for pair in range(MLA_HEADS // 2):
        blocks = []
        for hd in (2 * pair, 2 * pair + 1):
            base = hd * HEAD_PROJ
            blocks.append((q[:, base:base + LANES], q[:, base + LANES:base + 2 * LANES],
                           q[:, base + 2 * LANES:base + 3 * LANES]))
        ss = _pair_sums(*[qh * qh + qr * qr for qh, qr, _ in blocks], ones_ref)
        r = lax.rsqrt(ss * (1.0 / QK_DIM) + EPS) * scale
        for idx, (qh, qr, qp) in enumerate(blocks):
            hd = 2 * pair + idx
            rh = r[:, idx * LANES:(idx + 1) * LANES]
            q_ref[:, hd * HEAD_PAD:hd * HEAD_PAD + QK_NOPE] = (qh * rh * qn_nope).astype(BF16)
            q_ref[:, hd * HEAD_PAD + QK_NOPE:(hd + 1) * HEAD_PAD] = ((qr * qn_rope + qp * qn_part) * rh).astype(BF16)
    kva = _dot(h, wdkv_ref[...])
    ckv = _rms(kva[:, 0:KV_RANK]) * kvan_ref[...]
    kpe = kva[:, KV_RANK:KV_RANK + LANES]
    kpe_part = kva[:, KV_RANK + LANES:KV_RANK + 2 * LANES]
    ckv_ref[...] = ckv
    kpe_ref[...] = kpe
    pe_rot = kpe * (kn_ref[:, LANES:2 * LANES] * cos) + kpe_part * (kn_ref[:, 2 * LANES:3 * LANES] * sin)
    _expand_keys(ckv.astype(BF16), kpe * kpe, pe_rot, wuk_ref, wuv_ref, kn_ref, ones_ref, k_ref, v_ref)


def _mla_projection(x, mod, rope_tabs, wdq, wdkv, qan, kvan, wuq, wuk, wuv, qn, kn, ones):
    tm = TM_ROWS
    hw = MLA_HEADS * HEAD_PAD
    out_w = (hw, hw, MLA_HEADS * V_HEAD, KV_RANK, LANES)
    out_dt = (BF16, BF16, BF16, F32, F32)
    consts = (wdq, wdkv, qan, kvan, wuq, wuk, wuv, qn, kn, ones)
    return pl.pallas_call(
        _mla_proj_kernel,
        grid=(N_TOK // tm,),
        in_specs=[_row_spec(tm, D_MODEL), _mod_spec(tm)] + [_row_spec(tm, LANES)] * 2
                 + [_const_spec(cst.shape) for cst in consts],
        out_specs=[_row_spec(tm, w) for w in out_w],
        out_shape=[jax.ShapeDtypeStruct((N_TOK, w), dt) for w, dt in zip(out_w, out_dt)],
        compiler_params=_cparams("parallel"),
        name="mla_projection",
    )(x, mod, *rope_tabs, *consts)


def _cache_kv_kernel(ckv_ref, kpe_ref, wuk_ref, wuv_ref, kn_ref, ones_ref, k_ref, v_ref):
    kpe = kpe_ref[...]
    _expand_keys(ckv_ref[...].astype(BF16), kpe * kpe, kpe * kn_ref[:, LANES:2 * LANES],
                 wuk_ref, wuv_ref, kn_ref, ones_ref, k_ref, v_ref)


def _cache_keys(ckv, kpe, wuk, wuv, kn, ones):
    rows = ckv.shape[0]
    tm = TM_ROWS
    consts = (wuk, wuv, kn, ones)
    return pl.pallas_call(
        _cache_kv_kernel,
        grid=(rows // tm,),
        in_specs=[_row_spec(tm, KV_RANK), _row_spec(tm, LANES)] + [_const_spec(cst.shape) for cst in consts],
        out_specs=[_row_spec(tm, MLA_HEADS * HEAD_PAD), _row_spec(tm, MLA_HEADS * V_HEAD)],
        out_shape=[jax.ShapeDtypeStruct((rows, MLA_HEADS * HEAD_PAD), BF16),
                   jax.ShapeDtypeStruct((rows, MLA_HEADS * V_HEAD), BF16)],
        compiler_params=_cparams("parallel"),
        name="cache_keys",
    )(ckv, kpe, *consts)


def _attn_kernel(has_cache, tq, *refs):
    if has_cache:
        q_ref, k_ref, v_ref, kc_ref, vc_ref, _, o_ref = refs
    else:
        q_ref, k_ref, v_ref, o_ref = refs

    def tile(r0):
        q = q_ref[pl.ds(r0, tq), :]
        s = _dot_nt(q, k_ref[...])
        m = jnp.max(s, axis=-1, keepdims=True)
        if has_cache:
            sc = _dot_nt(q, kc_ref[...])
            m = jnp.maximum(m, jnp.max(sc, axis=-1, keepdims=True))
            pc = jnp.exp(sc - m)
        p = jnp.exp(s - m)
        den = jnp.sum(p, axis=-1, keepdims=True)
        num = _dot(p.astype(BF16), v_ref[...])
        if has_cache:
            den = den + jnp.sum(pc, axis=-1, keepdims=True)
            num = num + _dot(pc.astype(BF16), vc_ref[...])
        o_ref[pl.ds(r0, tq), :] = (num / den).astype(BF16)

    n_pairs = q_ref.shape[0] // (2 * tq)
    if n_pairs == 1:
        tile(0)
        tile(tq)
    else:
        def body(i, carry):
            r0 = pl.multiple_of(i * (2 * tq), 2 * tq)
            tile(r0)
            tile(r0 + tq)
            return carry
        lax.fori_loop(0, n_pairs, body, 0)


def _attention(q, k, v, kc, vc, o_prev, n_seq, seq, row_block0):
    has_cache = kc is not None
    tq = min(TQ, seq // 2)
    seq_spec = lambda w: pl.BlockSpec((seq, w), lambda b, h: (row_block0 + b, h))
    in_specs = [seq_spec(HEAD_PAD), seq_spec(HEAD_PAD), seq_spec(V_HEAD)]
    args = [q, k, v]
    if has_cache:
        past = kc.shape[0] // n_seq
        in_specs += [pl.BlockSpec((past, HEAD_PAD), lambda b, h: (b, h)),
                     pl.BlockSpec((past, V_HEAD), lambda b, h: (b, h)),
                     pl.BlockSpec(memory_space=pl.ANY)]
        args += [kc, vc, o_prev]
    return pl.pallas_call(
        functools.partial(_attn_kernel, has_cache, tq),
        grid=(n_seq, MLA_HEADS),
        in_specs=in_specs,
        out_specs=seq_spec(V_HEAD),
        out_shape=jax.ShapeDtypeStruct((N_TOK, MLA_HEADS * V_HEAD), BF16),
        input_output_aliases={5: 0} if has_cache else {},
        compiler_params=_cparams("parallel", "parallel"),
        name="attention_%d" % seq,
    )(*args)


def _residual_proj_kernel(gate_row, x_ref, mod_ref, a_ref, w_ref, o_ref):
    o_ref[...] = x_ref[...] + mod_ref[0, gate_row:gate_row + 1, :] * _dot(a_ref[...], w_ref[...])


def _residual_projection(x, mod, a, w, gate_row):
    tm = TM_FFN
    k = a.shape[1]
    return pl.pallas_call(
        functools.partial(_residual_proj_kernel, gate_row),
        grid=(N_TOK // tm,),
        in_specs=[_row_spec(tm, D_MODEL), _mod_spec(tm), _row_spec(tm, k), _const_spec((k, D_MODEL))],
        out_specs=_row_spec(tm, D_MODEL),
        out_shape=jax.ShapeDtypeStruct((N_TOK, D_MODEL), F32),
        compiler_params=_cparams("parallel"),
        name="residual_projection",
    )(x, mod, a, w)


def _router_kernel(x_ref, mod_ref, wr_ref, tri_ref, h_ref, meta_ref, cnt_ref, run_ref):
    @pl.when(pl.program_id(0) == 0)
    def _():
        run_ref[...] = jnp.zeros_like(run_ref)

    h = _modulated(x_ref, mod_ref, 3)
    h_ref[...] = h.astype(BF16)
    lane = lax.broadcasted_iota(jnp.int32, (x_ref.shape[0], LANES), 1)
    lane_f = lane.astype(F32)
    logits = jnp.where(lane < N_EXPERTS, _dot3(h, wr_ref[...]), -jnp.inf)
    m1 = jnp.max(logits, axis=-1, keepdims=True)
    e1 = jnp.min(jnp.where(logits == m1, lane_f, float(LANES)), axis=-1, keepdims=True)
    rest = jnp.where(lane_f == e1, -jnp.inf, logits)
    m2 = jnp.max(rest, axis=-1, keepdims=True)
    e2 = jnp.min(jnp.where(rest == m2, lane_f, float(LANES)), axis=-1, keepdims=True)
    t = jnp.exp(m2 - m1)
    g1 = 1.0 / (1.0 + t)
    pick1 = lane_f == e1
    pick2 = lane_f == e2
    onehot = jnp.where(pick1, 1.0, jnp.where(pick2, 1.0, 0.0))
    before = run_ref[...] + _dot(tri_ref[...], onehot.astype(BF16)) - onehot
    rank1 = jnp.sum(jnp.where(pick1, before, 0.0), axis=-1, keepdims=True)
    rank2 = jnp.sum(jnp.where(pick2, before, 0.0), axis=-1, keepdims=True)
    run_ref[...] += jnp.sum(onehot, axis=0, keepdims=True)
    cnt_ref[...] = jnp.broadcast_to(run_ref[...], cnt_ref.shape)
    cols = (e1, e2, g1, 1.0 - g1, rank1, rank2)
    meta = jnp.zeros(meta_ref.shape, F32)
    for idx, val in enumerate(cols):
        meta = jnp.where(lane == idx, val, meta)
    meta_ref[...] = meta


META_EXPERT, META_GATE, META_RANK = 0, 2, 4


def _router(x, mod, wr):
    tm = TM_ROWS
    r = jnp.arange(tm)
    tri = (r[:, None] >= r[None, :]).astype(BF16)
    return pl.pallas_call(
        _router_kernel,
        grid=(N_TOK // tm,),
        in_specs=[_row_spec(tm, D_MODEL), _mod_spec(tm), _const_spec((D_MODEL, LANES)), _const_spec((tm, tm))],
        out_specs=[_row_spec(tm, D_MODEL), _row_spec(tm, LANES), _const_spec((SEG_PAD, LANES))],
        out_shape=[jax.ShapeDtypeStruct((N_TOK, D_MODEL), BF16), jax.ShapeDtypeStruct((N_TOK, LANES), F32),
                   jax.ShapeDtypeStruct((SEG_PAD, LANES), F32)],
        scratch_shapes=[pltpu.VMEM((1, LANES), F32)],
        compiler_params=_cparams("arbitrary"),
        name="router",
    )(x, mod, wr, tri)


def _experts_kernel(be_ref, nu_ref, rows_ref, wg_ref, wu_ref, wd_ref, o_ref, acc_ref):
    i = pl.program_id(0)
    f = pl.program_id(1)

    @pl.when(i < nu_ref[0])
    def _():
        @pl.when(f == 0)
        def _():
            acc_ref[...] = jnp.zeros_like(acc_ref)

        rows = rows_ref[...]
        act = (_silu(_dot(rows, wg_ref[0, 0].astype(BF16))) * _dot(rows, wu_ref[0, 0].astype(BF16))).astype(BF16)
        acc_ref[...] += _dot(act, wd_ref[0, 0].astype(BF16))

    @pl.when(f == pl.num_programs(1) - 1)
    def _():
        o_ref[...] = acc_ref[...].astype(BF16)


def _expert_ffn(rows, block_expert, n_used, wg, wu, wd, layer):
    n_rows = rows.shape[0]
    tm = TM_MOE
    tf = FF_CHUNK
    n_f = D_FF // tf

    def f_idx(i, f, nu):
        return jnp.where(i < nu[0], f, n_f - 1)

    grid_spec = pltpu.PrefetchScalarGridSpec(
        num_scalar_prefetch=2,
        grid=(n_rows // tm, n_f),
        in_specs=[pl.BlockSpec((tm, D_MODEL), lambda i, f, be, nu: (i, 0)),
                  pl.BlockSpec((1, 1, D_MODEL, tf), lambda i, f, be, nu: (layer, be[i], 0, f_idx(i, f, nu))),
                  pl.BlockSpec((1, 1, D_MODEL, tf), lambda i, f, be, nu: (layer, be[i], 0, f_idx(i, f, nu))),
                  pl.BlockSpec((1, 1, tf, D_MODEL), lambda i, f, be, nu: (layer, be[i], f_idx(i, f, nu), 0))],
        out_specs=pl.BlockSpec((tm, D_MODEL), lambda i, f, be, nu: (i, 0)),
        scratch_shapes=[pltpu.VMEM((tm, D_MODEL), F32)])
    return pl.pallas_call(
        _experts_kernel,
        grid_spec=grid_spec,
        out_shape=jax.ShapeDtypeStruct((n_rows, D_MODEL), BF16),
        compiler_params=_cparams("arbitrary", "arbitrary"),
        name="expert_ffn",
    )(block_expert, n_used, rows, wg, wu, wd)


def _combine_kernel(x_ref, mod_ref, meta_ref, a_ref, b_ref, o_ref):
    g1 = meta_ref[:, META_GATE:META_GATE + 1]
    g2 = meta_ref[:, META_GATE + 1:META_GATE + 2]
    y = a_ref[...].astype(F32) * g1 + b_ref[...].astype(F32) * g2
    o_ref[...] = x_ref[...] + mod_ref[0, 5:6, :] * y


def _moe_combine(x, mod, meta, a, b):
    tm = TM_FFN
    return pl.pallas_call(
        _combine_kernel,
        grid=(N_TOK // tm,),
        in_specs=[_row_spec(tm, D_MODEL), _mod_spec(tm), _row_spec(tm, LANES), _row_spec(tm, D_MODEL),
                  _row_spec(tm, D_MODEL)],
        out_specs=_row_spec(tm, D_MODEL),
        out_shape=jax.ShapeDtypeStruct((N_TOK, D_MODEL), F32),
        compiler_params=_cparams("parallel"),
        name="moe_combine",
    )(x, mod, meta, a, b)


def _moe(x, mod, wr, wg, wu, wd, layer):
    h, meta, counts = _router(x, mod, wr)
    tm = TM_MOE
    n_assign = N_TOK * TOP_K
    experts = meta[:, META_EXPERT:META_EXPERT + TOP_K].astype(jnp.int32)
    rank = meta[:, META_RANK:META_RANK + TOP_K].astype(jnp.int32)
    padded = (counts[0, :N_EXPERTS].astype(jnp.int32) + tm - 1) // tm * tm
    pad_end = jnp.cumsum(padded)
    pad_start = pad_end - padded
    onehot = experts[:, :, None] == jnp.arange(N_EXPERTS, dtype=jnp.int32)
    dest = jnp.sum(jnp.where(onehot, pad_start, 0), axis=-1) + rank
    n_rows = n_assign + N_EXPERTS * tm
    n_blocks = n_rows // tm
    row_token = jnp.zeros((n_rows,), jnp.int32).at[dest.reshape(-1)].set(
        jnp.arange(n_assign, dtype=jnp.int32) // TOP_K)
    n_used = (pad_end[-1] // tm).astype(jnp.int32).reshape(1)
    block_start = jnp.minimum(jnp.arange(n_blocks, dtype=jnp.int32), n_used[0] - 1) * tm
    block_expert = jnp.minimum(jnp.sum((pad_end[None, :] <= block_start[:, None]).astype(jnp.int32), axis=1),
                               N_EXPERTS - 1)
    out = _expert_ffn(h[row_token], block_expert, n_used, wg, wu, wd, layer)
    return _moe_combine(x, mod, meta, out[dest[:, 0]], out[dest[:, 1]])


def _rope_tables():
    half = ROPE_AXIS // 2
    pos = jnp.arange(DEC_SEQ)
    row = (pos // GRID_W).astype(F32)
    col = (pos % GRID_W).astype(F32)
    inv = ROPE_THETA ** (-jnp.arange(0, ROPE_AXIS, 2, dtype=F32) / ROPE_AXIS)
    ang_r = row[:, None] * inv
    ang_c = col[:, None] * inv
    pad = jnp.zeros((DEC_SEQ, LANES - QK_ROPE), F32)
    cos = jnp.concatenate([jnp.cos(ang_r), jnp.cos(ang_r), jnp.cos(ang_c), jnp.cos(ang_c), pad + 1.0], axis=1)
    sin = jnp.concatenate([-jnp.sin(ang_r), jnp.sin(ang_r), -jnp.sin(ang_c), jnp.sin(ang_c), pad], axis=1)
    prompt = jnp.zeros((N_PROMPT, LANES), F32)
    tile = lambda t: jnp.tile(t, (DEC_BATCH, 1))
    return (jnp.concatenate([prompt + 1.0, tile(cos)], axis=0),
            jnp.concatenate([prompt, tile(sin)], axis=0))


def _rope_blocks(a):
    half = ROPE_AXIS // 2
    partner = jnp.concatenate([a[..., half:2 * half], a[..., 0:half], a[..., 3 * half:4 * half],
                               a[..., 2 * half:3 * half]], axis=-1)
    return jnp.concatenate([_pad_lanes(a, LANES), _pad_lanes(partner, LANES)], axis=-1)


def _pad_lanes(a, width):
    return jnp.pad(a, [(0, 0)] * (a.ndim - 1) + [(0, width - a.shape[-1])])


def _ssd_constants():
    r = jnp.arange(CHUNK)
    tri = (r[:, None] >= r[None, :]).astype(BF16)
    head_of = jnp.arange(D_SSD) // SSD_HEADDIM
    e_f = (r[:, None] == head_of[None, :]).astype(BF16)
    e_b = (r[:, None] == head_of[None, :] + SSD_HEADS).astype(BF16)
    return tri, e_f, e_b


def kernel(x_prompt, x_sample, c, state_ssm, cache_ckv, cache_kpe, c_ctx, w_mod, b_mod, w_in, conv_w, conv_b, dt_bias, a_log, d_skip, ssd_norm, sgu_norm, w_sp, b_sp, w_out, ffn_w_gate, ffn_w_up, ffn_w_down, w_dq, q_a_norm, w_uq, w_dkv, kv_a_norm, w_ukv, q_norm, k_norm, w_o, router, moe_w_gate, moe_w_up, moe_w_down):
    x = jnp.concatenate([x_prompt.reshape(N_PROMPT, D_MODEL), x_sample.reshape(N_SAMPLE, D_MODEL)], axis=0)
    cond = jnp.concatenate([c_ctx[None, :], c, jnp.zeros((SEG_PAD - N_SEG, D_MODEL), F32)], axis=0)
    mods = _modulation_tables(cond, w_mod, b_mod).reshape(DEPTH, SEG_PAD, N_MOD, D_MODEL)
    rope_tabs = _rope_tables()
    tri, e_f, e_b = _ssd_constants()
    blk = jnp.arange(2 * LANES) // LANES
    pair_ones = (blk[:, None] == blk[None, :]).astype(BF16)
    i1 = D_SSD
    i2 = i1 + CONV_CH
    i3 = i2 + 2 * SSD_HEADS
    i4 = i3 + D_SGU
    new_ssm, new_ckv, new_kpe = [], [], []
    for i in range(DEPTH):
        j = i // 2
        mod = mods[i]
        if i % 2 == 0:
            w = w_in[j]
            wz, wx, wu, wv = (w[:, 0:i1].astype(BF16), w[:, i1:i2].astype(BF16),
                              w[:, i3:i4].astype(BF16), w[:, i4:].astype(BF16))
            wdt = _pad_lanes(w[:, i2:i3], LANES).astype(BF16)
            z, xbc, u, v, dt = _in_projection(x, mod, wz, wx, wu, wv, wdt)
            consts = (conv_w[j], conv_b[j][None, :], _pad_lanes(dt_bias[j].reshape(1, -1), LANES),
                      _pad_lanes(a_log[j].reshape(1, -1), LANES),
                      jnp.repeat(d_skip[j], SSD_HEADDIM)[None, :], tri, e_f, e_b)
            y, st_p = _ssd_scan(xbc, dt, None, consts, BATCH, SEQ, 0)
            init = state_ssm[:, j].reshape(DEC_BATCH, 2, D_SSD, SSD_STATE)
            y, _ = _ssd_scan(xbc, dt, init, consts, DEC_BATCH, DEC_SEQ, N_PROMPT // DEC_SEQ, y_prev=y)
            new_ssm.append(st_p.reshape(BATCH, 2, SSD_HEADS, SSD_HEADDIM, SSD_STATE))
            b_sp_e = jnp.repeat(b_sp[j].T, SGU_GDIM, axis=1)
            x = _mixer_output(x, mod, y, z, u, v, ssd_norm[j][None, :], sgu_norm[j][None, :],
                              w_sp[j].astype(BF16), b_sp_e, w_out[j].astype(BF16))
            x = _dense_ffn(x, mod, ffn_w_gate[j].astype(BF16), ffn_w_up[j].astype(BF16),
                           ffn_w_down[j].astype(BF16))
        else:
            split = lambda a: jnp.concatenate([a[..., :QK_NOPE], _rope_blocks(a[..., QK_NOPE:])], axis=-1)
            wuq = split(w_uq[j].reshape(Q_RANK, MLA_HEADS, QK_DIM)).reshape(Q_RANK, MLA_HEADS * HEAD_PROJ)
            wdkv = jnp.concatenate([w_dkv[j][:, :KV_RANK], _rope_blocks(w_dkv[j][:, KV_RANK:])], axis=-1)
            wukv = w_ukv[j].reshape(KV_RANK, MLA_HEADS, QK_NOPE + V_HEAD)
            wuk = wukv[:, :, :QK_NOPE].reshape(KV_RANK, -1).astype(BF16)
            wuv = wukv[:, :, QK_NOPE:].reshape(KV_RANK, -1).astype(BF16)
            qn = split(q_norm[j][None, :])
            kn = split(k_norm[j][None, :])
            q, k, v, ckv, kpe = _mla_projection(
                x, mod, rope_tabs, w_dq[j].astype(BF16), wdkv.astype(BF16), q_a_norm[j][None, :],
                kv_a_norm[j][None, :], wuq.astype(BF16), wuk, wuv, qn, kn, pair_ones)
            new_ckv.append(ckv[:N_PROMPT].reshape(BATCH, SEQ, KV_RANK))
            new_kpe.append(kpe[:N_PROMPT, :QK_ROPE].reshape(BATCH, SEQ, QK_ROPE))
            kc, vc = _cache_keys(cache_ckv[:, j].reshape(DEC_BATCH * PAST_LEN, KV_RANK),
                                 _pad_lanes(cache_kpe[:, j].reshape(DEC_BATCH * PAST_LEN, QK_ROPE), LANES),
                                 wuk, wuv, kn, pair_ones)
            o = _attention(q, k, v, None, None, None, BATCH, SEQ, 0)
            o = _attention(q, k, v, kc, vc, o, DEC_BATCH, DEC_SEQ, N_PROMPT // DEC_SEQ)
            x = _residual_projection(x, mod, o, w_o[j].astype(BF16), 2)
            x = _moe(x, mod, _pad_lanes(router[j], LANES), moe_w_gate, moe_w_up, moe_w_down, j)
    return (x[:N_PROMPT].reshape(BATCH, SEQ, D_MODEL),
            x[N_PROMPT:].reshape(DEC_BATCH, DEC_SEQ, D_MODEL),
            jnp.stack(new_ssm, axis=1),
            jnp.stack(new_ckv, axis=1),
            jnp.stack(new_kpe, axis=1))
```

```python
import functools
import math

import jax
import jax.numpy as jnp
from jax import lax
from jax.experimental import pallas as pl
from jax.experimental.pallas import tpu as pltpu

F32 = jnp.float32
BF16 = jnp.bfloat16

D_MODEL = 1024
BATCH = 16
SEQ = 256
DEPTH = 4
DEC_BATCH = 4
DEC_SEQ = 2048
PAST_LEN = 512
GRID_W = 64
N_MOD = 6
EPS = 1e-6

SSD_HEADDIM = 64
SSD_HEADS = 16
D_SSD = 1024
SSD_GROUPS = 4
SSD_STATE = 128
CHUNK = 128
CONV_W = 5
CONV_CH = D_SSD + 2 * SSD_GROUPS * SSD_STATE
D_SGU = 1024
SGU_GROUPS = 4
SGU_GDIM = D_SGU // SGU_GROUPS

MLA_HEADS = 8
Q_RANK = 384
KV_RANK = 256
QK_NOPE = 128
QK_ROPE = 64
V_HEAD = 128
QK_DIM = QK_NOPE + QK_ROPE
ROPE_AXIS = QK_ROPE // 2
ROPE_THETA = 10000.0
HEAD_PAD = 256

D_FF = 2816
N_EXPERTS = 8
TOP_K = 2

N_PROMPT = BATCH * SEQ
N_SAMPLE = DEC_BATCH * DEC_SEQ
N_TOK = N_PROMPT + N_SAMPLE
N_SEG = 1 + DEC_BATCH
SEG_PAD = 8

LANES = 128
VMEM_LIMIT = 56 * 1024 * 1024

TM_ROWS = 256
TM_FFN = 512
FF_CHUNK = 256
TM_MOE = 1024
TQ = 256


def _cparams(*sem):
    return pltpu.CompilerParams(dimension_semantics=sem, vmem_limit_bytes=VMEM_LIMIT)


def _dot(a, b):
    return jnp.dot(a, b, preferred_element_type=F32)


def _dot_nt(a, b):
    return lax.dot_general(a, b, (((1,), (1,)), ((), ())), preferred_element_type=F32)


def _split(x):
    hi = x.astype(BF16)
    lo = (x - hi.astype(F32)).astype(BF16)
    return hi, lo


def _dot3(a, b):
    ah, al = _split(a)
    bh, bl = _split(b)
    return _dot(ah, bh) + _dot(ah, bl) + _dot(al, bh)


def _silu(x):
    return x / (1.0 + jnp.exp(-x))


def _rms(x):
    return x * lax.rsqrt(jnp.mean(x * x, axis=-1, keepdims=True) + EPS)


def _modulated(x_ref, mod_ref, first):
    shift = mod_ref[0, first:first + 1, :]
    scale = mod_ref[0, first + 1:first + 2, :]
    return _rms(x_ref[...]) * (1.0 + scale) + shift


def _seg_map(tm):
    def index_map(i, *_):
        r = i * tm
        return (jnp.where(r < N_PROMPT, 0, 1 + (r - N_PROMPT) // DEC_SEQ), 0, 0)
    return index_map


def _row_spec(tm, width):
    return pl.BlockSpec((tm, width), lambda i, *_: (i, 0))


def _const_spec(shape):
    zeros = (0,) * len(shape)
    return pl.BlockSpec(shape, lambda i, *_: zeros)


def _mod_spec(tm):
    return pl.BlockSpec((1, N_MOD, D_MODEL), _seg_map(tm))


def _mod_kernel(c_ref, w_ref, b_ref, o_ref):
    o_ref[0] = _dot3(_silu(c_ref[...]), w_ref[0]) + b_ref[0]


def _modulation_tables(cond, w_mod, b_mod):
    tn = 1536
    return pl.pallas_call(
        _mod_kernel,
        grid=(DEPTH, N_MOD * D_MODEL // tn),
        in_specs=[pl.BlockSpec((SEG_PAD, D_MODEL), lambda l, j: (0, 0)),
                  pl.BlockSpec((1, D_MODEL, tn), lambda l, j: (l, 0, j)),
                  pl.BlockSpec((1, 1, tn), lambda l, j: (l, 0, j))],
        out_specs=pl.BlockSpec((1, SEG_PAD, tn), lambda l, j: (l, 0, j)),
        out_shape=jax.ShapeDtypeStruct((DEPTH, SEG_PAD, N_MOD * D_MODEL), F32),
        compiler_params=_cparams("parallel", "parallel"),
        name="modulation",
    )(cond, w_mod, b_mod.reshape(DEPTH, 1, N_MOD * D_MODEL))


HALO = 8


def _inproj_kernel(x_ref, xp_ref, xn_ref, mod_ref, wz, wx, wu, wv, wdt, convw_ref, convb_ref, dtb_ref, alog_ref,
                   tri_ref, z_ref, xbc_ref, u_ref, v_ref, dt_ref, cs_ref):
    tm = x_ref.shape[0]
    r0 = pl.program_id(0) * tm
    shift = mod_ref[0, 0:1, :]
    scale = mod_ref[0, 1:2, :]
    x_ext = jnp.concatenate([x_ref[...], xp_ref[...], xn_ref[...]], axis=0)
    h_ext = (_rms(x_ext) * (1.0 + scale) + shift).astype(BF16)
    h = h_ext[0:tm, :]

    rel = r0 - N_PROMPT
    in_latent = r0 >= N_PROMPT
    keep_prev = jnp.where(jnp.logical_and(in_latent, lax.rem(rel, DEC_SEQ) != 0), 1.0, 0.0)
    keep_next = jnp.where(jnp.logical_and(in_latent, lax.rem(rel + tm, DEC_SEQ) != 0), 1.0, 0.0)
    reach = CONV_W // 2
    n_blocks = CONV_CH // 256
    others = {1: (z_ref, wz), 3: (u_ref, wu), 5: (v_ref, wv)}
    for cb in range(n_blocks):
        cols = slice(cb * 256, (cb + 1) * 256)
        xa = _dot(h_ext, wx[:, cols])
        win = jnp.concatenate([xa[tm:tm + HALO, :] * keep_prev, xa[0:tm, :],
                               xa[tm + HALO:tm + 2 * HALO, :] * keep_next], axis=0)
        acc = jnp.zeros((tm, 256), F32) + convb_ref[:, cols]
        for k in range(CONV_W):
            start = HALO - reach + k
            acc = acc + win[start:start + tm, :] * convw_ref[k:k + 1, cols]
        xbc_ref[:, cols] = _silu(acc).astype(BF16)
        if cb in others:
            o_ref, w_ref = others[cb]
            o_ref[...] = _dot(h, w_ref[...]).astype(BF16)

    lane = lax.broadcasted_iota(jnp.int32, (1, LANES), 1)
    raw = _dot(h, wdt[...]) + dtb_ref[...]
    dt = jnp.where(lane < 2 * SSD_HEADS, jnp.maximum(raw, 0.0) + jnp.log(1.0 + jnp.exp(-jnp.abs(raw))), 0.0)
    dt_ref[...] = dt
    ac = dt * -jnp.exp(alog_ref[...])
    hi = ac.astype(BF16)
    rest = ac - hi.astype(F32)
    mid = rest.astype(BF16)
    lo = (rest - mid.astype(F32)).astype(BF16)
    tri = tri_ref[...]
    for k in range(tm // CHUNK):
        rows = slice(k * CHUNK, (k + 1) * CHUNK)
        cs_ref[rows, :] = _dot(tri, hi[rows, :]) + _dot(tri, mid[rows, :]) + _dot(tri, lo[rows, :])


def _in_projection(x, mod, weights, consts):
    tm = TM_ROWS
    widths = (D_SSD, CONV_CH, D_SGU, D_SGU, LANES, LANES)
    dtypes = (BF16, BF16, BF16, BF16, F32, F32)
    per_tile = tm // HALO
    last = N_TOK // HALO - 1
    halo_prev = pl.BlockSpec((HALO, D_MODEL), lambda i: (jnp.maximum(i * per_tile - 1, 0), 0))
    halo_next = pl.BlockSpec((HALO, D_MODEL), lambda i: (jnp.minimum((i + 1) * per_tile, last), 0))
    return pl.pallas_call(
        _inproj_kernel,
        grid=(N_TOK // tm,),
        in_specs=[_row_spec(tm, D_MODEL), halo_prev, halo_next, _mod_spec(tm)]
                 + [_const_spec(a.shape) for a in weights + consts],
        out_specs=[_row_spec(tm, w) for w in widths],
        out_shape=[jax.ShapeDtypeStruct((N_TOK, w), dt) for w, dt in zip(widths, dtypes)],
        compiler_params=_cparams("parallel"),
        name="in_projection",
    )(x, x, x, mod, *weights, *consts)


def _ssd_kernel(has_init, has_prev, n_chunks, *refs):
    refs = list(refs)
    cv_ref, dtv_ref, cs_ref = refs[0:3]
    del refs[0:3]
    init_ref = refs.pop(0) if has_init else None
    alog_ref, dskip_ref, ef_ref, eb_ref = refs[0:4]
    del refs[0:4]
    if has_prev:
        refs.pop(0)
    y_ref, fin_ref, st_ref = refs
    a_neg = -jnp.exp(alog_ref[...])
    rows_i = lax.broadcasted_iota(jnp.int32, (CHUNK, CHUNK), 0)
    cols_i = lax.broadcasted_iota(jnp.int32, (CHUNK, CHUNK), 1)
    lower = rows_i >= cols_i
    upper = cols_i >= rows_i
    even_head = jnp.bitwise_and(lax.broadcasted_iota(jnp.int32, (CHUNK, D_SSD), 1), LANES - 1) < SSD_HEADDIM

    def row0(c):
        return pl.multiple_of(c * CHUNK, CHUNK)

    def chunk_pass(c, forward):
        r0 = row0(c)
        rows = pl.ds(r0, CHUNK)
        e_ref = ef_ref if forward else eb_ref
        off = 0 if forward else SSD_HEADS
        dt = dtv_ref[rows, :]
        cs = cs_ref[rows, :]
        total = cs_ref[pl.ds(r0 + CHUNK - 1, 1), :]
        if forward:
            pos = cs
            to_edge = jnp.exp(total - cs)
            from_edge = jnp.exp(cs)
            mask = lower
        else:
            pos = cs - dt * a_neg
            to_edge = jnp.exp(pos)
            from_edge = jnp.exp(total - pos)
            mask = upper
        pos_t = pos.T
        x = cv_ref[rows, 0:D_SSD].astype(F32)
        stacked = jnp.concatenate([dt, dt * to_edge, from_edge], axis=0).astype(BF16)
        spread = _dot(stacked, e_ref[...])
        xc = x * spread[0:CHUNK]
        xc_sub = (jnp.where(even_head, xc, 0.0).astype(BF16), jnp.where(even_head, 0.0, xc).astype(BF16))
        xd_b = (x * spread[CHUNK:2 * CHUNK]).astype(BF16)
        from_e = spread[2 * CHUNK:3 * CHUNK]
        tot_hi, tot_lo = _split(jnp.exp(jnp.broadcast_to(total, (8, LANES))))
        tot_e = (_dot(tot_hi, e_ref[...]) + _dot(tot_lo, e_ref[...]))[0:1, :]
        st = st_ref[0 if forward else 1]
        st_b = st.astype(BF16)
        y_parts = []
        st_parts = []
        for g in range(SSD_GROUPS):
            bm = cv_ref[rows, D_SSD + g * SSD_STATE:D_SSD + (g + 1) * SSD_STATE]
            cm = cv_ref[rows, D_SSD + (SSD_GROUPS + g) * SSD_STATE:D_SSD + (SSD_GROUPS + g + 1) * SSD_STATE]
            bm_t = bm.astype(F32).T.astype(BF16)
            cb = _dot(cm, bm_t)
            gcols = slice(g * 256, (g + 1) * 256)
            y_off = _dot(cm, st_b[:, gcols])
            st_parts.append(_dot(bm_t, xd_b[:, gcols]))
            diag = []
            for pair in range(2):
                pcols = slice(g * 256 + pair * LANES, g * 256 + (pair + 1) * LANES)
                decayed = []
                for sub in range(2):
                    h = g * 4 + pair * 2 + sub
                    col = pos[:, off + h:off + h + 1]
                    row = pos_t[off + h:off + h + 1, :]
                    diff = (col - row) if forward else (row - col)
                    decayed.append((jnp.where(mask, jnp.exp(diff), 0.0) * cb).astype(BF16))
                diag.append(_dot(jnp.concatenate(decayed, axis=1),
                                 jnp.concatenate([xc_sub[0][:, pcols], xc_sub[1][:, pcols]], axis=0)))
            y_parts.append(jnp.concatenate(diag, axis=1) + y_off * from_e[:, gcols])
        y_new = jnp.concatenate(y_parts, axis=1)
        st_ref[0 if forward else 1] = st * tot_e + jnp.concatenate(st_parts, axis=1)
        if forward:
            y_new = y_new + x * dskip_ref[...]
        return rows, y_new

    if has_init:
        st_ref[0] = init_ref[0, 0].T
        st_ref[1] = init_ref[0, 1].T
    else:
        st_ref[...] = jnp.zeros_like(st_ref)

    def first_half(i, carry):
        for rows, y_new in (chunk_pass(i, True), chunk_pass(n_chunks - 1 - i, False)):
            y_ref[rows, :] = y_new
        return carry

    def second_half(i, carry):
        for rows, y_new in (chunk_pass(i, True), chunk_pass(n_chunks - 1 - i, False)):
            y_ref[rows, :] = y_ref[rows, :] + y_new
        return carry

    lax.fori_loop(0, n_chunks // 2, first_half, 0)
    lax.fori_loop(n_chunks // 2, n_chunks, second_half, 0)
    fin_ref[0, 0] = st_ref[0].T
    fin_ref[0, 1] = st_ref[1].T


def _ssd_scan(xbc, dt, cs, init, consts, n_seq, seq, row_block0, y_prev=None):
    n_chunks = seq // CHUNK
    assert n_chunks % 2 == 0
    has_init = init is not None
    has_prev = y_prev is not None
    once = pl.Buffered(1)
    seq_spec = lambda w: pl.BlockSpec((seq, w), lambda b: (row_block0 + b, 0), pipeline_mode=once)
    in_specs = [seq_spec(CONV_CH), seq_spec(LANES), seq_spec(LANES)]
    args = [xbc, dt, cs]
    if has_init:
        in_specs.append(pl.BlockSpec((1, 2, D_SSD, SSD_STATE), lambda b: (b, 0, 0, 0), pipeline_mode=once))
        args.append(init)
    for cst in consts:
        in_specs.append(pl.BlockSpec(cst.shape, lambda b, nd=cst.ndim: (0,) * nd))
        args.append(cst)
    aliases = {}
    if has_prev:
        aliases[len(args)] = 0
        in_specs.append(pl.BlockSpec(memory_space=pl.ANY))
        args.append(y_prev)
    return pl.pallas_call(
        functools.partial(_ssd_kernel, has_init, has_prev, n_chunks),
        grid=(n_seq,),
        in_specs=in_specs,
        out_specs=[pl.BlockSpec((seq, D_SSD), lambda b: (row_block0 + b, 0)),
                   pl.BlockSpec((1, 2, D_SSD, SSD_STATE), lambda b: (b, 0, 0, 0))],
        out_shape=[jax.ShapeDtypeStruct((N_TOK, D_SSD), F32),
                   jax.ShapeDtypeStruct((n_seq, 2, D_SSD, SSD_STATE), F32)],
        input_output_aliases=aliases,
        scratch_shapes=[pltpu.VMEM((2, SSD_STATE, D_SSD), F32)],
        compiler_params=_cparams("parallel"),
        name="ssd_scan_%d" % seq,
    )(*args)


def _mixer_out_kernel(x_ref, mod_ref, y_ref, z_ref, u_ref, v_ref, ssdn_ref, sgun_ref, wsp_ref, bsp_ref,
                      wo_ref, o_ref):
    tm = x_ref.shape[0]
    gated = y_ref[...] * _silu(z_ref[...].astype(F32))
    a = (_rms(gated) * ssdn_ref[...]).astype(BF16)
    vb = (_rms(v_ref[...].astype(F32)) * sgun_ref[...]).astype(BF16)
    chunks = []
    for k in range(tm // CHUNK):
        rows = slice(k * CHUNK, (k + 1) * CHUNK)
        groups = [_dot(wsp_ref[g], vb[rows, g * SGU_GDIM:(g + 1) * SGU_GDIM]) for g in range(SGU_GROUPS)]
        chunks.append(jnp.concatenate(groups, axis=1) + bsp_ref[...])
    s = (u_ref[...].astype(F32) * jnp.concatenate(chunks, axis=0)).astype(BF16)
    out = _dot(a, wo_ref[0:D_SSD, :]) + _dot(s, wo_ref[D_SSD:D_SSD + D_SGU, :])
    o_ref[...] = x_ref[...] + mod_ref[0, 2:3, :] * out


def _mixer_output(x, mod, y, z, u, v, ssd_norm, sgu_norm, w_sp, b_sp_e, w_out):
    tm = TM_ROWS
    return pl.pallas_call(
        _mixer_out_kernel,
        grid=(N_TOK // tm,),
        in_specs=[_row_spec(tm, D_MODEL), _mod_spec(tm), _row_spec(tm, D_SSD), _row_spec(tm, D_SSD),
                  _row_spec(tm, D_SGU), _row_spec(tm, D_SGU), _const_spec((1, D_SSD)), _const_spec((1, D_SGU)),
                  _const_spec((SGU_GROUPS, CHUNK, CHUNK)), _const_spec((CHUNK, D_SGU)),
                  _const_spec((D_SSD + D_SGU, D_MODEL))],
        out_specs=_row_spec(tm, D_MODEL),
        out_shape=jax.ShapeDtypeStruct((N_TOK, D_MODEL), F32),
        compiler_params=_cparams("parallel"),
        name="mixer_output",
    )(x, mod, y, z, u, v, ssd_norm, sgu_norm, w_sp, b_sp_e, w_out)


def _ffn_kernel(x_ref, mod_ref, wg_ref, wu_ref, wd_ref, o_ref):
    h = _modulated(x_ref, mod_ref, 3).astype(BF16)
    acc = jnp.zeros(o_ref.shape, F32)
    for f in range(D_FF // FF_CHUNK):
        cols = slice(f * FF_CHUNK, (f + 1) * FF_CHUNK)
        act = (_silu(_dot(h, wg_ref[:, cols])) * _dot(h, wu_ref[:, cols])).astype(BF16)
        acc = acc + _dot(act, wd_ref[cols, :])
    o_ref[...] = x_ref[...] + mod_ref[0, 5:6, :] * acc


def _dense_ffn(x, mod, wg, wu, wd):
    tm = TM_FFN
    once = pl.Buffered(1)
    return pl.pallas_call(
        _ffn_kernel,
        grid=(N_TOK // tm,),
        in_specs=[_row_spec(tm, D_MODEL), _mod_spec(tm),
                  pl.BlockSpec((D_MODEL, D_FF), lambda i: (0, 0), pipeline_mode=once),
                  pl.BlockSpec((D_MODEL, D_FF), lambda i: (0, 0), pipeline_mode=once),
                  pl.BlockSpec((D_FF, D_MODEL), lambda i: (0, 0), pipeline_mode=once)],
        out_specs=_row_spec(tm, D_MODEL),
        out_shape=jax.ShapeDtypeStruct((N_TOK, D_MODEL), F32),
        compiler_params=_cparams("parallel"),
        name="dense_ffn",
    )(x, mod, wg, wu, wd)


HEAD_PROJ = 3 * LANES


def _pair_sums(sq_a, sq_b, ones_ref):
    return _dot(jnp.concatenate([sq_a, sq_b], axis=1).astype(BF16), ones_ref[...])


def _expand_keys(ckv_b, pe_sq, pe_rot, wuk_ref, wuv_ref, kn_ref, ones_ref, k_ref, v_ref):
    kn_nope = kn_ref[:, 0:QK_NOPE]
    v_ref[...] = _dot(ckv_b, wuv_ref[...]).astype(BF16)
    k_nope = _dot(ckv_b, wuk_ref[...])
    for pair in range(MLA_HEADS // 2):
        kh = [k_nope[:, h * QK_NOPE:(h + 1) * QK_NOPE] for h in (2 * pair, 2 * pair + 1)]
        ss = _pair_sums(kh[0] * kh[0] + pe_sq, kh[1] * kh[1] + pe_sq, ones_ref)
        r = lax.rsqrt(ss * (1.0 / QK_DIM) + EPS)
        for idx in range(2):
            h = 2 * pair + idx
            rh = r[:, idx * LANES:(idx + 1) * LANES]
            k_ref[:, h * HEAD_PAD:h * HEAD_PAD + QK_NOPE] = (kh[idx] * rh * kn_nope).astype(BF16)
            k_ref[:, h * HEAD_PAD + QK_NOPE:(h + 1) * HEAD_PAD] = (pe_rot * rh).astype(BF16)


def _mla_proj_kernel(x_ref, mod_ref, cos_ref, sin_ref, wdq_ref, wdkv_ref, qan_ref, kvan_ref,
                     wuq_ref, wuk_ref, wuv_ref, qn_ref, kn_ref, ones_ref, q_ref, k_ref, v_ref, ckv_ref, kpe_ref):
    h = _modulated(x_ref, mod_ref, 0).astype(BF16)
    cos, sin = cos_ref[...], sin_ref[...]
    qa = (_rms(_dot(h, wdq_ref[...])) * qan_ref[...]).astype(BF16)
    q = _dot(qa, wuq_ref[...])
    qn_nope = qn_ref[:, 0:LANES]
    qn_rope = qn_ref[:, LANES:2 * LANES] * cos
    qn_part = qn_ref[:, 2 * LANES:3 * LANES] * sin
    scale = QK_DIM ** -0.5
    for pair in range(MLA_HEADS // 2):
        blocks = []
        for hd in (2 * pair, 2 * pair + 1):
            base = hd * HEAD_PROJ
            blocks.append((q[:, base:base + LANES], q[:, base + LANES:base + 2 * LANES],
                           q[:, base + 2 * LANES:base + 3 * LANES]))
        ss = _pair_sums(*[qh * qh + qr * qr for qh, qr, _ in blocks], ones_ref)
        r = lax.rsqrt(ss * (1.0 / QK_DIM) + EPS) * scale
        for idx, (qh, qr, qp) in enumerate(blocks):
            hd = 2 * pair + idx
            rh = r[:, idx * LANES:(idx + 1) * LANES]
            q_ref[:, hd * HEAD_PAD:hd * HEAD_PAD + QK_NOPE] = (qh * rh * qn_nope).astype(BF16)
            q_ref[:, hd * HEAD_PAD + QK_NOPE:(hd + 1) * HEAD_PAD] = ((qr * qn_rope + qp * qn_part) * rh).astype(BF16)
    kva = _dot(h, wdkv_ref[...])
    ckv = _rms(kva[:, 0:KV_RANK]) * kvan_ref[...]
    kpe = kva[:, KV_RANK:KV_RANK + LANES]
    kpe_part = kva[:, KV_RANK + LANES:KV_RANK + 2 * LANES]
    ckv_ref[...] = ckv
    kpe_ref[...] = kpe
    pe_rot = kpe * (kn_ref[:, LANES:2 * LANES] * cos) + kpe_part * (kn_ref[:, 2 * LANES:3 * LANES] * sin)
    _expand_keys(ckv.astype(BF16), kpe * kpe, pe_rot, wuk_ref, wuv_ref, kn_ref, ones_ref, k_ref, v_ref)


def _mla_projection(x, mod, rope_tabs, wdq, wdkv, qan, kvan, wuq, wuk, wuv, qn, kn, ones):
    tm = TM_ROWS
    hw = MLA_HEADS * HEAD_PAD
    out_w = (hw, hw, MLA_HEADS * V_HEAD, KV_RANK, LANES)
    out_dt = (BF16, BF16, BF16, F32, F32)
    consts = (wdq, wdkv, qan, kvan, wuq, wuk, wuv, qn, kn, ones)
    return pl.pallas_call(
        _mla_proj_kernel,
        grid=(N_TOK // tm,),
        in_specs=[_row_spec(tm, D_MODEL), _mod_spec(tm)] + [_row_spec(tm, LANES)] * 2
                 + [_const_spec(cst.shape) for cst in consts],
        out_specs=[_row_spec(tm, w) for w in out_w],
        out_shape=[jax.ShapeDtypeStruct((N_TOK, w), dt) for w, dt in zip(out_w, out_dt)],
        compiler_params=_cparams("parallel"),
        name="mla_projection",
    )(x, mod, *rope_tabs, *consts)


def _cache_kv_kernel(ckv_ref, kpe_ref, wuk_ref, wuv_ref, kn_ref, ones_ref, k_ref, v_ref):
    kpe = kpe_ref[...]
    _expand_keys(ckv_ref[...].astype(BF16), kpe * kpe, kpe * kn_ref[:, LANES:2 * LANES],
                 wuk_ref, wuv_ref, kn_ref, ones_ref, k_ref, v_ref)


def _cache_keys(ckv, kpe, wuk, wuv, kn, ones):
    rows = ckv.shape[0]
    tm = TM_ROWS
    consts = (wuk, wuv, kn, ones)
    return pl.pallas_call(
        _cache_kv_kernel,
        grid=(rows // tm,),
        in_specs=[_row_spec(tm, KV_RANK), _row_spec(tm, LANES)] + [_const_spec(cst.shape) for cst in consts],
        out_specs=[_row_spec(tm, MLA_HEADS * HEAD_PAD), _row_spec(tm, MLA_HEADS * V_HEAD)],
        out_shape=[jax.ShapeDtypeStruct((rows, MLA_HEADS * HEAD_PAD), BF16),
                   jax.ShapeDtypeStruct((rows, MLA_HEADS * V_HEAD), BF16)],
        compiler_params=_cparams("parallel"),
        name="cache_keys",
    )(ckv, kpe, *consts)


def _attn_kernel(has_cache, tq, *refs):
    if has_cache:
        q_ref, k_ref, v_ref, kc_ref, vc_ref, _, o_ref = refs
    else:
        q_ref, k_ref, v_ref, o_ref = refs

    def tile(r0):
        q = q_ref[pl.ds(r0, tq), :]
        s = _dot_nt(q, k_ref[...])
        m = jnp.max(s, axis=-1, keepdims=True)
        if has_cache:
            sc = _dot_nt(q, kc_ref[...])
            m = jnp.maximum(m, jnp.max(sc, axis=-1, keepdims=True))
            pc = jnp.exp(sc - m)
        p = jnp.exp(s - m)
        den = jnp.sum(p, axis=-1, keepdims=True)
        num = _dot(p.astype(BF16), v_ref[...])
        if has_cache:
            den = den + jnp.sum(pc, axis=-1, keepdims=True)
            num = num + _dot(pc.astype(BF16), vc_ref[...])
        o_ref[pl.ds(r0, tq), :] = (num / den).astype(BF16)

    n_pairs = q_ref.shape[0] // (2 * tq)
    if n_pairs == 1:
        tile(0)
        tile(tq)
    else:
        def body(i, carry):
            r0 = pl.multiple_of(i * (2 * tq), 2 * tq)
            tile(r0)
            tile(r0 + tq)
            return carry
        lax.fori_loop(0, n_pairs, body, 0)


def _attention(q, k, v, kc, vc, o_prev, n_seq, seq, row_block0):
    has_cache = kc is not None
    tq = min(TQ, seq // 2)
    seq_spec = lambda w: pl.BlockSpec((seq, w), lambda b, h: (row_block0 + b, h))
    in_specs = [seq_spec(HEAD_PAD), seq_spec(HEAD_PAD), seq_spec(V_HEAD)]
    args = [q, k, v]
    if has_cache:
        past = kc.shape[0] // n_seq
        in_specs += [pl.BlockSpec((past, HEAD_PAD), lambda b, h: (b, h)),
                     pl.BlockSpec((past, V_HEAD), lambda b, h: (b, h)),
                     pl.BlockSpec(memory_space=pl.ANY)]
        args += [kc, vc, o_prev]
    return pl.pallas_call(
        functools.partial(_attn_kernel, has_cache, tq),
        grid=(n_seq, MLA_HEADS),
        in_specs=in_specs,
        out_specs=seq_spec(V_HEAD),
        out_shape=jax.ShapeDtypeStruct((N_TOK, MLA_HEADS * V_HEAD), BF16),
        input_output_aliases={5: 0} if has_cache else {},
        compiler_params=_cparams("parallel", "parallel"),
        name="attention_%d" % seq,
    )(*args)


def _residual_proj_kernel(gate_row, x_ref, mod_ref, a_ref, w_ref, o_ref):
    o_ref[...] = x_ref[...] + mod_ref[0, gate_row:gate_row + 1, :] * _dot(a_ref[...], w_ref[...])


def _residual_projection(x, mod, a, w, gate_row):
    tm = TM_FFN
    k = a.shape[1]
    return pl.pallas_call(
        functools.partial(_residual_proj_kernel, gate_row),
        grid=(N_TOK // tm,),
        in_specs=[_row_spec(tm, D_MODEL), _mod_spec(tm), _row_spec(tm, k), _const_spec((k, D_MODEL))],
        out_specs=_row_spec(tm, D_MODEL),
        out_shape=jax.ShapeDtypeStruct((N_TOK, D_MODEL), F32),
        compiler_params=_cparams("parallel"),
        name="residual_projection",
    )(x, mod, a, w)


def _router_kernel(x_ref, mod_ref, wr_ref, tri_ref, h_ref, meta_ref, cnt_ref, run_ref):
    @pl.when(pl.program_id(0) == 0)
    def _():
        run_ref[...] = jnp.zeros_like(run_ref)

    h = _modulated(x_ref, mod_ref, 3)
    h_ref[...] = h.astype(BF16)
    lane = lax.broadcasted_iota(jnp.int32, (x_ref.shape[0], LANES), 1)
    lane_f = lane.astype(F32)
    logits = jnp.where(lane < N_EXPERTS, _dot3(h, wr_ref[...]), -jnp.inf)
    m1 = jnp.max(logits, axis=-1, keepdims=True)
    e1 = jnp.min(jnp.where(logits == m1, lane_f, float(LANES)), axis=-1, keepdims=True)
    rest = jnp.where(lane_f == e1, -jnp.inf, logits)
    m2 = jnp.max(rest, axis=-1, keepdims=True)
    e2 = jnp.min(jnp.where(rest == m2, lane_f, float(LANES)), axis=-1, keepdims=True)
    t = jnp.exp(m2 - m1)
    g1 = 1.0 / (1.0 + t)
    pick1 = lane_f == e1
    pick2 = lane_f == e2
    onehot = jnp.where(pick1, 1.0, jnp.where(pick2, 1.0, 0.0))
    before = run_ref[...] + _dot(tri_ref[...], onehot.astype(BF16)) - onehot
    rank1 = jnp.sum(jnp.where(pick1, before, 0.0), axis=-1, keepdims=True)
    rank2 = jnp.sum(jnp.where(pick2, before, 0.0), axis=-1, keepdims=True)
    run_ref[...] += jnp.sum(onehot, axis=0, keepdims=True)
    cnt_ref[...] = jnp.broadcast_to(run_ref[...], cnt_ref.shape)
    cols = (e1, e2, g1, 1.0 - g1, rank1, rank2)
    meta = jnp.zeros(meta_ref.shape, F32)
    for idx, val in enumerate(cols):
        meta = jnp.where(lane == idx, val, meta)
    meta_ref[...] = meta


META_EXPERT, META_GATE, META_RANK = 0, 2, 4


def _router(x, mod, wr):
    tm = TM_ROWS
    r = jnp.arange(tm)
    tri = (r[:, None] >= r[None, :]).astype(BF16)
    return pl.pallas_call(
        _router_kernel,
        grid=(N_TOK // tm,),
        in_specs=[_row_spec(tm, D_MODEL), _mod_spec(tm), _const_spec((D_MODEL, LANES)), _const_spec((tm, tm))],
        out_specs=[_row_spec(tm, D_MODEL), _row_spec(tm, LANES), _const_spec((SEG_PAD, LANES))],
        out_shape=[jax.ShapeDtypeStruct((N_TOK, D_MODEL), BF16), jax.ShapeDtypeStruct((N_TOK, LANES), F32),
                   jax.ShapeDtypeStruct((SEG_PAD, LANES), F32)],
        scratch_shapes=[pltpu.VMEM((1, LANES), F32)],
        compiler_params=_cparams("arbitrary"),
        name="router",
    )(x, mod, wr, tri)


def _experts_kernel(be_ref, nu_ref, rows_ref, wg_ref, wu_ref, wd_ref, o_ref, acc_ref):
    i = pl.program_id(0)
    f = pl.program_id(1)

    @pl.when(i < nu_ref[0])
    def _():
        @pl.when(f == 0)
        def _():
            acc_ref[...] = jnp.zeros_like(acc_ref)

        rows = rows_ref[...]
        act = (_silu(_dot(rows, wg_ref[0, 0].astype(BF16))) * _dot(rows, wu_ref[0, 0].astype(BF16))).astype(BF16)
        acc_ref[...] += _dot(act, wd_ref[0, 0].astype(BF16))

    @pl.when(f == pl.num_programs(1) - 1)
    def _():
        o_ref[...] = acc_ref[...].astype(BF16)


def _expert_ffn(rows, block_expert, n_used, wg, wu, wd, layer):
    n_rows = rows.shape[0]
    tm = TM_MOE
    tf = FF_CHUNK
    n_f = D_FF // tf

    def f_idx(i, f, nu):
        return jnp.where(i < nu[0], f, n_f - 1)

    grid_spec = pltpu.PrefetchScalarGridSpec(
        num_scalar_prefetch=2,
        grid=(n_rows // tm, n_f),
        in_specs=[pl.BlockSpec((tm, D_MODEL), lambda i, f, be, nu: (i, 0)),
                  pl.BlockSpec((1, 1, D_MODEL, tf), lambda i, f, be, nu: (layer, be[i], 0, f_idx(i, f, nu))),
                  pl.BlockSpec((1, 1, D_MODEL, tf), lambda i, f, be, nu: (layer, be[i], 0, f_idx(i, f, nu))),
                  pl.BlockSpec((1, 1, tf, D_MODEL), lambda i, f, be, nu: (layer, be[i], f_idx(i, f, nu), 0))],
        out_specs=pl.BlockSpec((tm, D_MODEL), lambda i, f, be, nu: (i, 0)),
        scratch_shapes=[pltpu.VMEM((tm, D_MODEL), F32)])
    return pl.pallas_call(
        _experts_kernel,
        grid_spec=grid_spec,
        out_shape=jax.ShapeDtypeStruct((n_rows, D_MODEL), BF16),
        compiler_params=_cparams("arbitrary", "arbitrary"),
        name="expert_ffn",
    )(block_expert, n_used, rows, wg, wu, wd)


def _combine_kernel(x_ref, mod_ref, meta_ref, a_ref, b_ref, o_ref):
    g1 = meta_ref[:, META_GATE:META_GATE + 1]
    g2 = meta_ref[:, META_GATE + 1:META_GATE + 2]
    y = a_ref[...].astype(F32) * g1 + b_ref[...].astype(F32) * g2
    o_ref[...] = x_ref[...] + mod_ref[0, 5:6, :] * y


def _moe_combine(x, mod, meta, a, b):
    tm = TM_FFN
    return pl.pallas_call(
        _combine_kernel,
        grid=(N_TOK // tm,),
        in_specs=[_row_spec(tm, D_MODEL), _mod_spec(tm), _row_spec(tm, LANES), _row_spec(tm, D_MODEL),
                  _row_spec(tm, D_MODEL)],
        out_specs=_row_spec(tm, D_MODEL),
        out_shape=jax.ShapeDtypeStruct((N_TOK, D_MODEL), F32),
        compiler_params=_cparams("parallel"),
        name="moe_combine",
    )(x, mod, meta, a, b)


def _moe(x, mod, wr, wg, wu, wd, layer):
    h, meta, counts = _router(x, mod, wr)
    tm = TM_MOE
    n_assign = N_TOK * TOP_K
    experts = meta[:, META_EXPERT:META_EXPERT + TOP_K].astype(jnp.int32)
    rank = meta[:, META_RANK:META_RANK + TOP_K].astype(jnp.int32)
    padded = (counts[0, :N_EXPERTS].astype(jnp.int32) + tm - 1) // tm * tm
    pad_end = jnp.cumsum(padded)
    pad_start = pad_end - padded
    onehot = experts[:, :, None] == jnp.arange(N_EXPERTS, dtype=jnp.int32)
    dest = jnp.sum(jnp.where(onehot, pad_start, 0), axis=-1) + rank
    n_rows = n_assign + N_EXPERTS * tm
    n_blocks = n_rows // tm
    row_token = jnp.zeros((n_rows,), jnp.int32).at[dest.reshape(-1)].set(
        jnp.arange(n_assign, dtype=jnp.int32) // TOP_K)
    n_used = (pad_end[-1] // tm).astype(jnp.int32).reshape(1)
    block_start = jnp.minimum(jnp.arange(n_blocks, dtype=jnp.int32), n_used[0] - 1) * tm
    block_expert = jnp.minimum(jnp.sum((pad_end[None, :] <= block_start[:, None]).astype(jnp.int32), axis=1),
                               N_EXPERTS - 1)
    out = _expert_ffn(h[row_token], block_expert, n_used, wg, wu, wd, layer)
    return _moe_combine(x, mod, meta, out[dest[:, 0]], out[dest[:, 1]])


def _rope_tables():
    half = ROPE_AXIS // 2
    pos = jnp.arange(DEC_SEQ)
    row = (pos // GRID_W).astype(F32)
    col = (pos % GRID_W).astype(F32)
    inv = ROPE_THETA ** (-jnp.arange(0, ROPE_AXIS, 2, dtype=F32) / ROPE_AXIS)
    ang_r = row[:, None] * inv
    ang_c = col[:, None] * inv
    pad = jnp.zeros((DEC_SEQ, LANES - QK_ROPE), F32)
    cos = jnp.concatenate([jnp.cos(ang_r), jnp.cos(ang_r), jnp.cos(ang_c), jnp.cos(ang_c), pad + 1.0], axis=1)
    sin = jnp.concatenate([-jnp.sin(ang_r), jnp.sin(ang_r), -jnp.sin(ang_c), jnp.sin(ang_c), pad], axis=1)
    prompt = jnp.zeros((N_PROMPT, LANES), F32)
    tile = lambda t: jnp.tile(t, (DEC_BATCH, 1))
    return (jnp.concatenate([prompt + 1.0, tile(cos)], axis=0),
            jnp.concatenate([prompt, tile(sin)], axis=0))


def _rope_blocks(a):
    half = ROPE_AXIS // 2
    partner = jnp.concatenate([a[..., half:2 * half], a[..., 0:half], a[..., 3 * half:4 * half],
                               a[..., 2 * half:3 * half]], axis=-1)
    return jnp.concatenate([_pad_lanes(a, LANES), _pad_lanes(partner, LANES)], axis=-1)


def _pad_lanes(a, width):
    return jnp.pad(a, [(0, 0)] * (a.ndim - 1) + [(0, width - a.shape[-1])])


def _ssd_constants():
    r = jnp.arange(CHUNK)
    tri = (r[:, None] >= r[None, :]).astype(BF16)
    head_of = jnp.arange(D_SSD) // SSD_HEADDIM
    e_f = (r[:, None] == head_of[None, :]).astype(BF16)
    e_b = (r[:, None] == head_of[None, :] + SSD_HEADS).astype(BF16)
    return tri, e_f, e_b


def kernel(x_prompt, x_sample, c, state_ssm, cache_ckv, cache_kpe, c_ctx, w_mod, b_mod, w_in, conv_w, conv_b, dt_bias, a_log, d_skip, ssd_norm, sgu_norm, w_sp, b_sp, w_out, ffn_w_gate, ffn_w_up, ffn_w_down, w_dq, q_a_norm, w_uq, w_dkv, kv_a_norm, w_ukv, q_norm, k_norm, w_o, router, moe_w_gate, moe_w_up, moe_w_down):
    x = jnp.concatenate([x_prompt.reshape(N_PROMPT, D_MODEL), x_sample.reshape(N_SAMPLE, D_MODEL)], axis=0)
    cond = jnp.concatenate([c_ctx[None, :], c, jnp.zeros((SEG_PAD - N_SEG, D_MODEL), F32)], axis=0)
    mods = _modulation_tables(cond, w_mod, b_mod).reshape(DEPTH, SEG_PAD, N_MOD, D_MODEL)
    rope_tabs = _rope_tables()
    tri, e_f, e_b = _ssd_constants()
    blk = jnp.arange(2 * LANES) // LANES
    pair_ones = (blk[:, None] == blk[None, :]).astype(BF16)
    i1 = D_SSD
    i2 = i1 + CONV_CH
    i3 = i2 + 2 * SSD_HEADS
    i4 = i3 + D_SGU
    new_ssm, new_ckv, new_kpe = [], [], []
    for i in range(DEPTH):
        j = i // 2
        mod = mods[i]
        if i % 2 == 0:
            w = w_in[j]
            wz, wx, wu, wv = (w[:, 0:i1].astype(BF16), w[:, i1:i2].astype(BF16),
                              w[:, i3:i4].astype(BF16), w[:, i4:].astype(BF16))
            wdt = _pad_lanes(w[:, i2:i3], LANES).astype(BF16)
            a_log_row = _pad_lanes(a_log[j].reshape(1, -1), LANES)
            z, xbc, u, v, dt, cs = _in_projection(
                x, mod, (wz, wx, wu, wv, wdt),
                (conv_w[j], conv_b[j][None, :], _pad_lanes(dt_bias[j].reshape(1, -1), LANES), a_log_row, tri))
            consts = (a_log_row, jnp.repeat(d_skip[j], SSD_HEADDIM)[None, :], e_f, e_b)
            y, st_p = _ssd_scan(xbc, dt, cs, None, consts, BATCH, SEQ, 0)
            init = state_ssm[:, j].reshape(DEC_BATCH, 2, D_SSD, SSD_STATE)
            y, _ = _ssd_scan(xbc, dt, cs, init, consts, DEC_BATCH, DEC_SEQ, N_PROMPT // DEC_SEQ, y_prev=y)
            new_ssm.append(st_p.reshape(BATCH, 2, SSD_HEADS, SSD_HEADDIM, SSD_STATE))
            b_sp_e = jnp.repeat(b_sp[j].T, SGU_GDIM, axis=1)
            x = _mixer_output(x, mod, y, z, u, v, ssd_norm[j][None, :], sgu_norm[j][None, :],
                              w_sp[j].astype(BF16), b_sp_e, w_out[j].astype(BF16))
            x = _dense_ffn(x, mod, ffn_w_gate[j].astype(BF16), ffn_w_up[j].astype(BF16),
                           ffn_w_down[j].astype(BF16))
        else:
            split = lambda a: jnp.concatenate([a[..., :QK_NOPE], _rope_blocks(a[..., QK_NOPE:])], axis=-1)
            wuq = split(w_uq[j].reshape(Q_RANK, MLA_HEADS, QK_DIM)).reshape(Q_RANK, MLA_HEADS * HEAD_PROJ)
            wdkv = jnp.concatenate([w_dkv[j][:, :KV_RANK], _rope_blocks(w_dkv[j][:, KV_RANK:])], axis=-1)
            wukv = w_ukv[j].reshape(KV_RANK, MLA_HEADS, QK_NOPE + V_HEAD)
            wuk = wukv[:, :, :QK_NOPE].reshape(KV_RANK, -1).astype(BF16)
            wuv = wukv[:, :, QK_NOPE:].reshape(KV_RANK, -1).astype(BF16)
            qn = split(q_norm[j][None, :])
            kn = split(k_norm[j][None, :])
            q, k, v, ckv, kpe = _mla_projection(
                x, mod, rope_tabs, w_dq[j].astype(BF16), wdkv.astype(BF16), q_a_norm[j][None, :],
                kv_a_norm[j][None, :], wuq.astype(BF16), wuk, wuv, qn, kn, pair_ones)
            new_ckv.append(ckv[:N_PROMPT].reshape(BATCH, SEQ, KV_RANK))
            new_kpe.append(kpe[:N_PROMPT, :QK_ROPE].reshape(BATCH, SEQ, QK_ROPE))
            kc, vc = _cache_keys(cache_ckv[:, j].reshape(DEC_BATCH * PAST_LEN, KV_RANK),
                                 _pad_lanes(cache_kpe[:, j].reshape(DEC_BATCH * PAST_LEN, QK_ROPE), LANES),
                                 wuk, wuv, kn, pair_ones)
            o = _attention(q, k, v, None, None, None, BATCH, SEQ, 0)
            o = _attention(q, k, v, kc, vc, o, DEC_BATCH, DEC_SEQ, N_PROMPT // DEC_SEQ)
            x = _residual_projection(x, mod, o, w_o[j].astype(BF16), 2)
            x = _moe(x, mod, _pad_lanes(router[j], LANES), moe_w_gate, moe_w_up, moe_w_down, j)
    return (x[:N_PROMPT].reshape(BATCH, SEQ, D_MODEL),
            x[N_PROMPT:].reshape(DEC_BATCH, DEC_SEQ, D_MODEL),
            jnp.stack(new_ssm, axis=1),
            jnp.stack(new_ckv, axis=1),
            jnp.stack(new_kpe, axis=1))
```

```python
import functools
import math

import jax
import jax.numpy as jnp
from jax import lax
from jax.experimental import pallas as pl
from jax.experimental.pallas import tpu as pltpu

F32 = jnp.float32
BF16 = jnp.bfloat16

D_MODEL = 1024
BATCH = 16
SEQ = 256
DEPTH = 4
DEC_BATCH = 4
DEC_SEQ = 2048
PAST_LEN = 512
GRID_W = 64
N_MOD = 6
EPS = 1e-6

SSD_HEADDIM = 64
SSD_HEADS = 16
D_SSD = 1024
SSD_GROUPS = 4
SSD_STATE = 128
CHUNK = 128
CONV_W = 5
CONV_CH = D_SSD + 2 * SSD_GROUPS * SSD_STATE
D_SGU = 1024
SGU_GROUPS = 4
SGU_GDIM = D_SGU // SGU_GROUPS

MLA_HEADS = 8
Q_RANK = 384
KV_RANK = 256
QK_NOPE = 128
QK_ROPE = 64
V_HEAD = 128
QK_DIM = QK_NOPE + QK_ROPE
ROPE_AXIS = QK_ROPE // 2
ROPE_THETA = 10000.0
HEAD_PAD = 256

D_FF = 2816
N_EXPERTS = 8
TOP_K = 2

N_PROMPT = BATCH * SEQ
N_SAMPLE = DEC_BATCH * DEC_SEQ
N_TOK = N_PROMPT + N_SAMPLE
N_SEG = 1 + DEC_BATCH
SEG_PAD = 8

LANES = 128
VMEM_LIMIT = 56 * 1024 * 1024

TM_ROWS = 256
TM_FFN = 512
FF_CHUNK = 256
TM_MOE = 1024
TQ = 256


def _cparams(*sem):
    return pltpu.CompilerParams(dimension_semantics=sem, vmem_limit_bytes=VMEM_LIMIT)


def _dot(a, b):
    return jnp.dot(a, b, preferred_element_type=F32)


def _dot_nt(a, b):
    return lax.dot_general(a, b, (((1,), (1,)), ((), ())), preferred_element_type=F32)


def _split(x):
    hi = x.astype(BF16)
    lo = (x - hi.astype(F32)).astype(BF16)
    return hi, lo


def _dot3(a, b):
    ah, al = _split(a)
    bh, bl = _split(b)
    return _dot(ah, bh) + _dot(ah, bl) + _dot(al, bh)


def _silu(x):
    return x / (1.0 + jnp.exp(-x))


def _rms(x):
    return x * lax.rsqrt(jnp.mean(x * x, axis=-1, keepdims=True) + EPS)


def _modulated(x_ref, mod_ref, first):
    shift = mod_ref[0, first:first + 1, :]
    scale = mod_ref[0, first + 1:first + 2, :]
    return _rms(x_ref[...]) * (1.0 + scale) + shift


def _seg_map(tm):
    def index_map(i, *_):
        r = i * tm
        return (jnp.where(r < N_PROMPT, 0, 1 + (r - N_PROMPT) // DEC_SEQ), 0, 0)
    return index_map


def _row_spec(tm, width):
    return pl.BlockSpec((tm, width), lambda i, *_: (i, 0))


def _const_spec(shape):
    zeros = (0,) * len(shape)
    return pl.BlockSpec(shape, lambda i, *_: zeros)


def _mod_spec(tm):
    return pl.BlockSpec((1, N_MOD, D_MODEL), _seg_map(tm))


def _mod_kernel(c_ref, w_ref, b_ref, o_ref):
    o_ref[0] = _dot3(_silu(c_ref[...]), w_ref[0]) + b_ref[0]


def _modulation_tables(cond, w_mod, b_mod):
    tn = 1536
    return pl.pallas_call(
        _mod_kernel,
        grid=(DEPTH, N_MOD * D_MODEL // tn),
        in_specs=[pl.BlockSpec((SEG_PAD, D_MODEL), lambda l, j: (0, 0)),
                  pl.BlockSpec((1, D_MODEL, tn), lambda l, j: (l, 0, j)),
                  pl.BlockSpec((1, 1, tn), lambda l, j: (l, 0, j))],
        out_specs=pl.BlockSpec((1, SEG_PAD, tn), lambda l, j: (l, 0, j)),
        out_shape=jax.ShapeDtypeStruct((DEPTH, SEG_PAD, N_MOD * D_MODEL), F32),
        compiler_params=_cparams("parallel", "parallel"),
        name="modulation",
    )(cond, w_mod, b_mod.reshape(DEPTH, 1, N_MOD * D_MODEL))


HALO = 8


def _inproj_kernel(x_ref, xp_ref, xn_ref, mod_ref, wz, wx, wu, wv, wdt, convw_ref, convb_ref, dtb_ref, alog_ref,
                   tri_ref, z_ref, xbc_ref, u_ref, v_ref, dt_ref, cs_ref):
    tm = x_ref.shape[0]
    r0 = pl.program_id(0) * tm
    shift = mod_ref[0, 0:1, :]
    scale = mod_ref[0, 1:2, :]
    x_ext = jnp.concatenate([x_ref[...], xp_ref[...], xn_ref[...]], axis=0)
    h_ext = (_rms(x_ext) * (1.0 + scale) + shift).astype(BF16)
    h = h_ext[0:tm, :]

    rel = r0 - N_PROMPT
    in_latent = r0 >= N_PROMPT
    keep_prev = jnp.where(jnp.logical_and(in_latent, lax.rem(rel, DEC_SEQ) != 0), 1.0, 0.0)
    keep_next = jnp.where(jnp.logical_and(in_latent, lax.rem(rel + tm, DEC_SEQ) != 0), 1.0, 0.0)
    reach = CONV_W // 2
    n_blocks = CONV_CH // 256
    others = {1: (z_ref, wz), 3: (u_ref, wu), 5: (v_ref, wv)}
    for cb in range(n_blocks):
        cols = slice(cb * 256, (cb + 1) * 256)
        xa = _dot(h_ext, wx[:, cols])
        win = jnp.concatenate([xa[tm:tm + HALO, :] * keep_prev, xa[0:tm, :],
                               xa[tm + HALO:tm + 2 * HALO, :] * keep_next], axis=0)
        acc = jnp.zeros((tm, 256), F32) + convb_ref[:, cols]
        for k in range(CONV_W):
            start = HALO - reach + k
            acc = acc + win[start:start + tm, :] * convw_ref[k:k + 1, cols]
        xbc_ref[:, cols] = _silu(acc).astype(BF16)
        if cb in others:
            o_ref, w_ref = others[cb]
            o_ref[...] = _dot(h, w_ref[...]).astype(BF16)

    lane = lax.broadcasted_iota(jnp.int32, (1, LANES), 1)
    raw = _dot(h, wdt[...]) + dtb_ref[...]
    dt = jnp.where(lane < 2 * SSD_HEADS, jnp.maximum(raw, 0.0) + jnp.log(1.0 + jnp.exp(-jnp.abs(raw))), 0.0)
    dt_ref[...] = dt
    ac = dt * -jnp.exp(alog_ref[...])
    hi = ac.astype(BF16)
    rest = ac - hi.astype(F32)
    mid = rest.astype(BF16)
    lo = (rest - mid.astype(F32)).astype(BF16)
    tri = tri_ref[...]
    for k in range(tm // CHUNK):
        rows = slice(k * CHUNK, (k + 1) * CHUNK)
        cs_ref[rows, :] = _dot(tri, hi[rows, :]) + _dot(tri, mid[rows, :]) + _dot(tri, lo[rows, :])


def _in_projection(x, mod, weights, consts):
    tm = TM_ROWS
    widths = (D_SSD, CONV_CH, D_SGU, D_SGU, LANES, LANES)
    dtypes = (BF16, BF16, BF16, BF16, F32, F32)
    per_tile = tm // HALO
    last = N_TOK // HALO - 1
    halo_prev = pl.BlockSpec((HALO, D_MODEL), lambda i: (jnp.maximum(i * per_tile - 1, 0), 0))
    halo_next = pl.BlockSpec((HALO, D_MODEL), lambda i: (jnp.minimum((i + 1) * per_tile, last), 0))
    return pl.pallas_call(
        _inproj_kernel,
        grid=(N_TOK // tm,),
        in_specs=[_row_spec(tm, D_MODEL), halo_prev, halo_next, _mod_spec(tm)]
                 + [_const_spec(a.shape) for a in weights + consts],
        out_specs=[_row_spec(tm, w) for w in widths],
        out_shape=[jax.ShapeDtypeStruct((N_TOK, w), dt) for w, dt in zip(widths, dtypes)],
        compiler_params=_cparams("parallel"),
        name="in_projection",
    )(x, x, x, mod, *weights, *consts)


def _ssd_kernel(has_init, n_chunks, *refs):
    refs = list(refs)
    cv_ref, dtv_ref, cs_ref = refs[0:3]
    del refs[0:3]
    init_ref = refs.pop(0) if has_init else None
    alog_ref, dskip_ref, ef_ref, eb_ref, y_ref, fin_ref, st_ref = refs
    a_neg = -jnp.exp(alog_ref[...])
    rows_i = lax.broadcasted_iota(jnp.int32, (CHUNK, CHUNK), 0)
    cols_i = lax.broadcasted_iota(jnp.int32, (CHUNK, CHUNK), 1)
    lower = rows_i >= cols_i
    upper = cols_i >= rows_i
    even_head = jnp.bitwise_and(lax.broadcasted_iota(jnp.int32, (CHUNK, D_SSD), 1), LANES - 1) < SSD_HEADDIM

    def row0(c):
        return pl.multiple_of(c * CHUNK, CHUNK)

    def chunk_pass(c, forward):
        r0 = row0(c)
        rows = pl.ds(r0, CHUNK)
        e_ref = ef_ref if forward else eb_ref
        off = 0 if forward else SSD_HEADS
        dt = dtv_ref[rows, :]
        cs = cs_ref[rows, :]
        total = cs_ref[pl.ds(r0 + CHUNK - 1, 1), :]
        if forward:
            pos = cs
            to_edge = jnp.exp(total - cs)
            from_edge = jnp.exp(cs)
            mask = lower
        else:
            pos = cs - dt * a_neg
            to_edge = jnp.exp(pos)
            from_edge = jnp.exp(total - pos)
            mask = upper
        pos_t = pos.T
        x = cv_ref[rows, 0:D_SSD].astype(F32)
        stacked = jnp.concatenate([dt, dt * to_edge, from_edge], axis=0).astype(BF16)
        spread = _dot(stacked, e_ref[...])
        xc = x * spread[0:CHUNK]
        xc_sub = (jnp.where(even_head, xc, 0.0).astype(BF16), jnp.where(even_head, 0.0, xc).astype(BF16))
        xd_b = (x * spread[CHUNK:2 * CHUNK]).astype(BF16)
        from_e = spread[2 * CHUNK:3 * CHUNK]
        tot_hi, tot_lo = _split(jnp.exp(jnp.broadcast_to(total, (8, LANES))))
        tot_e = (_dot(tot_hi, e_ref[...]) + _dot(tot_lo, e_ref[...]))[0:1, :]
        st = st_ref[0 if forward else 1]
        st_b = st.astype(BF16)
        y_parts = []
        st_parts = []
        for g in range(SSD_GROUPS):
            bm = cv_ref[rows, D_SSD + g * SSD_STATE:D_SSD + (g + 1) * SSD_STATE]
            cm = cv_ref[rows, D_SSD + (SSD_GROUPS + g) * SSD_STATE:D_SSD + (SSD_GROUPS + g + 1) * SSD_STATE]
            bm_t = bm.astype(F32).T.astype(BF16)
            cb = _dot(cm, bm_t)
            gcols = slice(g * 256, (g + 1) * 256)
            y_off = _dot(cm, st_b[:, gcols])
            st_parts.append(_dot(bm_t, xd_b[:, gcols]))
            diag = []
            for pair in range(2):
                pcols = slice(g * 256 + pair * LANES, g * 256 + (pair + 1) * LANES)
                decayed = []
                for sub in range(2):
                    h = g * 4 + pair * 2 + sub
                    col = pos[:, off + h:off + h + 1]
                    row = pos_t[off + h:off + h + 1, :]
                    diff = (col - row) if forward else (row - col)
                    decayed.append((jnp.where(mask, jnp.exp(diff), 0.0) * cb).astype(BF16))
                diag.append(_dot(jnp.concatenate(decayed, axis=1),
                                 jnp.concatenate([xc_sub[0][:, pcols], xc_sub[1][:, pcols]], axis=0)))
            y_parts.append(jnp.concatenate(diag, axis=1) + y_off * from_e[:, gcols])
        y_new = jnp.concatenate(y_parts, axis=1)
        st_ref[0 if forward else 1] = st * tot_e + jnp.concatenate(st_parts, axis=1)
        if forward:
            y_new = y_new + x * dskip_ref[...]
        return rows, y_new

    if has_init:
        st_ref[0] = init_ref[0, 0].T
        st_ref[1] = init_ref[0, 1].T
    else:
        st_ref[...] = jnp.zeros_like(st_ref)

    def first_half(i, carry):
        for rows, y_new in (chunk_pass(i, True), chunk_pass(n_chunks - 1 - i, False)):
            y_ref[rows, :] = y_new
        return carry

    def second_half(i, carry):
        for rows, y_new in (chunk_pass(i, True), chunk_pass(n_chunks - 1 - i, False)):
            y_ref[rows, :] = y_ref[rows, :] + y_new
        return carry

    lax.fori_loop(0, n_chunks // 2, first_half, 0)
    lax.fori_loop(n_chunks // 2, n_chunks, second_half, 0)
    fin_ref[0, 0] = st_ref[0].T
    fin_ref[0, 1] = st_ref[1].T


def _ssd_scan(xbc, dt, cs, init, consts, n_seq, seq, row_block0):
    n_chunks = seq // CHUNK
    assert n_chunks % 2 == 0
    has_init = init is not None
    once = pl.Buffered(1)
    seq_spec = lambda w: pl.BlockSpec((seq, w), lambda b: (row_block0 + b, 0), pipeline_mode=once)
    in_specs = [seq_spec(CONV_CH), seq_spec(LANES), seq_spec(LANES)]
    args = [xbc, dt, cs]
    if has_init:
        in_specs.append(pl.BlockSpec((1, 2, D_SSD, SSD_STATE), lambda b: (b, 0, 0, 0), pipeline_mode=once))
        args.append(init)
    for cst in consts:
        in_specs.append(pl.BlockSpec(cst.shape, lambda b, nd=cst.ndim: (0,) * nd))
        args.append(cst)
    return pl.pallas_call(
        functools.partial(_ssd_kernel, has_init, n_chunks),
        grid=(n_seq,),
        in_specs=in_specs,
        out_specs=[pl.BlockSpec((seq, D_SSD), lambda b: (b, 0)),
                   pl.BlockSpec((1, 2, D_SSD, SSD_STATE), lambda b: (b, 0, 0, 0))],
        out_shape=[jax.ShapeDtypeStruct((n_seq * seq, D_SSD), F32),
                   jax.ShapeDtypeStruct((n_seq, 2, D_SSD, SSD_STATE), F32)],
        scratch_shapes=[pltpu.VMEM((2, SSD_STATE, D_SSD), F32)],
        compiler_params=_cparams("parallel"),
        name="ssd_scan_%d" % seq,
    )(*args)


def _mixer_out_kernel(x_ref, mod_ref, yc_ref, yl_ref, z_ref, u_ref, v_ref, ssdn_ref, sgun_ref, wsp_ref, bsp_ref,
                      wo_ref, o_ref):
    tm = x_ref.shape[0]
    gated = _select_rows(yc_ref, yl_ref) * _silu(z_ref[...].astype(F32))
    a = (_rms(gated) * ssdn_ref[...]).astype(BF16)
    vb = (_rms(v_ref[...].astype(F32)) * sgun_ref[...]).astype(BF16)
    chunks = []
    for k in range(tm // CHUNK):
        rows = slice(k * CHUNK, (k + 1) * CHUNK)
        groups = [_dot(wsp_ref[g], vb[rows, g * SGU_GDIM:(g + 1) * SGU_GDIM]) for g in range(SGU_GROUPS)]
        chunks.append(jnp.concatenate(groups, axis=1) + bsp_ref[...])
    s = (u_ref[...].astype(F32) * jnp.concatenate(chunks, axis=0)).astype(BF16)
    out = _dot(a, wo_ref[0:D_SSD, :]) + _dot(s, wo_ref[D_SSD:D_SSD + D_SGU, :])
    o_ref[...] = x_ref[...] + mod_ref[0, 2:3, :] * out


def _mixer_output(x, mod, y_ctx, y_lat, z, u, v, ssd_norm, sgu_norm, w_sp, b_sp_e, w_out):
    tm = TM_ROWS
    return pl.pallas_call(
        _mixer_out_kernel,
        grid=(N_TOK // tm,),
        in_specs=[_row_spec(tm, D_MODEL), _mod_spec(tm)] + _split_row_specs(tm, D_SSD) + [_row_spec(tm, D_SSD),
                  _row_spec(tm, D_SGU), _row_spec(tm, D_SGU), _const_spec((1, D_SSD)), _const_spec((1, D_SGU)),
                  _const_spec((SGU_GROUPS, CHUNK, CHUNK)), _const_spec((CHUNK, D_SGU)),
                  _const_spec((D_SSD + D_SGU, D_MODEL))],
        out_specs=_row_spec(tm, D_MODEL),
        out_shape=jax.ShapeDtypeStruct((N_TOK, D_MODEL), F32),
        compiler_params=_cparams("parallel"),
        name="mixer_output",
    )(x, mod, y_ctx, y_lat, z, u, v, ssd_norm, sgu_norm, w_sp, b_sp_e, w_out)


def _ffn_kernel(x_ref, mod_ref, wg_ref, wu_ref, wd_ref, o_ref):
    h = _modulated(x_ref, mod_ref, 3).astype(BF16)
    acc = jnp.zeros(o_ref.shape, F32)
    for f in range(D_FF // FF_CHUNK):
        cols = slice(f * FF_CHUNK, (f + 1) * FF_CHUNK)
        act = (_silu(_dot(h, wg_ref[:, cols])) * _dot(h, wu_ref[:, cols])).astype(BF16)
        acc = acc + _dot(act, wd_ref[cols, :])
    o_ref[...] = x_ref[...] + mod_ref[0, 5:6, :] * acc


def _dense_ffn(x, mod, wg, wu, wd):
    tm = TM_FFN
    once = pl.Buffered(1)
    return pl.pallas_call(
        _ffn_kernel,
        grid=(N_TOK // tm,),
        in_specs=[_row_spec(tm, D_MODEL), _mod_spec(tm),
                  pl.BlockSpec((D_MODEL, D_FF), lambda i: (0, 0), pipeline_mode=once),
                  pl.BlockSpec((D_MODEL, D_FF), lambda i: (0, 0), pipeline_mode=once),
                  pl.BlockSpec((D_FF, D_MODEL), lambda i: (0, 0), pipeline_mode=once)],
        out_specs=_row_spec(tm, D_MODEL),
        out_shape=jax.ShapeDtypeStruct((N_TOK, D_MODEL), F32),
        compiler_params=_cparams("parallel"),
        name="dense_ffn",
    )(x, mod, wg, wu, wd)


HEAD_PROJ = 3 * LANES


def _pair_sums(sq_a, sq_b, ones_ref):
    return _dot(jnp.concatenate([sq_a, sq_b], axis=1).astype(BF16), ones_ref[...])


def _expand_keys(ckv_b, pe_sq, pe_rot, wuk_ref, wuv_ref, kn_ref, ones_ref, k_ref, v_ref):
    kn_nope = kn_ref[:, 0:QK_NOPE]
    v_ref[...] = _dot(ckv_b, wuv_ref[...]).astype(BF16)
    k_nope = _dot(ckv_b, wuk_ref[...])
    for pair in range(MLA_HEADS // 2):
        kh = [k_nope[:, h * QK_NOPE:(h + 1) * QK_NOPE] for h in (2 * pair, 2 * pair + 1)]
        ss = _pair_sums(kh[0] * kh[0] + pe_sq, kh[1] * kh[1] + pe_sq, ones_ref)
        r = lax.rsqrt(ss * (1.0 / QK_DIM) + EPS)
        for idx in range(2):
            h = 2 * pair + idx
            rh = r[:, idx * LANES:(idx + 1) * LANES]
            k_ref[:, h * HEAD_PAD:h * HEAD_PAD + QK_NOPE] = (kh[idx] * rh * kn_nope).astype(BF16)
            k_ref[:, h * HEAD_PAD + QK_NOPE:(h + 1) * HEAD_PAD] = (pe_rot * rh).astype(BF16)


def _mla_proj_kernel(x_ref, mod_ref, cos_ref, sin_ref, wdq_ref, wdkv_ref, qan_ref, kvan_ref,
                     wuq_ref, wuk_ref, wuv_ref, qn_ref, kn_ref, ones_ref, q_ref, k_ref, v_ref, ckv_ref, kpe_ref):
    h = _modulated(x_ref, mod_ref, 0).astype(BF16)
    cos, sin = cos_ref[...], sin_ref[...]
    qa = (_rms(_dot(h, wdq_ref[...])) * qan_ref[...]).astype(BF16)
    q = _dot(qa, wuq_ref[...])
    qn_nope = qn_ref[:, 0:LANES]
    qn_rope = qn_ref[:, LANES:2 * LANES] * cos
    qn_part = qn_ref[:, 2 * LANES:3 * LANES] * sin
    scale = QK_DIM ** -0.5
    for pair in range(MLA_HEADS // 2):
        blocks = []
        for hd in (2 * pair, 2 * pair + 1):
            base = hd * HEAD_PROJ
            blocks.append((q[:, base:base + LANES], q[:, base + LANES:base + 2 * LANES],
                           q[:, base + 2 * LANES:base + 3 * LANES]))
        ss = _pair_sums(*[qh * qh + qr * qr for qh, qr, _ in blocks], ones_ref)
        r = lax.rsqrt(ss * (1.0 / QK_DIM) + EPS) * scale
        for idx, (qh, qr, qp) in enumerate(blocks):
            hd = 2 * pair + idx
            rh = r[:, idx * LANES:(idx + 1) * LANES]
            q_ref[:, hd * HEAD_PAD:hd * HEAD_PAD + QK_NOPE] = (qh * rh * qn_nope).astype(BF16)
            q_ref[:, hd * HEAD_PAD + QK_NOPE:(hd + 1) * HEAD_PAD] = ((qr * qn_rope + qp * qn_part) * rh).astype(BF16)
    kva = _dot(h, wdkv_ref[...])
    ckv = _rms(kva[:, 0:KV_RANK]) * kvan_ref[...]
    kpe = kva[:, KV_RANK:KV_RANK + LANES]
    kpe_part = kva[:, KV_RANK + LANES:KV_RANK + 2 * LANES]
    ckv_ref[...] = ckv
    kpe_ref[...] = kpe
    pe_rot = kpe * (kn_ref[:, LANES:2 * LANES] * cos) + kpe_part * (kn_ref[:, 2 * LANES:3 * LANES] * sin)
    _expand_keys(ckv.astype(BF16), kpe * kpe, pe_rot, wuk_ref, wuv_ref, kn_ref, ones_ref, k_ref, v_ref)


def _mla_projection(x, mod, rope_tabs, wdq, wdkv, qan, kvan, wuq, wuk, wuv, qn, kn, ones):
    tm = TM_ROWS
    hw = MLA_HEADS * HEAD_PAD
    out_w = (hw, hw, MLA_HEADS * V_HEAD, KV_RANK, LANES)
    out_dt = (BF16, BF16, BF16, F32, F32)
    consts = (wdq, wdkv, qan, kvan, wuq, wuk, wuv, qn, kn, ones)
    return pl.pallas_call(
        _mla_proj_kernel,
        grid=(N_TOK // tm,),
        in_specs=[_row_spec(tm, D_MODEL), _mod_spec(tm)] + [_row_spec(tm, LANES)] * 2
                 + [_const_spec(cst.shape) for cst in consts],
        out_specs=[_row_spec(tm, w) for w in out_w],
        out_shape=[jax.ShapeDtypeStruct((N_TOK, w), dt) for w, dt in zip(out_w, out_dt)],
        compiler_params=_cparams("parallel"),
        name="mla_projection",
    )(x, mod, *rope_tabs, *consts)


def _cache_kv_kernel(ckv_ref, kpe_ref, wuk_ref, wuv_ref, kn_ref, ones_ref, k_ref, v_ref):
    kpe = kpe_ref[...]
    _expand_keys(ckv_ref[...].astype(BF16), kpe * kpe, kpe * kn_ref[:, LANES:2 * LANES],
                 wuk_ref, wuv_ref, kn_ref, ones_ref, k_ref, v_ref)


def _cache_keys(ckv, kpe, wuk, wuv, kn, ones):
    rows = ckv.shape[0]
    tm = TM_ROWS
    consts = (wuk, wuv, kn, ones)
    return pl.pallas_call(
        _cache_kv_kernel,
        grid=(rows // tm,),
        in_specs=[_row_spec(tm, KV_RANK), _row_spec(tm, LANES)] + [_const_spec(cst.shape) for cst in consts],
        out_specs=[_row_spec(tm, MLA_HEADS * HEAD_PAD), _row_spec(tm, MLA_HEADS * V_HEAD)],
        out_shape=[jax.ShapeDtypeStruct((rows, MLA_HEADS * HEAD_PAD), BF16),
                   jax.ShapeDtypeStruct((rows, MLA_HEADS * V_HEAD), BF16)],
        compiler_params=_cparams("parallel"),
        name="cache_keys",
    )(ckv, kpe, *consts)


def _attn_kernel(has_cache, tq, n_par, *refs):
    if has_cache:
        q_ref, k_ref, v_ref, kc_ref, vc_ref, o_ref = refs
    else:
        q_ref, k_ref, v_ref, o_ref = refs

    def tile(r0):
        q = q_ref[pl.ds(r0, tq), :]
        s = _dot_nt(q, k_ref[...])
        m = jnp.max(s, axis=-1, keepdims=True)
        if has_cache:
            sc = _dot_nt(q, kc_ref[...])
            m = jnp.maximum(m, jnp.max(sc, axis=-1, keepdims=True))
            pc = jnp.exp(sc - m)
        p = jnp.exp(s - m)
        den = jnp.sum(p, axis=-1, keepdims=True)
        num = _dot(p.astype(BF16), v_ref[...])
        if has_cache:
            den = den + jnp.sum(pc, axis=-1, keepdims=True)
            num = num + _dot(pc.astype(BF16), vc_ref[...])
        o_ref[pl.ds(r0, tq), :] = (num / den).astype(BF16)

    group = n_par * tq
    n_groups = q_ref.shape[0] // group
    if n_groups == 1:
        for t in range(n_par):
            tile(t * tq)
    else:
        def body(i, carry):
            r0 = pl.multiple_of(i * group, group)
            for t in range(n_par):
                tile(r0 + t * tq)
            return carry
        lax.fori_loop(0, n_groups, body, 0)


def _attention(q, k, v, kc, vc, n_seq, seq, row_block0):
    has_cache = kc is not None
    tq, n_par = (TQ, 2) if seq >= 2 * TQ else (seq // 2, 2)
    seq_spec = lambda w: pl.BlockSpec((seq, w), lambda b, h: (row_block0 + b, h))
    in_specs = [seq_spec(HEAD_PAD), seq_spec(HEAD_PAD), seq_spec(V_HEAD)]
    args = [q, k, v]
    if has_cache:
        past = kc.shape[0] // n_seq
        in_specs += [pl.BlockSpec((past, HEAD_PAD), lambda b, h: (b, h)),
                     pl.BlockSpec((past, V_HEAD), lambda b, h: (b, h))]
        args += [kc, vc]
    return pl.pallas_call(
        functools.partial(_attn_kernel, has_cache, tq, n_par),
        grid=(n_seq, MLA_HEADS),
        in_specs=in_specs,
        out_specs=pl.BlockSpec((seq, V_HEAD), lambda b, h: (b, h)),
        out_shape=jax.ShapeDtypeStruct((n_seq * seq, MLA_HEADS * V_HEAD), BF16),
        compiler_params=_cparams("parallel", "parallel"),
        name="attention_%d" % seq,
    )(*args)


def _split_row_specs(tm, width):
    n_ctx = N_PROMPT // tm
    return [pl.BlockSpec((tm, width), lambda i, *_: (jnp.minimum(i, n_ctx - 1), 0)),
            pl.BlockSpec((tm, width), lambda i, *_: (jnp.maximum(i - n_ctx, 0), 0))]


def _select_rows(ctx_ref, lat_ref):
    tm = ctx_ref.shape[0]
    return jnp.where(pl.program_id(0) < N_PROMPT // tm, ctx_ref[...], lat_ref[...])


def _residual_proj_kernel(gate_row, x_ref, mod_ref, ac_ref, al_ref, w_ref, o_ref):
    a = _select_rows(ac_ref, al_ref)
    o_ref[...] = x_ref[...] + mod_ref[0, gate_row:gate_row + 1, :] * _dot(a, w_ref[...])


def _residual_projection(x, mod, a_ctx, a_lat, w, gate_row):
    tm = TM_FFN
    k = a_ctx.shape[1]
    return pl.pallas_call(
        functools.partial(_residual_proj_kernel, gate_row),
        grid=(N_TOK // tm,),
        in_specs=[_row_spec(tm, D_MODEL), _mod_spec(tm)] + _split_row_specs(tm, k) + [_const_spec((k, D_MODEL))],
        out_specs=_row_spec(tm, D_MODEL),
        out_shape=jax.ShapeDtypeStruct((N_TOK, D_MODEL), F32),
        compiler_params=_cparams("parallel"),
        name="residual_projection",
    )(x, mod, a_ctx, a_lat, w)


def _router_kernel(x_ref, mod_ref, wr_ref, tri_ref, h_ref, meta_ref, cnt_ref, run_ref):
    @pl.when(pl.program_id(0) == 0)
    def _():
        run_ref[...] = jnp.zeros_like(run_ref)

    h = _modulated(x_ref, mod_ref, 3)
    h_ref[...] = h.astype(BF16)
    lane = lax.broadcasted_iota(jnp.int32, (x_ref.shape[0], LANES), 1)
    lane_f = lane.astype(F32)
    logits = jnp.where(lane < N_EXPERTS, _dot3(h, wr_ref[...]), -jnp.inf)
    m1 = jnp.max(logits, axis=-1, keepdims=True)
    e1 = jnp.min(jnp.where(logits == m1, lane_f, float(LANES)), axis=-1, keepdims=True)
    rest = jnp.where(lane_f == e1, -jnp.inf, logits)
    m2 = jnp.max(rest, axis=-1, keepdims=True)
    e2 = jnp.min(jnp.where(rest == m2, lane_f, float(LANES)), axis=-1, keepdims=True)
    t = jnp.exp(m2 - m1)
    g1 = 1.0 / (1.0 + t)
    pick1 = lane_f == e1
    pick2 = lane_f == e2
    onehot = jnp.where(pick1, 1.0, jnp.where(pick2, 1.0, 0.0))
    before = run_ref[...] + _dot(tri_ref[...], onehot.astype(BF16)) - onehot
    rank1 = jnp.sum(jnp.where(pick1, before, 0.0), axis=-1, keepdims=True)
    rank2 = jnp.sum(jnp.where(pick2, before, 0.0), axis=-1, keepdims=True)
    run_ref[...] += jnp.sum(onehot, axis=0, keepdims=True)
    cnt_ref[...] = jnp.broadcast_to(run_ref[...], cnt_ref.shape)
    cols = (e1, e2, g1, 1.0 - g1, rank1, rank2)
    meta = jnp.zeros(meta_ref.shape, F32)
    for idx, val in enumerate(cols):
        meta = jnp.where(lane == idx, val, meta)
    meta_ref[...] = meta


META_EXPERT, META_GATE, META_RANK = 0, 2, 4


def _router(x, mod, wr):
    tm = TM_ROWS
    r = jnp.arange(tm)
    tri = (r[:, None] >= r[None, :]).astype(BF16)
    return pl.pallas_call(
        _router_kernel,
        grid=(N_TOK // tm,),
        in_specs=[_row_spec(tm, D_MODEL), _mod_spec(tm), _const_spec((D_MODEL, LANES)), _const_spec((tm, tm))],
        out_specs=[_row_spec(tm, D_MODEL), _row_spec(tm, LANES), _const_spec((SEG_PAD, LANES))],
        out_shape=[jax.ShapeDtypeStruct((N_TOK, D_MODEL), BF16), jax.ShapeDtypeStruct((N_TOK, LANES), F32),
                   jax.ShapeDtypeStruct((SEG_PAD, LANES), F32)],
        scratch_shapes=[pltpu.VMEM((1, LANES), F32)],
        compiler_params=_cparams("arbitrary"),
        name="router",
    )(x, mod, wr, tri)


TF_MOE = D_FF // 2


def _experts_kernel(be_ref, bv_ref, nu_ref, rows_ref, wg_ref, wu_ref, wd_ref, o_ref, acc_ref):
    i = pl.program_id(0)
    f = pl.program_id(1)

    @pl.when(bv_ref[i] > 0)
    def _():
        rows = rows_ref[...]
        acc = jnp.zeros(acc_ref.shape, F32)
        for c0 in range(0, TF_MOE, FF_CHUNK):
            cols = slice(c0, min(c0 + FF_CHUNK, TF_MOE))
            gate = _dot(rows, wg_ref[0, 0, :, cols].astype(BF16))
            up = _dot(rows, wu_ref[0, 0, :, cols].astype(BF16))
            acc = acc + _dot((_silu(gate) * up).astype(BF16), wd_ref[0, 0, cols, :].astype(BF16))

        @pl.when(f == 0)
        def _():
            acc_ref[...] = acc

        @pl.when(f > 0)
        def _():
            acc_ref[...] += acc

    @pl.when(f == pl.num_programs(1) - 1)
    def _():
        o_ref[...] = acc_ref[...].astype(BF16)


def _expert_ffn(rows, block_expert, block_valid, n_used, wg, wu, wd, layer):
    n_rows = rows.shape[0]
    tm = TM_MOE
    tf = TF_MOE
    n_f = D_FF // tf

    def f_idx(i, f, nu):
        return jnp.where(i < nu[0], f, n_f - 1)

    grid_spec = pltpu.PrefetchScalarGridSpec(
        num_scalar_prefetch=3,
        grid=(n_rows // tm, n_f),
        in_specs=[pl.BlockSpec((tm, D_MODEL), lambda i, f, be, bv, nu: (i, 0)),
                  pl.BlockSpec((1, 1, D_MODEL, tf), lambda i, f, be, bv, nu: (layer, be[i], 0, f_idx(i, f, nu))),
                  pl.BlockSpec((1, 1, D_MODEL, tf), lambda i, f, be, bv, nu: (layer, be[i], 0, f_idx(i, f, nu))),
                  pl.BlockSpec((1, 1, tf, D_MODEL), lambda i, f, be, bv, nu: (layer, be[i], f_idx(i, f, nu), 0))],
        out_specs=pl.BlockSpec((tm, D_MODEL), lambda i, f, be, bv, nu: (i, 0)),
        scratch_shapes=[pltpu.VMEM((tm, D_MODEL), F32)])
    return pl.pallas_call(
        _experts_kernel,
        grid_spec=grid_spec,
        out_shape=jax.ShapeDtypeStruct((n_rows, D_MODEL), BF16),
        compiler_params=_cparams("arbitrary", "arbitrary"),
        name="expert_ffn",
    )(block_expert, block_valid, n_used, rows, wg, wu, wd)


def _combine_kernel(x_ref, mod_ref, meta_ref, a_ref, b_ref, o_ref):
    g1 = meta_ref[:, META_GATE:META_GATE + 1]
    g2 = meta_ref[:, META_GATE + 1:META_GATE + 2]
    y = a_ref[...].astype(F32) * g1 + b_ref[...].astype(F32) * g2
    o_ref[...] = x_ref[...] + mod_ref[0, 5:6, :] * y


def _moe_combine(x, mod, meta, a, b):
    tm = TM_FFN
    return pl.pallas_call(
        _combine_kernel,
        grid=(N_TOK // tm,),
        in_specs=[_row_spec(tm, D_MODEL), _mod_spec(tm), _row_spec(tm, LANES), _row_spec(tm, D_MODEL),
                  _row_spec(tm, D_MODEL)],
        out_specs=_row_spec(tm, D_MODEL),
        out_shape=jax.ShapeDtypeStruct((N_TOK, D_MODEL), F32),
        compiler_params=_cparams("parallel"),
        name="moe_combine",
    )(x, mod, meta, a, b)


def _moe(x, mod, wr, wg, wu, wd, layer):
    h, meta, counts = _router(x, mod, wr)
    tm = TM_MOE
    n_assign = N_TOK * TOP_K
    experts = meta[:, META_EXPERT:META_EXPERT + TOP_K].astype(jnp.int32)
    rank = meta[:, META_RANK:META_RANK + TOP_K].astype(jnp.int32)
    count = counts[0, :N_EXPERTS].astype(jnp.int32)
    padded = (count + tm - 1) // tm * tm
    pad_end = jnp.cumsum(padded)
    pad_start = pad_end - padded
    onehot = experts[:, :, None] == jnp.arange(N_EXPERTS, dtype=jnp.int32)
    dest = jnp.sum(jnp.where(onehot, pad_start, 0), axis=-1) + rank
    n_rows = n_assign + N_EXPERTS * tm
    n_blocks = n_rows // tm
    row_token = jnp.zeros((n_rows,), jnp.int32).at[dest.reshape(-1)].set(
        jnp.arange(n_assign, dtype=jnp.int32) // TOP_K)
    n_used = (pad_end[-1] // tm).astype(jnp.int32).reshape(1)
    block_id = jnp.arange(n_blocks, dtype=jnp.int32)
    block_start = jnp.minimum(block_id, n_used[0] - 1) * tm
    block_expert = jnp.minimum(jnp.sum((pad_end[None, :] <= block_start[:, None]).astype(jnp.int32), axis=1),
                               N_EXPERTS - 1)
    last_row = jnp.sum(jnp.where(block_expert[:, None] == jnp.arange(N_EXPERTS, dtype=jnp.int32),
                                 pad_start + count, 0), axis=1)
    block_valid = jnp.where(block_id < n_used[0], jnp.clip(last_row - block_start, 0, tm), 0)
    out = _expert_ffn(h[row_token], block_expert, block_valid, n_used, wg, wu, wd, layer)
    return _moe_combine(x, mod, meta, out[dest[:, 0]], out[dest[:, 1]])


def _rope_tables():
    half = ROPE_AXIS // 2
    pos = jnp.arange(DEC_SEQ)
    row = (pos // GRID_W).astype(F32)
    col = (pos % GRID_W).astype(F32)
    inv = ROPE_THETA ** (-jnp.arange(0, ROPE_AXIS, 2, dtype=F32) / ROPE_AXIS)
    ang_r = row[:, None] * inv
    ang_c = col[:, None] * inv
    pad = jnp.zeros((DEC_SEQ, LANES - QK_ROPE), F32)
    cos = jnp.concatenate([jnp.cos(ang_r), jnp.cos(ang_r), jnp.cos(ang_c), jnp.cos(ang_c), pad + 1.0], axis=1)
    sin = jnp.concatenate([-jnp.sin(ang_r), jnp.sin(ang_r), -jnp.sin(ang_c), jnp.sin(ang_c), pad], axis=1)
    prompt = jnp.zeros((N_PROMPT, LANES), F32)
    tile = lambda t: jnp.tile(t, (DEC_BATCH, 1))
    return (jnp.concatenate([prompt + 1.0, tile(cos)], axis=0),
            jnp.concatenate([prompt, tile(sin)], axis=0))


def _rope_blocks(a):
    half = ROPE_AXIS // 2
    partner = jnp.concatenate([a[..., half:2 * half], a[..., 0:half], a[..., 3 * half:4 * half],
                               a[..., 2 * half:3 * half]], axis=-1)
    return jnp.concatenate([_pad_lanes(a, LANES), _pad_lanes(partner, LANES)], axis=-1)


def _pad_lanes(a, width):
    return jnp.pad(a, [(0, 0)] * (a.ndim - 1) + [(0, width - a.shape[-1])])


def _ssd_constants():
    r = jnp.arange(CHUNK)
    tri = (r[:, None] >= r[None, :]).astype(BF16)
    head_of = jnp.arange(D_SSD) // SSD_HEADDIM
    e_f = (r[:, None] == head_of[None, :]).astype(BF16)
    e_b = (r[:, None] == head_of[None, :] + SSD_HEADS).astype(BF16)
    return tri, e_f, e_b


def kernel(x_prompt, x_sample, c, state_ssm, cache_ckv, cache_kpe, c_ctx, w_mod, b_mod, w_in, conv_w, conv_b, dt_bias, a_log, d_skip, ssd_norm, sgu_norm, w_sp, b_sp, w_out, ffn_w_gate, ffn_w_up, ffn_w_down, w_dq, q_a_norm, w_uq, w_dkv, kv_a_norm, w_ukv, q_norm, k_norm, w_o, router, moe_w_gate, moe_w_up, moe_w_down):
    x = jnp.concatenate([x_prompt.reshape(N_PROMPT, D_MODEL), x_sample.reshape(N_SAMPLE, D_MODEL)], axis=0)
    cond = jnp.concatenate([c_ctx[None, :], c, jnp.zeros((SEG_PAD - N_SEG, D_MODEL), F32)], axis=0)
    mods = _modulation_tables(cond, w_mod, b_mod).reshape(DEPTH, SEG_PAD, N_MOD, D_MODEL)
    rope_tabs = _rope_tables()
    tri, e_f, e_b = _ssd_constants()
    blk = jnp.arange(2 * LANES) // LANES
    pair_ones = (blk[:, None] == blk[None, :]).astype(BF16)
    i1 = D_SSD
    i2 = i1 + CONV_CH
    i3 = i2 + 2 * SSD_HEADS
    i4 = i3 + D_SGU
    new_ssm, new_ckv, new_kpe = [], [], []
    for i in range(DEPTH):
        j = i // 2
        mod = mods[i]
        if i % 2 == 0:
            w = w_in[j]
            wz, wx, wu, wv = (w[:, 0:i1].astype(BF16), w[:, i1:i2].astype(BF16),
                              w[:, i3:i4].astype(BF16), w[:, i4:].astype(BF16))
            wdt = _pad_lanes(w[:, i2:i3], LANES).astype(BF16)
            a_log_row = _pad_lanes(a_log[j].reshape(1, -1), LANES)
            z, xbc, u, v, dt, cs = _in_projection(
                x, mod, (wz, wx, wu, wv, wdt),
                (conv_w[j], conv_b[j][None, :], _pad_lanes(dt_bias[j].reshape(1, -1), LANES), a_log_row, tri))
            consts = (a_log_row, jnp.repeat(d_skip[j], SSD_HEADDIM)[None, :], e_f, e_b)
            y_ctx, st_p = _ssd_scan(xbc, dt, cs, None, consts, BATCH, SEQ, 0)
            init = state_ssm[:, j].reshape(DEC_BATCH, 2, D_SSD, SSD_STATE)
            y_lat, _ = _ssd_scan(xbc, dt, cs, init, consts, DEC_BATCH, DEC_SEQ, N_PROMPT // DEC_SEQ)
            new_ssm.append(st_p.reshape(BATCH, 2, SSD_HEADS, SSD_HEADDIM, SSD_STATE))
            b_sp_e = jnp.repeat(b_sp[j].T, SGU_GDIM, axis=1)
            x = _mixer_output(x, mod, y_ctx, y_lat, z, u, v, ssd_norm[j][None, :], sgu_norm[j][None, :],
                              w_sp[j].astype(BF16), b_sp_e, w_out[j].astype(BF16))
            x = _dense_ffn(x, mod, ffn_w_gate[j].astype(BF16), ffn_w_up[j].astype(BF16),
                           ffn_w_down[j].astype(BF16))
        else:
            split = lambda a: jnp.concatenate([a[..., :QK_NOPE], _rope_blocks(a[..., QK_NOPE:])], axis=-1)
            wuq = split(w_uq[j].reshape(Q_RANK, MLA_HEADS, QK_DIM)).reshape(Q_RANK, MLA_HEADS * HEAD_PROJ)
            wdkv = jnp.concatenate([w_dkv[j][:, :KV_RANK], _rope_blocks(w_dkv[j][:, KV_RANK:])], axis=-1)
            wukv = w_ukv[j].reshape(KV_RANK, MLA_HEADS, QK_NOPE + V_HEAD)
            wuk = wukv[:, :, :QK_NOPE].reshape(KV_RANK, -1).astype(BF16)
            wuv = wukv[:, :, QK_NOPE:].reshape(KV_RANK, -1).astype(BF16)
            qn = split(q_norm[j][None, :])
            kn = split(k_norm[j][None, :])
            q, k, v, ckv, kpe = _mla_projection(
                x, mod, rope_tabs, w_dq[j].astype(BF16), wdkv.astype(BF16), q_a_norm[j][None, :],
                kv_a_norm[j][None, :], wuq.astype(BF16), wuk, wuv, qn, kn, pair_ones)
            new_ckv.append(ckv[:N_PROMPT].reshape(BATCH, SEQ, KV_RANK))
            new_kpe.append(kpe[:N_PROMPT, :QK_ROPE].reshape(BATCH, SEQ, QK_ROPE))
            kc, vc = _cache_keys(cache_ckv[:, j].reshape(DEC_BATCH * PAST_LEN, KV_RANK),
                                 _pad_lanes(cache_kpe[:, j].reshape(DEC_BATCH * PAST_LEN, QK_ROPE), LANES),
                                 wuk, wuv, kn, pair_ones)
            o_ctx = _attention(q, k, v, None, None, BATCH, SEQ, 0)
            o_lat = _attention(q, k, v, kc, vc, DEC_BATCH, DEC_SEQ, N_PROMPT // DEC_SEQ)
            x = _residual_projection(x, mod, o_ctx, o_lat, w_o[j].astype(BF16), 2)
            x = _moe(x, mod, _pad_lanes(router[j], LANES), moe_w_gate, moe_w_up, moe_w_down, j)
    return (x[:N_PROMPT].reshape(BATCH, SEQ, D_MODEL),
            x[N_PROMPT:].reshape(DEC_BATCH, DEC_SEQ, D_MODEL),
            jnp.stack(new_ssm, axis=1),
            jnp.stack(new_ckv, axis=1),
            jnp.stack(new_kpe, axis=1))
```

```python
import functools
import math

import jax
import jax.numpy as jnp
from jax import lax
from jax.experimental import pallas as pl
from jax.experimental.pallas import tpu as pltpu

F32 = jnp.float32
BF16 = jnp.bfloat16

D_MODEL = 1024
BATCH = 16
SEQ = 256
DEPTH = 4
DEC_BATCH = 4
DEC_SEQ = 2048
PAST_LEN = 512
GRID_W = 64
N_MOD = 6
EPS = 1e-6

SSD_HEADDIM = 64
SSD_HEADS = 16
D_SSD = 1024
SSD_GROUPS = 4
SSD_STATE = 128
CHUNK = 128
CONV_W = 5
CONV_CH = D_SSD + 2 * SSD_GROUPS * SSD_STATE
D_SGU = 1024
SGU_GROUPS = 4
SGU_GDIM = D_SGU // SGU_GROUPS

MLA_HEADS = 8
Q_RANK = 384
KV_RANK = 256
QK_NOPE = 128
QK_ROPE = 64
V_HEAD = 128
QK_DIM = QK_NOPE + QK_ROPE
ROPE_AXIS = QK_ROPE // 2
ROPE_THETA = 10000.0
HEAD_PAD = 256

D_FF = 2816
N_EXPERTS = 8
TOP_K = 2

N_PROMPT = BATCH * SEQ
N_SAMPLE = DEC_BATCH * DEC_SEQ
N_TOK = N_PROMPT + N_SAMPLE
N_SEG = 1 + DEC_BATCH
SEG_PAD = 8

LANES = 128
VMEM_LIMIT = 56 * 1024 * 1024

TM_ROWS = 256
TM_FFN = 512
FF_CHUNK = 256
TM_MOE = 1024
TQ = 256


def _cparams(*sem):
    return pltpu.CompilerParams(dimension_semantics=sem, vmem_limit_bytes=VMEM_LIMIT)


def _dot(a, b):
    return jnp.dot(a, b, preferred_element_type=F32)


def _dot_nt(a, b):
    return lax.dot_general(a, b, (((1,), (1,)), ((), ())), preferred_element_type=F32)


def _split(x):
    hi = x.astype(BF16)
    lo = (x - hi.astype(F32)).astype(BF16)
    return hi, lo


def _dot3(a, b):
    ah, al = _split(a)
    bh, bl = _split(b)
    return _dot(ah, bh) + _dot(ah, bl) + _dot(al, bh)


def _silu(x):
    return x / (1.0 + jnp.exp(-x))


def _rms(x):
    return x * lax.rsqrt(jnp.mean(x * x, axis=-1, keepdims=True) + EPS)


def _modulated(x_ref, mod_ref, first):
    shift = mod_ref[0, first:first + 1, :]
    scale = mod_ref[0, first + 1:first + 2, :]
    return _rms(x_ref[...]) * (1.0 + scale) + shift


def _seg_map(tm):
    def index_map(i, *_):
        r = i * tm
        return (jnp.where(r < N_PROMPT, 0, 1 + (r - N_PROMPT) // DEC_SEQ), 0, 0)
    return index_map


def _row_spec(tm, width):
    return pl.BlockSpec((tm, width), lambda i, *_: (i, 0))


def _const_spec(shape):
    zeros = (0,) * len(shape)
    return pl.BlockSpec(shape, lambda i, *_: zeros)


def _mod_spec(tm):
    return pl.BlockSpec((1, N_MOD, D_MODEL), _seg_map(tm))


def _mod_kernel(c_ref, w_ref, b_ref, o_ref):
    o_ref[0] = _dot3(_silu(c_ref[...]), w_ref[0]) + b_ref[0]


def _modulation_tables(cond, w_mod, b_mod):
    tn = 1536
    return pl.pallas_call(
        _mod_kernel,
        grid=(DEPTH, N_MOD * D_MODEL // tn),
        in_specs=[pl.BlockSpec((SEG_PAD, D_MODEL), lambda l, j: (0, 0)),
                  pl.BlockSpec((1, D_MODEL, tn), lambda l, j: (l, 0, j)),
                  pl.BlockSpec((1, 1, tn), lambda l, j: (l, 0, j))],
        out_specs=pl.BlockSpec((1, SEG_PAD, tn), lambda l, j: (l, 0, j)),
        out_shape=jax.ShapeDtypeStruct((DEPTH, SEG_PAD, N_MOD * D_MODEL), F32),
        compiler_params=_cparams("parallel", "parallel"),
        name="modulation",
    )(cond, w_mod, b_mod.reshape(DEPTH, 1, N_MOD * D_MODEL))


HALO = 8


XBC_BLOCK = 512
N_XBC = CONV_CH // XBC_BLOCK
GATE_BLOCK = 256
N_GATE = (D_SSD + 2 * D_SGU) // GATE_BLOCK
GATE_PER_STAGE = N_GATE // N_XBC


def _inproj_kernel(x_ref, xp_ref, xn_ref, mod_ref, wx_ref, wg_ref, wdt, convw_ref, convb_ref, dtb_ref, alog_ref,
                   tri_ref, xbc_ref, gate_ref, dt_ref, cs_ref, h_ref, xa_ref, xb_ref):
    tm = x_ref.shape[0]
    r0 = pl.program_id(0) * tm
    shift = mod_ref[0, 0:1, :]
    scale = mod_ref[0, 1:2, :]
    x_ext = jnp.concatenate([x_ref[...], xp_ref[...], xn_ref[...]], axis=0)
    h_ref[...] = (_rms(x_ext) * (1.0 + scale) + shift).astype(BF16)

    rel = r0 - N_PROMPT
    in_latent = r0 >= N_PROMPT
    keep_prev = jnp.where(jnp.logical_and(in_latent, lax.rem(rel, DEC_SEQ) != 0), 1.0, 0.0)
    keep_next = jnp.where(jnp.logical_and(in_latent, lax.rem(rel + tm, DEC_SEQ) != 0), 1.0, 0.0)
    reach = CONV_W // 2

    bufs = (xa_ref, xb_ref)
    bufs[0][...] = _dot(h_ref[...], wx_ref[0])

    def stage(s, cur_ref, nxt_ref):
        if s + 1 < N_XBC:
            nxt_ref[...] = _dot(h_ref[...], wx_ref[s + 1])
        for t in range(GATE_PER_STAGE):
            g = s * GATE_PER_STAGE + t
            gate_ref[g] = _dot(h_ref[0:tm, :], wg_ref[g]).astype(BF16)
        for half in range(XBC_BLOCK // 256):
            cols = slice(half * 256, (half + 1) * 256)
            win = jnp.concatenate([cur_ref[tm:tm + HALO, cols] * keep_prev, cur_ref[0:tm, cols],
                                   cur_ref[tm + HALO:tm + 2 * HALO, cols] * keep_next], axis=0)
            acc = jnp.zeros((tm, 256), F32) + convb_ref[s, :, cols]
            for k in range(CONV_W):
                start = HALO - reach + k
                acc = acc + win[start:start + tm, :] * convw_ref[s, k:k + 1, cols]
            xbc_ref[s, :, cols] = _silu(acc).astype(BF16)

    for s in range(N_XBC):
        stage(s, bufs[s % 2], bufs[1 - s % 2])

    h = h_ref[0:tm, :]
    lane = lax.broadcasted_iota(jnp.int32, (1, LANES), 1)
    raw = _dot(h, wdt[...]) + dtb_ref[...]
    dt = jnp.where(lane < 2 * SSD_HEADS, jnp.maximum(raw, 0.0) + jnp.log(1.0 + jnp.exp(-jnp.abs(raw))), 0.0)
    dt_ref[...] = dt
    ac = dt * -jnp.exp(alog_ref[...])
    hi = ac.astype(BF16)
    rest = ac - hi.astype(F32)
    mid = rest.astype(BF16)
    lo = (rest - mid.astype(F32)).astype(BF16)
    tri = tri_ref[...]
    for k in range(tm // CHUNK):
        rows = slice(k * CHUNK, (k + 1) * CHUNK)
        cs_ref[rows, :] = _dot(tri, hi[rows, :]) + _dot(tri, mid[rows, :]) + _dot(tri, lo[rows, :])


def _col_blocks(w, block):
    return w.reshape(w.shape[0], -1, block).transpose(1, 0, 2)


def _in_projection(x, mod, weights, consts):
    tm = TM_ROWS
    per_tile = tm // HALO
    last = N_TOK // HALO - 1
    halo_prev = pl.BlockSpec((HALO, D_MODEL), lambda i: (jnp.maximum(i * per_tile - 1, 0), 0))
    halo_next = pl.BlockSpec((HALO, D_MODEL), lambda i: (jnp.minimum((i + 1) * per_tile, last), 0))
    blocked = lambda n, w: pl.BlockSpec((n, tm, w), lambda i: (0, i, 0))
    return pl.pallas_call(
        _inproj_kernel,
        grid=(N_TOK // tm,),
        in_specs=[_row_spec(tm, D_MODEL), halo_prev, halo_next, _mod_spec(tm)]
                 + [_const_spec(a.shape) for a in weights + consts],
        out_specs=[blocked(N_XBC, XBC_BLOCK), blocked(N_GATE, GATE_BLOCK), _row_spec(tm, LANES),
                   _row_spec(tm, LANES)],
        out_shape=[jax.ShapeDtypeStruct((N_XBC, N_TOK, XBC_BLOCK), BF16),
                   jax.ShapeDtypeStruct((N_GATE, N_TOK, GATE_BLOCK), BF16),
                   jax.ShapeDtypeStruct((N_TOK, LANES), F32), jax.ShapeDtypeStruct((N_TOK, LANES), F32)],
        scratch_shapes=[pltpu.VMEM((tm + 2 * HALO, D_MODEL), BF16),
                        pltpu.VMEM((tm + 2 * HALO, XBC_BLOCK), F32),
                        pltpu.VMEM((tm + 2 * HALO, XBC_BLOCK), F32)],
        compiler_params=_cparams("parallel"),
        name="in_projection",
    )(x, x, x, mod, *weights, *consts)


def _ssd_kernel(has_init, n_chunks, *refs):
    refs = list(refs)
    cv_ref, dtv_ref, cs_ref = refs[0:3]
    del refs[0:3]
    init_ref = refs.pop(0) if has_init else None
    alog_ref, dskip_ref, ef_ref, eb_ref, y_ref, fin_ref, st_ref = refs
    a_neg = -jnp.exp(alog_ref[...])
    rows_i = lax.broadcasted_iota(jnp.int32, (CHUNK, CHUNK), 0)
    cols_i = lax.broadcasted_iota(jnp.int32, (CHUNK, CHUNK), 1)
    lower = rows_i >= cols_i
    upper = cols_i >= rows_i
    even_head = jnp.bitwise_and(lax.broadcasted_iota(jnp.int32, (CHUNK, D_SSD), 1), LANES - 1) < SSD_HEADDIM

    assert SSD_GROUPS * SSD_STATE == XBC_BLOCK
    b_block = D_SSD // XBC_BLOCK

    def row0(c):
        return pl.multiple_of(c * CHUNK, CHUNK)

    def chunk_pass(c, forward):
        r0 = row0(c)
        rows = pl.ds(r0, CHUNK)
        e_ref = ef_ref if forward else eb_ref
        off = 0 if forward else SSD_HEADS
        dt = dtv_ref[rows, :]
        cs = cs_ref[rows, :]
        total = cs_ref[pl.ds(r0 + CHUNK - 1, 1), :]
        if forward:
            pos = cs
            to_edge = jnp.exp(total - cs)
            from_edge = jnp.exp(cs)
            mask = lower
        else:
            pos = cs - dt * a_neg
            to_edge = jnp.exp(pos)
            from_edge = jnp.exp(total - pos)
            mask = upper
        pos_t = pos.T
        x = jnp.concatenate([cv_ref[b, rows, :] for b in range(D_SSD // XBC_BLOCK)], axis=1).astype(F32)
        stacked = jnp.concatenate([dt, dt * to_edge, from_edge], axis=0).astype(BF16)
        spread = _dot(stacked, e_ref[...])
        xc = x * spread[0:CHUNK]
        xc_sub = (jnp.where(even_head, xc, 0.0).astype(BF16), jnp.where(even_head, 0.0, xc).astype(BF16))
        xd_b = (x * spread[CHUNK:2 * CHUNK]).astype(BF16)
        from_e = spread[2 * CHUNK:3 * CHUNK]
        tot_hi, tot_lo = _split(jnp.exp(jnp.broadcast_to(total, (8, LANES))))
        tot_e = (_dot(tot_hi, e_ref[...]) + _dot(tot_lo, e_ref[...]))[0:1, :]
        st = st_ref[0 if forward else 1]
        st_b = st.astype(BF16)
        y_parts = []
        st_parts = []
        for g in range(SSD_GROUPS):
            bm = cv_ref[b_block, rows, g * SSD_STATE:(g + 1) * SSD_STATE]
            cm = cv_ref[b_block + 1, rows, g * SSD_STATE:(g + 1) * SSD_STATE]
            bm_t = bm.astype(F32).T.astype(BF16)
            cb = _dot(cm, bm_t)
            gcols = slice(g * 256, (g + 1) * 256)
            y_off = _dot(cm, st_b[:, gcols])
            st_parts.append(_dot(bm_t, xd_b[:, gcols]))
            diag = []
            for pair in range(2):
                pcols = slice(g * 256 + pair * LANES, g * 256 + (pair + 1) * LANES)
                decayed = []
                for sub in range(2):
                    h = g * 4 + pair * 2 + sub
                    col = pos[:, off + h:off + h + 1]
                    row = pos_t[off + h:off + h + 1, :]
                    diff = (col - row) if forward else (row - col)
                    decayed.append((jnp.where(mask, jnp.exp(diff), 0.0) * cb).astype(BF16))
                diag.append(_dot(jnp.concatenate(decayed, axis=1),
                                 jnp.concatenate([xc_sub[0][:, pcols], xc_sub[1][:, pcols]], axis=0)))
            y_parts.append(jnp.concatenate(diag, axis=1) + y_off * from_e[:, gcols])
        y_new = jnp.concatenate(y_parts, axis=1)
        st_ref[0 if forward else 1] = st * tot_e + jnp.concatenate(st_parts, axis=1)
        if forward:
            y_new = y_new + x * dskip_ref[...]
        return rows, y_new

    if has_init:
        st_ref[0] = init_ref[0, 0].T
        st_ref[1] = init_ref[0, 1].T
    else:
        st_ref[...] = jnp.zeros_like(st_ref)

    def first_half(i, carry):
        for rows, y_new in (chunk_pass(i, True), chunk_pass(n_chunks - 1 - i, False)):
            y_ref[rows, :] = y_new
        return carry

    def second_half(i, carry):
        for rows, y_new in (chunk_pass(i, True), chunk_pass(n_chunks - 1 - i, False)):
            y_ref[rows, :] = y_ref[rows, :] + y_new
        return carry

    lax.fori_loop(0, n_chunks // 2, first_half, 0)
    lax.fori_loop(n_chunks // 2, n_chunks, second_half, 0)
    fin_ref[0, 0] = st_ref[0].T
    fin_ref[0, 1] = st_ref[1].T


def _ssd_scan(xbc, dt, cs, init, consts, n_seq, seq, row_block0):
    n_chunks = seq // CHUNK
    assert n_chunks % 2 == 0
    has_init = init is not None
    once = pl.Buffered(1)
    seq_spec = lambda w: pl.BlockSpec((seq, w), lambda b: (row_block0 + b, 0), pipeline_mode=once)
    xbc_spec = pl.BlockSpec((N_XBC, seq, XBC_BLOCK), lambda b: (0, row_block0 + b, 0), pipeline_mode=once)
    in_specs = [xbc_spec, seq_spec(LANES), seq_spec(LANES)]
    args = [xbc, dt, cs]
    if has_init:
        in_specs.append(pl.BlockSpec((1, 2, D_SSD, SSD_STATE), lambda b: (b, 0, 0, 0), pipeline_mode=once))
        args.append(init)
    for cst in consts:
        in_specs.append(pl.BlockSpec(cst.shape, lambda b, nd=cst.ndim: (0,) * nd))
        args.append(cst)
    return pl.pallas_call(
        functools.partial(_ssd_kernel, has_init, n_chunks),
        grid=(n_seq,),
        in_specs=in_specs,
        out_specs=[pl.BlockSpec((seq, D_SSD), lambda b: (b, 0)),
                   pl.BlockSpec((1, 2, D_SSD, SSD_STATE), lambda b: (b, 0, 0, 0))],
        out_shape=[jax.ShapeDtypeStruct((n_seq * seq, D_SSD), F32),
                   jax.ShapeDtypeStruct((n_seq, 2, D_SSD, SSD_STATE), F32)],
        scratch_shapes=[pltpu.VMEM((2, SSD_STATE, D_SSD), F32)],
        compiler_params=_cparams("parallel"),
        name="ssd_scan_%d" % seq,
    )(*args)


def _mixer_out_kernel(x_ref, mod_ref, yc_ref, yl_ref, z_ref, u_ref, v_ref, ssdn_ref, sgun_ref, wsp_ref, bsp_ref,
                      wo_ref, o_ref):
    tm = x_ref.shape[0]
    wide = lambda ref: jnp.concatenate([ref[b] for b in range(ref.shape[0])], axis=1).astype(F32)
    gated = _select_rows(yc_ref, yl_ref) * _silu(wide(z_ref))
    a = (_rms(gated) * ssdn_ref[...]).astype(BF16)
    vb = (_rms(wide(v_ref)) * sgun_ref[...]).astype(BF16)
    chunks = []
    for k in range(tm // CHUNK):
        rows = slice(k * CHUNK, (k + 1) * CHUNK)
        groups = [_dot(wsp_ref[g], vb[rows, g * SGU_GDIM:(g + 1) * SGU_GDIM]) for g in range(SGU_GROUPS)]
        chunks.append(jnp.concatenate(groups, axis=1) + bsp_ref[...])
    s = (wide(u_ref) * jnp.concatenate(chunks, axis=0)).astype(BF16)
    out = _dot(a, wo_ref[0:D_SSD, :]) + _dot(s, wo_ref[D_SSD:D_SSD + D_SGU, :])
    o_ref[...] = x_ref[...] + mod_ref[0, 2:3, :] * out


def _mixer_output(x, mod, y_ctx, y_lat, gates, ssd_norm, sgu_norm, w_sp, b_sp_e, w_out):
    tm = TM_ROWS
    per = D_SSD // GATE_BLOCK
    part = lambda k: pl.BlockSpec((per, tm, GATE_BLOCK), lambda i: (k, i, 0))
    return pl.pallas_call(
        _mixer_out_kernel,
        grid=(N_TOK // tm,),
        in_specs=[_row_spec(tm, D_MODEL), _mod_spec(tm)] + _split_row_specs(tm, D_SSD) + [part(0), part(1), part(2),
                  _const_spec((1, D_SSD)), _const_spec((1, D_SGU)),
                  _const_spec((SGU_GROUPS, CHUNK, CHUNK)), _const_spec((CHUNK, D_SGU)),
                  _const_spec((D_SSD + D_SGU, D_MODEL))],
        out_specs=_row_spec(tm, D_MODEL),
        out_shape=jax.ShapeDtypeStruct((N_TOK, D_MODEL), F32),
        compiler_params=_cparams("parallel"),
        name="mixer_output",
    )(x, mod, y_ctx, y_lat, gates, gates, gates, ssd_norm, sgu_norm, w_sp, b_sp_e, w_out)


def _ffn_kernel(x_ref, mod_ref, wg_ref, wu_ref, wd_ref, o_ref):
    h = _modulated(x_ref, mod_ref, 3).astype(BF16)
    acc = jnp.zeros(o_ref.shape, F32)
    for f in range(D_FF // FF_CHUNK):
        cols = slice(f * FF_CHUNK, (f + 1) * FF_CHUNK)
        act = (_silu(_dot(h, wg_ref[:, cols])) * _dot(h, wu_ref[:, cols])).astype(BF16)
        acc = acc + _dot(act, wd_ref[cols, :])
    o_ref[...] = x_ref[...] + mod_ref[0, 5:6, :] * acc


def _dense_ffn(x, mod, wg, wu, wd):
    tm = TM_FFN
    once = pl.Buffered(1)
    return pl.pallas_call(
        _ffn_kernel,
        grid=(N_TOK // tm,),
        in_specs=[_row_spec(tm, D_MODEL), _mod_spec(tm),
                  pl.BlockSpec((D_MODEL, D_FF), lambda i: (0, 0), pipeline_mode=once),
                  pl.BlockSpec((D_MODEL, D_FF), lambda i: (0, 0), pipeline_mode=once),
                  pl.BlockSpec((D_FF, D_MODEL), lambda i: (0, 0), pipeline_mode=once)],
        out_specs=_row_spec(tm, D_MODEL),
        out_shape=jax.ShapeDtypeStruct((N_TOK, D_MODEL), F32),
        compiler_params=_cparams("parallel"),
        name="dense_ffn",
    )(x, mod, wg, wu, wd)


HEAD_PROJ = 3 * LANES


def _pair_sums(sq_a, sq_b, ones_ref):
    return _dot(jnp.concatenate([sq_a, sq_b], axis=1).astype(BF16), ones_ref[...])


def _expand_keys(ckv_b, pe_sq, pe_rot, wuk_ref, wuv_ref, kn_ref, ones_ref, k_ref, v_ref):
    kn_nope = kn_ref[:, 0:QK_NOPE]
    v_ref[...] = _dot(ckv_b, wuv_ref[...]).astype(BF16)
    k_nope = _dot(ckv_b, wuk_ref[...])
    for pair in range(MLA_HEADS // 2):
        kh = [k_nope[:, h * QK_NOPE:(h + 1) * QK_NOPE] for h in (2 * pair, 2 * pair + 1)]
        ss = _pair_sums(kh[0] * kh[0] + pe_sq, kh[1] * kh[1] + pe_sq, ones_ref)
        r = lax.rsqrt(ss * (1.0 / QK_DIM) + EPS)
        for idx in range(2):
            h = 2 * pair + idx
            rh = r[:, idx * LANES:(idx + 1) * LANES]
            k_ref[:, h * HEAD_PAD:h * HEAD_PAD + QK_NOPE] = (kh[idx] * rh * kn_nope).astype(BF16)
            k_ref[:, h * HEAD_PAD + QK_NOPE:(h + 1) * HEAD_PAD] = (pe_rot * rh).astype(BF16)


def _mla_proj_kernel(x_ref, mod_ref, cos_ref, sin_ref, wdq_ref, wdkv_ref, qan_ref, kvan_ref,
                     wuq_ref, wuk_ref, wuv_ref, qn_ref, kn_ref, ones_ref, q_ref, k_ref, v_ref, ckv_ref, kpe_ref):
    h = _modulated(x_ref, mod_ref, 0).astype(BF16)
    cos, sin = cos_ref[...], sin_ref[...]
    qa = (_rms(_dot(h, wdq_ref[...])) * qan_ref[...]).astype(BF16)
    q = _dot(qa, wuq_ref[...])
    qn_nope = qn_ref[:, 0:LANES]
    qn_rope = qn_ref[:, LANES:2 * LANES] * cos
    qn_part = qn_ref[:, 2 * LANES:3 * LANES] * sin
    scale = QK_DIM ** -0.5
    for pair in range(MLA_HEADS // 2):
        blocks = []
        for hd in (2 * pair, 2 * pair + 1):
            base = hd * HEAD_PROJ
            blocks.append((q[:, base:base + LANES], q[:, base + LANES:base + 2 * LANES],
                           q[:, base + 2 * LANES:base + 3 * LANES]))
        ss = _pair_sums(*[qh * qh + qr * qr for qh, qr, _ in blocks], ones_ref)
        r = lax.rsqrt(ss * (1.0 / QK_DIM) + EPS) * scale
        for idx, (qh, qr, qp) in enumerate(blocks):
            hd = 2 * pair + idx
            rh = r[:, idx * LANES:(idx + 1) * LANES]
            q_ref[:, hd * HEAD_PAD:hd * HEAD_PAD + QK_NOPE] = (qh * rh * qn_nope).astype(BF16)
            q_ref[:, hd * HEAD_PAD + QK_NOPE:(hd + 1) * HEAD_PAD] = ((qr * qn_rope + qp * qn_part) * rh).astype(BF16)
    kva = _dot(h, wdkv_ref[...])
    ckv = _rms(kva[:, 0:KV_RANK]) * kvan_ref[...]
    kpe = kva[:, KV_RANK:KV_RANK + LANES]
    kpe_part = kva[:, KV_RANK + LANES:KV_RANK + 2 * LANES]
    ckv_ref[...] = ckv
    kpe_ref[...] = kpe
    pe_rot = kpe * (kn_ref[:, LANES:2 * LANES] * cos) + kpe_part * (kn_ref[:, 2 * LANES:3 * LANES] * sin)
    _expand_keys(ckv.astype(BF16), kpe * kpe, pe_rot, wuk_ref, wuv_ref, kn_ref, ones_ref, k_ref, v_ref)


def _mla_projection(x, mod, rope_tabs, wdq, wdkv, qan, kvan, wuq, wuk, wuv, qn, kn, ones):
    tm = TM_ROWS
    hw = MLA_HEADS * HEAD_PAD
    out_w = (hw, hw, MLA_HEADS * V_HEAD, KV_RANK, LANES)
    out_dt = (BF16, BF16, BF16, F32, F32)
    consts = (wdq, wdkv, qan, kvan, wuq, wuk, wuv, qn, kn, ones)
    return pl.pallas_call(
        _mla_proj_kernel,
        grid=(N_TOK // tm,),
        in_specs=[_row_spec(tm, D_MODEL), _mod_spec(tm)] + [_row_spec(tm, LANES)] * 2
                 + [_const_spec(cst.shape) for cst in consts],
        out_specs=[_row_spec(tm, w) for w in out_w],
        out_shape=[jax.ShapeDtypeStruct((N_TOK, w), dt) for w, dt in zip(out_w, out_dt)],
        compiler_params=_cparams("parallel"),
        name="mla_projection",
    )(x, mod, *rope_tabs, *consts)


def _cache_kv_kernel(ckv_ref, kpe_ref, wuk_ref, wuv_ref, kn_ref, ones_ref, k_ref, v_ref):
    kpe = kpe_ref[...]
    _expand_keys(ckv_ref[...].astype(BF16), kpe * kpe, kpe * kn_ref[:, LANES:2 * LANES],
                 wuk_ref, wuv_ref, kn_ref, ones_ref, k_ref, v_ref)


def _cache_keys(ckv, kpe, wuk, wuv, kn, ones):
    rows = ckv.shape[0]
    tm = TM_ROWS
    consts = (wuk, wuv, kn, ones)
    return pl.pallas_call(
        _cache_kv_kernel,
        grid=(rows // tm,),
        in_specs=[_row_spec(tm, KV_RANK), _row_spec(tm, LANES)] + [_const_spec(cst.shape) for cst in consts],
        out_specs=[_row_spec(tm, MLA_HEADS * HEAD_PAD), _row_spec(tm, MLA_HEADS * V_HEAD)],
        out_shape=[jax.ShapeDtypeStruct((rows, MLA_HEADS * HEAD_PAD), BF16),
                   jax.ShapeDtypeStruct((rows, MLA_HEADS * V_HEAD), BF16)],
        compiler_params=_cparams("parallel"),
        name="cache_keys",
    )(ckv, kpe, *consts)


def _attn_kernel(has_cache, tq, n_par, heads, *refs):
    if has_cache:
        q_ref, k_ref, v_ref, kc_ref, vc_ref, o_ref = refs
    else:
        q_ref, k_ref, v_ref, o_ref = refs

    def tile(r0, hd):
        qk = slice(hd * HEAD_PAD, (hd + 1) * HEAD_PAD)
        vo = slice(hd * V_HEAD, (hd + 1) * V_HEAD)
        q = q_ref[pl.ds(r0, tq), qk]
        s = _dot_nt(q, k_ref[:, qk])
        m = jnp.max(s, axis=-1, keepdims=True)
        if has_cache:
            sc = _dot_nt(q, kc_ref[:, qk])
            m = jnp.maximum(m, jnp.max(sc, axis=-1, keepdims=True))
            pc = jnp.exp(sc - m)
        p = jnp.exp(s - m)
        den = jnp.sum(p, axis=-1, keepdims=True)
        num = _dot(p.astype(BF16), v_ref[:, vo])
        if has_cache:
            den = den + jnp.sum(pc, axis=-1, keepdims=True)
            num = num + _dot(pc.astype(BF16), vc_ref[:, vo])
        o_ref[pl.ds(r0, tq), vo] = (num / den).astype(BF16)

    group = n_par * tq
    n_groups = q_ref.shape[0] // group
    if n_groups == 1:
        for hd in range(heads):
            for t in range(n_par):
                tile(t * tq, hd)
    else:
        def body(i, carry):
            r0 = pl.multiple_of(i * group, group)
            for hd in range(heads):
                for t in range(n_par):
                    tile(r0 + t * tq, hd)
            return carry
        lax.fori_loop(0, n_groups, body, 0)


def _attention(q, k, v, kc, vc, n_seq, seq, row_block0):
    has_cache = kc is not None
    if seq >= 2 * TQ:
        tq, n_par, heads = TQ, 2, 1
    else:
        tq, n_par, heads = seq, 1, MLA_HEADS
    seq_spec = lambda w: pl.BlockSpec((seq, heads * w), lambda b, h: (row_block0 + b, h))
    in_specs = [seq_spec(HEAD_PAD), seq_spec(HEAD_PAD), seq_spec(V_HEAD)]
    args = [q, k, v]
    if has_cache:
        past = kc.shape[0] // n_seq
        in_specs += [pl.BlockSpec((past, heads * HEAD_PAD), lambda b, h: (b, h)),
                     pl.BlockSpec((past, heads * V_HEAD), lambda b, h: (b, h))]
        args += [kc, vc]
    return pl.pallas_call(
        functools.partial(_attn_kernel, has_cache, tq, n_par, heads),
        grid=(n_seq, MLA_HEADS // heads),
        in_specs=in_specs,
        out_specs=pl.BlockSpec((seq, heads * V_HEAD), lambda b, h: (b, h)),
        out_shape=jax.ShapeDtypeStruct((n_seq * seq, MLA_HEADS * V_HEAD), BF16),
        compiler_params=_cparams("parallel", "parallel"),
        name="attention_%d" % seq,
    )(*args)


def _split_row_specs(tm, width):
    n_ctx = N_PROMPT // tm
    return [pl.BlockSpec((tm, width), lambda i, *_: (jnp.minimum(i, n_ctx - 1), 0)),
            pl.BlockSpec((tm, width), lambda i, *_: (jnp.maximum(i - n_ctx, 0), 0))]


def _select_rows(ctx_ref, lat_ref):
    tm = ctx_ref.shape[0]
    return jnp.where(pl.program_id(0) < N_PROMPT // tm, ctx_ref[...], lat_ref[...])


def _attn_out_router_kernel(x_ref, mod_ref, ac_ref, al_ref, wo_ref, wr_ref, tri_ref,
                            xo_ref, h_ref, meta_ref, cnt_ref, run_ref):
    @pl.when(pl.program_id(0) == 0)
    def _():
        run_ref[...] = jnp.zeros_like(run_ref)

    x_new = x_ref[...] + mod_ref[0, 2:3, :] * _dot(_select_rows(ac_ref, al_ref), wo_ref[...])
    xo_ref[...] = x_new
    h = _rms(x_new) * (1.0 + mod_ref[0, 4:5, :]) + mod_ref[0, 3:4, :]
    h_ref[...] = h.astype(BF16)
    lane = lax.broadcasted_iota(jnp.int32, (x_ref.shape[0], LANES), 1)
    lane_f = lane.astype(F32)
    logits = jnp.where(lane < N_EXPERTS, _dot3(h, wr_ref[...]), -jnp.inf)
    m1 = jnp.max(logits, axis=-1, keepdims=True)
    e1 = jnp.min(jnp.where(logits == m1, lane_f, float(LANES)), axis=-1, keepdims=True)
    rest = jnp.where(lane_f == e1, -jnp.inf, logits)
    m2 = jnp.max(rest, axis=-1, keepdims=True)
    e2 = jnp.min(jnp.where(rest == m2, lane_f, float(LANES)), axis=-1, keepdims=True)
    t = jnp.exp(m2 - m1)
    g1 = 1.0 / (1.0 + t)
    pick1 = lane_f == e1
    pick2 = lane_f == e2
    onehot = jnp.where(pick1, 1.0, jnp.where(pick2, 1.0, 0.0))
    before = run_ref[...] + _dot(tri_ref[...], onehot.astype(BF16)) - onehot
    rank1 = jnp.sum(jnp.where(pick1, before, 0.0), axis=-1, keepdims=True)
    rank2 = jnp.sum(jnp.where(pick2, before, 0.0), axis=-1, keepdims=True)
    run_ref[...] += jnp.sum(onehot, axis=0, keepdims=True)
    cnt_ref[...] = jnp.broadcast_to(run_ref[...], cnt_ref.shape)
    cols = (e1, e2, g1, 1.0 - g1, rank1, rank2)
    meta = jnp.zeros(meta_ref.shape, F32)
    for idx, val in enumerate(cols):
        meta = jnp.where(lane == idx, val, meta)
    meta_ref[...] = meta


META_EXPERT, META_GATE, META_RANK = 0, 2, 4


def _attn_out_router(x, mod, a_ctx, a_lat, w_o, wr):
    tm = TM_ROWS
    k = a_ctx.shape[1]
    r = jnp.arange(tm)
    tri = (r[:, None] >= r[None, :]).astype(BF16)
    return pl.pallas_call(
        _attn_out_router_kernel,
        grid=(N_TOK // tm,),
        in_specs=[_row_spec(tm, D_MODEL), _mod_spec(tm)] + _split_row_specs(tm, k)
                 + [_const_spec((k, D_MODEL)), _const_spec((D_MODEL, LANES)), _const_spec((tm, tm))],
        out_specs=[_row_spec(tm, D_MODEL), _row_spec(tm, D_MODEL), _row_spec(tm, LANES),
                   _const_spec((SEG_PAD, LANES))],
        out_shape=[jax.ShapeDtypeStruct((N_TOK, D_MODEL), F32), jax.ShapeDtypeStruct((N_TOK, D_MODEL), BF16),
                   jax.ShapeDtypeStruct((N_TOK, LANES), F32), jax.ShapeDtypeStruct((SEG_PAD, LANES), F32)],
        scratch_shapes=[pltpu.VMEM((1, LANES), F32)],
        compiler_params=_cparams("arbitrary"),
        name="attn_out_router",
    )(x, mod, a_ctx, a_lat, w_o, wr, tri)


TF_MOE = D_FF // 2


def _experts_kernel(be_ref, bv_ref, nu_ref, rows_ref, wg_ref, wu_ref, wd_ref, o_ref, acc_ref):
    i = pl.program_id(0)
    f = pl.program_id(1)

    @pl.when(bv_ref[i] > 0)
    def _():
        rows = rows_ref[...]
        acc = jnp.zeros(acc_ref.shape, F32)
        for c0 in range(0, TF_MOE, FF_CHUNK):
            cols = slice(c0, min(c0 + FF_CHUNK, TF_MOE))
            gate = _dot(rows, wg_ref[0, 0, :, cols].astype(BF16))
            up = _dot(rows, wu_ref[0, 0, :, cols].astype(BF16))
            acc = acc + _dot((_silu(gate) * up).astype(BF16), wd_ref[0, 0, cols, :].astype(BF16))

        @pl.when(f == 0)
        def _():
            acc_ref[...] = acc

        @pl.when(f > 0)
        def _():
            acc_ref[...] += acc

    @pl.when(f == pl.num_programs(1) - 1)
    def _():
        o_ref[...] = acc_ref[...].astype(BF16)


def _expert_ffn(rows, block_expert, block_valid, n_used, wg, wu, wd, layer):
    n_rows = rows.shape[0]
    tm = TM_MOE
    tf = TF_MOE
    n_f = D_FF // tf

    def f_idx(i, f, nu):
        return jnp.where(i < nu[0], f, n_f - 1)

    grid_spec = pltpu.PrefetchScalarGridSpec(
        num_scalar_prefetch=3,
        grid=(n_rows // tm, n_f),
        in_specs=[pl.BlockSpec((tm, D_MODEL), lambda i, f, be, bv, nu: (i, 0)),
                  pl.BlockSpec((1, 1, D_MODEL, tf), lambda i, f, be, bv, nu: (layer, be[i], 0, f_idx(i, f, nu))),
                  pl.BlockSpec((1, 1, D_MODEL, tf), lambda i, f, be, bv, nu: (layer, be[i], 0, f_idx(i, f, nu))),
                  pl.BlockSpec((1, 1, tf, D_MODEL), lambda i, f, be, bv, nu: (layer, be[i], f_idx(i, f, nu), 0))],
        out_specs=pl.BlockSpec((tm, D_MODEL), lambda i, f, be, bv, nu: (i, 0)),
        scratch_shapes=[pltpu.VMEM((tm, D_MODEL), F32)])
    return pl.pallas_call(
        _experts_kernel,
        grid_spec=grid_spec,
        out_shape=jax.ShapeDtypeStruct((n_rows, D_MODEL), BF16),
        compiler_params=_cparams("arbitrary", "arbitrary"),
        name="expert_ffn",
    )(block_expert, block_valid, n_used, rows, wg, wu, wd)


def _combine_kernel(x_ref, mod_ref, meta_ref, a_ref, b_ref, o_ref):
    g1 = meta_ref[:, META_GATE:META_GATE + 1]
    g2 = meta_ref[:, META_GATE + 1:META_GATE + 2]
    y = a_ref[...].astype(F32) * g1 + b_ref[...].astype(F32) * g2
    o_ref[...] = x_ref[...] + mod_ref[0, 5:6, :] * y


def _moe_combine(x, mod, meta, a, b):
    tm = TM_FFN
    return pl.pallas_call(
        _combine_kernel,
        grid=(N_TOK // tm,),
        in_specs=[_row_spec(tm, D_MODEL), _mod_spec(tm), _row_spec(tm, LANES), _row_spec(tm, D_MODEL),
                  _row_spec(tm, D_MODEL)],
        out_specs=_row_spec(tm, D_MODEL),
        out_shape=jax.ShapeDtypeStruct((N_TOK, D_MODEL), F32),
        compiler_params=_cparams("parallel"),
        name="moe_combine",
    )(x, mod, meta, a, b)


def _moe(x, mod, h, meta, counts, wg, wu, wd, layer):
    tm = TM_MOE
    n_assign = N_TOK * TOP_K
    experts = meta[:, META_EXPERT:META_EXPERT + TOP_K].astype(jnp.int32)
    rank = meta[:, META_RANK:META_RANK + TOP_K].astype(jnp.int32)
    count = counts[0, :N_EXPERTS].astype(jnp.int32)
    padded = (count + tm - 1) // tm * tm
    pad_end = jnp.cumsum(padded)
    pad_start = pad_end - padded
    onehot = experts[:, :, None] == jnp.arange(N_EXPERTS, dtype=jnp.int32)
    dest = jnp.sum(jnp.where(onehot, pad_start, 0), axis=-1) + rank
    n_rows = n_assign + N_EXPERTS * tm
    n_blocks = n_rows // tm
    row_token = jnp.zeros((n_rows,), jnp.int32).at[dest.reshape(-1)].set(
        jnp.arange(n_assign, dtype=jnp.int32) // TOP_K)
    n_used = (pad_end[-1] // tm).astype(jnp.int32).reshape(1)
    block_id = jnp.arange(n_blocks, dtype=jnp.int32)
    block_start = jnp.minimum(block_id, n_used[0] - 1) * tm
    block_expert = jnp.minimum(jnp.sum((pad_end[None, :] <= block_start[:, None]).astype(jnp.int32), axis=1),
                               N_EXPERTS - 1)
    last_row = jnp.sum(jnp.where(block_expert[:, None] == jnp.arange(N_EXPERTS, dtype=jnp.int32),
                                 pad_start + count, 0), axis=1)
    block_valid = jnp.where(block_id < n_used[0], jnp.clip(last_row - block_start, 0, tm), 0)
    out = _expert_ffn(h[row_token], block_expert, block_valid, n_used, wg, wu, wd, layer)
    return _moe_combine(x, mod, meta, out[dest[:, 0]], out[dest[:, 1]])


def _rope_tables():
    half = ROPE_AXIS // 2
    pos = jnp.arange(DEC_SEQ)
    row = (pos // GRID_W).astype(F32)
    col = (pos % GRID_W).astype(F32)
    inv = ROPE_THETA ** (-jnp.arange(0, ROPE_AXIS, 2, dtype=F32) / ROPE_AXIS)
    ang_r = row[:, None] * inv
    ang_c = col[:, None] * inv
    pad = jnp.zeros((DEC_SEQ, LANES - QK_ROPE), F32)
    cos = jnp.concatenate([jnp.cos(ang_r), jnp.cos(ang_r), jnp.cos(ang_c), jnp.cos(ang_c), pad + 1.0], axis=1)
    sin = jnp.concatenate([-jnp.sin(ang_r), jnp.sin(ang_r), -jnp.sin(ang_c), jnp.sin(ang_c), pad], axis=1)
    prompt = jnp.zeros((N_PROMPT, LANES), F32)
    tile = lambda t: jnp.tile(t, (DEC_BATCH, 1))
    return (jnp.concatenate([prompt + 1.0, tile(cos)], axis=0),
            jnp.concatenate([prompt, tile(sin)], axis=0))


def _rope_blocks(a):
    half = ROPE_AXIS // 2
    partner = jnp.concatenate([a[..., half:2 * half], a[..., 0:half], a[..., 3 * half:4 * half],
                               a[..., 2 * half:3 * half]], axis=-1)
    return jnp.concatenate([_pad_lanes(a, LANES), _pad_lanes(partner, LANES)], axis=-1)


def _pad_lanes(a, width):
    return jnp.pad(a, [(0, 0)] * (a.ndim - 1) + [(0, width - a.shape[-1])])


def _ssd_constants():
    r = jnp.arange(CHUNK)
    tri = (r[:, None] >= r[None, :]).astype(BF16)
    head_of = jnp.arange(D_SSD) // SSD_HEADDIM
    e_f = (r[:, None] == head_of[None, :]).astype(BF16)
    e_b = (r[:, None] == head_of[None, :] + SSD_HEADS).astype(BF16)
    return tri, e_f, e_b


def kernel(x_prompt, x_sample, c, state_ssm, cache_ckv, cache_kpe, c_ctx, w_mod, b_mod, w_in, conv_w, conv_b, dt_bias, a_log, d_skip, ssd_norm, sgu_norm, w_sp, b_sp, w_out, ffn_w_gate, ffn_w_up, ffn_w_down, w_dq, q_a_norm, w_uq, w_dkv, kv_a_norm, w_ukv, q_norm, k_norm, w_o, router, moe_w_gate, moe_w_up, moe_w_down):
    x = jnp.concatenate([x_prompt.reshape(N_PROMPT, D_MODEL), x_sample.reshape(N_SAMPLE, D_MODEL)], axis=0)
    cond = jnp.concatenate([c_ctx[None, :], c, jnp.zeros((SEG_PAD - N_SEG, D_MODEL), F32)], axis=0)
    mods = _modulation_tables(cond, w_mod, b_mod).reshape(DEPTH, SEG_PAD, N_MOD, D_MODEL)
    rope_tabs = _rope_tables()
    tri, e_f, e_b = _ssd_constants()
    blk = jnp.arange(2 * LANES) // LANES
    pair_ones = (blk[:, None] == blk[None, :]).astype(BF16)
    i1 = D_SSD
    i2 = i1 + CONV_CH
    i3 = i2 + 2 * SSD_HEADS
    i4 = i3 + D_SGU
    new_ssm, new_ckv, new_kpe = [], [], []
    for i in range(DEPTH):
        j = i // 2
        mod = mods[i]
        if i % 2 == 0:
            w = w_in[j]
            wx = _col_blocks(w[:, i1:i2].astype(BF16), XBC_BLOCK)
            wgate = _col_blocks(jnp.concatenate([w[:, 0:i1], w[:, i3:]], axis=1).astype(BF16), GATE_BLOCK)
            wdt = _pad_lanes(w[:, i2:i3], LANES).astype(BF16)
            a_log_row = _pad_lanes(a_log[j].reshape(1, -1), LANES)
            xbc, gates, dt, cs = _in_projection(
                x, mod, (wx, wgate, wdt),
                (_col_blocks(conv_w[j], XBC_BLOCK), _col_blocks(conv_b[j][None, :], XBC_BLOCK),
                 _pad_lanes(dt_bias[j].reshape(1, -1), LANES), a_log_row, tri))
            consts = (a_log_row, jnp.repeat(d_skip[j], SSD_HEADDIM)[None, :], e_f, e_b)
            y_ctx, st_p = _ssd_scan(xbc, dt, cs, None, consts, BATCH, SEQ, 0)
            init = state_ssm[:, j].reshape(DEC_BATCH, 2, D_SSD, SSD_STATE)
            y_lat, _ = _ssd_scan(xbc, dt, cs, init, consts, DEC_BATCH, DEC_SEQ, N_PROMPT // DEC_SEQ)
            new_ssm.append(st_p.reshape(BATCH, 2, SSD_HEADS, SSD_HEADDIM, SSD_STATE))
            b_sp_e = jnp.repeat(b_sp[j].T, SGU_GDIM, axis=1)
            x = _mixer_output(x, mod, y_ctx, y_lat, gates, ssd_norm[j][None, :], sgu_norm[j][None, :],
                              w_sp[j].astype(BF16), b_sp_e, w_out[j].astype(BF16))
            x = _dense_ffn(x, mod, ffn_w_gate[j].astype(BF16), ffn_w_up[j].astype(BF16),
                           ffn_w_down[j].astype(BF16))
        else:
            split = lambda a: jnp.concatenate([a[..., :QK_NOPE], _rope_blocks(a[..., QK_NOPE:])], axis=-1)
            wuq = split(w_uq[j].reshape(Q_RANK, MLA_HEADS, QK_DIM)).reshape(Q_RANK, MLA_HEADS * HEAD_PROJ)
            wdkv = jnp.concatenate([w_dkv[j][:, :KV_RANK], _rope_blocks(w_dkv[j][:, KV_RANK:])], axis=-1)
            wukv = w_ukv[j].reshape(KV_RANK, MLA_HEADS, QK_NOPE + V_HEAD)
            wuk = wukv[:, :, :QK_NOPE].reshape(KV_RANK, -1).astype(BF16)
            wuv = wukv[:, :, QK_NOPE:].reshape(KV_RANK, -1).astype(BF16)
            qn = split(q_norm[j][None, :])
            kn = split(k_norm[j][None, :])
            q, k, v, ckv, kpe = _mla_projection(
                x, mod, rope_tabs, w_dq[j].astype(BF16), wdkv.astype(BF16), q_a_norm[j][None, :],
                kv_a_norm[j][None, :], wuq.astype(BF16), wuk, wuv, qn, kn, pair_ones)
            new_ckv.append(ckv[:N_PROMPT].reshape(BATCH, SEQ, KV_RANK))
            new_kpe.append(kpe[:N_PROMPT, :QK_ROPE].reshape(BATCH, SEQ, QK_ROPE))
            kc, vc = _cache_keys(cache_ckv[:, j].reshape(DEC_BATCH * PAST_LEN, KV_RANK),
                                 _pad_lanes(cache_kpe[:, j].reshape(DEC_BATCH * PAST_LEN, QK_ROPE), LANES),
                                 wuk, wuv, kn, pair_ones)
            o_ctx = _attention(q, k, v, None, None, BATCH, SEQ, 0)
            o_lat = _attention(q, k, v, kc, vc, DEC_BATCH, DEC_SEQ, N_PROMPT // DEC_SEQ)
            x, h, meta, counts = _attn_out_router(x, mod, o_ctx, o_lat, w_o[j].astype(BF16),
                                                  _pad_lanes(router[j], LANES))
            x = _moe(x, mod, h, meta, counts, moe_w_gate, moe_w_up, moe_w_down, j)
    return (x[:N_PROMPT].reshape(BATCH, SEQ, D_MODEL),
            x[N_PROMPT:].reshape(DEC_BATCH, DEC_SEQ, D_MODEL),
            jnp.stack(new_ssm, axis=1),
            jnp.stack(new_ckv, axis=1),
            jnp.stack(new_kpe, axis=1))
```

```python
import functools
import math

import jax
import jax.numpy as jnp
from jax import lax
from jax.experimental import pallas as pl
from jax.experimental.pallas import tpu as pltpu

F32 = jnp.float32
BF16 = jnp.bfloat16

D_MODEL = 1024
BATCH = 16
SEQ = 256
DEPTH = 4
DEC_BATCH = 4
DEC_SEQ = 2048
PAST_LEN = 512
GRID_W = 64
N_MOD = 6
EPS = 1e-6

SSD_HEADDIM = 64
SSD_HEADS = 16
D_SSD = 1024
SSD_GROUPS = 4
SSD_STATE = 128
CHUNK = 128
CONV_W = 5
CONV_CH = D_SSD + 2 * SSD_GROUPS * SSD_STATE
D_SGU = 1024
SGU_GROUPS = 4
SGU_GDIM = D_SGU // SGU_GROUPS

MLA_HEADS = 8
Q_RANK = 384
KV_RANK = 256
QK_NOPE = 128
QK_ROPE = 64
V_HEAD = 128
QK_DIM = QK_NOPE + QK_ROPE
ROPE_AXIS = QK_ROPE // 2
ROPE_THETA = 10000.0
HEAD_PAD = 256

D_FF = 2816
N_EXPERTS = 8
TOP_K = 2

N_PROMPT = BATCH * SEQ
N_SAMPLE = DEC_BATCH * DEC_SEQ
N_TOK = N_PROMPT + N_SAMPLE
N_SEG = 1 + DEC_BATCH
SEG_PAD = 8

LANES = 128
VMEM_LIMIT = 56 * 1024 * 1024

TM_ROWS = 256
TM_FFN = 512
FF_CHUNK = 256
TM_MOE = 1024
TQ = 256


def _cparams(*sem):
    return pltpu.CompilerParams(dimension_semantics=sem, vmem_limit_bytes=VMEM_LIMIT)


def _dot(a, b):
    return jnp.dot(a, b, preferred_element_type=F32)


def _dot_nt(a, b):
    return lax.dot_general(a, b, (((1,), (1,)), ((), ())), preferred_element_type=F32)


def _split(x):
    hi = x.astype(BF16)
    lo = (x - hi.astype(F32)).astype(BF16)
    return hi, lo


def _dot3(a, b):
    ah, al = _split(a)
    bh, bl = _split(b)
    return _dot(ah, bh) + _dot(ah, bl) + _dot(al, bh)


def _silu(x):
    return x / (1.0 + jnp.exp(-x))


def _rms(x):
    return x * lax.rsqrt(jnp.mean(x * x, axis=-1, keepdims=True) + EPS)


def _modulated(x_ref, mod_ref, first):
    shift = mod_ref[0, first:first + 1, :]
    scale = mod_ref[0, first + 1:first + 2, :]
    return _rms(x_ref[...]) * (1.0 + scale) + shift


def _seg_map(tm):
    def index_map(i, *_):
        r = i * tm
        return (jnp.where(r < N_PROMPT, 0, 1 + (r - N_PROMPT) // DEC_SEQ), 0, 0)
    return index_map


def _row_spec(tm, width):
    return pl.BlockSpec((tm, width), lambda i, *_: (i, 0))


def _const_spec(shape):
    zeros = (0,) * len(shape)
    return pl.BlockSpec(shape, lambda i, *_: zeros)


def _mod_spec(tm):
    return pl.BlockSpec((1, N_MOD, D_MODEL), _seg_map(tm))


def _mod_kernel(c_ref, w_ref, b_ref, o_ref):
    o_ref[0] = _dot3(_silu(c_ref[...]), w_ref[0]) + b_ref[0]


def _modulation_tables(cond, w_mod, b_mod):
    tn = 1536
    return pl.pallas_call(
        _mod_kernel,
        grid=(DEPTH, N_MOD * D_MODEL // tn),
        in_specs=[pl.BlockSpec((SEG_PAD, D_MODEL), lambda l, j: (0, 0)),
                  pl.BlockSpec((1, D_MODEL, tn), lambda l, j: (l, 0, j)),
                  pl.BlockSpec((1, 1, tn), lambda l, j: (l, 0, j))],
        out_specs=pl.BlockSpec((1, SEG_PAD, tn), lambda l, j: (l, 0, j)),
        out_shape=jax.ShapeDtypeStruct((DEPTH, SEG_PAD, N_MOD * D_MODEL), F32),
        compiler_params=_cparams("parallel", "parallel"),
        name="modulation",
    )(cond, w_mod, b_mod.reshape(DEPTH, 1, N_MOD * D_MODEL))


HALO = 8


XBC_BLOCK = 512
N_XBC = CONV_CH // XBC_BLOCK
GATE_BLOCK = 256
N_GATE = (D_SSD + 2 * D_SGU) // GATE_BLOCK
GATE_PER_STAGE = N_GATE // N_XBC


def _inproj_kernel(x_ref, xp_ref, xn_ref, mod_ref, wx_ref, wg_ref, wdt, convw_ref, convb_ref, dtb_ref, alog_ref,
                   tri_ref, xbc_ref, gate_ref, dt_ref, cs_ref, h_ref, xa_ref, xb_ref):
    tm = x_ref.shape[0]
    r0 = pl.program_id(0) * tm
    shift = mod_ref[0, 0:1, :]
    scale = mod_ref[0, 1:2, :]
    x_ext = jnp.concatenate([x_ref[...], xp_ref[...], xn_ref[...]], axis=0)
    h_ref[...] = (_rms(x_ext) * (1.0 + scale) + shift).astype(BF16)

    rel = r0 - N_PROMPT
    in_latent = r0 >= N_PROMPT
    keep_prev = jnp.where(jnp.logical_and(in_latent, lax.rem(rel, DEC_SEQ) != 0), 1.0, 0.0)
    keep_next = jnp.where(jnp.logical_and(in_latent, lax.rem(rel + tm, DEC_SEQ) != 0), 1.0, 0.0)
    reach = CONV_W // 2

    bufs = (xa_ref, xb_ref)
    bufs[0][...] = _dot(h_ref[...], wx_ref[0])

    def stage(s, cur_ref, nxt_ref):
        if s + 1 < N_XBC:
            nxt_ref[...] = _dot(h_ref[...], wx_ref[s + 1])
        for t in range(GATE_PER_STAGE):
            g = s * GATE_PER_STAGE + t
            gate_ref[g] = _dot(h_ref[0:tm, :], wg_ref[g]).astype(BF16)
        for half in range(XBC_BLOCK // 256):
            cols = slice(half * 256, (half + 1) * 256)
            win = jnp.concatenate([cur_ref[tm:tm + HALO, cols] * keep_prev, cur_ref[0:tm, cols],
                                   cur_ref[tm + HALO:tm + 2 * HALO, cols] * keep_next], axis=0)
            acc = jnp.zeros((tm, 256), F32) + convb_ref[s, :, cols]
            for k in range(CONV_W):
                start = HALO - reach + k
                acc = acc + win[start:start + tm, :] * convw_ref[s, k:k + 1, cols]
            xbc_ref[s, :, cols] = _silu(acc).astype(BF16)

    for s in range(N_XBC):
        stage(s, bufs[s % 2], bufs[1 - s % 2])

    h = h_ref[0:tm, :]
    lane = lax.broadcasted_iota(jnp.int32, (1, LANES), 1)
    raw = _dot(h, wdt[...]) + dtb_ref[...]
    dt = jnp.where(lane < 2 * SSD_HEADS, jnp.maximum(raw, 0.0) + jnp.log(1.0 + jnp.exp(-jnp.abs(raw))), 0.0)
    dt_ref[...] = dt
    ac = dt * -jnp.exp(alog_ref[...])
    hi = ac.astype(BF16)
    rest = ac - hi.astype(F32)
    mid = rest.astype(BF16)
    lo = (rest - mid.astype(F32)).astype(BF16)
    tri = tri_ref[...]
    for k in range(tm // CHUNK):
        rows = slice(k * CHUNK, (k + 1) * CHUNK)
        cs_ref[rows, :] = _dot(tri, hi[rows, :]) + _dot(tri, mid[rows, :]) + _dot(tri, lo[rows, :])


def _col_blocks(w, block):
    return w.reshape(w.shape[0], -1, block).transpose(1, 0, 2)


def _in_projection(x, mod, weights, consts):
    tm = TM_ROWS
    per_tile = tm // HALO
    last = N_TOK // HALO - 1
    halo_prev = pl.BlockSpec((HALO, D_MODEL), lambda i: (jnp.maximum(i * per_tile - 1, 0), 0))
    halo_next = pl.BlockSpec((HALO, D_MODEL), lambda i: (jnp.minimum((i + 1) * per_tile, last), 0))
    blocked = lambda n, w: pl.BlockSpec((n, tm, w), lambda i: (0, i, 0))
    return pl.pallas_call(
        _inproj_kernel,
        grid=(N_TOK // tm,),
        in_specs=[_row_spec(tm, D_MODEL), halo_prev, halo_next, _mod_spec(tm)]
                 + [_const_spec(a.shape) for a in weights + consts],
        out_specs=[blocked(N_XBC, XBC_BLOCK), blocked(N_GATE, GATE_BLOCK), _row_spec(tm, LANES),
                   _row_spec(tm, LANES)],
        out_shape=[jax.ShapeDtypeStruct((N_XBC, N_TOK, XBC_BLOCK), BF16),
                   jax.ShapeDtypeStruct((N_GATE, N_TOK, GATE_BLOCK), BF16),
                   jax.ShapeDtypeStruct((N_TOK, LANES), F32), jax.ShapeDtypeStruct((N_TOK, LANES), F32)],
        scratch_shapes=[pltpu.VMEM((tm + 2 * HALO, D_MODEL), BF16),
                        pltpu.VMEM((tm + 2 * HALO, XBC_BLOCK), F32),
                        pltpu.VMEM((tm + 2 * HALO, XBC_BLOCK), F32)],
        compiler_params=_cparams("parallel"),
        name="in_projection",
    )(x, x, x, mod, *weights, *consts)


def _ssd_kernel(has_init, n_chunks, *refs):
    refs = list(refs)
    cv_ref, dtv_ref, cs_ref = refs[0:3]
    del refs[0:3]
    init_ref = refs.pop(0) if has_init else None
    alog_ref, dskip_ref, ef_ref, eb_ref, y_ref, fin_ref, st_ref = refs
    a_neg = -jnp.exp(alog_ref[...])
    rows_i = lax.broadcasted_iota(jnp.int32, (CHUNK, CHUNK), 0)
    cols_i = lax.broadcasted_iota(jnp.int32, (CHUNK, CHUNK), 1)
    lower = rows_i >= cols_i
    upper = cols_i >= rows_i
    even_head = jnp.bitwise_and(lax.broadcasted_iota(jnp.int32, (CHUNK, D_SSD), 1), LANES - 1) < SSD_HEADDIM

    assert SSD_GROUPS * SSD_STATE == XBC_BLOCK
    b_block = D_SSD // XBC_BLOCK

    def row0(c):
        return pl.multiple_of(c * CHUNK, CHUNK)

    def chunk_pass(c, forward):
        r0 = row0(c)
        rows = pl.ds(r0, CHUNK)
        e_ref = ef_ref if forward else eb_ref
        off = 0 if forward else SSD_HEADS
        dt = dtv_ref[rows, :]
        cs = cs_ref[rows, :]
        total = cs_ref[pl.ds(r0 + CHUNK - 1, 1), :]
        if forward:
            pos = cs
            to_edge = jnp.exp(total - cs)
            from_edge = jnp.exp(cs)
            mask = lower
        else:
            pos = cs - dt * a_neg
            to_edge = jnp.exp(pos)
            from_edge = jnp.exp(total - pos)
            mask = upper
        pos_t = pos.T
        x = jnp.concatenate([cv_ref[b, rows, :] for b in range(D_SSD // XBC_BLOCK)], axis=1).astype(F32)
        stacked = jnp.concatenate([dt, dt * to_edge, from_edge], axis=0).astype(BF16)
        spread = _dot(stacked, e_ref[...])
        xc = x * spread[0:CHUNK]
        xc_sub = (jnp.where(even_head, xc, 0.0).astype(BF16), jnp.where(even_head, 0.0, xc).astype(BF16))
        xd_b = (x * spread[CHUNK:2 * CHUNK]).astype(BF16)
        from_e = spread[2 * CHUNK:3 * CHUNK]
        tot_hi, tot_lo = _split(jnp.exp(jnp.broadcast_to(total, (8, LANES))))
        tot_e = (_dot(tot_hi, e_ref[...]) + _dot(tot_lo, e_ref[...]))[0:1, :]
        st = st_ref[0 if forward else 1]
        st_b = st.astype(BF16)
        y_parts = []
        st_parts = []
        for g in range(SSD_GROUPS):
            bm = cv_ref[b_block, rows, g * SSD_STATE:(g + 1) * SSD_STATE]
            cm = cv_ref[b_block + 1, rows, g * SSD_STATE:(g + 1) * SSD_STATE]
            bm_t = bm.astype(F32).T.astype(BF16)
            cb = _dot(cm, bm_t)
            gcols = slice(g * 256, (g + 1) * 256)
            y_off = _dot(cm, st_b[:, gcols])
            st_parts.append(_dot(bm_t, xd_b[:, gcols]))
            diag = []
            for pair in range(2):
                pcols = slice(g * 256 + pair * LANES, g * 256 + (pair + 1) * LANES)
                decayed = []
                for sub in range(2):
                    h = g * 4 + pair * 2 + sub
                    col = pos[:, off + h:off + h + 1]
                    row = pos_t[off + h:off + h + 1, :]
                    diff = (col - row) if forward else (row - col)
                    decayed.append((jnp.where(mask, jnp.exp(diff), 0.0) * cb).astype(BF16))
                diag.append(_dot(jnp.concatenate(decayed, axis=1),
                                 jnp.concatenate([xc_sub[0][:, pcols], xc_sub[1][:, pcols]], axis=0)))
            y_parts.append(jnp.concatenate(diag, axis=1) + y_off * from_e[:, gcols])
        y_new = jnp.concatenate(y_parts, axis=1)
        st_ref[0 if forward else 1] = st * tot_e + jnp.concatenate(st_parts, axis=1)
        if forward:
            y_new = y_new + x * dskip_ref[...]
        return rows, y_new

    if has_init:
        st_ref[0] = init_ref[0, 0].T
        st_ref[1] = init_ref[0, 1].T
    else:
        st_ref[...] = jnp.zeros_like(st_ref)

    def first_half(i, carry):
        for rows, y_new in (chunk_pass(i, True), chunk_pass(n_chunks - 1 - i, False)):
            y_ref[rows, :] = y_new
        return carry

    def second_half(i, carry):
        for rows, y_new in (chunk_pass(i, True), chunk_pass(n_chunks - 1 - i, False)):
            y_ref[rows, :] = y_ref[rows, :] + y_new
        return carry

    lax.fori_loop(0, n_chunks // 2, first_half, 0)
    lax.fori_loop(n_chunks // 2, n_chunks, second_half, 0)
    fin_ref[0, 0] = st_ref[0].T
    fin_ref[0, 1] = st_ref[1].T


def _ssd_scan(xbc, dt, cs, init, consts, n_seq, seq, row_block0):
    n_chunks = seq // CHUNK
    assert n_chunks % 2 == 0
    has_init = init is not None
    once = pl.Buffered(1)
    seq_spec = lambda w: pl.BlockSpec((seq, w), lambda b: (row_block0 + b, 0), pipeline_mode=once)
    xbc_spec = pl.BlockSpec((N_XBC, seq, XBC_BLOCK), lambda b: (0, row_block0 + b, 0), pipeline_mode=once)
    in_specs = [xbc_spec, seq_spec(LANES), seq_spec(LANES)]
    args = [xbc, dt, cs]
    if has_init:
        in_specs.append(pl.BlockSpec((1, 2, D_SSD, SSD_STATE), lambda b: (b, 0, 0, 0), pipeline_mode=once))
        args.append(init)
    for cst in consts:
        in_specs.append(pl.BlockSpec(cst.shape, lambda b, nd=cst.ndim: (0,) * nd))
        args.append(cst)
    return pl.pallas_call(
        functools.partial(_ssd_kernel, has_init, n_chunks),
        grid=(n_seq,),
        in_specs=in_specs,
        out_specs=[pl.BlockSpec((seq, D_SSD), lambda b: (b, 0)),
                   pl.BlockSpec((1, 2, D_SSD, SSD_STATE), lambda b: (b, 0, 0, 0))],
        out_shape=[jax.ShapeDtypeStruct((n_seq * seq, D_SSD), F32),
                   jax.ShapeDtypeStruct((n_seq, 2, D_SSD, SSD_STATE), F32)],
        scratch_shapes=[pltpu.VMEM((2, SSD_STATE, D_SSD), F32)],
        compiler_params=_cparams("parallel"),
        name="ssd_scan_%d" % seq,
    )(*args)


def _mixer_out_kernel(x_ref, mod_ref, yc_ref, yl_ref, z_ref, u_ref, v_ref, ssdn_ref, sgun_ref, wsp_ref, bsp_ref,
                      wo_ref, o_ref):
    tm = x_ref.shape[0]
    wide = lambda ref: jnp.concatenate([ref[b] for b in range(ref.shape[0])], axis=1).astype(F32)
    gated = _select_rows(yc_ref, yl_ref) * _silu(wide(z_ref))
    a = (_rms(gated) * ssdn_ref[...]).astype(BF16)
    vb = (_rms(wide(v_ref)) * sgun_ref[...]).astype(BF16)
    chunks = []
    for k in range(tm // CHUNK):
        rows = slice(k * CHUNK, (k + 1) * CHUNK)
        groups = [_dot(wsp_ref[g], vb[rows, g * SGU_GDIM:(g + 1) * SGU_GDIM]) for g in range(SGU_GROUPS)]
        chunks.append(jnp.concatenate(groups, axis=1) + bsp_ref[...])
    s = (wide(u_ref) * jnp.concatenate(chunks, axis=0)).astype(BF16)
    out = _dot(a, wo_ref[0:D_SSD, :]) + _dot(s, wo_ref[D_SSD:D_SSD + D_SGU, :])
    o_ref[...] = x_ref[...] + mod_ref[0, 2:3, :] * out


def _mixer_output(x, mod, y_ctx, y_lat, gates, ssd_norm, sgu_norm, w_sp, b_sp_e, w_out):
    tm = TM_ROWS
    per = D_SSD // GATE_BLOCK
    part = lambda k: pl.BlockSpec((per, tm, GATE_BLOCK), lambda i: (k, i, 0))
    return pl.pallas_call(
        _mixer_out_kernel,
        grid=(N_TOK // tm,),
        in_specs=[_row_spec(tm, D_MODEL), _mod_spec(tm)] + _split_row_specs(tm, D_SSD) + [part(0), part(1), part(2),
                  _const_spec((1, D_SSD)), _const_spec((1, D_SGU)),
                  _const_spec((SGU_GROUPS, CHUNK, CHUNK)), _const_spec((CHUNK, D_SGU)),
                  _const_spec((D_SSD + D_SGU, D_MODEL))],
        out_specs=_row_spec(tm, D_MODEL),
        out_shape=jax.ShapeDtypeStruct((N_TOK, D_MODEL), F32),
        compiler_params=_cparams("parallel"),
        name="mixer_output",
    )(x, mod, y_ctx, y_lat, gates, gates, gates, ssd_norm, sgu_norm, w_sp, b_sp_e, w_out)


def _ffn_kernel(x_ref, mod_ref, wg_ref, wu_ref, wd_ref, o_ref):
    h = _modulated(x_ref, mod_ref, 3).astype(BF16)
    acc = jnp.zeros(o_ref.shape, F32)
    for f in range(D_FF // FF_CHUNK):
        cols = slice(f * FF_CHUNK, (f + 1) * FF_CHUNK)
        act = (_silu(_dot(h, wg_ref[:, cols])) * _dot(h, wu_ref[:, cols])).astype(BF16)
        acc = acc + _dot(act, wd_ref[cols, :])
    o_ref[...] = x_ref[...] + mod_ref[0, 5:6, :] * acc


def _dense_ffn(x, mod, wg, wu, wd):
    tm = TM_FFN
    once = pl.Buffered(1)
    return pl.pallas_call(
        _ffn_kernel,
        grid=(N_TOK // tm,),
        in_specs=[_row_spec(tm, D_MODEL), _mod_spec(tm),
                  pl.BlockSpec((D_MODEL, D_FF), lambda i: (0, 0), pipeline_mode=once),
                  pl.BlockSpec((D_MODEL, D_FF), lambda i: (0, 0), pipeline_mode=once),
                  pl.BlockSpec((D_FF, D_MODEL), lambda i: (0, 0), pipeline_mode=once)],
        out_specs=_row_spec(tm, D_MODEL),
        out_shape=jax.ShapeDtypeStruct((N_TOK, D_MODEL), F32),
        compiler_params=_cparams("parallel"),
        name="dense_ffn",
    )(x, mod, wg, wu, wd)


HEAD_PROJ = 3 * LANES


def _pair_sums(sq_a, sq_b, ones_ref):
    return _dot(jnp.concatenate([sq_a, sq_b], axis=1).astype(BF16), ones_ref[...])


def _expand_keys(ckv_b, pe_sq, pe_rot, wuk_ref, wuv_ref, kn_ref, ones_ref, k_ref, v_ref):
    kn_nope = kn_ref[:, 0:QK_NOPE]
    v_ref[...] = _dot(ckv_b, wuv_ref[...]).astype(BF16)
    k_nope = _dot(ckv_b, wuk_ref[...])
    for pair in range(MLA_HEADS // 2):
        kh = [k_nope[:, h * QK_NOPE:(h + 1) * QK_NOPE] for h in (2 * pair, 2 * pair + 1)]
        ss = _pair_sums(kh[0] * kh[0] + pe_sq, kh[1] * kh[1] + pe_sq, ones_ref)
        r = lax.rsqrt(ss * (1.0 / QK_DIM) + EPS)
        for idx in range(2):
            h = 2 * pair + idx
            rh = r[:, idx * LANES:(idx + 1) * LANES]
            k_ref[:, h * HEAD_PAD:h * HEAD_PAD + QK_NOPE] = (kh[idx] * rh * kn_nope).astype(BF16)
            k_ref[:, h * HEAD_PAD + QK_NOPE:(h + 1) * HEAD_PAD] = (pe_rot * rh).astype(BF16)


def _mla_proj_kernel(x_ref, mod_ref, cos_ref, sin_ref, wdq_ref, wdkv_ref, qan_ref, kvan_ref,
                     wuq_ref, wuk_ref, wuv_ref, qn_ref, kn_ref, ones_ref, q_ref, k_ref, v_ref, ckv_ref, kpe_ref):
    h = _modulated(x_ref, mod_ref, 0).astype(BF16)
    cos, sin = cos_ref[...], sin_ref[...]
    qa = (_rms(_dot(h, wdq_ref[...])) * qan_ref[...]).astype(BF16)
    q = _dot(qa, wuq_ref[...])
    qn_nope = qn_ref[:, 0:LANES]
    qn_rope = qn_ref[:, LANES:2 * LANES] * cos
    qn_part = qn_ref[:, 2 * LANES:3 * LANES] * sin
    scale = QK_DIM ** -0.5 * math.log2(math.e)
    for pair in range(MLA_HEADS // 2):
        blocks = []
        for hd in (2 * pair, 2 * pair + 1):
            base = hd * HEAD_PROJ
            blocks.append((q[:, base:base + LANES], q[:, base + LANES:base + 2 * LANES],
                           q[:, base + 2 * LANES:base + 3 * LANES]))
        ss = _pair_sums(*[qh * qh + qr * qr for qh, qr, _ in blocks], ones_ref)
        r = lax.rsqrt(ss * (1.0 / QK_DIM) + EPS) * scale
        for idx, (qh, qr, qp) in enumerate(blocks):
            hd = 2 * pair + idx
            rh = r[:, idx * LANES:(idx + 1) * LANES]
            q_ref[:, hd * HEAD_PAD:hd * HEAD_PAD + QK_NOPE] = (qh * rh * qn_nope).astype(BF16)
            q_ref[:, hd * HEAD_PAD + QK_NOPE:(hd + 1) * HEAD_PAD] = ((qr * qn_rope + qp * qn_part) * rh).astype(BF16)
    kva = _dot(h, wdkv_ref[...])
    ckv = _rms(kva[:, 0:KV_RANK]) * kvan_ref[...]
    kpe = kva[:, KV_RANK:KV_RANK + LANES]
    kpe_part = kva[:, KV_RANK + LANES:KV_RANK + 2 * LANES]
    ckv_ref[...] = ckv
    kpe_ref[...] = kpe
    pe_rot = kpe * (kn_ref[:, LANES:2 * LANES] * cos) + kpe_part * (kn_ref[:, 2 * LANES:3 * LANES] * sin)
    _expand_keys(ckv.astype(BF16), kpe * kpe, pe_rot, wuk_ref, wuv_ref, kn_ref, ones_ref, k_ref, v_ref)


def _mla_projection(x, mod, rope_tabs, wdq, wdkv, qan, kvan, wuq, wuk, wuv, qn, kn, ones):
    tm = TM_ROWS
    hw = MLA_HEADS * HEAD_PAD
    out_w = (hw, hw, MLA_HEADS * V_HEAD, KV_RANK, LANES)
    out_dt = (BF16, BF16, BF16, F32, F32)
    consts = (wdq, wdkv, qan, kvan, wuq, wuk, wuv, qn, kn, ones)
    return pl.pallas_call(
        _mla_proj_kernel,
        grid=(N_TOK // tm,),
        in_specs=[_row_spec(tm, D_MODEL), _mod_spec(tm)] + [_row_spec(tm, LANES)] * 2
                 + [_const_spec(cst.shape) for cst in consts],
        out_specs=[_row_spec(tm, w) for w in out_w],
        out_shape=[jax.ShapeDtypeStruct((N_TOK, w), dt) for w, dt in zip(out_w, out_dt)],
        compiler_params=_cparams("parallel"),
        name="mla_projection",
    )(x, mod, *rope_tabs, *consts)


def _cache_kv_kernel(ckv_ref, kpe_ref, wuk_ref, wuv_ref, kn_ref, ones_ref, k_ref, v_ref):
    kpe = kpe_ref[...]
    _expand_keys(ckv_ref[...].astype(BF16), kpe * kpe, kpe * kn_ref[:, LANES:2 * LANES],
                 wuk_ref, wuv_ref, kn_ref, ones_ref, k_ref, v_ref)


def _cache_keys(ckv, kpe, wuk, wuv, kn, ones):
    rows = ckv.shape[0]
    tm = TM_ROWS
    consts = (wuk, wuv, kn, ones)
    return pl.pallas_call(
        _cache_kv_kernel,
        grid=(rows // tm,),
        in_specs=[_row_spec(tm, KV_RANK), _row_spec(tm, LANES)] + [_const_spec(cst.shape) for cst in consts],
        out_specs=[_row_spec(tm, MLA_HEADS * HEAD_PAD), _row_spec(tm, MLA_HEADS * V_HEAD)],
        out_shape=[jax.ShapeDtypeStruct((rows, MLA_HEADS * HEAD_PAD), BF16),
                   jax.ShapeDtypeStruct((rows, MLA_HEADS * V_HEAD), BF16)],
        compiler_params=_cparams("parallel"),
        name="cache_keys",
    )(ckv, kpe, *consts)


def _attn_kernel(has_cache, tq, n_par, heads, *refs):
    if has_cache:
        q_ref, k_ref, v_ref, kc_ref, vc_ref, o_ref = refs
    else:
        q_ref, k_ref, v_ref, o_ref = refs

    def tile(r0, hd):
        qk = slice(hd * HEAD_PAD, (hd + 1) * HEAD_PAD)
        vo = slice(hd * V_HEAD, (hd + 1) * V_HEAD)
        q = q_ref[pl.ds(r0, tq), qk]
        s = _dot_nt(q, k_ref[:, qk])
        m = jnp.max(s, axis=-1, keepdims=True)
        if has_cache:
            sc = _dot_nt(q, kc_ref[:, qk])
            m = jnp.maximum(m, jnp.max(sc, axis=-1, keepdims=True))
            pc = jnp.exp2(sc - m)
        p = jnp.exp2(s - m)
        den = jnp.sum(p, axis=-1, keepdims=True)
        num = _dot(p.astype(BF16), v_ref[:, vo])
        if has_cache:
            den = den + jnp.sum(pc, axis=-1, keepdims=True)
            num = num + _dot(pc.astype(BF16), vc_ref[:, vo])
        o_ref[pl.ds(r0, tq), vo] = (num / den).astype(BF16)

    group = n_par * tq
    n_groups = q_ref.shape[0] // group
    if n_groups == 1:
        for hd in range(heads):
            for t in range(n_par):
                tile(t * tq, hd)
    else:
        def body(i, carry):
            r0 = pl.multiple_of(i * group, group)
            for hd in range(heads):
                for t in range(n_par):
                    tile(r0 + t * tq, hd)
            return carry
        lax.fori_loop(0, n_groups, body, 0)


def _attention(q, k, v, kc, vc, n_seq, seq, row_block0):
    has_cache = kc is not None
    if seq >= 2 * TQ:
        tq, n_par, heads = TQ, 8, 1
    else:
        tq, n_par, heads = seq, 1, MLA_HEADS
    seq_spec = lambda w: pl.BlockSpec((seq, heads * w), lambda b, h: (row_block0 + b, h))
    in_specs = [seq_spec(HEAD_PAD), seq_spec(HEAD_PAD), seq_spec(V_HEAD)]
    args = [q, k, v]
    if has_cache:
        past = kc.shape[0] // n_seq
        in_specs += [pl.BlockSpec((past, heads * HEAD_PAD), lambda b, h: (b, h)),
                     pl.BlockSpec((past, heads * V_HEAD), lambda b, h: (b, h))]
        args += [kc, vc]
    return pl.pallas_call(
        functools.partial(_attn_kernel, has_cache, tq, n_par, heads),
        grid=(n_seq, MLA_HEADS // heads),
        in_specs=in_specs,
        out_specs=pl.BlockSpec((seq, heads * V_HEAD), lambda b, h: (b, h)),
        out_shape=jax.ShapeDtypeStruct((n_seq * seq, MLA_HEADS * V_HEAD), BF16),
        compiler_params=_cparams("parallel", "parallel"),
        name="attention_%d" % seq,
    )(*args)


def _split_row_specs(tm, width):
    n_ctx = N_PROMPT // tm
    return [pl.BlockSpec((tm, width), lambda i, *_: (jnp.minimum(i, n_ctx - 1), 0)),
            pl.BlockSpec((tm, width), lambda i, *_: (jnp.maximum(i - n_ctx, 0), 0))]


def _select_rows(ctx_ref, lat_ref):
    tm = ctx_ref.shape[0]
    return jnp.where(pl.program_id(0) < N_PROMPT // tm, ctx_ref[...], lat_ref[...])


def _attn_out_router_kernel(x_ref, mod_ref, ac_ref, al_ref, wo_ref, wr_ref, tri_ref,
                            xo_ref, h_ref, meta_ref, cnt_ref, run_ref):
    @pl.when(pl.program_id(0) == 0)
    def _():
        run_ref[...] = jnp.zeros_like(run_ref)

    x_new = x_ref[...] + mod_ref[0, 2:3, :] * _dot(_select_rows(ac_ref, al_ref), wo_ref[...])
    xo_ref[...] = x_new
    h = _rms(x_new) * (1.0 + mod_ref[0, 4:5, :]) + mod_ref[0, 3:4, :]
    h_ref[...] = h.astype(BF16)
    lane = lax.broadcasted_iota(jnp.int32, (x_ref.shape[0], LANES), 1)
    lane_f = lane.astype(F32)
    logits = jnp.where(lane < N_EXPERTS, _dot3(h, wr_ref[...]), -jnp.inf)
    m1 = jnp.max(logits, axis=-1, keepdims=True)
    e1 = jnp.min(jnp.where(logits == m1, lane_f, float(LANES)), axis=-1, keepdims=True)
    rest = jnp.where(lane_f == e1, -jnp.inf, logits)
    m2 = jnp.max(rest, axis=-1, keepdims=True)
    e2 = jnp.min(jnp.where(rest == m2, lane_f, float(LANES)), axis=-1, keepdims=True)
    t = jnp.exp(m2 - m1)
    g1 = 1.0 / (1.0 + t)
    pick1 = lane_f == e1
    pick2 = lane_f == e2
    onehot = jnp.where(pick1, 1.0, jnp.where(pick2, 1.0, 0.0))
    before = run_ref[...] + _dot(tri_ref[...], onehot.astype(BF16)) - onehot
    rank1 = jnp.sum(jnp.where(pick1, before, 0.0), axis=-1, keepdims=True)
    rank2 = jnp.sum(jnp.where(pick2, before, 0.0), axis=-1, keepdims=True)
    run_ref[...] += jnp.sum(onehot, axis=0, keepdims=True)
    cnt_ref[...] = jnp.broadcast_to(run_ref[...], cnt_ref.shape)
    cols = (e1, e2, g1, 1.0 - g1, rank1, rank2)
    meta = jnp.zeros(meta_ref.shape, F32)
    for idx, val in enumerate(cols):
        meta = jnp.where(lane == idx, val, meta)
    meta_ref[...] = meta


META_EXPERT, META_GATE, META_RANK = 0, 2, 4


def _attn_out_router(x, mod, a_ctx, a_lat, w_o, wr):
    tm = TM_ROWS
    k = a_ctx.shape[1]
    r = jnp.arange(tm)
    tri = (r[:, None] >= r[None, :]).astype(BF16)
    return pl.pallas_call(
        _attn_out_router_kernel,
        grid=(N_TOK // tm,),
        in_specs=[_row_spec(tm, D_MODEL), _mod_spec(tm)] + _split_row_specs(tm, k)
                 + [_const_spec((k, D_MODEL)), _const_spec((D_MODEL, LANES)), _const_spec((tm, tm))],
        out_specs=[_row_spec(tm, D_MODEL), _row_spec(tm, D_MODEL), _row_spec(tm, LANES),
                   _const_spec((SEG_PAD, LANES))],
        out_shape=[jax.ShapeDtypeStruct((N_TOK, D_MODEL), F32), jax.ShapeDtypeStruct((N_TOK, D_MODEL), BF16),
                   jax.ShapeDtypeStruct((N_TOK, LANES), F32), jax.ShapeDtypeStruct((SEG_PAD, LANES), F32)],
        scratch_shapes=[pltpu.VMEM((1, LANES), F32)],
        compiler_params=_cparams("arbitrary"),
        name="attn_out_router",
    )(x, mod, a_ctx, a_lat, w_o, wr, tri)


N_FF_CHUNKS = D_FF // FF_CHUNK
MOE_STEPS = 2
CHUNKS_PER_STEP = -(-N_FF_CHUNKS // MOE_STEPS)
TF_MOE = CHUNKS_PER_STEP * FF_CHUNK


def _experts_kernel(be_ref, bv_ref, nu_ref, rows_ref, wg_ref, wu_ref, wd_ref, o_ref, acc_ref):
    i = pl.program_id(0)
    f = pl.program_id(1)

    def chunk(c, rows):
        cols = slice(c * FF_CHUNK, (c + 1) * FF_CHUNK)
        gate = _dot(rows, wg_ref[0, 0, :, cols].astype(BF16))
        up = _dot(rows, wu_ref[0, 0, :, cols].astype(BF16))
        return _dot((_silu(gate) * up).astype(BF16), wd_ref[0, 0, cols, :].astype(BF16))

    @pl.when(bv_ref[i] > 0)
    def _():
        rows = rows_ref[...]
        acc = chunk(0, rows)
        for c in range(1, CHUNKS_PER_STEP - 1):
            acc = acc + chunk(c, rows)

        @pl.when(f == 0)
        def _():
            acc_ref[...] = acc

        @pl.when(f > 0)
        def _():
            acc_ref[...] += acc

        @pl.when((f * CHUNKS_PER_STEP + CHUNKS_PER_STEP) * FF_CHUNK <= D_FF)
        def _():
            acc_ref[...] += chunk(CHUNKS_PER_STEP - 1, rows_ref[...])

    @pl.when(f == pl.num_programs(1) - 1)
    def _():
        o_ref[...] = acc_ref[...].astype(BF16)


def _expert_ffn(rows, block_expert, block_valid, n_used, wg, wu, wd, layer):
    n_rows = rows.shape[0]
    tm = TM_MOE
    tf = TF_MOE
    n_f = MOE_STEPS

    def f_idx(i, f, nu):
        return jnp.where(i < nu[0], f, n_f - 1)

    grid_spec = pltpu.PrefetchScalarGridSpec(
        num_scalar_prefetch=3,
        grid=(n_rows // tm, n_f),
        in_specs=[pl.BlockSpec((tm, D_MODEL), lambda i, f, be, bv, nu: (i, 0)),
                  pl.BlockSpec((1, 1, D_MODEL, tf), lambda i, f, be, bv, nu: (layer, be[i], 0, f_idx(i, f, nu))),
                  pl.BlockSpec((1, 1, D_MODEL, tf), lambda i, f, be, bv, nu: (layer, be[i], 0, f_idx(i, f, nu))),
                  pl.BlockSpec((1, 1, tf, D_MODEL), lambda i, f, be, bv, nu: (layer, be[i], f_idx(i, f, nu), 0))],
        out_specs=pl.BlockSpec((tm, D_MODEL), lambda i, f, be, bv, nu: (i, 0)),
        scratch_shapes=[pltpu.VMEM((tm, D_MODEL), F32)])
    return pl.pallas_call(
        _experts_kernel,
        grid_spec=grid_spec,
        out_shape=jax.ShapeDtypeStruct((n_rows, D_MODEL), BF16),
        compiler_params=_cparams("arbitrary", "arbitrary"),
        name="expert_ffn",
    )(block_expert, block_valid, n_used, rows, wg, wu, wd)


def _combine_kernel(split, x_ref, mod_ref, meta_ref, a_ref, b_ref, *o_refs):
    g1 = meta_ref[:, META_GATE:META_GATE + 1]
    g2 = meta_ref[:, META_GATE + 1:META_GATE + 2]
    y = a_ref[...].astype(F32) * g1 + b_ref[...].astype(F32) * g2
    res = x_ref[...] + mod_ref[0, 5:6, :] * y
    if not split:
        o_refs[0][...] = res
        return
    is_ctx = pl.program_id(0) < N_PROMPT // x_ref.shape[0]

    @pl.when(is_ctx)
    def _():
        o_refs[0][...] = res

    @pl.when(jnp.logical_not(is_ctx))
    def _():
        o_refs[1][...] = res


def _moe_combine(x, mod, meta, a, b, split):
    tm = TM_FFN
    if split:
        out_specs = _split_row_specs(tm, D_MODEL)
        out_shape = [jax.ShapeDtypeStruct((N_PROMPT, D_MODEL), F32), jax.ShapeDtypeStruct((N_SAMPLE, D_MODEL), F32)]
    else:
        out_specs = _row_spec(tm, D_MODEL)
        out_shape = jax.ShapeDtypeStruct((N_TOK, D_MODEL), F32)
    return pl.pallas_call(
        functools.partial(_combine_kernel, split),
        grid=(N_TOK // tm,),
        in_specs=[_row_spec(tm, D_MODEL), _mod_spec(tm), _row_spec(tm, LANES), _row_spec(tm, D_MODEL),
                  _row_spec(tm, D_MODEL)],
        out_specs=out_specs,
        out_shape=out_shape,
        compiler_params=_cparams("arbitrary"),
        name="moe_combine",
    )(x, mod, meta, a, b)


def _moe(x, mod, h, meta, counts, wg, wu, wd, layer, split_out):
    tm = TM_MOE
    n_assign = N_TOK * TOP_K
    experts = meta[:, META_EXPERT:META_EXPERT + TOP_K].astype(jnp.int32)
    rank = meta[:, META_RANK:META_RANK + TOP_K].astype(jnp.int32)
    count = counts[0, :N_EXPERTS].astype(jnp.int32)
    padded = (count + tm - 1) // tm * tm
    pad_end = jnp.cumsum(padded)
    pad_start = pad_end - padded
    onehot = experts[:, :, None] == jnp.arange(N_EXPERTS, dtype=jnp.int32)
    dest = jnp.sum(jnp.where(onehot, pad_start, 0), axis=-1) + rank
    n_rows = n_assign + N_EXPERTS * tm
    n_blocks = n_rows // tm
    row_token = jnp.zeros((n_rows,), jnp.int32).at[dest.reshape(-1)].set(
        jnp.arange(n_assign, dtype=jnp.int32) // TOP_K)
    n_used = (pad_end[-1] // tm).astype(jnp.int32).reshape(1)
    block_id = jnp.arange(n_blocks, dtype=jnp.int32)
    block_start = jnp.minimum(block_id, n_used[0] - 1) * tm
    block_expert = jnp.minimum(jnp.sum((pad_end[None, :] <= block_start[:, None]).astype(jnp.int32), axis=1),
                               N_EXPERTS - 1)
    last_row = jnp.sum(jnp.where(block_expert[:, None] == jnp.arange(N_EXPERTS, dtype=jnp.int32),
                                 pad_start + count, 0), axis=1)
    block_valid = jnp.where(block_id < n_used[0], jnp.clip(last_row - block_start, 0, tm), 0)
    out = _expert_ffn(h[row_token], block_expert, block_valid, n_used, wg, wu, wd, layer)
    return _moe_combine(x, mod, meta, out[dest[:, 0]], out[dest[:, 1]], split_out)


def _rope_tables():
    half = ROPE_AXIS // 2
    pos = jnp.arange(DEC_SEQ)
    row = (pos // GRID_W).astype(F32)
    col = (pos % GRID_W).astype(F32)
    inv = ROPE_THETA ** (-jnp.arange(0, ROPE_AXIS, 2, dtype=F32) / ROPE_AXIS)
    ang_r = row[:, None] * inv
    ang_c = col[:, None] * inv
    pad = jnp.zeros((DEC_SEQ, LANES - QK_ROPE), F32)
    cos = jnp.concatenate([jnp.cos(ang_r), jnp.cos(ang_r), jnp.cos(ang_c), jnp.cos(ang_c), pad + 1.0], axis=1)
    sin = jnp.concatenate([-jnp.sin(ang_r), jnp.sin(ang_r), -jnp.sin(ang_c), jnp.sin(ang_c), pad], axis=1)
    prompt = jnp.zeros((N_PROMPT, LANES), F32)
    tile = lambda t: jnp.tile(t, (DEC_BATCH, 1))
    return (jnp.concatenate([prompt + 1.0, tile(cos)], axis=0),
            jnp.concatenate([prompt, tile(sin)], axis=0))


def _rope_blocks(a):
    half = ROPE_AXIS // 2
    partner = jnp.concatenate([a[..., half:2 * half], a[..., 0:half], a[..., 3 * half:4 * half],
                               a[..., 2 * half:3 * half]], axis=-1)
    return jnp.concatenate([_pad_lanes(a, LANES), _pad_lanes(partner, LANES)], axis=-1)


def _pad_lanes(a, width):
    return jnp.pad(a, [(0, 0)] * (a.ndim - 1) + [(0, width - a.shape[-1])])


def _ssd_constants():
    r = jnp.arange(CHUNK)
    tri = (r[:, None] >= r[None, :]).astype(BF16)
    head_of = jnp.arange(D_SSD) // SSD_HEADDIM
    e_f = (r[:, None] == head_of[None, :]).astype(BF16)
    e_b = (r[:, None] == head_of[None, :] + SSD_HEADS).astype(BF16)
    return tri, e_f, e_b


def kernel(x_prompt, x_sample, c, state_ssm, cache_ckv, cache_kpe, c_ctx, w_mod, b_mod, w_in, conv_w, conv_b, dt_bias, a_log, d_skip, ssd_norm, sgu_norm, w_sp, b_sp, w_out, ffn_w_gate, ffn_w_up, ffn_w_down, w_dq, q_a_norm, w_uq, w_dkv, kv_a_norm, w_ukv, q_norm, k_norm, w_o, router, moe_w_gate, moe_w_up, moe_w_down):
    x = jnp.concatenate([x_prompt.reshape(N_PROMPT, D_MODEL), x_sample.reshape(N_SAMPLE, D_MODEL)], axis=0)
    cond = jnp.concatenate([c_ctx[None, :], c, jnp.zeros((SEG_PAD - N_SEG, D_MODEL), F32)], axis=0)
    mods = _modulation_tables(cond, w_mod, b_mod).reshape(DEPTH, SEG_PAD, N_MOD, D_MODEL)
    rope_tabs = _rope_tables()
    tri, e_f, e_b = _ssd_constants()
    blk = jnp.arange(2 * LANES) // LANES
    pair_ones = (blk[:, None] == blk[None, :]).astype(BF16)
    i1 = D_SSD
    i2 = i1 + CONV_CH
    i3 = i2 + 2 * SSD_HEADS
    i4 = i3 + D_SGU
    new_ssm, new_ckv, new_kpe = [], [], []
    for i in range(DEPTH):
        j = i // 2
        mod = mods[i]
        if i % 2 == 0:
            w = w_in[j]
            wx = _col_blocks(w[:, i1:i2].astype(BF16), XBC_BLOCK)
            wgate = _col_blocks(jnp.concatenate([w[:, 0:i1], w[:, i3:]], axis=1).astype(BF16), GATE_BLOCK)
            wdt = _pad_lanes(w[:, i2:i3], LANES).astype(BF16)
            a_log_row = _pad_lanes(a_log[j].reshape(1, -1), LANES)
            xbc, gates, dt, cs = _in_projection(
                x, mod, (wx, wgate, wdt),
                (_col_blocks(conv_w[j], XBC_BLOCK), _col_blocks(conv_b[j][None, :], XBC_BLOCK),
                 _pad_lanes(dt_bias[j].reshape(1, -1), LANES), a_log_row, tri))
            consts = (a_log_row, jnp.repeat(d_skip[j], SSD_HEADDIM)[None, :], e_f, e_b)
            y_ctx, st_p = _ssd_scan(xbc, dt, cs, None, consts, BATCH, SEQ, 0)
            init = state_ssm[:, j].reshape(DEC_BATCH, 2, D_SSD, SSD_STATE)
            y_lat, _ = _ssd_scan(xbc, dt, cs, init, consts, DEC_BATCH, DEC_SEQ, N_PROMPT // DEC_SEQ)
            new_ssm.append(st_p.reshape(BATCH, 2, SSD_HEADS, SSD_HEADDIM, SSD_STATE))
            b_sp_e = jnp.repeat(b_sp[j].T, SGU_GDIM, axis=1)
            x = _mixer_output(x, mod, y_ctx, y_lat, gates, ssd_norm[j][None, :], sgu_norm[j][None, :],
                              w_sp[j].astype(BF16), b_sp_e, w_out[j].astype(BF16))
            x = _dense_ffn(x, mod, ffn_w_gate[j].astype(BF16), ffn_w_up[j].astype(BF16),
                           ffn_w_down[j].astype(BF16))
        else:
            split = lambda a: jnp.concatenate([a[..., :QK_NOPE], _rope_blocks(a[..., QK_NOPE:])], axis=-1)
            wuq = split(w_uq[j].reshape(Q_RANK, MLA_HEADS, QK_DIM)).reshape(Q_RANK, MLA_HEADS * HEAD_PROJ)
            wdkv = jnp.concatenate([w_dkv[j][:, :KV_RANK], _rope_blocks(w_dkv[j][:, KV_RANK:])], axis=-1)
            wukv = w_ukv[j].reshape(KV_RANK, MLA_HEADS, QK_NOPE + V_HEAD)
            wuk = wukv[:, :, :QK_NOPE].reshape(KV_RANK, -1).astype(BF16)
            wuv = wukv[:, :, QK_NOPE:].reshape(KV_RANK, -1).astype(BF16)
            qn = split(q_norm[j][None, :])
            kn = split(k_norm[j][None, :])
            q, k, v, ckv, kpe = _mla_projection(
                x, mod, rope_tabs, w_dq[j].astype(BF16), wdkv.astype(BF16), q_a_norm[j][None, :],
                kv_a_norm[j][None, :], wuq.astype(BF16), wuk, wuv, qn, kn, pair_ones)
            new_ckv.append(ckv[:N_PROMPT].reshape(BATCH, SEQ, KV_RANK))
            new_kpe.append(kpe[:N_PROMPT, :QK_ROPE].reshape(BATCH, SEQ, QK_ROPE))
            kc, vc = _cache_keys(cache_ckv[:, j].reshape(DEC_BATCH * PAST_LEN, KV_RANK),
                                 _pad_lanes(cache_kpe[:, j].reshape(DEC_BATCH * PAST_LEN, QK_ROPE), LANES),
                                 wuk, wuv, kn, pair_ones)
            o_ctx = _attention(q, k, v, None, None, BATCH, SEQ, 0)
            o_lat = _attention(q, k, v, kc, vc, DEC_BATCH, DEC_SEQ, N_PROMPT // DEC_SEQ)
            x, h, meta, counts = _attn_out_router(x, mod, o_ctx, o_lat, w_o[j].astype(BF16),
                                                  _pad_lanes(router[j], LANES))
            x = _moe(x, mod, h, meta, counts, moe_w_gate, moe_w_up, moe_w_down, j, split_out=(i == DEPTH - 1))
    assert DEPTH % 2 == 0
    x_ctx, x_lat = x
    return (x_ctx.reshape(BATCH, SEQ, D_MODEL),
            x_lat.reshape(DEC_BATCH, DEC_SEQ, D_MODEL),
            jnp.stack(new_ssm, axis=1),
            jnp.stack(new_ckv, axis=1),
            jnp.stack(new_kpe, axis=1))
```

```python
import functools
import math

import jax
import jax.numpy as jnp
from jax import lax
from jax.experimental import pallas as pl
from jax.experimental.pallas import tpu as pltpu

F32 = jnp.float32
BF16 = jnp.bfloat16

D_MODEL = 1024
BATCH = 16
SEQ = 256
DEPTH = 4
DEC_BATCH = 4
DEC_SEQ = 2048
PAST_LEN = 512
GRID_W = 64
N_MOD = 6
EPS = 1e-6

SSD_HEADDIM = 64
SSD_HEADS = 16
D_SSD = 1024
SSD_GROUPS = 4
SSD_STATE = 128
CHUNK = 128
CONV_W = 5
CONV_CH = D_SSD + 2 * SSD_GROUPS * SSD_STATE
D_SGU = 1024
SGU_GROUPS = 4
SGU_GDIM = D_SGU // SGU_GROUPS

MLA_HEADS = 8
Q_RANK = 384
KV_RANK = 256
QK_NOPE = 128
QK_ROPE = 64
V_HEAD = 128
QK_DIM = QK_NOPE + QK_ROPE
ROPE_AXIS = QK_ROPE // 2
ROPE_THETA = 10000.0
HEAD_PAD = 256

D_FF = 2816
N_EXPERTS = 8
TOP_K = 2

N_PROMPT = BATCH * SEQ
N_SAMPLE = DEC_BATCH * DEC_SEQ
N_TOK = N_PROMPT + N_SAMPLE
N_SEG = 1 + DEC_BATCH
SEG_PAD = 8

LANES = 128
VMEM_LIMIT = 56 * 1024 * 1024

TM_ROWS = 256
TM_FFN = 512
FF_CHUNK = 256
TM_MOE = 640
TQ = 256


def _cparams(*sem):
    return pltpu.CompilerParams(dimension_semantics=sem, vmem_limit_bytes=VMEM_LIMIT)


def _dot(a, b):
    return jnp.dot(a, b, preferred_element_type=F32)


def _dot_nt(a, b):
    return lax.dot_general(a, b, (((1,), (1,)), ((), ())), preferred_element_type=F32)


def _split(x):
    hi = x.astype(BF16)
    lo = (x - hi.astype(F32)).astype(BF16)
    return hi, lo


def _dot3(a, b):
    ah, al = _split(a)
    bh, bl = _split(b)
    return _dot(ah, bh) + _dot(ah, bl) + _dot(al, bh)


def _silu(x):
    return x / (1.0 + jnp.exp(-x))


def _rms(x):
    return x * lax.rsqrt(jnp.mean(x * x, axis=-1, keepdims=True) + EPS)


def _modulated(x_ref, mod_ref, first):
    shift = mod_ref[0, first:first + 1, :]
    scale = mod_ref[0, first + 1:first + 2, :]
    return _rms(x_ref[...]) * (1.0 + scale) + shift


def _seg_map(tm):
    def index_map(i, *_):
        r = i * tm
        return (jnp.where(r < N_PROMPT, 0, 1 + (r - N_PROMPT) // DEC_SEQ), 0, 0)
    return index_map


def _row_spec(tm, width):
    return pl.BlockSpec((tm, width), lambda i, *_: (i, 0))


def _const_spec(shape):
    zeros = (0,) * len(shape)
    return pl.BlockSpec(shape, lambda i, *_: zeros)


def _mod_spec(tm):
    return pl.BlockSpec((1, N_MOD, D_MODEL), _seg_map(tm))


def _mod_kernel(c_ref, w_ref, b_ref, o_ref):
    o_ref[0] = _dot3(_silu(c_ref[...]), w_ref[0]) + b_ref[0]


def _modulation_tables(cond, w_mod, b_mod):
    tn = 1536
    return pl.pallas_call(
        _mod_kernel,
        grid=(DEPTH, N_MOD * D_MODEL // tn),
        in_specs=[pl.BlockSpec((SEG_PAD, D_MODEL), lambda l, j: (0, 0)),
                  pl.BlockSpec((1, D_MODEL, tn), lambda l, j: (l, 0, j)),
                  pl.BlockSpec((1, 1, tn), lambda l, j: (l, 0, j))],
        out_specs=pl.BlockSpec((1, SEG_PAD, tn), lambda l, j: (l, 0, j)),
        out_shape=jax.ShapeDtypeStruct((DEPTH, SEG_PAD, N_MOD * D_MODEL), F32),
        compiler_params=_cparams("parallel", "parallel"),
        name="modulation",
    )(cond, w_mod, b_mod.reshape(DEPTH, 1, N_MOD * D_MODEL))


HALO = 8


XBC_BLOCK = 512
N_XBC = CONV_CH // XBC_BLOCK
GATE_BLOCK = 256
N_GATE = (D_SSD + 2 * D_SGU) // GATE_BLOCK
GATE_PER_STAGE = N_GATE // N_XBC


def _inproj_kernel(x_ref, xp_ref, xn_ref, mod_ref, wx_ref, wg_ref, wdt, convw_ref, convb_ref, dtb_ref, alog_ref,
                   tri_ref, xbc_ref, gate_ref, dt_ref, cs_ref, h_ref, xa_ref, xb_ref):
    tm = x_ref.shape[0]
    r0 = pl.program_id(0) * tm
    shift = mod_ref[0, 0:1, :]
    scale = mod_ref[0, 1:2, :]
    x_ext = jnp.concatenate([x_ref[...], xp_ref[...], xn_ref[...]], axis=0)
    h_ref[...] = (_rms(x_ext) * (1.0 + scale) + shift).astype(BF16)

    rel = r0 - N_PROMPT
    in_latent = r0 >= N_PROMPT
    keep_prev = jnp.where(jnp.logical_and(in_latent, lax.rem(rel, DEC_SEQ) != 0), 1.0, 0.0)
    keep_next = jnp.where(jnp.logical_and(in_latent, lax.rem(rel + tm, DEC_SEQ) != 0), 1.0, 0.0)
    reach = CONV_W // 2

    bufs = (xa_ref, xb_ref)
    bufs[0][...] = _dot(h_ref[...], wx_ref[0])

    def stage(s, cur_ref, nxt_ref):
        if s + 1 < N_XBC:
            nxt_ref[...] = _dot(h_ref[...], wx_ref[s + 1])
        for t in range(GATE_PER_STAGE):
            g = s * GATE_PER_STAGE + t
            gate_ref[g] = _dot(h_ref[0:tm, :], wg_ref[g]).astype(BF16)
        for half in range(XBC_BLOCK // 256):
            cols = slice(half * 256, (half + 1) * 256)
            win = jnp.concatenate([cur_ref[tm:tm + HALO, cols] * keep_prev, cur_ref[0:tm, cols],
                                   cur_ref[tm + HALO:tm + 2 * HALO, cols] * keep_next], axis=0)
            acc = jnp.zeros((tm, 256), F32) + convb_ref[s, :, cols]
            for k in range(CONV_W):
                start = HALO - reach + k
                acc = acc + win[start:start + tm, :] * convw_ref[s, k:k + 1, cols]
            xbc_ref[s, :, cols] = _silu(acc).astype(BF16)

    for s in range(N_XBC):
        stage(s, bufs[s % 2], bufs[1 - s % 2])

    h = h_ref[0:tm, :]
    lane = lax.broadcasted_iota(jnp.int32, (1, LANES), 1)
    raw = _dot(h, wdt[...]) + dtb_ref[...]
    dt = jnp.where(lane < 2 * SSD_HEADS, jnp.maximum(raw, 0.0) + jnp.log(1.0 + jnp.exp(-jnp.abs(raw))), 0.0)
    dt_ref[...] = dt
    ac = dt * -jnp.exp(alog_ref[...])
    hi = ac.astype(BF16)
    rest = ac - hi.astype(F32)
    mid = rest.astype(BF16)
    lo = (rest - mid.astype(F32)).astype(BF16)
    tri = tri_ref[...]
    for k in range(tm // CHUNK):
        rows = slice(k * CHUNK, (k + 1) * CHUNK)
        cs_ref[rows, :] = _dot(tri, hi[rows, :]) + _dot(tri, mid[rows, :]) + _dot(tri, lo[rows, :])


def _col_blocks(w, block):
    return w.reshape(w.shape[0], -1, block).transpose(1, 0, 2)


def _in_projection(x, mod, weights, consts):
    tm = TM_ROWS
    per_tile = tm // HALO
    last = N_TOK // HALO - 1
    halo_prev = pl.BlockSpec((HALO, D_MODEL), lambda i: (jnp.maximum(i * per_tile - 1, 0), 0))
    halo_next = pl.BlockSpec((HALO, D_MODEL), lambda i: (jnp.minimum((i + 1) * per_tile, last), 0))
    blocked = lambda n, w: pl.BlockSpec((n, tm, w), lambda i: (0, i, 0))
    return pl.pallas_call(
        _inproj_kernel,
        grid=(N_TOK // tm,),
        in_specs=[_row_spec(tm, D_MODEL), halo_prev, halo_next, _mod_spec(tm)]
                 + [_const_spec(a.shape) for a in weights + consts],
        out_specs=[blocked(N_XBC, XBC_BLOCK), blocked(N_GATE, GATE_BLOCK), _row_spec(tm, LANES),
                   _row_spec(tm, LANES)],
        out_shape=[jax.ShapeDtypeStruct((N_XBC, N_TOK, XBC_BLOCK), BF16),
                   jax.ShapeDtypeStruct((N_GATE, N_TOK, GATE_BLOCK), BF16),
                   jax.ShapeDtypeStruct((N_TOK, LANES), F32), jax.ShapeDtypeStruct((N_TOK, LANES), F32)],
        scratch_shapes=[pltpu.VMEM((tm + 2 * HALO, D_MODEL), BF16),
                        pltpu.VMEM((tm + 2 * HALO, XBC_BLOCK), F32),
                        pltpu.VMEM((tm + 2 * HALO, XBC_BLOCK), F32)],
        compiler_params=_cparams("parallel"),
        name="in_projection",
    )(x, x, x, mod, *weights, *consts)


def _ssd_kernel(has_init, n_chunks, *refs):
    refs = list(refs)
    cv_ref, dtv_ref, cs_ref = refs[0:3]
    del refs[0:3]
    init_ref = refs.pop(0) if has_init else None
    alog_ref, dskip_ref, ef_ref, eb_ref, y_ref, fin_ref, st_ref = refs
    a_neg = -jnp.exp(alog_ref[...])
    rows_i = lax.broadcasted_iota(jnp.int32, (CHUNK, CHUNK), 0)
    cols_i = lax.broadcasted_iota(jnp.int32, (CHUNK, CHUNK), 1)
    lower = rows_i >= cols_i
    upper = cols_i >= rows_i
    even_head = jnp.bitwise_and(lax.broadcasted_iota(jnp.int32, (CHUNK, D_SSD), 1), LANES - 1) < SSD_HEADDIM

    assert SSD_GROUPS * SSD_STATE == XBC_BLOCK
    b_block = D_SSD // XBC_BLOCK

    def row0(c):
        return pl.multiple_of(c * CHUNK, CHUNK)

    def chunk_pass(c, forward):
        r0 = row0(c)
        rows = pl.ds(r0, CHUNK)
        e_ref = ef_ref if forward else eb_ref
        off = 0 if forward else SSD_HEADS
        dt = dtv_ref[rows, :]
        cs = cs_ref[rows, :]
        total = cs_ref[pl.ds(r0 + CHUNK - 1, 1), :]
        if forward:
            pos = cs
            to_edge = jnp.exp(total - cs)
            from_edge = jnp.exp(cs)
            mask = lower
        else:
            pos = cs - dt * a_neg
            to_edge = jnp.exp(pos)
            from_edge = jnp.exp(total - pos)
            mask = upper
        pos_t = pos.T
        x = jnp.concatenate([cv_ref[b, rows, :] for b in range(D_SSD // XBC_BLOCK)], axis=1).astype(F32)
        stacked = jnp.concatenate([dt, dt * to_edge, from_edge], axis=0).astype(BF16)
        spread = _dot(stacked, e_ref[...])
        xc = x * spread[0:CHUNK]
        xc_sub = (jnp.where(even_head, xc, 0.0).astype(BF16), jnp.where(even_head, 0.0, xc).astype(BF16))
        xd_b = (x * spread[CHUNK:2 * CHUNK]).astype(BF16)
        from_e = spread[2 * CHUNK:3 * CHUNK]
        tot_hi, tot_lo = _split(jnp.exp(jnp.broadcast_to(total, (8, LANES))))
        tot_e = (_dot(tot_hi, e_ref[...]) + _dot(tot_lo, e_ref[...]))[0:1, :]
        st = st_ref[0 if forward else 1]
        st_b = st.astype(BF16)
        y_parts = []
        st_parts = []
        for g in range(SSD_GROUPS):
            bm = cv_ref[b_block, rows, g * SSD_STATE:(g + 1) * SSD_STATE]
            cm = cv_ref[b_block + 1, rows, g * SSD_STATE:(g + 1) * SSD_STATE]
            bm_t = bm.astype(F32).T.astype(BF16)
            cb = _dot(cm, bm_t)
            gcols = slice(g * 256, (g + 1) * 256)
            y_off = _dot(cm, st_b[:, gcols])
            st_parts.append(_dot(bm_t, xd_b[:, gcols]))
            diag = []
            for pair in range(2):
                pcols = slice(g * 256 + pair * LANES, g * 256 + (pair + 1) * LANES)
                decayed = []
                for sub in range(2):
                    h = g * 4 + pair * 2 + sub
                    col = pos[:, off + h:off + h + 1]
                    row = pos_t[off + h:off + h + 1, :]
                    diff = (col - row) if forward else (row - col)
                    decayed.append((jnp.where(mask, jnp.exp(diff), 0.0) * cb).astype(BF16))
                diag.append(_dot(jnp.concatenate(decayed, axis=1),
                                 jnp.concatenate([xc_sub[0][:, pcols], xc_sub[1][:, pcols]], axis=0)))
            y_parts.append(jnp.concatenate(diag, axis=1) + y_off * from_e[:, gcols])
        y_new = jnp.concatenate(y_parts, axis=1)
        st_ref[0 if forward else 1] = st * tot_e + jnp.concatenate(st_parts, axis=1)
        if forward:
            y_new = y_new + x * dskip_ref[...]
        return rows, y_new

    if has_init:
        st_ref[0] = init_ref[0, 0].T
        st_ref[1] = init_ref[0, 1].T
    else:
        st_ref[...] = jnp.zeros_like(st_ref)

    def first_half(i, carry):
        for rows, y_new in (chunk_pass(i, True), chunk_pass(n_chunks - 1 - i, False)):
            y_ref[rows, :] = y_new
        return carry

    def second_half(i, carry):
        for rows, y_new in (chunk_pass(i, True), chunk_pass(n_chunks - 1 - i, False)):
            y_ref[rows, :] = y_ref[rows, :] + y_new
        return carry

    lax.fori_loop(0, n_chunks // 2, first_half, 0)
    lax.fori_loop(n_chunks // 2, n_chunks, second_half, 0)
    fin_ref[0, 0] = st_ref[0].T
    fin_ref[0, 1] = st_ref[1].T


def _ssd_scan(xbc, dt, cs, init, consts, n_seq, seq, row_block0):
    n_chunks = seq // CHUNK
    assert n_chunks % 2 == 0
    has_init = init is not None
    once = pl.Buffered(1)
    seq_spec = lambda w: pl.BlockSpec((seq, w), lambda b: (row_block0 + b, 0), pipeline_mode=once)
    xbc_spec = pl.BlockSpec((N_XBC, seq, XBC_BLOCK), lambda b: (0, row_block0 + b, 0), pipeline_mode=once)
    in_specs = [xbc_spec, seq_spec(LANES), seq_spec(LANES)]
    args = [xbc, dt, cs]
    if has_init:
        in_specs.append(pl.BlockSpec((1, 2, D_SSD, SSD_STATE), lambda b: (b, 0, 0, 0), pipeline_mode=once))
        args.append(init)
    for cst in consts:
        in_specs.append(pl.BlockSpec(cst.shape, lambda b, nd=cst.ndim: (0,) * nd))
        args.append(cst)
    return pl.pallas_call(
        functools.partial(_ssd_kernel, has_init, n_chunks),
        grid=(n_seq,),
        in_specs=in_specs,
        out_specs=[pl.BlockSpec((seq, D_SSD), lambda b: (b, 0)),
                   pl.BlockSpec((1, 2, D_SSD, SSD_STATE), lambda b: (b, 0, 0, 0))],
        out_shape=[jax.ShapeDtypeStruct((n_seq * seq, D_SSD), F32),
                   jax.ShapeDtypeStruct((n_seq, 2, D_SSD, SSD_STATE), F32)],
        scratch_shapes=[pltpu.VMEM((2, SSD_STATE, D_SSD), F32)],
        compiler_params=_cparams("parallel"),
        name="ssd_scan_%d" % seq,
    )(*args)


def _mixer_out_kernel(x_ref, mod_ref, yc_ref, yl_ref, z_ref, u_ref, v_ref, ssdn_ref, sgun_ref, wsp_ref, bsp_ref,
                      wo_ref, o_ref):
    tm = x_ref.shape[0]
    wide = lambda ref: jnp.concatenate([ref[b] for b in range(ref.shape[0])], axis=1).astype(F32)
    gated = _select_rows(yc_ref, yl_ref) * _silu(wide(z_ref))
    a = (_rms(gated) * ssdn_ref[...]).astype(BF16)
    vb = (_rms(wide(v_ref)) * sgun_ref[...]).astype(BF16)
    chunks = []
    for k in range(tm // CHUNK):
        rows = slice(k * CHUNK, (k + 1) * CHUNK)
        groups = [_dot(wsp_ref[g], vb[rows, g * SGU_GDIM:(g + 1) * SGU_GDIM]) for g in range(SGU_GROUPS)]
        chunks.append(jnp.concatenate(groups, axis=1) + bsp_ref[...])
    s = (wide(u_ref) * jnp.concatenate(chunks, axis=0)).astype(BF16)
    out = _dot(a, wo_ref[0:D_SSD, :]) + _dot(s, wo_ref[D_SSD:D_SSD + D_SGU, :])
    o_ref[...] = x_ref[...] + mod_ref[0, 2:3, :] * out


def _mixer_output(x, mod, y_ctx, y_lat, gates, ssd_norm, sgu_norm, w_sp, b_sp_e, w_out):
    tm = TM_FFN
    per = D_SSD // GATE_BLOCK
    part = lambda k: pl.BlockSpec((per, tm, GATE_BLOCK), lambda i: (k, i, 0))
    return pl.pallas_call(
        _mixer_out_kernel,
        grid=(N_TOK // tm,),
        in_specs=[_row_spec(tm, D_MODEL), _mod_spec(tm)] + _split_row_specs(tm, D_SSD) + [part(0), part(1), part(2),
                  _const_spec((1, D_SSD)), _const_spec((1, D_SGU)),
                  _const_spec((SGU_GROUPS, CHUNK, CHUNK)), _const_spec((CHUNK, D_SGU)),
                  _const_spec((D_SSD + D_SGU, D_MODEL))],
        out_specs=_row_spec(tm, D_MODEL),
        out_shape=jax.ShapeDtypeStruct((N_TOK, D_MODEL), F32),
        compiler_params=_cparams("parallel"),
        name="mixer_output",
    )(x, mod, y_ctx, y_lat, gates, gates, gates, ssd_norm, sgu_norm, w_sp, b_sp_e, w_out)


def _ffn_kernel(x_ref, mod_ref, wg_ref, wu_ref, wd_ref, o_ref):
    h = _modulated(x_ref, mod_ref, 3).astype(BF16)
    acc = jnp.zeros(o_ref.shape, F32)
    for f in range(D_FF // FF_CHUNK):
        cols = slice(f * FF_CHUNK, (f + 1) * FF_CHUNK)
        act = (_silu(_dot(h, wg_ref[:, cols])) * _dot(h, wu_ref[:, cols])).astype(BF16)
        acc = acc + _dot(act, wd_ref[cols, :])
    o_ref[...] = x_ref[...] + mod_ref[0, 5:6, :] * acc


def _dense_ffn(x, mod, wg, wu, wd):
    tm = TM_FFN
    once = pl.Buffered(1)
    return pl.pallas_call(
        _ffn_kernel,
        grid=(N_TOK // tm,),
        in_specs=[_row_spec(tm, D_MODEL), _mod_spec(tm),
                  pl.BlockSpec((D_MODEL, D_FF), lambda i: (0, 0), pipeline_mode=once),
                  pl.BlockSpec((D_MODEL, D_FF), lambda i: (0, 0), pipeline_mode=once),
                  pl.BlockSpec((D_FF, D_MODEL), lambda i: (0, 0), pipeline_mode=once)],
        out_specs=_row_spec(tm, D_MODEL),
        out_shape=jax.ShapeDtypeStruct((N_TOK, D_MODEL), F32),
        compiler_params=_cparams("parallel"),
        name="dense_ffn",
    )(x, mod, wg, wu, wd)


HEAD_PROJ = 3 * LANES


def _pair_sums(sq_a, sq_b, ones_ref):
    return _dot(jnp.concatenate([sq_a, sq_b], axis=1).astype(BF16), ones_ref[...])


def _expand_keys(ckv_b, pe_sq, pe_rot, wuk_ref, wuv_ref, kn_ref, ones_ref, k_ref, v_ref):
    kn_nope = kn_ref[:, 0:QK_NOPE]
    v_ref[...] = _dot(ckv_b, wuv_ref[...]).astype(BF16)
    k_nope = _dot(ckv_b, wuk_ref[...])
    for pair in range(MLA_HEADS // 2):
        kh = [k_nope[:, h * QK_NOPE:(h + 1) * QK_NOPE] for h in (2 * pair, 2 * pair + 1)]
        ss = _pair_sums(kh[0] * kh[0] + pe_sq, kh[1] * kh[1] + pe_sq, ones_ref)
        r = lax.rsqrt(ss * (1.0 / QK_DIM) + EPS)
        for idx in range(2):
            h = 2 * pair + idx
            rh = r[:, idx * LANES:(idx + 1) * LANES]
            k_ref[:, h * HEAD_PAD:h * HEAD_PAD + QK_NOPE] = (kh[idx] * rh * kn_nope).astype(BF16)
            k_ref[:, h * HEAD_PAD + QK_NOPE:(h + 1) * HEAD_PAD] = (pe_rot * rh).astype(BF16)


def _mla_proj_kernel(x_ref, mod_ref, cos_ref, sin_ref, wdq_ref, wdkv_ref, qan_ref, kvan_ref,
                     wuq_ref, wuk_ref, wuv_ref, qn_ref, kn_ref, ones_ref, q_ref, k_ref, v_ref, ckv_ref, kpe_ref):
    h = _modulated(x_ref, mod_ref, 0).astype(BF16)
    cos, sin = cos_ref[...], sin_ref[...]
    qa = (_rms(_dot(h, wdq_ref[...])) * qan_ref[...]).astype(BF16)
    q = _dot(qa, wuq_ref[...])
    qn_nope = qn_ref[:, 0:LANES]
    qn_rope = qn_ref[:, LANES:2 * LANES] * cos
    qn_part = qn_ref[:, 2 * LANES:3 * LANES] * sin
    scale = QK_DIM ** -0.5 * math.log2(math.e)
    for pair in range(MLA_HEADS // 2):
        blocks = []
        for hd in (2 * pair, 2 * pair + 1):
            base = hd * HEAD_PROJ
            blocks.append((q[:, base:base + LANES], q[:, base + LANES:base + 2 * LANES],
                           q[:, base + 2 * LANES:base + 3 * LANES]))
        ss = _pair_sums(*[qh * qh + qr * qr for qh, qr, _ in blocks], ones_ref)
        r = lax.rsqrt(ss * (1.0 / QK_DIM) + EPS) * scale
        for idx, (qh, qr, qp) in enumerate(blocks):
            hd = 2 * pair + idx
            rh = r[:, idx * LANES:(idx + 1) * LANES]
            q_ref[:, hd * HEAD_PAD:hd * HEAD_PAD + QK_NOPE] = (qh * rh * qn_nope).astype(BF16)
            q_ref[:, hd * HEAD_PAD + QK_NOPE:(hd + 1) * HEAD_PAD] = ((qr * qn_rope + qp * qn_part) * rh).astype(BF16)
    kva = _dot(h, wdkv_ref[...])
    ckv = _rms(kva[:, 0:KV_RANK]) * kvan_ref[...]
    kpe = kva[:, KV_RANK:KV_RANK + LANES]
    kpe_part = kva[:, KV_RANK + LANES:KV_RANK + 2 * LANES]
    ckv_ref[...] = ckv
    kpe_ref[...] = kpe
    pe_rot = kpe * (kn_ref[:, LANES:2 * LANES] * cos) + kpe_part * (kn_ref[:, 2 * LANES:3 * LANES] * sin)
    _expand_keys(ckv.astype(BF16), kpe * kpe, pe_rot, wuk_ref, wuv_ref, kn_ref, ones_ref, k_ref, v_ref)


def _mla_projection(x, mod, rope_tabs, wdq, wdkv, qan, kvan, wuq, wuk, wuv, qn, kn, ones):
    tm = TM_FFN
    hw = MLA_HEADS * HEAD_PAD
    out_w = (hw, hw, MLA_HEADS * V_HEAD, KV_RANK, LANES)
    out_dt = (BF16, BF16, BF16, F32, F32)
    consts = (wdq, wdkv, qan, kvan, wuq, wuk, wuv, qn, kn, ones)
    return pl.pallas_call(
        _mla_proj_kernel,
        grid=(N_TOK // tm,),
        in_specs=[_row_spec(tm, D_MODEL), _mod_spec(tm)] + [_row_spec(tm, LANES)] * 2
                 + [_const_spec(cst.shape) for cst in consts],
        out_specs=[_row_spec(tm, w) for w in out_w],
        out_shape=[jax.ShapeDtypeStruct((N_TOK, w), dt) for w, dt in zip(out_w, out_dt)],
        compiler_params=_cparams("parallel"),
        name="mla_projection",
    )(x, mod, *rope_tabs, *consts)


def _cache_kv_kernel(ckv_ref, kpe_ref, wuk_ref, wuv_ref, kn_ref, ones_ref, k_ref, v_ref):
    kpe = kpe_ref[...]
    _expand_keys(ckv_ref[...].astype(BF16), kpe * kpe, kpe * kn_ref[:, LANES:2 * LANES],
                 wuk_ref, wuv_ref, kn_ref, ones_ref, k_ref, v_ref)


def _cache_keys(ckv, kpe, wuk, wuv, kn, ones):
    rows = ckv.shape[0]
    tm = TM_ROWS
    consts = (wuk, wuv, kn, ones)
    return pl.pallas_call(
        _cache_kv_kernel,
        grid=(rows // tm,),
        in_specs=[_row_spec(tm, KV_RANK), _row_spec(tm, LANES)] + [_const_spec(cst.shape) for cst in consts],
        out_specs=[_row_spec(tm, MLA_HEADS * HEAD_PAD), _row_spec(tm, MLA_HEADS * V_HEAD)],
        out_shape=[jax.ShapeDtypeStruct((rows, MLA_HEADS * HEAD_PAD), BF16),
                   jax.ShapeDtypeStruct((rows, MLA_HEADS * V_HEAD), BF16)],
        compiler_params=_cparams("parallel"),
        name="cache_keys",
    )(ckv, kpe, *consts)


def _attn_kernel(has_cache, tq, n_par, heads, *refs):
    if has_cache:
        q_ref, k_ref, v_ref, kc_ref, vc_ref, o_ref = refs
    else:
        q_ref, k_ref, v_ref, o_ref = refs

    def tile(r0, hd):
        qk = slice(hd * HEAD_PAD, (hd + 1) * HEAD_PAD)
        vo = slice(hd * V_HEAD, (hd + 1) * V_HEAD)
        q = q_ref[pl.ds(r0, tq), qk]
        s = _dot_nt(q, k_ref[:, qk])
        m = jnp.max(s, axis=-1, keepdims=True)
        if has_cache:
            sc = _dot_nt(q, kc_ref[:, qk])
            m = jnp.maximum(m, jnp.max(sc, axis=-1, keepdims=True))
            pc = jnp.exp2(sc - m)
        p = jnp.exp2(s - m)
        den = jnp.sum(p, axis=-1, keepdims=True)
        num = _dot(p.astype(BF16), v_ref[:, vo])
        if has_cache:
            den = den + jnp.sum(pc, axis=-1, keepdims=True)
            num = num + _dot(pc.astype(BF16), vc_ref[:, vo])
        o_ref[pl.ds(r0, tq), vo] = (num / den).astype(BF16)

    group = n_par * tq
    n_groups = q_ref.shape[0] // group
    if n_groups == 1:
        for hd in range(heads):
            for t in range(n_par):
                tile(t * tq, hd)
    else:
        def body(i, carry):
            r0 = pl.multiple_of(i * group, group)
            for hd in range(heads):
                for t in range(n_par):
                    tile(r0 + t * tq, hd)
            return carry
        lax.fori_loop(0, n_groups, body, 0)


def _attention(q, k, v, kc, vc, n_seq, seq, row_block0):
    has_cache = kc is not None
    if seq >= 2 * TQ:
        tq, n_par, heads = TQ, 8, 1
    else:
        tq, n_par, heads = seq, 1, MLA_HEADS
    seq_spec = lambda w: pl.BlockSpec((seq, heads * w), lambda b, h: (row_block0 + b, h))
    in_specs = [seq_spec(HEAD_PAD), seq_spec(HEAD_PAD), seq_spec(V_HEAD)]
    args = [q, k, v]
    if has_cache:
        past = kc.shape[0] // n_seq
        in_specs += [pl.BlockSpec((past, heads * HEAD_PAD), lambda b, h: (b, h)),
                     pl.BlockSpec((past, heads * V_HEAD), lambda b, h: (b, h))]
        args += [kc, vc]
    return pl.pallas_call(
        functools.partial(_attn_kernel, has_cache, tq, n_par, heads),
        grid=(n_seq, MLA_HEADS // heads),
        in_specs=in_specs,
        out_specs=pl.BlockSpec((seq, heads * V_HEAD), lambda b, h: (b, h)),
        out_shape=jax.ShapeDtypeStruct((n_seq * seq, MLA_HEADS * V_HEAD), BF16),
        compiler_params=_cparams("parallel", "parallel"),
        name="attention_%d" % seq,
    )(*args)


def _split_row_specs(tm, width):
    n_ctx = N_PROMPT // tm
    return [pl.BlockSpec((tm, width), lambda i, *_: (jnp.minimum(i, n_ctx - 1), 0)),
            pl.BlockSpec((tm, width), lambda i, *_: (jnp.maximum(i - n_ctx, 0), 0))]


def _select_rows(ctx_ref, lat_ref):
    tm = ctx_ref.shape[0]
    return jnp.where(pl.program_id(0) < N_PROMPT // tm, ctx_ref[...], lat_ref[...])


def _attn_out_router_kernel(x_ref, mod_ref, ac_ref, al_ref, wo_ref, wr_ref, tri_ref,
                            xo_ref, h_ref, meta_ref, cnt_ref, run_ref):
    @pl.when(pl.program_id(0) == 0)
    def _():
        run_ref[...] = jnp.zeros_like(run_ref)

    x_new = x_ref[...] + mod_ref[0, 2:3, :] * _dot(_select_rows(ac_ref, al_ref), wo_ref[...])
    xo_ref[...] = x_new
    h = _rms(x_new) * (1.0 + mod_ref[0, 4:5, :]) + mod_ref[0, 3:4, :]
    h_ref[...] = h.astype(BF16)
    lane = lax.broadcasted_iota(jnp.int32, (x_ref.shape[0], LANES), 1)
    lane_f = lane.astype(F32)
    logits = jnp.where(lane < N_EXPERTS, _dot3(h, wr_ref[...]), -jnp.inf)
    m1 = jnp.max(logits, axis=-1, keepdims=True)
    e1 = jnp.min(jnp.where(logits == m1, lane_f, float(LANES)), axis=-1, keepdims=True)
    rest = jnp.where(lane_f == e1, -jnp.inf, logits)
    m2 = jnp.max(rest, axis=-1, keepdims=True)
    e2 = jnp.min(jnp.where(rest == m2, lane_f, float(LANES)), axis=-1, keepdims=True)
    t = jnp.exp(m2 - m1)
    g1 = 1.0 / (1.0 + t)
    pick1 = lane_f == e1
    pick2 = lane_f == e2
    onehot = jnp.where(pick1, 1.0, jnp.where(pick2, 1.0, 0.0))
    before = run_ref[...] + _dot(tri_ref[...], onehot.astype(BF16)) - onehot
    rank1 = jnp.sum(jnp.where(pick1, before, 0.0), axis=-1, keepdims=True)
    rank2 = jnp.sum(jnp.where(pick2, before, 0.0), axis=-1, keepdims=True)
    run_ref[...] += jnp.sum(onehot, axis=0, keepdims=True)
    cnt_ref[...] = jnp.broadcast_to(run_ref[...], cnt_ref.shape)
    cols = (e1, e2, g1, 1.0 - g1, rank1, rank2)
    meta = jnp.zeros(meta_ref.shape, F32)
    for idx, val in enumerate(cols):
        meta = jnp.where(lane == idx, val, meta)
    meta_ref[...] = meta


META_EXPERT, META_GATE, META_RANK = 0, 2, 4


def _attn_out_router(x, mod, a_ctx, a_lat, w_o, wr):
    tm = TM_ROWS
    k = a_ctx.shape[1]
    r = jnp.arange(tm)
    tri = (r[:, None] >= r[None, :]).astype(BF16)
    return pl.pallas_call(
        _attn_out_router_kernel,
        grid=(N_TOK // tm,),
        in_specs=[_row_spec(tm, D_MODEL), _mod_spec(tm)] + _split_row_specs(tm, k)
                 + [_const_spec((k, D_MODEL)), _const_spec((D_MODEL, LANES)), _const_spec((tm, tm))],
        out_specs=[_row_spec(tm, D_MODEL), _row_spec(tm, D_MODEL), _row_spec(tm, LANES),
                   _const_spec((SEG_PAD, LANES))],
        out_shape=[jax.ShapeDtypeStruct((N_TOK, D_MODEL), F32), jax.ShapeDtypeStruct((N_TOK, D_MODEL), BF16),
                   jax.ShapeDtypeStruct((N_TOK, LANES), F32), jax.ShapeDtypeStruct((SEG_PAD, LANES), F32)],
        scratch_shapes=[pltpu.VMEM((1, LANES), F32)],
        compiler_params=_cparams("arbitrary"),
        name="attn_out_router",
    )(x, mod, a_ctx, a_lat, w_o, wr, tri)


N_FF_CHUNKS = D_FF // FF_CHUNK
MOE_STEPS = 2
CHUNKS_PER_STEP = -(-N_FF_CHUNKS // MOE_STEPS)
TF_MOE = CHUNKS_PER_STEP * FF_CHUNK


def _experts_kernel(be_ref, bv_ref, nu_ref, rows_ref, wg_ref, wu_ref, wd_ref, o_ref, acc_ref):
    i = pl.program_id(0)
    f = pl.program_id(1)

    def chunk(c, rows):
        cols = slice(c * FF_CHUNK, (c + 1) * FF_CHUNK)
        gate = _dot(rows, wg_ref[0, 0, :, cols].astype(BF16))
        up = _dot(rows, wu_ref[0, 0, :, cols].astype(BF16))
        return _dot((_silu(gate) * up).astype(BF16), wd_ref[0, 0, cols, :].astype(BF16))

    @pl.when(bv_ref[i] > 0)
    def _():
        rows = rows_ref[...]
        acc = chunk(0, rows)
        for c in range(1, CHUNKS_PER_STEP - 1):
            acc = acc + chunk(c, rows)

        @pl.when(f == 0)
        def _():
            acc_ref[...] = acc

        @pl.when(f > 0)
        def _():
            acc_ref[...] += acc

        @pl.when((f * CHUNKS_PER_STEP + CHUNKS_PER_STEP) * FF_CHUNK <= D_FF)
        def _():
            acc_ref[...] += chunk(CHUNKS_PER_STEP - 1, rows_ref[...])

    @pl.when(f == pl.num_programs(1) - 1)
    def _():
        o_ref[...] = acc_ref[...].astype(BF16)


def _expert_ffn(rows, block_expert, block_valid, n_used, wg, wu, wd, layer):
    n_rows = rows.shape[0]
    tm = TM_MOE
    tf = TF_MOE
    n_f = MOE_STEPS

    def f_idx(i, f, nu):
        return jnp.where(i < nu[0], f, n_f - 1)

    grid_spec = pltpu.PrefetchScalarGridSpec(
        num_scalar_prefetch=3,
        grid=(n_rows // tm, n_f),
        in_specs=[pl.BlockSpec((tm, D_MODEL), lambda i, f, be, bv, nu: (i, 0)),
                  pl.BlockSpec((1, 1, D_MODEL, tf), lambda i, f, be, bv, nu: (layer, be[i], 0, f_idx(i, f, nu))),
                  pl.BlockSpec((1, 1, D_MODEL, tf), lambda i, f, be, bv, nu: (layer, be[i], 0, f_idx(i, f, nu))),
                  pl.BlockSpec((1, 1, tf, D_MODEL), lambda i, f, be, bv, nu: (layer, be[i], f_idx(i, f, nu), 0))],
        out_specs=pl.BlockSpec((tm, D_MODEL), lambda i, f, be, bv, nu: (i, 0)),
        scratch_shapes=[pltpu.VMEM((tm, D_MODEL), F32)])
    return pl.pallas_call(
        _experts_kernel,
        grid_spec=grid_spec,
        out_shape=jax.ShapeDtypeStruct((n_rows, D_MODEL), BF16),
        compiler_params=_cparams("arbitrary", "arbitrary"),
        name="expert_ffn",
    )(block_expert, block_valid, n_used, rows, wg, wu, wd)


def _combine_kernel(split, x_ref, mod_ref, meta_ref, a_ref, b_ref, *o_refs):
    g1 = meta_ref[:, META_GATE:META_GATE + 1]
    g2 = meta_ref[:, META_GATE + 1:META_GATE + 2]
    y = a_ref[...].astype(F32) * g1 + b_ref[...].astype(F32) * g2
    res = x_ref[...] + mod_ref[0, 5:6, :] * y
    if not split:
        o_refs[0][...] = res
        return
    is_ctx = pl.program_id(0) < N_PROMPT // x_ref.shape[0]

    @pl.when(is_ctx)
    def _():
        o_refs[0][...] = res

    @pl.when(jnp.logical_not(is_ctx))
    def _():
        o_refs[1][...] = res


def _moe_combine(x, mod, meta, a, b, split):
    tm = TM_FFN
    if split:
        out_specs = _split_row_specs(tm, D_MODEL)
        out_shape = [jax.ShapeDtypeStruct((N_PROMPT, D_MODEL), F32), jax.ShapeDtypeStruct((N_SAMPLE, D_MODEL), F32)]
    else:
        out_specs = _row_spec(tm, D_MODEL)
        out_shape = jax.ShapeDtypeStruct((N_TOK, D_MODEL), F32)
    return pl.pallas_call(
        functools.partial(_combine_kernel, split),
        grid=(N_TOK // tm,),
        in_specs=[_row_spec(tm, D_MODEL), _mod_spec(tm), _row_spec(tm, LANES), _row_spec(tm, D_MODEL),
                  _row_spec(tm, D_MODEL)],
        out_specs=out_specs,
        out_shape=out_shape,
        compiler_params=_cparams("arbitrary"),
        name="moe_combine",
    )(x, mod, meta, a, b)


def _moe(x, mod, h, meta, counts, wg, wu, wd, layer, split_out):
    tm = TM_MOE
    n_assign = N_TOK * TOP_K
    experts = meta[:, META_EXPERT:META_EXPERT + TOP_K].astype(jnp.int32)
    rank = meta[:, META_RANK:META_RANK + TOP_K].astype(jnp.int32)
    count = counts[0, :N_EXPERTS].astype(jnp.int32)
    padded = (count + tm - 1) // tm * tm
    pad_end = jnp.cumsum(padded)
    pad_start = pad_end - padded
    onehot = experts[:, :, None] == jnp.arange(N_EXPERTS, dtype=jnp.int32)
    dest = jnp.sum(jnp.where(onehot, pad_start, 0), axis=-1) + rank
    n_blocks = (n_assign + N_EXPERTS * (tm - 1)) // tm
    n_rows = n_blocks * tm
    row_token = jnp.zeros((n_rows,), jnp.int32).at[dest.reshape(-1)].set(
        jnp.arange(n_assign, dtype=jnp.int32) // TOP_K)
    n_used = (pad_end[-1] // tm).astype(jnp.int32).reshape(1)
    block_id = jnp.arange(n_blocks, dtype=jnp.int32)
    block_start = jnp.minimum(block_id, n_used[0] - 1) * tm
    block_expert = jnp.minimum(jnp.sum((pad_end[None, :] <= block_start[:, None]).astype(jnp.int32), axis=1),
                               N_EXPERTS - 1)
    last_row = jnp.sum(jnp.where(block_expert[:, None] == jnp.arange(N_EXPERTS, dtype=jnp.int32),
                                 pad_start + count, 0), axis=1)
    block_valid = jnp.where(block_id < n_used[0], jnp.clip(last_row - block_start, 0, tm), 0)
    out = _expert_ffn(h[row_token], block_expert, block_valid, n_used, wg, wu, wd, layer)
    return _moe_combine(x, mod, meta, out[dest[:, 0]], out[dest[:, 1]], split_out)


def _rope_tables():
    half = ROPE_AXIS // 2
    pos = jnp.arange(DEC_SEQ)
    row = (pos // GRID_W).astype(F32)
    col = (pos % GRID_W).astype(F32)
    inv = ROPE_THETA ** (-jnp.arange(0, ROPE_AXIS, 2, dtype=F32) / ROPE_AXIS)
    ang_r = row[:, None] * inv
    ang_c = col[:, None] * inv
    pad = jnp.zeros((DEC_SEQ, LANES - QK_ROPE), F32)
    cos = jnp.concatenate([jnp.cos(ang_r), jnp.cos(ang_r), jnp.cos(ang_c), jnp.cos(ang_c), pad + 1.0], axis=1)
    sin = jnp.concatenate([-jnp.sin(ang_r), jnp.sin(ang_r), -jnp.sin(ang_c), jnp.sin(ang_c), pad], axis=1)
    prompt = jnp.zeros((N_PROMPT, LANES), F32)
    tile = lambda t: jnp.tile(t, (DEC_BATCH, 1))
    return (jnp.concatenate([prompt + 1.0, tile(cos)], axis=0),
            jnp.concatenate([prompt, tile(sin)], axis=0))


def _rope_blocks(a):
    half = ROPE_AXIS // 2
    partner = jnp.concatenate([a[..., half:2 * half], a[..., 0:half], a[..., 3 * half:4 * half],
                               a[..., 2 * half:3 * half]], axis=-1)
    return jnp.concatenate([_pad_lanes(a, LANES), _pad_lanes(partner, LANES)], axis=-1)


def _pad_lanes(a, width):
    return jnp.pad(a, [(0, 0)] * (a.ndim - 1) + [(0, width - a.shape[-1])])


def _ssd_constants():
    r = jnp.arange(CHUNK)
    tri = (r[:, None] >= r[None, :]).astype(BF16)
    head_of = jnp.arange(D_SSD) // SSD_HEADDIM
    e_f = (r[:, None] == head_of[None, :]).astype(BF16)
    e_b = (r[:, None] == head_of[None, :] + SSD_HEADS).astype(BF16)
    return tri, e_f, e_b


def kernel(x_prompt, x_sample, c, state_ssm, cache_ckv, cache_kpe, c_ctx, w_mod, b_mod, w_in, conv_w, conv_b, dt_bias, a_log, d_skip, ssd_norm, sgu_norm, w_sp, b_sp, w_out, ffn_w_gate, ffn_w_up, ffn_w_down, w_dq, q_a_norm, w_uq, w_dkv, kv_a_norm, w_ukv, q_norm, k_norm, w_o, router, moe_w_gate, moe_w_up, moe_w_down):
    x = jnp.concatenate([x_prompt.reshape(N_PROMPT, D_MODEL), x_sample.reshape(N_SAMPLE, D_MODEL)], axis=0)
    cond = jnp.concatenate([c_ctx[None, :], c, jnp.zeros((SEG_PAD - N_SEG, D_MODEL), F32)], axis=0)
    mods = _modulation_tables(cond, w_mod, b_mod).reshape(DEPTH, SEG_PAD, N_MOD, D_MODEL)
    rope_tabs = _rope_tables()
    tri, e_f, e_b = _ssd_constants()
    blk = jnp.arange(2 * LANES) // LANES
    pair_ones = (blk[:, None] == blk[None, :]).astype(BF16)
    i1 = D_SSD
    i2 = i1 + CONV_CH
    i3 = i2 + 2 * SSD_HEADS
    i4 = i3 + D_SGU
    new_ssm, new_ckv, new_kpe = [], [], []
    for i in range(DEPTH):
        j = i // 2
        mod = mods[i]
        if i % 2 == 0:
            w = w_in[j]
            wx = _col_blocks(w[:, i1:i2].astype(BF16), XBC_BLOCK)
            wgate = _col_blocks(jnp.concatenate([w[:, 0:i1], w[:, i3:]], axis=1).astype(BF16), GATE_BLOCK)
            wdt = _pad_lanes(w[:, i2:i3], LANES).astype(BF16)
            a_log_row = _pad_lanes(a_log[j].reshape(1, -1), LANES)
            xbc, gates, dt, cs = _in_projection(
                x, mod, (wx, wgate, wdt),
                (_col_blocks(conv_w[j], XBC_BLOCK), _col_blocks(conv_b[j][None, :], XBC_BLOCK),
                 _pad_lanes(dt_bias[j].reshape(1, -1), LANES), a_log_row, tri))
            consts = (a_log_row, jnp.repeat(d_skip[j], SSD_HEADDIM)[None, :], e_f, e_b)
            y_ctx, st_p = _ssd_scan(xbc, dt, cs, None, consts, BATCH, SEQ, 0)
            init = state_ssm[:, j].reshape(DEC_BATCH, 2, D_SSD, SSD_STATE)
            y_lat, _ = _ssd_scan(xbc, dt, cs, init, consts, DEC_BATCH, DEC_SEQ, N_PROMPT // DEC_SEQ)
            new_ssm.append(st_p.reshape(BATCH, 2, SSD_HEADS, SSD_HEADDIM, SSD_STATE))
            b_sp_e = jnp.repeat(b_sp[j].T, SGU_GDIM, axis=1)
            x = _mixer_output(x, mod, y_ctx, y_lat, gates, ssd_norm[j][None, :], sgu_norm[j][None, :],
                              w_sp[j].astype(BF16), b_sp_e, w_out[j].astype(BF16))
            x = _dense_ffn(x, mod, ffn_w_gate[j].astype(BF16), ffn_w_up[j].astype(BF16),
                           ffn_w_down[j].astype(BF16))
        else:
            split = lambda a: jnp.concatenate([a[..., :QK_NOPE], _rope_blocks(a[..., QK_NOPE:])], axis=-1)
            wuq = split(w_uq[j].reshape(Q_RANK, MLA_HEADS, QK_DIM)).reshape(Q_RANK, MLA_HEADS * HEAD_PROJ)
            wdkv = jnp.concatenate([w_dkv[j][:, :KV_RANK], _rope_blocks(w_dkv[j][:, KV_RANK:])], axis=-1)
            wukv = w_ukv[j].reshape(KV_RANK, MLA_HEADS, QK_NOPE + V_HEAD)
            wuk = wukv[:, :, :QK_NOPE].reshape(KV_RANK, -1).astype(BF16)
            wuv = wukv[:, :, QK_NOPE:].reshape(KV_RANK, -1).astype(BF16)
            qn = split(q_norm[j][None, :])
            kn = split(k_norm[j][None, :])
            q, k, v, ckv, kpe = _mla_projection(
                x, mod, rope_tabs, w_dq[j].astype(BF16), wdkv.astype(BF16), q_a_norm[j][None, :],
                kv_a_norm[j][None, :], wuq.astype(BF16), wuk, wuv, qn, kn, pair_ones)
            new_ckv.append(ckv[:N_PROMPT].reshape(BATCH, SEQ, KV_RANK))
            new_kpe.append(kpe[:N_PROMPT, :QK_ROPE].reshape(BATCH, SEQ, QK_ROPE))
            kc, vc = _cache_keys(cache_ckv[:, j].reshape(DEC_BATCH * PAST_LEN, KV_RANK),
                                 _pad_lanes(cache_kpe[:, j].reshape(DEC_BATCH * PAST_LEN, QK_ROPE), LANES),
                                 wuk, wuv, kn, pair_ones)
            o_ctx = _attention(q, k, v, None, None, BATCH, SEQ, 0)
            o_lat = _attention(q, k, v, kc, vc, DEC_BATCH, DEC_SEQ, N_PROMPT // DEC_SEQ)
            x, h, meta, counts = _attn_out_router(x, mod, o_ctx, o_lat, w_o[j].astype(BF16),
                                                  _pad_lanes(router[j], LANES))
            x = _moe(x, mod, h, meta, counts, moe_w_gate, moe_w_up, moe_w_down, j, split_out=(i == DEPTH - 1))
    assert DEPTH % 2 == 0
    x_ctx, x_lat = x
    return (x_ctx.reshape(BATCH, SEQ, D_MODEL),
            x_lat.reshape(DEC_BATCH, DEC_SEQ, D_MODEL),
            jnp.stack(new_ssm, axis=1),
            jnp.stack(new_ckv, axis=1),
            jnp.stack(new_kpe, axis=1))
```

```python
import functools
import math

import jax
import jax.numpy as jnp
from jax import lax
from jax.experimental import pallas as pl
from jax.experimental.pallas import tpu as pltpu

F32 = jnp.float32
BF16 = jnp.bfloat16

D_MODEL = 1024
BATCH = 16
SEQ = 256
DEPTH = 4
DEC_BATCH = 4
DEC_SEQ = 2048
PAST_LEN = 512
GRID_W = 64
N_MOD = 6
EPS = 1e-6

SSD_HEADDIM = 64
SSD_HEADS = 16
D_SSD = 1024
SSD_GROUPS = 4
SSD_STATE = 128
CHUNK = 128
CONV_W = 5
CONV_CH = D_SSD + 2 * SSD_GROUPS * SSD_STATE
D_SGU = 1024
SGU_GROUPS = 4
SGU_GDIM = D_SGU // SGU_GROUPS

MLA_HEADS = 8
Q_RANK = 384
KV_RANK = 256
QK_NOPE = 128
QK_ROPE = 64
V_HEAD = 128
QK_DIM = QK_NOPE + QK_ROPE
ROPE_AXIS = QK_ROPE // 2
ROPE_THETA = 10000.0
HEAD_PAD = 256

D_FF = 2816
N_EXPERTS = 8
TOP_K = 2

N_PROMPT = BATCH * SEQ
N_SAMPLE = DEC_BATCH * DEC_SEQ
N_TOK = N_PROMPT + N_SAMPLE
N_SEG = 1 + DEC_BATCH
SEG_PAD = 8

LANES = 128
VMEM_LIMIT = 56 * 1024 * 1024

TM_ROWS = 256
TM_FFN = 512
FF_CHUNK = 256
TM_MOE = 1024
TQ = 256


def _cparams(*sem):
    return pltpu.CompilerParams(dimension_semantics=sem, vmem_limit_bytes=VMEM_LIMIT)


def _dot(a, b):
    return jnp.dot(a, b, preferred_element_type=F32)


def _dot_nt(a, b):
    return lax.dot_general(a, b, (((1,), (1,)), ((), ())), preferred_element_type=F32)


def _split(x):
    hi = x.astype(BF16)
    lo = (x - hi.astype(F32)).astype(BF16)
    return hi, lo


def _dot3(a, b):
    ah, al = _split(a)
    bh, bl = _split(b)
    return _dot(ah, bh) + _dot(ah, bl) + _dot(al, bh)


def _silu(x):
    return x / (1.0 + jnp.exp(-x))


def _rms(x):
    return x * lax.rsqrt(jnp.mean(x * x, axis=-1, keepdims=True) + EPS)


def _modulated(x_ref, mod_ref, first):
    shift = mod_ref[0, first:first + 1, :]
    scale = mod_ref[0, first + 1:first + 2, :]
    return _rms(x_ref[...]) * (1.0 + scale) + shift


def _seg_map(tm):
    def index_map(i, *_):
        r = i * tm
        return (jnp.where(r < N_PROMPT, 0, 1 + (r - N_PROMPT) // DEC_SEQ), 0, 0)
    return index_map


def _row_spec(tm, width):
    return pl.BlockSpec((tm, width), lambda i, *_: (i, 0))


def _const_spec(shape):
    zeros = (0,) * len(shape)
    return pl.BlockSpec(shape, lambda i, *_: zeros)


def _mod_spec(tm):
    return pl.BlockSpec((1, N_MOD, D_MODEL), _seg_map(tm))


def _mod_kernel(c_ref, w_ref, b_ref, o_ref):
    o_ref[0] = _dot3(_silu(c_ref[...]), w_ref[0]) + b_ref[0]


def _modulation_tables(cond, w_mod, b_mod):
    tn = 1536
    return pl.pallas_call(
        _mod_kernel,
        grid=(DEPTH, N_MOD * D_MODEL // tn),
        in_specs=[pl.BlockSpec((SEG_PAD, D_MODEL), lambda l, j: (0, 0)),
                  pl.BlockSpec((1, D_MODEL, tn), lambda l, j: (l, 0, j)),
                  pl.BlockSpec((1, 1, tn), lambda l, j: (l, 0, j))],
        out_specs=pl.BlockSpec((1, SEG_PAD, tn), lambda l, j: (l, 0, j)),
        out_shape=jax.ShapeDtypeStruct((DEPTH, SEG_PAD, N_MOD * D_MODEL), F32),
        compiler_params=_cparams("parallel", "parallel"),
        name="modulation",
    )(cond, w_mod, b_mod.reshape(DEPTH, 1, N_MOD * D_MODEL))


HALO = 8


XBC_BLOCK = 512
N_XBC = CONV_CH // XBC_BLOCK
GATE_BLOCK = 256
N_GATE = (D_SSD + 2 * D_SGU) // GATE_BLOCK
GATE_PER_STAGE = N_GATE // N_XBC


def _inproj_kernel(x_ref, xp_ref, xn_ref, mod_ref, wx_ref, wg_ref, wdt, convw_ref, convb_ref, dtb_ref, alog_ref,
                   tri_ref, xbc_ref, gate_ref, dt_ref, cs_ref, h_ref, xa_ref, xb_ref):
    tm = x_ref.shape[0]
    r0 = pl.program_id(0) * tm
    shift = mod_ref[0, 0:1, :]
    scale = mod_ref[0, 1:2, :]
    x_ext = jnp.concatenate([x_ref[...], xp_ref[...], xn_ref[...]], axis=0)
    h_ref[...] = (_rms(x_ext) * (1.0 + scale) + shift).astype(BF16)

    rel = r0 - N_PROMPT
    in_latent = r0 >= N_PROMPT
    keep_prev = jnp.where(jnp.logical_and(in_latent, lax.rem(rel, DEC_SEQ) != 0), 1.0, 0.0)
    keep_next = jnp.where(jnp.logical_and(in_latent, lax.rem(rel + tm, DEC_SEQ) != 0), 1.0, 0.0)
    reach = CONV_W // 2

    bufs = (xa_ref, xb_ref)
    bufs[0][...] = _dot(h_ref[...], wx_ref[0])

    def stage(s, cur_ref, nxt_ref):
        if s + 1 < N_XBC:
            nxt_ref[...] = _dot(h_ref[...], wx_ref[s + 1])
        for t in range(GATE_PER_STAGE):
            g = s * GATE_PER_STAGE + t
            gate_ref[g] = _dot(h_ref[0:tm, :], wg_ref[g]).astype(BF16)
        for half in range(XBC_BLOCK // 256):
            cols = slice(half * 256, (half + 1) * 256)
            win = jnp.concatenate([cur_ref[tm:tm + HALO, cols] * keep_prev, cur_ref[0:tm, cols],
                                   cur_ref[tm + HALO:tm + 2 * HALO, cols] * keep_next], axis=0)
            acc = jnp.zeros((tm, 256), F32) + convb_ref[s, :, cols]
            for k in range(CONV_W):
                start = HALO - reach + k
                acc = acc + win[start:start + tm, :] * convw_ref[s, k:k + 1, cols]
            xbc_ref[s, :, cols] = _silu(acc).astype(BF16)

    for s in range(N_XBC):
        stage(s, bufs[s % 2], bufs[1 - s % 2])

    h = h_ref[0:tm, :]
    lane = lax.broadcasted_iota(jnp.int32, (1, LANES), 1)
    raw = _dot(h, wdt[...]) + dtb_ref[...]
    dt = jnp.where(lane < 2 * SSD_HEADS, jnp.maximum(raw, 0.0) + jnp.log(1.0 + jnp.exp(-jnp.abs(raw))), 0.0)
    dt_ref[...] = dt
    ac = dt * -jnp.exp(alog_ref[...])
    hi = ac.astype(BF16)
    rest = ac - hi.astype(F32)
    mid = rest.astype(BF16)
    lo = (rest - mid.astype(F32)).astype(BF16)
    tri = tri_ref[...]
    for k in range(tm // CHUNK):
        rows = slice(k * CHUNK, (k + 1) * CHUNK)
        cs_ref[rows, :] = _dot(tri, hi[rows, :]) + _dot(tri, mid[rows, :]) + _dot(tri, lo[rows, :])


def _col_blocks(w, block):
    return w.reshape(w.shape[0], -1, block).transpose(1, 0, 2)


def _in_projection(x, mod, weights, consts):
    tm = TM_ROWS
    per_tile = tm // HALO
    last = N_TOK // HALO - 1
    halo_prev = pl.BlockSpec((HALO, D_MODEL), lambda i: (jnp.maximum(i * per_tile - 1, 0), 0))
    halo_next = pl.BlockSpec((HALO, D_MODEL), lambda i: (jnp.minimum((i + 1) * per_tile, last), 0))
    blocked = lambda n, w: pl.BlockSpec((n, tm, w), lambda i: (0, i, 0))
    return pl.pallas_call(
        _inproj_kernel,
        grid=(N_TOK // tm,),
        in_specs=[_row_spec(tm, D_MODEL), halo_prev, halo_next, _mod_spec(tm)]
                 + [_const_spec(a.shape) for a in weights + consts],
        out_specs=[blocked(N_XBC, XBC_BLOCK), blocked(N_GATE, GATE_BLOCK), _row_spec(tm, LANES),
                   _row_spec(tm, LANES)],
        out_shape=[jax.ShapeDtypeStruct((N_XBC, N_TOK, XBC_BLOCK), BF16),
                   jax.ShapeDtypeStruct((N_GATE, N_TOK, GATE_BLOCK), BF16),
                   jax.ShapeDtypeStruct((N_TOK, LANES), F32), jax.ShapeDtypeStruct((N_TOK, LANES), F32)],
        scratch_shapes=[pltpu.VMEM((tm + 2 * HALO, D_MODEL), BF16),
                        pltpu.VMEM((tm + 2 * HALO, XBC_BLOCK), F32),
                        pltpu.VMEM((tm + 2 * HALO, XBC_BLOCK), F32)],
        compiler_params=_cparams("parallel"),
        name="in_projection",
    )(x, x, x, mod, *weights, *consts)


def _ssd_kernel(has_init, n_chunks, *refs):
    refs = list(refs)
    cv_ref, dtv_ref, cs_ref = refs[0:3]
    del refs[0:3]
    init_ref = refs.pop(0) if has_init else None
    alog_ref, dskip_ref, ef_ref, eb_ref, y_ref, fin_ref, st_ref = refs
    a_neg = -jnp.exp(alog_ref[...])
    rows_i = lax.broadcasted_iota(jnp.int32, (CHUNK, CHUNK), 0)
    cols_i = lax.broadcasted_iota(jnp.int32, (CHUNK, CHUNK), 1)
    lower = rows_i >= cols_i
    upper = cols_i >= rows_i
    even_head = jnp.bitwise_and(lax.broadcasted_iota(jnp.int32, (CHUNK, D_SSD), 1), LANES - 1) < SSD_HEADDIM

    assert SSD_GROUPS * SSD_STATE == XBC_BLOCK
    b_block = D_SSD // XBC_BLOCK

    def row0(c):
        return pl.multiple_of(c * CHUNK, CHUNK)

    def chunk_pass(c, forward):
        r0 = row0(c)
        rows = pl.ds(r0, CHUNK)
        e_ref = ef_ref if forward else eb_ref
        off = 0 if forward else SSD_HEADS
        dt = dtv_ref[rows, :]
        cs = cs_ref[rows, :]
        total = cs_ref[pl.ds(r0 + CHUNK - 1, 1), :]
        if forward:
            pos = cs
            to_edge = jnp.exp(total - cs)
            from_edge = jnp.exp(cs)
            mask = lower
        else:
            pos = cs - dt * a_neg
            to_edge = jnp.exp(pos)
            from_edge = jnp.exp(total - pos)
            mask = upper
        pos_t = pos.T
        x = jnp.concatenate([cv_ref[b, rows, :] for b in range(D_SSD // XBC_BLOCK)], axis=1).astype(F32)
        stacked = jnp.concatenate([dt, dt * to_edge, from_edge], axis=0).astype(BF16)
        spread = _dot(stacked, e_ref[...])
        xc = x * spread[0:CHUNK]
        xc_sub = (jnp.where(even_head, xc, 0.0).astype(BF16), jnp.where(even_head, 0.0, xc).astype(BF16))
        xd_b = (x * spread[CHUNK:2 * CHUNK]).astype(BF16)
        from_e = spread[2 * CHUNK:3 * CHUNK]
        tot_hi, tot_lo = _split(jnp.exp(jnp.broadcast_to(total, (8, LANES))))
        tot_e = (_dot(tot_hi, e_ref[...]) + _dot(tot_lo, e_ref[...]))[0:1, :]
        st = st_ref[0 if forward else 1]
        st_b = st.astype(BF16)
        y_parts = []
        st_parts = []
        for g in range(SSD_GROUPS):
            bm = cv_ref[b_block, rows, g * SSD_STATE:(g + 1) * SSD_STATE]
            cm = cv_ref[b_block + 1, rows, g * SSD_STATE:(g + 1) * SSD_STATE]
            bm_t = bm.astype(F32).T.astype(BF16)
            cb = _dot(cm, bm_t)
            gcols = slice(g * 256, (g + 1) * 256)
            y_off = _dot(cm, st_b[:, gcols])
            st_parts.append(_dot(bm_t, xd_b[:, gcols]))
            diag = []
            for pair in range(2):
                pcols = slice(g * 256 + pair * LANES, g * 256 + (pair + 1) * LANES)
                decayed = []
                for sub in range(2):
                    h = g * 4 + pair * 2 + sub
                    col = pos[:, off + h:off + h + 1]
                    row = pos_t[off + h:off + h + 1, :]
                    diff = (col - row) if forward else (row - col)
                    decayed.append((jnp.where(mask, jnp.exp(diff), 0.0) * cb).astype(BF16))
                diag.append(_dot(jnp.concatenate(decayed, axis=1),
                                 jnp.concatenate([xc_sub[0][:, pcols], xc_sub[1][:, pcols]], axis=0)))
            y_parts.append(jnp.concatenate(diag, axis=1) + y_off * from_e[:, gcols])
        y_new = jnp.concatenate(y_parts, axis=1)
        st_ref[0 if forward else 1] = st * tot_e + jnp.concatenate(st_parts, axis=1)
        if forward:
            y_new = y_new + x * dskip_ref[...]
        return rows, y_new

    if has_init:
        st_ref[0] = init_ref[0, 0].T
        st_ref[1] = init_ref[0, 1].T
    else:
        st_ref[...] = jnp.zeros_like(st_ref)

    def first_half(i, carry):
        for rows, y_new in (chunk_pass(i, True), chunk_pass(n_chunks - 1 - i, False)):
            y_ref[rows, :] = y_new
        return carry

    def second_half(i, carry):
        for rows, y_new in (chunk_pass(i, True), chunk_pass(n_chunks - 1 - i, False)):
            y_ref[rows, :] = y_ref[rows, :] + y_new
        return carry

    lax.fori_loop(0, n_chunks // 2, first_half, 0)
    lax.fori_loop(n_chunks // 2, n_chunks, second_half, 0)
    fin_ref[0, 0] = st_ref[0].T
    fin_ref[0, 1] = st_ref[1].T


def _ssd_scan(xbc, dt, cs, init, consts, n_seq, seq, row_block0):
    n_chunks = seq // CHUNK
    assert n_chunks % 2 == 0
    has_init = init is not None
    once = pl.Buffered(1)
    seq_spec = lambda w: pl.BlockSpec((seq, w), lambda b: (row_block0 + b, 0), pipeline_mode=once)
    xbc_spec = pl.BlockSpec((N_XBC, seq, XBC_BLOCK), lambda b: (0, row_block0 + b, 0), pipeline_mode=once)
    in_specs = [xbc_spec, seq_spec(LANES), seq_spec(LANES)]
    args = [xbc, dt, cs]
    if has_init:
        in_specs.append(pl.BlockSpec((1, 2, D_SSD, SSD_STATE), lambda b: (b, 0, 0, 0), pipeline_mode=once))
        args.append(init)
    for cst in consts:
        in_specs.append(pl.BlockSpec(cst.shape, lambda b, nd=cst.ndim: (0,) * nd))
        args.append(cst)
    return pl.pallas_call(
        functools.partial(_ssd_kernel, has_init, n_chunks),
        grid=(n_seq,),
        in_specs=in_specs,
        out_specs=[pl.BlockSpec((seq, D_SSD), lambda b: (b, 0)),
                   pl.BlockSpec((1, 2, D_SSD, SSD_STATE), lambda b: (b, 0, 0, 0))],
        out_shape=[jax.ShapeDtypeStruct((n_seq * seq, D_SSD), F32),
                   jax.ShapeDtypeStruct((n_seq, 2, D_SSD, SSD_STATE), F32)],
        scratch_shapes=[pltpu.VMEM((2, SSD_STATE, D_SSD), F32)],
        compiler_params=_cparams("parallel"),
        name="ssd_scan_%d" % seq,
    )(*args)


def _mixer_out_kernel(x_ref, mod_ref, yc_ref, yl_ref, z_ref, u_ref, v_ref, ssdn_ref, sgun_ref, wsp_ref, bsp_ref,
                      wo_ref, o_ref):
    tm = x_ref.shape[0]
    wide = lambda ref: jnp.concatenate([ref[b] for b in range(ref.shape[0])], axis=1).astype(F32)
    gated = _select_rows(yc_ref, yl_ref) * _silu(wide(z_ref))
    a = (_rms(gated) * ssdn_ref[...]).astype(BF16)
    vb = (_rms(wide(v_ref)) * sgun_ref[...]).astype(BF16)
    chunks = []
    for k in range(tm // CHUNK):
        rows = slice(k * CHUNK, (k + 1) * CHUNK)
        groups = [_dot(wsp_ref[g], vb[rows, g * SGU_GDIM:(g + 1) * SGU_GDIM]) for g in range(SGU_GROUPS)]
        chunks.append(jnp.concatenate(groups, axis=1) + bsp_ref[...])
    s = (wide(u_ref) * jnp.concatenate(chunks, axis=0)).astype(BF16)
    out = _dot(a, wo_ref[0:D_SSD, :]) + _dot(s, wo_ref[D_SSD:D_SSD + D_SGU, :])
    o_ref[...] = x_ref[...] + mod_ref[0, 2:3, :] * out


def _mixer_output(x, mod, y_ctx, y_lat, gates, ssd_norm, sgu_norm, w_sp, b_sp_e, w_out):
    tm = TM_FFN
    per = D_SSD // GATE_BLOCK
    part = lambda k: pl.BlockSpec((per, tm, GATE_BLOCK), lambda i: (k, i, 0))
    return pl.pallas_call(
        _mixer_out_kernel,
        grid=(N_TOK // tm,),
        in_specs=[_row_spec(tm, D_MODEL), _mod_spec(tm)] + _split_row_specs(tm, D_SSD) + [part(0), part(1), part(2),
                  _const_spec((1, D_SSD)), _const_spec((1, D_SGU)),
                  _const_spec((SGU_GROUPS, CHUNK, CHUNK)), _const_spec((CHUNK, D_SGU)),
                  _const_spec((D_SSD + D_SGU, D_MODEL))],
        out_specs=_row_spec(tm, D_MODEL),
        out_shape=jax.ShapeDtypeStruct((N_TOK, D_MODEL), F32),
        compiler_params=_cparams("parallel"),
        name="mixer_output",
    )(x, mod, y_ctx, y_lat, gates, gates, gates, ssd_norm, sgu_norm, w_sp, b_sp_e, w_out)


def _ffn_kernel(x_ref, mod_ref, wg_ref, wu_ref, wd_ref, o_ref):
    h = _modulated(x_ref, mod_ref, 3).astype(BF16)
    acc = jnp.zeros(o_ref.shape, F32)
    for f in range(D_FF // FF_CHUNK):
        cols = slice(f * FF_CHUNK, (f + 1) * FF_CHUNK)
        act = (_silu(_dot(h, wg_ref[:, cols])) * _dot(h, wu_ref[:, cols])).astype(BF16)
        acc = acc + _dot(act, wd_ref[cols, :])
    o_ref[...] = x_ref[...] + mod_ref[0, 5:6, :] * acc


def _dense_ffn(x, mod, wg, wu, wd):
    tm = TM_FFN
    once = pl.Buffered(1)
    return pl.pallas_call(
        _ffn_kernel,
        grid=(N_TOK // tm,),
        in_specs=[_row_spec(tm, D_MODEL), _mod_spec(tm),
                  pl.BlockSpec((D_MODEL, D_FF), lambda i: (0, 0), pipeline_mode=once),
                  pl.BlockSpec((D_MODEL, D_FF), lambda i: (0, 0), pipeline_mode=once),
                  pl.BlockSpec((D_FF, D_MODEL), lambda i: (0, 0), pipeline_mode=once)],
        out_specs=_row_spec(tm, D_MODEL),
        out_shape=jax.ShapeDtypeStruct((N_TOK, D_MODEL), F32),
        compiler_params=_cparams("parallel"),
        name="dense_ffn",
    )(x, mod, wg, wu, wd)


HEAD_PROJ = 3 * LANES


def _pair_sums(sq_a, sq_b, ones_ref):
    return _dot(jnp.concatenate([sq_a, sq_b], axis=1).astype(BF16), ones_ref[...])


def _expand_keys(ckv_b, pe_sq, pe_rot, wuk_ref, wuv_ref, kn_ref, ones_ref, k_ref, v_ref):
    kn_nope = kn_ref[:, 0:QK_NOPE]
    v_ref[...] = _dot(ckv_b, wuv_ref[...]).astype(BF16)
    k_nope = _dot(ckv_b, wuk_ref[...])
    for pair in range(MLA_HEADS // 2):
        kh = [k_nope[:, h * QK_NOPE:(h + 1) * QK_NOPE] for h in (2 * pair, 2 * pair + 1)]
        ss = _pair_sums(kh[0] * kh[0] + pe_sq, kh[1] * kh[1] + pe_sq, ones_ref)
        r = lax.rsqrt(ss * (1.0 / QK_DIM) + EPS)
        for idx in range(2):
            h = 2 * pair + idx
            rh = r[:, idx * LANES:(idx + 1) * LANES]
            k_ref[:, h * HEAD_PAD:h * HEAD_PAD + QK_NOPE] = (kh[idx] * rh * kn_nope).astype(BF16)
            k_ref[:, h * HEAD_PAD + QK_NOPE:(h + 1) * HEAD_PAD] = (pe_rot * rh).astype(BF16)


def _mla_proj_kernel(x_ref, mod_ref, cos_ref, sin_ref, wdq_ref, wdkv_ref, qan_ref, kvan_ref,
                     wuq_ref, wuk_ref, wuv_ref, qn_ref, kn_ref, ones_ref, q_ref, k_ref, v_ref, ckv_ref, kpe_ref):
    h = _modulated(x_ref, mod_ref, 0).astype(BF16)
    cos, sin = cos_ref[...], sin_ref[...]
    qa = (_rms(_dot(h, wdq_ref[...])) * qan_ref[...]).astype(BF16)
    q = _dot(qa, wuq_ref[...])
    qn_nope = qn_ref[:, 0:LANES]
    qn_rope = qn_ref[:, LANES:2 * LANES] * cos
    qn_part = qn_ref[:, 2 * LANES:3 * LANES] * sin
    scale = QK_DIM ** -0.5 * math.log2(math.e)
    for pair in range(MLA_HEADS // 2):
        blocks = []
        for hd in (2 * pair, 2 * pair + 1):
            base = hd * HEAD_PROJ
            blocks.append((q[:, base:base + LANES], q[:, base + LANES:base + 2 * LANES],
                           q[:, base + 2 * LANES:base + 3 * LANES]))
        ss = _pair_sums(*[qh * qh + qr * qr for qh, qr, _ in blocks], ones_ref)
        r = lax.rsqrt(ss * (1.0 / QK_DIM) + EPS) * scale
        for idx, (qh, qr, qp) in enumerate(blocks):
            hd = 2 * pair + idx
            rh = r[:, idx * LANES:(idx + 1) * LANES]
            q_ref[:, hd * HEAD_PAD:hd * HEAD_PAD + QK_NOPE] = (qh * rh * qn_nope).astype(BF16)
            q_ref[:, hd * HEAD_PAD + QK_NOPE:(hd + 1) * HEAD_PAD] = ((qr * qn_rope + qp * qn_part) * rh).astype(BF16)
    kva = _dot(h, wdkv_ref[...])
    ckv = _rms(kva[:, 0:KV_RANK]) * kvan_ref[...]
    kpe = kva[:, KV_RANK:KV_RANK + LANES]
    kpe_part = kva[:, KV_RANK + LANES:KV_RANK + 2 * LANES]
    ckv_ref[...] = ckv
    kpe_ref[...] = kpe
    pe_rot = kpe * (kn_ref[:, LANES:2 * LANES] * cos) + kpe_part * (kn_ref[:, 2 * LANES:3 * LANES] * sin)
    _expand_keys(ckv.astype(BF16), kpe * kpe, pe_rot, wuk_ref, wuv_ref, kn_ref, ones_ref, k_ref, v_ref)


def _mla_projection(x, mod, rope_tabs, wdq, wdkv, qan, kvan, wuq, wuk, wuv, qn, kn, ones):
    tm = TM_FFN
    hw = MLA_HEADS * HEAD_PAD
    out_w = (hw, hw, MLA_HEADS * V_HEAD, KV_RANK, LANES)
    out_dt = (BF16, BF16, BF16, F32, F32)
    consts = (wdq, wdkv, qan, kvan, wuq, wuk, wuv, qn, kn, ones)
    return pl.pallas_call(
        _mla_proj_kernel,
        grid=(N_TOK // tm,),
        in_specs=[_row_spec(tm, D_MODEL), _mod_spec(tm)] + [_row_spec(tm, LANES)] * 2
                 + [_const_spec(cst.shape) for cst in consts],
        out_specs=[_row_spec(tm, w) for w in out_w],
        out_shape=[jax.ShapeDtypeStruct((N_TOK, w), dt) for w, dt in zip(out_w, out_dt)],
        compiler_params=_cparams("parallel"),
        name="mla_projection",
    )(x, mod, *rope_tabs, *consts)


def _cache_kv_kernel(ckv_ref, kpe_ref, wuk_ref, wuv_ref, kn_ref, ones_ref, k_ref, v_ref):
    kpe = kpe_ref[...]
    _expand_keys(ckv_ref[...].astype(BF16), kpe * kpe, kpe * kn_ref[:, LANES:2 * LANES],
                 wuk_ref, wuv_ref, kn_ref, ones_ref, k_ref, v_ref)


def _cache_keys(ckv, kpe, wuk, wuv, kn, ones):
    rows = ckv.shape[0]
    tm = TM_ROWS
    consts = (wuk, wuv, kn, ones)
    return pl.pallas_call(
        _cache_kv_kernel,
        grid=(rows // tm,),
        in_specs=[_row_spec(tm, KV_RANK), _row_spec(tm, LANES)] + [_const_spec(cst.shape) for cst in consts],
        out_specs=[_row_spec(tm, MLA_HEADS * HEAD_PAD), _row_spec(tm, MLA_HEADS * V_HEAD)],
        out_shape=[jax.ShapeDtypeStruct((rows, MLA_HEADS * HEAD_PAD), BF16),
                   jax.ShapeDtypeStruct((rows, MLA_HEADS * V_HEAD), BF16)],
        compiler_params=_cparams("parallel"),
        name="cache_keys",
    )(ckv, kpe, *consts)


def _attn_kernel(has_cache, tq, n_par, heads, *refs):
    if has_cache:
        q_ref, k_ref, v_ref, kc_ref, vc_ref, o_ref = refs
    else:
        q_ref, k_ref, v_ref, o_ref = refs

    def tile(r0, hd):
        qk = slice(hd * HEAD_PAD, (hd + 1) * HEAD_PAD)
        vo = slice(hd * V_HEAD, (hd + 1) * V_HEAD)
        q = q_ref[pl.ds(r0, tq), qk]
        s = _dot_nt(q, k_ref[:, qk])
        m = jnp.max(s, axis=-1, keepdims=True)
        if has_cache:
            sc = _dot_nt(q, kc_ref[:, qk])
            m = jnp.maximum(m, jnp.max(sc, axis=-1, keepdims=True))
            pc = jnp.exp2(sc - m)
        p = jnp.exp2(s - m)
        den = jnp.sum(p, axis=-1, keepdims=True)
        num = _dot(p.astype(BF16), v_ref[:, vo])
        if has_cache:
            den = den + jnp.sum(pc, axis=-1, keepdims=True)
            num = num + _dot(pc.astype(BF16), vc_ref[:, vo])
        o_ref[pl.ds(r0, tq), vo] = (num / den).astype(BF16)

    group = n_par * tq
    n_groups = q_ref.shape[0] // group
    if n_groups == 1:
        for hd in range(heads):
            for t in range(n_par):
                tile(t * tq, hd)
    else:
        def body(i, carry):
            r0 = pl.multiple_of(i * group, group)
            for hd in range(heads):
                for t in range(n_par):
                    tile(r0 + t * tq, hd)
            return carry
        lax.fori_loop(0, n_groups, body, 0)


def _attention(q, k, v, kc, vc, n_seq, seq, row_block0):
    has_cache = kc is not None
    if seq >= 2 * TQ:
        tq, n_par, heads = TQ, 8, 1
    else:
        tq, n_par, heads = seq, 1, MLA_HEADS
    seq_spec = lambda w: pl.BlockSpec((seq, heads * w), lambda b, h: (row_block0 + b, h))
    in_specs = [seq_spec(HEAD_PAD), seq_spec(HEAD_PAD), seq_spec(V_HEAD)]
    args = [q, k, v]
    if has_cache:
        past = kc.shape[0] // n_seq
        in_specs += [pl.BlockSpec((past, heads * HEAD_PAD), lambda b, h: (b, h)),
                     pl.BlockSpec((past, heads * V_HEAD), lambda b, h: (b, h))]
        args += [kc, vc]
    return pl.pallas_call(
        functools.partial(_attn_kernel, has_cache, tq, n_par, heads),
        grid=(n_seq, MLA_HEADS // heads),
        in_specs=in_specs,
        out_specs=pl.BlockSpec((seq, heads * V_HEAD), lambda b, h: (b, h)),
        out_shape=jax.ShapeDtypeStruct((n_seq * seq, MLA_HEADS * V_HEAD), BF16),
        compiler_params=_cparams("parallel", "parallel"),
        name="attention_%d" % seq,
    )(*args)


def _split_row_specs(tm, width):
    n_ctx = N_PROMPT // tm
    return [pl.BlockSpec((tm, width), lambda i, *_: (jnp.minimum(i, n_ctx - 1), 0)),
            pl.BlockSpec((tm, width), lambda i, *_: (jnp.maximum(i - n_ctx, 0), 0))]


def _select_rows(ctx_ref, lat_ref):
    tm = ctx_ref.shape[0]
    return jnp.where(pl.program_id(0) < N_PROMPT // tm, ctx_ref[...], lat_ref[...])


def _attn_out_router_kernel(x_ref, mod_ref, ac_ref, al_ref, wo_ref, wr_ref, tri_ref,
                            xo_ref, h_ref, meta_ref, cnt_ref, run_ref):
    @pl.when(pl.program_id(0) == 0)
    def _():
        run_ref[...] = jnp.zeros_like(run_ref)

    x_new = x_ref[...] + mod_ref[0, 2:3, :] * _dot(_select_rows(ac_ref, al_ref), wo_ref[...])
    xo_ref[...] = x_new
    h = _rms(x_new) * (1.0 + mod_ref[0, 4:5, :]) + mod_ref[0, 3:4, :]
    h_ref[...] = h.astype(BF16)
    lane = lax.broadcasted_iota(jnp.int32, (x_ref.shape[0], LANES), 1)
    lane_f = lane.astype(F32)
    logits = jnp.where(lane < N_EXPERTS, _dot3(h, wr_ref[...]), -jnp.inf)
    m1 = jnp.max(logits, axis=-1, keepdims=True)
    e1 = jnp.min(jnp.where(logits == m1, lane_f, float(LANES)), axis=-1, keepdims=True)
    rest = jnp.where(lane_f == e1, -jnp.inf, logits)
    m2 = jnp.max(rest, axis=-1, keepdims=True)
    e2 = jnp.min(jnp.where(rest == m2, lane_f, float(LANES)), axis=-1, keepdims=True)
    t = jnp.exp(m2 - m1)
    g1 = 1.0 / (1.0 + t)
    pick1 = lane_f == e1
    pick2 = lane_f == e2
    onehot = jnp.where(pick1, 1.0, jnp.where(pick2, 1.0, 0.0))
    before = run_ref[...] + _dot(tri_ref[...], onehot.astype(BF16)) - onehot
    rank1 = jnp.sum(jnp.where(pick1, before, 0.0), axis=-1, keepdims=True)
    rank2 = jnp.sum(jnp.where(pick2, before, 0.0), axis=-1, keepdims=True)
    run_ref[...] += jnp.sum(onehot, axis=0, keepdims=True)
    cnt_ref[...] = jnp.broadcast_to(run_ref[...], cnt_ref.shape)
    cols = (e1, e2, g1, 1.0 - g1, rank1, rank2)
    meta = jnp.zeros(meta_ref.shape, F32)
    for idx, val in enumerate(cols):
        meta = jnp.where(lane == idx, val, meta)
    meta_ref[...] = meta


META_EXPERT, META_GATE, META_RANK = 0, 2, 4


def _attn_out_router(x, mod, a_ctx, a_lat, w_o, wr):
    tm = TM_ROWS
    k = a_ctx.shape[1]
    r = jnp.arange(tm)
    tri = (r[:, None] >= r[None, :]).astype(BF16)
    return pl.pallas_call(
        _attn_out_router_kernel,
        grid=(N_TOK // tm,),
        in_specs=[_row_spec(tm, D_MODEL), _mod_spec(tm)] + _split_row_specs(tm, k)
                 + [_const_spec((k, D_MODEL)), _const_spec((D_MODEL, LANES)), _const_spec((tm, tm))],
        out_specs=[_row_spec(tm, D_MODEL), _row_spec(tm, D_MODEL), _row_spec(tm, LANES),
                   _const_spec((SEG_PAD, LANES))],
        out_shape=[jax.ShapeDtypeStruct((N_TOK, D_MODEL), F32), jax.ShapeDtypeStruct((N_TOK, D_MODEL), BF16),
                   jax.ShapeDtypeStruct((N_TOK, LANES), F32), jax.ShapeDtypeStruct((SEG_PAD, LANES), F32)],
        scratch_shapes=[pltpu.VMEM((1, LANES), F32)],
        compiler_params=_cparams("arbitrary"),
        name="attn_out_router",
    )(x, mod, a_ctx, a_lat, w_o, wr, tri)


N_FF_CHUNKS = D_FF // FF_CHUNK
MOE_STEPS = 2
CHUNKS_PER_STEP = -(-N_FF_CHUNKS // MOE_STEPS)
TF_MOE = CHUNKS_PER_STEP * FF_CHUNK


def _experts_kernel(be_ref, bv_ref, nu_ref, rows_ref, wg_ref, wu_ref, wd_ref, o_ref, acc_ref):
    i = pl.program_id(0)
    f = pl.program_id(1)

    def chunk(c, rows):
        cols = slice(c * FF_CHUNK, (c + 1) * FF_CHUNK)
        gate = _dot(rows, wg_ref[0, 0, :, cols].astype(BF16))
        up = _dot(rows, wu_ref[0, 0, :, cols].astype(BF16))
        return _dot((_silu(gate) * up).astype(BF16), wd_ref[0, 0, cols, :].astype(BF16))

    @pl.when(bv_ref[i] > 0)
    def _():
        rows = rows_ref[...]
        acc = chunk(0, rows)
        for c in range(1, CHUNKS_PER_STEP - 1):
            acc = acc + chunk(c, rows)

        @pl.when(f == 0)
        def _():
            acc_ref[...] = acc

        @pl.when(f > 0)
        def _():
            acc_ref[...] += acc

        @pl.when((f * CHUNKS_PER_STEP + CHUNKS_PER_STEP) * FF_CHUNK <= D_FF)
        def _():
            acc_ref[...] += chunk(CHUNKS_PER_STEP - 1, rows_ref[...])

    @pl.when(f == pl.num_programs(1) - 1)
    def _():
        o_ref[...] = acc_ref[...].astype(BF16)


def _expert_ffn(rows, block_expert, block_valid, n_used, wg, wu, wd, layer):
    n_rows = rows.shape[0]
    tm = TM_MOE
    tf = TF_MOE
    n_f = MOE_STEPS

    def f_idx(i, f, nu):
        return jnp.where(i < nu[0], f, n_f - 1)

    grid_spec = pltpu.PrefetchScalarGridSpec(
        num_scalar_prefetch=3,
        grid=(n_rows // tm, n_f),
        in_specs=[pl.BlockSpec((tm, D_MODEL), lambda i, f, be, bv, nu: (i, 0)),
                  pl.BlockSpec((1, 1, D_MODEL, tf), lambda i, f, be, bv, nu: (layer, be[i], 0, f_idx(i, f, nu))),
                  pl.BlockSpec((1, 1, D_MODEL, tf), lambda i, f, be, bv, nu: (layer, be[i], 0, f_idx(i, f, nu))),
                  pl.BlockSpec((1, 1, tf, D_MODEL), lambda i, f, be, bv, nu: (layer, be[i], f_idx(i, f, nu), 0))],
        out_specs=pl.BlockSpec((tm, D_MODEL), lambda i, f, be, bv, nu: (i, 0)),
        scratch_shapes=[pltpu.VMEM((tm, D_MODEL), F32)])
    return pl.pallas_call(
        _experts_kernel,
        grid_spec=grid_spec,
        out_shape=jax.ShapeDtypeStruct((n_rows, D_MODEL), BF16),
        compiler_params=_cparams("arbitrary", "arbitrary"),
        name="expert_ffn",
    )(block_expert, block_valid, n_used, rows, wg, wu, wd)


def _combine_kernel(split, x_ref, mod_ref, meta_ref, a_ref, b_ref, *o_refs):
    g1 = meta_ref[:, META_GATE:META_GATE + 1]
    g2 = meta_ref[:, META_GATE + 1:META_GATE + 2]
    y = a_ref[...].astype(F32) * g1 + b_ref[...].astype(F32) * g2
    res = x_ref[...] + mod_ref[0, 5:6, :] * y
    if not split:
        o_refs[0][...] = res
        return
    is_ctx = pl.program_id(0) < N_PROMPT // x_ref.shape[0]

    @pl.when(is_ctx)
    def _():
        o_refs[0][...] = res

    @pl.when(jnp.logical_not(is_ctx))
    def _():
        o_refs[1][...] = res


def _moe_combine(x, mod, meta, a, b, split):
    tm = TM_FFN
    if split:
        out_specs = _split_row_specs(tm, D_MODEL)
        out_shape = [jax.ShapeDtypeStruct((N_PROMPT, D_MODEL), F32), jax.ShapeDtypeStruct((N_SAMPLE, D_MODEL), F32)]
    else:
        out_specs = _row_spec(tm, D_MODEL)
        out_shape = jax.ShapeDtypeStruct((N_TOK, D_MODEL), F32)
    return pl.pallas_call(
        functools.partial(_combine_kernel, split),
        grid=(N_TOK // tm,),
        in_specs=[_row_spec(tm, D_MODEL), _mod_spec(tm), _row_spec(tm, LANES), _row_spec(tm, D_MODEL),
                  _row_spec(tm, D_MODEL)],
        out_specs=out_specs,
        out_shape=out_shape,
        compiler_params=_cparams("arbitrary"),
        name="moe_combine",
    )(x, mod, meta, a, b)


def _moe(x, mod, h, meta, counts, wg, wu, wd, layer, split_out):
    tm = TM_MOE
    n_assign = N_TOK * TOP_K
    experts = meta[:, META_EXPERT:META_EXPERT + TOP_K].astype(jnp.int32)
    rank = meta[:, META_RANK:META_RANK + TOP_K].astype(jnp.int32)
    count = counts[0, :N_EXPERTS].astype(jnp.int32)
    padded = (count + tm - 1) // tm * tm
    pad_end = jnp.cumsum(padded)
    pad_start = pad_end - padded
    onehot = experts[:, :, None] == jnp.arange(N_EXPERTS, dtype=jnp.int32)
    dest = jnp.sum(jnp.where(onehot, pad_start, 0), axis=-1) + rank
    n_rows = n_assign + N_EXPERTS * tm
    n_blocks = n_rows // tm
    row_token = jnp.zeros((n_rows,), jnp.int32).at[dest.reshape(-1)].set(
        jnp.arange(n_assign, dtype=jnp.int32) // TOP_K, mode="promise_in_bounds", unique_indices=True)
    n_used = (pad_end[-1] // tm).astype(jnp.int32).reshape(1)
    block_id = jnp.arange(n_blocks, dtype=jnp.int32)
    block_start = jnp.minimum(block_id, n_used[0] - 1) * tm
    block_expert = jnp.minimum(jnp.sum((pad_end[None, :] <= block_start[:, None]).astype(jnp.int32), axis=1),
                               N_EXPERTS - 1)
    last_row = jnp.sum(jnp.where(block_expert[:, None] == jnp.arange(N_EXPERTS, dtype=jnp.int32),
                                 pad_start + count, 0), axis=1)
    block_valid = jnp.where(block_id < n_used[0], jnp.clip(last_row - block_start, 0, tm), 0)
    take = lambda table, idx: table.at[idx].get(mode="promise_in_bounds")
    h_rows = jnp.pad(h, ((0, n_rows - N_TOK), (0, 0)))
    out = _expert_ffn(take(h_rows, row_token), block_expert, block_valid, n_used, wg, wu, wd, layer)
    return _moe_combine(x, mod, meta, take(out, dest[:, 0]), take(out, dest[:, 1]), split_out)


def _rope_tables():
    half = ROPE_AXIS // 2
    pos = jnp.arange(DEC_SEQ)
    row = (pos // GRID_W).astype(F32)
    col = (pos % GRID_W).astype(F32)
    inv = ROPE_THETA ** (-jnp.arange(0, ROPE_AXIS, 2, dtype=F32) / ROPE_AXIS)
    ang_r = row[:, None] * inv
    ang_c = col[:, None] * inv
    pad = jnp.zeros((DEC_SEQ, LANES - QK_ROPE), F32)
    cos = jnp.concatenate([jnp.cos(ang_r), jnp.cos(ang_r), jnp.cos(ang_c), jnp.cos(ang_c), pad + 1.0], axis=1)
    sin = jnp.concatenate([-jnp.sin(ang_r), jnp.sin(ang_r), -jnp.sin(ang_c), jnp.sin(ang_c), pad], axis=1)
    prompt = jnp.zeros((N_PROMPT, LANES), F32)
    tile = lambda t: jnp.tile(t, (DEC_BATCH, 1))
    return (jnp.concatenate([prompt + 1.0, tile(cos)], axis=0),
            jnp.concatenate([prompt, tile(sin)], axis=0))


def _rope_blocks(a):
    half = ROPE_AXIS // 2
    partner = jnp.concatenate([a[..., half:2 * half], a[..., 0:half], a[..., 3 * half:4 * half],
                               a[..., 2 * half:3 * half]], axis=-1)
    return jnp.concatenate([_pad_lanes(a, LANES), _pad_lanes(partner, LANES)], axis=-1)


def _pad_lanes(a, width):
    return jnp.pad(a, [(0, 0)] * (a.ndim - 1) + [(0, width - a.shape[-1])])


def _ssd_constants():
    r = jnp.arange(CHUNK)
    tri = (r[:, None] >= r[None, :]).astype(BF16)
    head_of = jnp.arange(D_SSD) // SSD_HEADDIM
    e_f = (r[:, None] == head_of[None, :]).astype(BF16)
    e_b = (r[:, None] == head_of[None, :] + SSD_HEADS).astype(BF16)
    return tri, e_f, e_b


def kernel(x_prompt, x_sample, c, state_ssm, cache_ckv, cache_kpe, c_ctx, w_mod, b_mod, w_in, conv_w, conv_b, dt_bias, a_log, d_skip, ssd_norm, sgu_norm, w_sp, b_sp, w_out, ffn_w_gate, ffn_w_up, ffn_w_down, w_dq, q_a_norm, w_uq, w_dkv, kv_a_norm, w_ukv, q_norm, k_norm, w_o, router, moe_w_gate, moe_w_up, moe_w_down):
    x = jnp.concatenate([x_prompt.reshape(N_PROMPT, D_MODEL), x_sample.reshape(N_SAMPLE, D_MODEL)], axis=0)
    cond = jnp.concatenate([c_ctx[None, :], c, jnp.zeros((SEG_PAD - N_SEG, D_MODEL), F32)], axis=0)
    mods = _modulation_tables(cond, w_mod, b_mod).reshape(DEPTH, SEG_PAD, N_MOD, D_MODEL)
    rope_tabs = _rope_tables()
    tri, e_f, e_b = _ssd_constants()
    blk = jnp.arange(2 * LANES) // LANES
    pair_ones = (blk[:, None] == blk[None, :]).astype(BF16)
    i1 = D_SSD
    i2 = i1 + CONV_CH
    i3 = i2 + 2 * SSD_HEADS
    i4 = i3 + D_SGU
    new_ssm, new_ckv, new_kpe = [], [], []
    for i in range(DEPTH):
        j = i // 2
        mod = mods[i]
        if i % 2 == 0:
            w = w_in[j]
            wx = _col_blocks(w[:, i1:i2].astype(BF16), XBC_BLOCK)
            wgate = _col_blocks(jnp.concatenate([w[:, 0:i1], w[:, i3:]], axis=1).astype(BF16), GATE_BLOCK)
            wdt = _pad_lanes(w[:, i2:i3], LANES).astype(BF16)
            a_log_row = _pad_lanes(a_log[j].reshape(1, -1), LANES)
            xbc, gates, dt, cs = _in_projection(
                x, mod, (wx, wgate, wdt),
                (_col_blocks(conv_w[j], XBC_BLOCK), _col_blocks(conv_b[j][None, :], XBC_BLOCK),
                 _pad_lanes(dt_bias[j].reshape(1, -1), LANES), a_log_row, tri))
            consts = (a_log_row, jnp.repeat(d_skip[j], SSD_HEADDIM)[None, :], e_f, e_b)
            y_ctx, st_p = _ssd_scan(xbc, dt, cs, None, consts, BATCH, SEQ, 0)
            init = state_ssm[:, j].reshape(DEC_BATCH, 2, D_SSD, SSD_STATE)
            y_lat, _ = _ssd_scan(xbc, dt, cs, init, consts, DEC_BATCH, DEC_SEQ, N_PROMPT // DEC_SEQ)
            new_ssm.append(st_p.reshape(BATCH, 2, SSD_HEADS, SSD_HEADDIM, SSD_STATE))
            b_sp_e = jnp.repeat(b_sp[j].T, SGU_GDIM, axis=1)
            x = _mixer_output(x, mod, y_ctx, y_lat, gates, ssd_norm[j][None, :], sgu_norm[j][None, :],
                              w_sp[j].astype(BF16), b_sp_e, w_out[j].astype(BF16))
            x = _dense_ffn(x, mod, ffn_w_gate[j].astype(BF16), ffn_w_up[j].astype(BF16),
                           ffn_w_down[j].astype(BF16))
        else:
            split = lambda a: jnp.concatenate([a[..., :QK_NOPE], _rope_blocks(a[..., QK_NOPE:])], axis=-1)
            wuq = split(w_uq[j].reshape(Q_RANK, MLA_HEADS, QK_DIM)).reshape(Q_RANK, MLA_HEADS * HEAD_PROJ)
            wdkv = jnp.concatenate([w_dkv[j][:, :KV_RANK], _rope_blocks(w_dkv[j][:, KV_RANK:])], axis=-1)
            wukv = w_ukv[j].reshape(KV_RANK, MLA_HEADS, QK_NOPE + V_HEAD)
            wuk = wukv[:, :, :QK_NOPE].reshape(KV_RANK, -1).astype(BF16)
            wuv = wukv[:, :, QK_NOPE:].reshape(KV_RANK, -1).astype(BF16)
            qn = split(q_norm[j][None, :])
            kn = split(k_norm[j][None, :])
            q, k, v, ckv, kpe = _mla_projection(
                x, mod, rope_tabs, w_dq[j].astype(BF16), wdkv.astype(BF16), q_a_norm[j][None, :],
                kv_a_norm[j][None, :], wuq.astype(BF16), wuk, wuv, qn, kn, pair_ones)
            new_ckv.append(ckv[:N_PROMPT].reshape(BATCH, SEQ, KV_RANK))
            new_kpe.append(kpe[:N_PROMPT, :QK_ROPE].reshape(BATCH, SEQ, QK_ROPE))
            kc, vc = _cache_keys(cache_ckv[:, j].reshape(DEC_BATCH * PAST_LEN, KV_RANK),
                                 _pad_lanes(cache_kpe[:, j].reshape(DEC_BATCH * PAST_LEN, QK_ROPE), LANES),
                                 wuk, wuv, kn, pair_ones)
            o_ctx = _attention(q, k, v, None, None, BATCH, SEQ, 0)
            o_lat = _attention(q, k, v, kc, vc, DEC_BATCH, DEC_SEQ, N_PROMPT // DEC_SEQ)
            x, h, meta, counts = _attn_out_router(x, mod, o_ctx, o_lat, w_o[j].astype(BF16),
                                                  _pad_lanes(router[j], LANES))
            x = _moe(x, mod, h, meta, counts, moe_w_gate, moe_w_up, moe_w_down, j, split_out=(i == DEPTH - 1))
    assert DEPTH % 2 == 0
    x_ctx, x_lat = x
    return (x_ctx.reshape(BATCH, SEQ, D_MODEL),
            x_lat.reshape(DEC_BATCH, DEC_SEQ, D_MODEL),
            jnp.stack(new_ssm, axis=1),
            jnp.stack(new_ckv, axis=1),
            jnp.stack(new_kpe, axis=1))
```

```python
import functools
import math

import jax
import jax.numpy as jnp
from jax import lax
from jax.experimental import pallas as pl
from jax.experimental.pallas import tpu as pltpu

F32 = jnp.float32
BF16 = jnp.bfloat16

D_MODEL = 1024
BATCH = 16
SEQ = 256
DEPTH = 4
DEC_BATCH = 4
DEC_SEQ = 2048
PAST_LEN = 512
GRID_W = 64
N_MOD = 6
EPS = 1e-6

SSD_HEADDIM = 64
SSD_HEADS = 16
D_SSD = 1024
SSD_GROUPS = 4
SSD_STATE = 128
CHUNK = 128
CONV_W = 5
CONV_CH = D_SSD + 2 * SSD_GROUPS * SSD_STATE
D_SGU = 1024
SGU_GROUPS = 4
SGU_GDIM = D_SGU // SGU_GROUPS

MLA_HEADS = 8
Q_RANK = 384
KV_RANK = 256
QK_NOPE = 128
QK_ROPE = 64
V_HEAD = 128
QK_DIM = QK_NOPE + QK_ROPE
ROPE_AXIS = QK_ROPE // 2
ROPE_THETA = 10000.0
HEAD_PAD = 256

D_FF = 2816
N_EXPERTS = 8
TOP_K = 2

N_PROMPT = BATCH * SEQ
N_SAMPLE = DEC_BATCH * DEC_SEQ
N_TOK = N_PROMPT + N_SAMPLE
N_SEG = 1 + DEC_BATCH
SEG_PAD = 8

LANES = 128
VMEM_LIMIT = 56 * 1024 * 1024

TM_ROWS = 256
TM_FFN = 512
FF_CHUNK = 256
TM_MOE = 1024
TQ = 256


def _cparams(*sem):
    return pltpu.CompilerParams(dimension_semantics=sem, vmem_limit_bytes=VMEM_LIMIT)


def _dot(a, b):
    return jnp.dot(a, b, preferred_element_type=F32)


def _dot_nt(a, b):
    return lax.dot_general(a, b, (((1,), (1,)), ((), ())), preferred_element_type=F32)


def _split(x):
    hi = x.astype(BF16)
    lo = (x - hi.astype(F32)).astype(BF16)
    return hi, lo


def _dot3(a, b):
    ah, al = _split(a)
    bh, bl = _split(b)
    return _dot(ah, bh) + _dot(ah, bl) + _dot(al, bh)


def _silu(x):
    return x / (1.0 + jnp.exp(-x))


def _rms(x):
    return x * lax.rsqrt(jnp.mean(x * x, axis=-1, keepdims=True) + EPS)


def _modulated(x_ref, mod_ref, first):
    shift = mod_ref[0, first:first + 1, :]
    scale = mod_ref[0, first + 1:first + 2, :]
    return _rms(x_ref[...]) * (1.0 + scale) + shift


def _seg_map(tm):
    def index_map(i, *_):
        r = i * tm
        return (jnp.where(r < N_PROMPT, 0, 1 + (r - N_PROMPT) // DEC_SEQ), 0, 0)
    return index_map


def _row_spec(tm, width):
    return pl.BlockSpec((tm, width), lambda i, *_: (i, 0))


def _const_spec(shape):
    zeros = (0,) * len(shape)
    return pl.BlockSpec(shape, lambda i, *_: zeros)


def _mod_spec(tm):
    return pl.BlockSpec((1, N_MOD, D_MODEL), _seg_map(tm))


def _mod_kernel(c_ref, w_ref, b_ref, o_ref):
    o_ref[0] = _dot3(_silu(c_ref[...]), w_ref[0]) + b_ref[0]


def _modulation_tables(cond, w_mod, b_mod):
    tn = 1536
    return pl.pallas_call(
        _mod_kernel,
        grid=(DEPTH, N_MOD * D_MODEL // tn),
        in_specs=[pl.BlockSpec((SEG_PAD, D_MODEL), lambda l, j: (0, 0)),
                  pl.BlockSpec((1, D_MODEL, tn), lambda l, j: (l, 0, j)),
                  pl.BlockSpec((1, 1, tn), lambda l, j: (l, 0, j))],
        out_specs=pl.BlockSpec((1, SEG_PAD, tn), lambda l, j: (l, 0, j)),
        out_shape=jax.ShapeDtypeStruct((DEPTH, SEG_PAD, N_MOD * D_MODEL), F32),
        compiler_params=_cparams("parallel", "parallel"),
        name="modulation",
    )(cond, w_mod, b_mod.reshape(DEPTH, 1, N_MOD * D_MODEL))


HALO = 8


XBC_BLOCK = 512
N_XBC = CONV_CH // XBC_BLOCK
GATE_BLOCK = 256
N_GATE = (D_SSD + 2 * D_SGU) // GATE_BLOCK
GATE_PER_STAGE = N_GATE // N_XBC


def _inproj_kernel(x_ref, xp_ref, xn_ref, mod_ref, wx_ref, wg_ref, wdt, convw_ref, convb_ref, dtb_ref, alog_ref,
                   tri_ref, xbc_ref, gate_ref, dt_ref, cs_ref, h_ref, xa_ref, xb_ref):
    tm = x_ref.shape[0]
    r0 = pl.program_id(0) * tm
    shift = mod_ref[0, 0:1, :]
    scale = mod_ref[0, 1:2, :]
    x_ext = jnp.concatenate([x_ref[...], xp_ref[...], xn_ref[...]], axis=0)
    h_ref[...] = (_rms(x_ext) * (1.0 + scale) + shift).astype(BF16)

    rel = r0 - N_PROMPT
    in_latent = r0 >= N_PROMPT
    keep_prev = jnp.where(jnp.logical_and(in_latent, lax.rem(rel, DEC_SEQ) != 0), 1.0, 0.0)
    keep_next = jnp.where(jnp.logical_and(in_latent, lax.rem(rel + tm, DEC_SEQ) != 0), 1.0, 0.0)
    reach = CONV_W // 2

    bufs = (xa_ref, xb_ref)
    bufs[0][...] = _dot(h_ref[...], wx_ref[0])

    def stage(s, cur_ref, nxt_ref):
        if s + 1 < N_XBC:
            nxt_ref[...] = _dot(h_ref[...], wx_ref[s + 1])
        for t in range(GATE_PER_STAGE):
            g = s * GATE_PER_STAGE + t
            gate_ref[g] = _dot(h_ref[0:tm, :], wg_ref[g]).astype(BF16)
        for half in range(XBC_BLOCK // 256):
            cols = slice(half * 256, (half + 1) * 256)
            win = jnp.concatenate([cur_ref[tm:tm + HALO, cols] * keep_prev, cur_ref[0:tm, cols],
                                   cur_ref[tm + HALO:tm + 2 * HALO, cols] * keep_next], axis=0)
            acc = jnp.zeros((tm, 256), F32) + convb_ref[s, :, cols]
            for k in range(CONV_W):
                start = HALO - reach + k
                acc = acc + win[start:start + tm, :] * convw_ref[s, k:k + 1, cols]
            xbc_ref[s, :, cols] = _silu(acc).astype(BF16)

    for s in range(N_XBC):
        stage(s, bufs[s % 2], bufs[1 - s % 2])

    h = h_ref[0:tm, :]
    lane = lax.broadcasted_iota(jnp.int32, (1, LANES), 1)
    raw = _dot(h, wdt[...]) + dtb_ref[...]
    dt = jnp.where(lane < 2 * SSD_HEADS, jnp.maximum(raw, 0.0) + jnp.log(1.0 + jnp.exp(-jnp.abs(raw))), 0.0)
    dt_ref[...] = dt
    ac = dt * -jnp.exp(alog_ref[...])
    hi = ac.astype(BF16)
    rest = ac - hi.astype(F32)
    mid = rest.astype(BF16)
    lo = (rest - mid.astype(F32)).astype(BF16)
    tri = tri_ref[...]
    for k in range(tm // CHUNK):
        rows = slice(k * CHUNK, (k + 1) * CHUNK)
        cs_ref[rows, :] = _dot(tri, hi[rows, :]) + _dot(tri, mid[rows, :]) + _dot(tri, lo[rows, :])


def _col_blocks(w, block):
    return w.reshape(w.shape[0], -1, block).transpose(1, 0, 2)


def _in_projection(x, mod, weights, consts):
    tm = TM_ROWS
    per_tile = tm // HALO
    last = N_TOK // HALO - 1
    halo_prev = pl.BlockSpec((HALO, D_MODEL), lambda i: (jnp.maximum(i * per_tile - 1, 0), 0))
    halo_next = pl.BlockSpec((HALO, D_MODEL), lambda i: (jnp.minimum((i + 1) * per_tile, last), 0))
    blocked = lambda n, w: pl.BlockSpec((n, tm, w), lambda i: (0, i, 0))
    return pl.pallas_call(
        _inproj_kernel,
        grid=(N_TOK // tm,),
        in_specs=[_row_spec(tm, D_MODEL), halo_prev, halo_next, _mod_spec(tm)]
                 + [_const_spec(a.shape) for a in weights + consts],
        out_specs=[blocked(N_XBC, XBC_BLOCK), blocked(N_GATE, GATE_BLOCK), _row_spec(tm, LANES),
                   _row_spec(tm, LANES)],
        out_shape=[jax.ShapeDtypeStruct((N_XBC, N_TOK, XBC_BLOCK), BF16),
                   jax.ShapeDtypeStruct((N_GATE, N_TOK, GATE_BLOCK), BF16),
                   jax.ShapeDtypeStruct((N_TOK, LANES), F32), jax.ShapeDtypeStruct((N_TOK, LANES), F32)],
        scratch_shapes=[pltpu.VMEM((tm + 2 * HALO, D_MODEL), BF16),
                        pltpu.VMEM((tm + 2 * HALO, XBC_BLOCK), F32),
                        pltpu.VMEM((tm + 2 * HALO, XBC_BLOCK), F32)],
        compiler_params=_cparams("parallel"),
        name="in_projection",
    )(x, x, x, mod, *weights, *consts)


def _ssd_kernel(has_init, n_chunks, *refs):
    refs = list(refs)
    cv_ref, dtv_ref, cs_ref = refs[0:3]
    del refs[0:3]
    init_ref = refs.pop(0) if has_init else None
    alog_ref, dskip_ref, ef_ref, eb_ref, y_ref, fin_ref, st_ref = refs
    a_neg = -jnp.exp(alog_ref[...])
    rows_i = lax.broadcasted_iota(jnp.int32, (CHUNK, CHUNK), 0)
    cols_i = lax.broadcasted_iota(jnp.int32, (CHUNK, CHUNK), 1)
    lower = rows_i >= cols_i
    upper = cols_i >= rows_i
    even_head = jnp.bitwise_and(lax.broadcasted_iota(jnp.int32, (CHUNK, D_SSD), 1), LANES - 1) < SSD_HEADDIM

    assert SSD_GROUPS * SSD_STATE == XBC_BLOCK
    b_block = D_SSD // XBC_BLOCK

    def row0(c):
        return pl.multiple_of(c * CHUNK, CHUNK)

    def chunk_pass(c, forward):
        r0 = row0(c)
        rows = pl.ds(r0, CHUNK)
        e_ref = ef_ref if forward else eb_ref
        off = 0 if forward else SSD_HEADS
        dt = dtv_ref[rows, :]
        cs = cs_ref[rows, :]
        total = cs_ref[pl.ds(r0 + CHUNK - 1, 1), :]
        if forward:
            pos = cs
            to_edge = jnp.exp(total - cs)
            from_edge = jnp.exp(cs)
            mask = lower
        else:
            pos = cs - dt * a_neg
            to_edge = jnp.exp(pos)
            from_edge = jnp.exp(total - pos)
            mask = upper
        pos_t = pos.T
        x = jnp.concatenate([cv_ref[b, rows, :] for b in range(D_SSD // XBC_BLOCK)], axis=1).astype(F32)
        stacked = jnp.concatenate([dt, dt * to_edge, from_edge], axis=0).astype(BF16)
        spread = _dot(stacked, e_ref[...])
        xc = x * spread[0:CHUNK]
        xc_sub = (jnp.where(even_head, xc, 0.0).astype(BF16), jnp.where(even_head, 0.0, xc).astype(BF16))
        xd_b = (x * spread[CHUNK:2 * CHUNK]).astype(BF16)
        from_e = spread[2 * CHUNK:3 * CHUNK]
        tot_hi, tot_lo = _split(jnp.exp(jnp.broadcast_to(total, (8, LANES))))
        tot_e = (_dot(tot_hi, e_ref[...]) + _dot(tot_lo, e_ref[...]))[0:1, :]
        st = st_ref[0 if forward else 1]
        st_b = st.astype(BF16)
        y_parts = []
        st_parts = []
        for g in range(SSD_GROUPS):
            bm = cv_ref[b_block, rows, g * SSD_STATE:(g + 1) * SSD_STATE]
            cm = cv_ref[b_block + 1, rows, g * SSD_STATE:(g + 1) * SSD_STATE]
            bm_t = bm.astype(F32).T.astype(BF16)
            cb = _dot(cm, bm_t)
            gcols = slice(g * 256, (g + 1) * 256)
            y_off = _dot(cm, st_b[:, gcols])
            st_parts.append(_dot(bm_t, xd_b[:, gcols]))
            diag = []
            for pair in range(2):
                pcols = slice(g * 256 + pair * LANES, g * 256 + (pair + 1) * LANES)
                decayed = []
                for sub in range(2):
                    h = g * 4 + pair * 2 + sub
                    col = pos[:, off + h:off + h + 1]
                    row = pos_t[off + h:off + h + 1, :]
                    diff = (col - row) if forward else (row - col)
                    decayed.append((jnp.where(mask, jnp.exp(diff), 0.0) * cb).astype(BF16))
                diag.append(_dot(jnp.concatenate(decayed, axis=1),
                                 jnp.concatenate([xc_sub[0][:, pcols], xc_sub[1][:, pcols]], axis=0)))
            y_parts.append(jnp.concatenate(diag, axis=1) + y_off * from_e[:, gcols])
        y_new = jnp.concatenate(y_parts, axis=1)
        st_ref[0 if forward else 1] = st * tot_e + jnp.concatenate(st_parts, axis=1)
        if forward:
            y_new = y_new + x * dskip_ref[...]
        return rows, y_new

    if has_init:
        st_ref[0] = init_ref[0, 0].T
        st_ref[1] = init_ref[0, 1].T
    else:
        st_ref[...] = jnp.zeros_like(st_ref)

    def first_half(i, carry):
        for rows, y_new in (chunk_pass(i, True), chunk_pass(n_chunks - 1 - i, False)):
            y_ref[rows, :] = y_new
        return carry

    def second_half(i, carry):
        for rows, y_new in (chunk_pass(i, True), chunk_pass(n_chunks - 1 - i, False)):
            y_ref[rows, :] = y_ref[rows, :] + y_new
        return carry

    lax.fori_loop(0, n_chunks // 2, first_half, 0)
    lax.fori_loop(n_chunks // 2, n_chunks, second_half, 0)
    fin_ref[0, 0] = st_ref[0].T
    fin_ref[0, 1] = st_ref[1].T


def _ssd_scan(xbc, dt, cs, init, consts, n_seq, seq, row_block0):
    n_chunks = seq // CHUNK
    assert n_chunks % 2 == 0
    has_init = init is not None
    once = pl.Buffered(1)
    seq_spec = lambda w: pl.BlockSpec((seq, w), lambda b: (row_block0 + b, 0), pipeline_mode=once)
    xbc_spec = pl.BlockSpec((N_XBC, seq, XBC_BLOCK), lambda b: (0, row_block0 + b, 0), pipeline_mode=once)
    in_specs = [xbc_spec, seq_spec(LANES), seq_spec(LANES)]
    args = [xbc, dt, cs]
    if has_init:
        in_specs.append(pl.BlockSpec((1, 2, D_SSD, SSD_STATE), lambda b: (b, 0, 0, 0), pipeline_mode=once))
        args.append(init)
    for cst in consts:
        in_specs.append(pl.BlockSpec(cst.shape, lambda b, nd=cst.ndim: (0,) * nd))
        args.append(cst)
    return pl.pallas_call(
        functools.partial(_ssd_kernel, has_init, n_chunks),
        grid=(n_seq,),
        in_specs=in_specs,
        out_specs=[pl.BlockSpec((seq, D_SSD), lambda b: (b, 0)),
                   pl.BlockSpec((1, 2, D_SSD, SSD_STATE), lambda b: (b, 0, 0, 0))],
        out_shape=[jax.ShapeDtypeStruct((n_seq * seq, D_SSD), F32),
                   jax.ShapeDtypeStruct((n_seq, 2, D_SSD, SSD_STATE), F32)],
        scratch_shapes=[pltpu.VMEM((2, SSD_STATE, D_SSD), F32)],
        compiler_params=_cparams("parallel"),
        name="ssd_scan_%d" % seq,
    )(*args)


def _mixer_out_kernel(x_ref, mod_ref, yc_ref, yl_ref, z_ref, u_ref, v_ref, ssdn_ref, sgun_ref, wsp_ref, bsp_ref,
                      wo_ref, o_ref):
    tm = x_ref.shape[0]
    wide = lambda ref: jnp.concatenate([ref[b] for b in range(ref.shape[0])], axis=1).astype(F32)
    gated = _select_rows(yc_ref, yl_ref) * _silu(wide(z_ref))
    a = (_rms(gated) * ssdn_ref[...]).astype(BF16)
    vb = (_rms(wide(v_ref)) * sgun_ref[...]).astype(BF16)
    chunks = []
    for k in range(tm // CHUNK):
        rows = slice(k * CHUNK, (k + 1) * CHUNK)
        groups = [_dot(wsp_ref[g], vb[rows, g * SGU_GDIM:(g + 1) * SGU_GDIM]) for g in range(SGU_GROUPS)]
        chunks.append(jnp.concatenate(groups, axis=1) + bsp_ref[...])
    s = (wide(u_ref) * jnp.concatenate(chunks, axis=0)).astype(BF16)
    out = _dot(a, wo_ref[0:D_SSD, :]) + _dot(s, wo_ref[D_SSD:D_SSD + D_SGU, :])
    o_ref[...] = x_ref[...] + mod_ref[0, 2:3, :] * out


def _mixer_output(x, mod, y_ctx, y_lat, gates, ssd_norm, sgu_norm, w_sp, b_sp_e, w_out):
    tm = TM_FFN
    per = D_SSD // GATE_BLOCK
    part = lambda k: pl.BlockSpec((per, tm, GATE_BLOCK), lambda i: (k, i, 0))
    return pl.pallas_call(
        _mixer_out_kernel,
        grid=(N_TOK // tm,),
        in_specs=[_row_spec(tm, D_MODEL), _mod_spec(tm)] + _split_row_specs(tm, D_SSD) + [part(0), part(1), part(2),
                  _const_spec((1, D_SSD)), _const_spec((1, D_SGU)),
                  _const_spec((SGU_GROUPS, CHUNK, CHUNK)), _const_spec((CHUNK, D_SGU)),
                  _const_spec((D_SSD + D_SGU, D_MODEL))],
        out_specs=_row_spec(tm, D_MODEL),
        out_shape=jax.ShapeDtypeStruct((N_TOK, D_MODEL), F32),
        compiler_params=_cparams("parallel"),
        name="mixer_output",
    )(x, mod, y_ctx, y_lat, gates, gates, gates, ssd_norm, sgu_norm, w_sp, b_sp_e, w_out)


def _ffn_kernel(x_ref, mod_ref, wg_ref, wu_ref, wd_ref, o_ref):
    h = _modulated(x_ref, mod_ref, 3).astype(BF16)
    acc = jnp.zeros(o_ref.shape, F32)
    for f in range(D_FF // FF_CHUNK):
        cols = slice(f * FF_CHUNK, (f + 1) * FF_CHUNK)
        act = (_silu(_dot(h, wg_ref[:, cols])) * _dot(h, wu_ref[:, cols])).astype(BF16)
        acc = acc + _dot(act, wd_ref[cols, :])
    o_ref[...] = x_ref[...] + mod_ref[0, 5:6, :] * acc


def _dense_ffn(x, mod, wg, wu, wd):
    tm = TM_FFN
    once = pl.Buffered(1)
    return pl.pallas_call(
        _ffn_kernel,
        grid=(N_TOK // tm,),
        in_specs=[_row_spec(tm, D_MODEL), _mod_spec(tm),
                  pl.BlockSpec((D_MODEL, D_FF), lambda i: (0, 0), pipeline_mode=once),
                  pl.BlockSpec((D_MODEL, D_FF), lambda i: (0, 0), pipeline_mode=once),
                  pl.BlockSpec((D_FF, D_MODEL), lambda i: (0, 0), pipeline_mode=once)],
        out_specs=_row_spec(tm, D_MODEL),
        out_shape=jax.ShapeDtypeStruct((N_TOK, D_MODEL), F32),
        compiler_params=_cparams("parallel"),
        name="dense_ffn",
    )(x, mod, wg, wu, wd)


HEAD_PROJ = 3 * LANES


def _pair_sums(sq_a, sq_b, ones_ref):
    return _dot(jnp.concatenate([sq_a, sq_b], axis=1).astype(BF16), ones_ref[...])


def _expand_keys(ckv_b, pe_sq, pe_rot, wuk_ref, wuv_ref, kn_ref, ones_ref, k_ref, v_ref):
    kn_nope = kn_ref[:, 0:QK_NOPE]
    v_ref[...] = _dot(ckv_b, wuv_ref[...]).astype(BF16)
    k_nope = _dot(ckv_b, wuk_ref[...])
    for pair in range(MLA_HEADS // 2):
        kh = [k_nope[:, h * QK_NOPE:(h + 1) * QK_NOPE] for h in (2 * pair, 2 * pair + 1)]
        ss = _pair_sums(kh[0] * kh[0] + pe_sq, kh[1] * kh[1] + pe_sq, ones_ref)
        r = lax.rsqrt(ss * (1.0 / QK_DIM) + EPS)
        for idx in range(2):
            h = 2 * pair + idx
            rh = r[:, idx * LANES:(idx + 1) * LANES]
            k_ref[:, h * HEAD_PAD:h * HEAD_PAD + QK_NOPE] = (kh[idx] * rh * kn_nope).astype(BF16)
            k_ref[:, h * HEAD_PAD + QK_NOPE:(h + 1) * HEAD_PAD] = (pe_rot * rh).astype(BF16)


def _mla_proj_kernel(x_ref, mod_ref, cos_ref, sin_ref, wdq_ref, wdkv_ref, qan_ref, kvan_ref,
                     wuq_ref, wuk_ref, wuv_ref, qn_ref, kn_ref, ones_ref, q_ref, k_ref, v_ref, ckv_ref, kpe_ref):
    h = _modulated(x_ref, mod_ref, 0).astype(BF16)
    cos, sin = cos_ref[...], sin_ref[...]
    qa = (_rms(_dot(h, wdq_ref[...])) * qan_ref[...]).astype(BF16)
    q = _dot(qa, wuq_ref[...])
    qn_nope = qn_ref[:, 0:LANES]
    qn_rope = qn_ref[:, LANES:2 * LANES] * cos
    qn_part = qn_ref[:, 2 * LANES:3 * LANES] * sin
    scale = QK_DIM ** -0.5 * math.log2(math.e)
    for pair in range(MLA_HEADS // 2):
        blocks = []
        for hd in (2 * pair, 2 * pair + 1):
            base = hd * HEAD_PROJ
            blocks.append((q[:, base:base + LANES], q[:, base + LANES:base + 2 * LANES],
                           q[:, base + 2 * LANES:base + 3 * LANES]))
        ss = _pair_sums(*[qh * qh + qr * qr for qh, qr, _ in blocks], ones_ref)
        r = lax.rsqrt(ss * (1.0 / QK_DIM) + EPS) * scale
        for idx, (qh, qr, qp) in enumerate(blocks):
            hd = 2 * pair + idx
            rh = r[:, idx * LANES:(idx + 1) * LANES]
            q_ref[:, hd * HEAD_PAD:hd * HEAD_PAD + QK_NOPE] = (qh * rh * qn_nope).astype(BF16)
            q_ref[:, hd * HEAD_PAD + QK_NOPE:(hd + 1) * HEAD_PAD] = ((qr * qn_rope + qp * qn_part) * rh).astype(BF16)
    kva = _dot(h, wdkv_ref[...])
    ckv = _rms(kva[:, 0:KV_RANK]) * kvan_ref[...]
    kpe = kva[:, KV_RANK:KV_RANK + LANES]
    kpe_part = kva[:, KV_RANK + LANES:KV_RANK + 2 * LANES]
    ckv_ref[...] = ckv
    kpe_ref[...] = kpe
    pe_rot = kpe * (kn_ref[:, LANES:2 * LANES] * cos) + kpe_part * (kn_ref[:, 2 * LANES:3 * LANES] * sin)
    _expand_keys(ckv.astype(BF16), kpe * kpe, pe_rot, wuk_ref, wuv_ref, kn_ref, ones_ref, k_ref, v_ref)


def _mla_projection(x, mod, rope_tabs, wdq, wdkv, qan, kvan, wuq, wuk, wuv, qn, kn, ones):
    tm = TM_FFN
    hw = MLA_HEADS * HEAD_PAD
    out_w = (hw, hw, MLA_HEADS * V_HEAD, KV_RANK, LANES)
    out_dt = (BF16, BF16, BF16, F32, F32)
    consts = (wdq, wdkv, qan, kvan, wuq, wuk, wuv, qn, kn, ones)
    return pl.pallas_call(
        _mla_proj_kernel,
        grid=(N_TOK // tm,),
        in_specs=[_row_spec(tm, D_MODEL), _mod_spec(tm)] + [_row_spec(tm, LANES)] * 2
                 + [_const_spec(cst.shape) for cst in consts],
        out_specs=[_row_spec(tm, w) for w in out_w],
        out_shape=[jax.ShapeDtypeStruct((N_TOK, w), dt) for w, dt in zip(out_w, out_dt)],
        compiler_params=_cparams("parallel"),
        name="mla_projection",
    )(x, mod, *rope_tabs, *consts)


def _cache_kv_kernel(ckv_ref, kpe_ref, wuk_ref, wuv_ref, kn_ref, ones_ref, k_ref, v_ref):
    kpe = kpe_ref[...]
    _expand_keys(ckv_ref[...].astype(BF16), kpe * kpe, kpe * kn_ref[:, LANES:2 * LANES],
                 wuk_ref, wuv_ref, kn_ref, ones_ref, k_ref, v_ref)


def _cache_keys(ckv, kpe, wuk, wuv, kn, ones):
    rows = ckv.shape[0]
    tm = TM_ROWS
    consts = (wuk, wuv, kn, ones)
    return pl.pallas_call(
        _cache_kv_kernel,
        grid=(rows // tm,),
        in_specs=[_row_spec(tm, KV_RANK), _row_spec(tm, LANES)] + [_const_spec(cst.shape) for cst in consts],
        out_specs=[_row_spec(tm, MLA_HEADS * HEAD_PAD), _row_spec(tm, MLA_HEADS * V_HEAD)],
        out_shape=[jax.ShapeDtypeStruct((rows, MLA_HEADS * HEAD_PAD), BF16),
                   jax.ShapeDtypeStruct((rows, MLA_HEADS * V_HEAD), BF16)],
        compiler_params=_cparams("parallel"),
        name="cache_keys",
    )(ckv, kpe, *consts)


def _attn_kernel(has_cache, tq, n_par, heads, *refs):
    if has_cache:
        q_ref, k_ref, v_ref, kc_ref, vc_ref, o_ref = refs
    else:
        q_ref, k_ref, v_ref, o_ref = refs

    def tile(r0, hd):
        qk = slice(hd * HEAD_PAD, (hd + 1) * HEAD_PAD)
        vo = slice(hd * V_HEAD, (hd + 1) * V_HEAD)
        q = q_ref[pl.ds(r0, tq), qk]
        s = _dot_nt(q, k_ref[:, qk])
        m = jnp.max(s, axis=-1, keepdims=True)
        if has_cache:
            sc = _dot_nt(q, kc_ref[:, qk])
            m = jnp.maximum(m, jnp.max(sc, axis=-1, keepdims=True))
            pc = jnp.exp2(sc - m)
        p = jnp.exp2(s - m)
        den = jnp.sum(p, axis=-1, keepdims=True)
        num = _dot(p.astype(BF16), v_ref[:, vo])
        if has_cache:
            den = den + jnp.sum(pc, axis=-1, keepdims=True)
            num = num + _dot(pc.astype(BF16), vc_ref[:, vo])
        o_ref[pl.ds(r0, tq), vo] = (num / den).astype(BF16)

    group = n_par * tq
    n_groups = q_ref.shape[0] // group
    if n_groups == 1:
        for hd in range(heads):
            for t in range(n_par):
                tile(t * tq, hd)
    else:
        def body(i, carry):
            r0 = pl.multiple_of(i * group, group)
            for hd in range(heads):
                for t in range(n_par):
                    tile(r0 + t * tq, hd)
            return carry
        lax.fori_loop(0, n_groups, body, 0)


def _attention(q, k, v, kc, vc, n_seq, seq, row_block0):
    has_cache = kc is not None
    if seq >= 2 * TQ:
        tq, n_par, heads = TQ, 8, 1
    else:
        tq, n_par, heads = seq, 1, MLA_HEADS
    seq_spec = lambda w: pl.BlockSpec((seq, heads * w), lambda b, h: (row_block0 + b, h))
    in_specs = [seq_spec(HEAD_PAD), seq_spec(HEAD_PAD), seq_spec(V_HEAD)]
    args = [q, k, v]
    if has_cache:
        past = kc.shape[0] // n_seq
        in_specs += [pl.BlockSpec((past, heads * HEAD_PAD), lambda b, h: (b, h)),
                     pl.BlockSpec((past, heads * V_HEAD), lambda b, h: (b, h))]
        args += [kc, vc]
    return pl.pallas_call(
        functools.partial(_attn_kernel, has_cache, tq, n_par, heads),
        grid=(n_seq, MLA_HEADS // heads),
        in_specs=in_specs,
        out_specs=pl.BlockSpec((seq, heads * V_HEAD), lambda b, h: (b, h)),
        out_shape=jax.ShapeDtypeStruct((n_seq * seq, MLA_HEADS * V_HEAD), BF16),
        compiler_params=_cparams("parallel", "parallel"),
        name="attention_%d" % seq,
    )(*args)


def _split_row_specs(tm, width):
    n_ctx = N_PROMPT // tm
    return [pl.BlockSpec((tm, width), lambda i, *_: (jnp.minimum(i, n_ctx - 1), 0)),
            pl.BlockSpec((tm, width), lambda i, *_: (jnp.maximum(i - n_ctx, 0), 0))]


def _select_rows(ctx_ref, lat_ref):
    tm = ctx_ref.shape[0]
    return jnp.where(pl.program_id(0) < N_PROMPT // tm, ctx_ref[...], lat_ref[...])


def _attn_out_router_kernel(x_ref, mod_ref, ac_ref, al_ref, wo_ref, wr_ref, tri_ref,
                            xo_ref, h_ref, meta_ref, cnt_ref, run_ref):
    @pl.when(pl.program_id(0) == 0)
    def _():
        run_ref[...] = jnp.zeros_like(run_ref)

    sub = tri_ref.shape[0]
    attn = _select_rows(ac_ref, al_ref)
    lane = lax.broadcasted_iota(jnp.int32, (sub, LANES), 1)
    lane_f = lane.astype(F32)
    run = run_ref[...]
    for s in range(x_ref.shape[0] // sub):
        rows = slice(s * sub, (s + 1) * sub)
        x_new = x_ref[rows, :] + mod_ref[0, 2:3, :] * _dot(attn[rows, :], wo_ref[...])
        xo_ref[rows, :] = x_new
        h = _rms(x_new) * (1.0 + mod_ref[0, 4:5, :]) + mod_ref[0, 3:4, :]
        h_ref[rows, :] = h.astype(BF16)
        logits = jnp.where(lane < N_EXPERTS, _dot3(h, wr_ref[...]), -jnp.inf)
        m1 = jnp.max(logits, axis=-1, keepdims=True)
        e1 = jnp.min(jnp.where(logits == m1, lane_f, float(LANES)), axis=-1, keepdims=True)
        rest = jnp.where(lane_f == e1, -jnp.inf, logits)
        m2 = jnp.max(rest, axis=-1, keepdims=True)
        e2 = jnp.min(jnp.where(rest == m2, lane_f, float(LANES)), axis=-1, keepdims=True)
        t = jnp.exp(m2 - m1)
        g1 = 1.0 / (1.0 + t)
        pick1 = lane_f == e1
        pick2 = lane_f == e2
        onehot = jnp.where(pick1, 1.0, jnp.where(pick2, 1.0, 0.0))
        before = run + _dot(tri_ref[...], onehot.astype(BF16)) - onehot
        rank1 = jnp.sum(jnp.where(pick1, before, 0.0), axis=-1, keepdims=True)
        rank2 = jnp.sum(jnp.where(pick2, before, 0.0), axis=-1, keepdims=True)
        run = run + jnp.sum(onehot, axis=0, keepdims=True)
        cols = (e1, e2, g1, 1.0 - g1, rank1, rank2)
        meta = jnp.zeros((sub, LANES), F32)
        for idx, val in enumerate(cols):
            meta = jnp.where(lane == idx, val, meta)
        meta_ref[rows, :] = meta
    run_ref[...] = run
    cnt_ref[...] = jnp.broadcast_to(run, cnt_ref.shape)


META_EXPERT, META_GATE, META_RANK = 0, 2, 4


def _attn_out_router(x, mod, a_ctx, a_lat, w_o, wr):
    tm = TM_FFN
    sub = TM_ROWS
    k = a_ctx.shape[1]
    r = jnp.arange(sub)
    tri = (r[:, None] >= r[None, :]).astype(BF16)
    return pl.pallas_call(
        _attn_out_router_kernel,
        grid=(N_TOK // tm,),
        in_specs=[_row_spec(tm, D_MODEL), _mod_spec(tm)] + _split_row_specs(tm, k)
                 + [_const_spec((k, D_MODEL)), _const_spec((D_MODEL, LANES)), _const_spec((sub, sub))],
        out_specs=[_row_spec(tm, D_MODEL), _row_spec(tm, D_MODEL), _row_spec(tm, LANES),
                   _const_spec((SEG_PAD, LANES))],
        out_shape=[jax.ShapeDtypeStruct((N_TOK, D_MODEL), F32), jax.ShapeDtypeStruct((N_TOK, D_MODEL), BF16),
                   jax.ShapeDtypeStruct((N_TOK, LANES), F32), jax.ShapeDtypeStruct((SEG_PAD, LANES), F32)],
        scratch_shapes=[pltpu.VMEM((1, LANES), F32)],
        compiler_params=_cparams("arbitrary"),
        name="attn_out_router",
    )(x, mod, a_ctx, a_lat, w_o, wr, tri)


N_FF_CHUNKS = D_FF // FF_CHUNK
MOE_STEPS = 2
CHUNKS_PER_STEP = -(-N_FF_CHUNKS // MOE_STEPS)
TF_MOE = CHUNKS_PER_STEP * FF_CHUNK


def _experts_kernel(be_ref, bv_ref, nu_ref, rows_ref, wg_ref, wu_ref, wd_ref, o_ref, acc_ref):
    i = pl.program_id(0)
    f = pl.program_id(1)

    def chunk(c, rows):
        cols = slice(c * FF_CHUNK, (c + 1) * FF_CHUNK)
        gate = _dot(rows, wg_ref[0, 0, :, cols].astype(BF16))
        up = _dot(rows, wu_ref[0, 0, :, cols].astype(BF16))
        return _dot((_silu(gate) * up).astype(BF16), wd_ref[0, 0, cols, :].astype(BF16))

    @pl.when(bv_ref[i] > 0)
    def _():
        rows = rows_ref[...]
        acc = chunk(0, rows)
        for c in range(1, CHUNKS_PER_STEP - 1):
            acc = acc + chunk(c, rows)

        @pl.when(f == 0)
        def _():
            acc_ref[...] = acc

        @pl.when(f > 0)
        def _():
            acc_ref[...] += acc

        @pl.when((f * CHUNKS_PER_STEP + CHUNKS_PER_STEP) * FF_CHUNK <= D_FF)
        def _():
            acc_ref[...] += chunk(CHUNKS_PER_STEP - 1, rows_ref[...])

    @pl.when(f == pl.num_programs(1) - 1)
    def _():
        o_ref[...] = acc_ref[...].astype(BF16)


def _expert_ffn(rows, block_expert, block_valid, n_used, wg, wu, wd, layer):
    n_rows = rows.shape[0]
    tm = TM_MOE
    tf = TF_MOE
    n_f = MOE_STEPS

    def f_idx(i, f, nu):
        return jnp.where(i < nu[0], f, n_f - 1)

    grid_spec = pltpu.PrefetchScalarGridSpec(
        num_scalar_prefetch=3,
        grid=(n_rows // tm, n_f),
        in_specs=[pl.BlockSpec((tm, D_MODEL), lambda i, f, be, bv, nu: (i, 0)),
                  pl.BlockSpec((1, 1, D_MODEL, tf), lambda i, f, be, bv, nu: (layer, be[i], 0, f_idx(i, f, nu))),
                  pl.BlockSpec((1, 1, D_MODEL, tf), lambda i, f, be, bv, nu: (layer, be[i], 0, f_idx(i, f, nu))),
                  pl.BlockSpec((1, 1, tf, D_MODEL), lambda i, f, be, bv, nu: (layer, be[i], f_idx(i, f, nu), 0))],
        out_specs=pl.BlockSpec((tm, D_MODEL), lambda i, f, be, bv, nu: (i, 0)),
        scratch_shapes=[pltpu.VMEM((tm, D_MODEL), F32)])
    return pl.pallas_call(
        _experts_kernel,
        grid_spec=grid_spec,
        out_shape=jax.ShapeDtypeStruct((n_rows, D_MODEL), BF16),
        compiler_params=_cparams("arbitrary", "arbitrary"),
        name="expert_ffn",
    )(block_expert, block_valid, n_used, rows, wg, wu, wd)


def _combine_kernel(split, x_ref, mod_ref, meta_ref, a_ref, b_ref, *o_refs):
    g1 = meta_ref[:, META_GATE:META_GATE + 1]
    g2 = meta_ref[:, META_GATE + 1:META_GATE + 2]
    y = a_ref[...].astype(F32) * g1 + b_ref[...].astype(F32) * g2
    res = x_ref[...] + mod_ref[0, 5:6, :] * y
    if not split:
        o_refs[0][...] = res
        return
    is_ctx = pl.program_id(0) < N_PROMPT // x_ref.shape[0]

    @pl.when(is_ctx)
    def _():
        o_refs[0][...] = res

    @pl.when(jnp.logical_not(is_ctx))
    def _():
        o_refs[1][...] = res


def _moe_combine(x, mod, meta, a, b, split):
    tm = TM_FFN
    if split:
        out_specs = _split_row_specs(tm, D_MODEL)
        out_shape = [jax.ShapeDtypeStruct((N_PROMPT, D_MODEL), F32), jax.ShapeDtypeStruct((N_SAMPLE, D_MODEL), F32)]
    else:
        out_specs = _row_spec(tm, D_MODEL)
        out_shape = jax.ShapeDtypeStruct((N_TOK, D_MODEL), F32)
    return pl.pallas_call(
        functools.partial(_combine_kernel, split),
        grid=(N_TOK // tm,),
        in_specs=[_row_spec(tm, D_MODEL), _mod_spec(tm), _row_spec(tm, LANES), _row_spec(tm, D_MODEL),
                  _row_spec(tm, D_MODEL)],
        out_specs=out_specs,
        out_shape=out_shape,
        compiler_params=_cparams("arbitrary"),
        name="moe_combine",
    )(x, mod, meta, a, b)


def _moe(x, mod, h, meta, counts, wg, wu, wd, layer, split_out):
    tm = TM_MOE
    n_assign = N_TOK * TOP_K
    experts = meta[:, META_EXPERT:META_EXPERT + TOP_K].astype(jnp.int32)
    rank = meta[:, META_RANK:META_RANK + TOP_K].astype(jnp.int32)
    count = counts[0, :N_EXPERTS].astype(jnp.int32)
    padded = (count + tm - 1) // tm * tm
    pad_end = jnp.cumsum(padded)
    pad_start = pad_end - padded
    onehot = experts[:, :, None] == jnp.arange(N_EXPERTS, dtype=jnp.int32)
    dest = jnp.sum(jnp.where(onehot, pad_start, 0), axis=-1) + rank
    n_rows = n_assign + N_EXPERTS * tm
    n_blocks = n_rows // tm
    row_token = jnp.zeros((n_rows,), jnp.int32).at[dest.reshape(-1)].set(
        jnp.arange(n_assign, dtype=jnp.int32) // TOP_K)
    n_used = (pad_end[-1] // tm).astype(jnp.int32).reshape(1)
    block_id = jnp.arange(n_blocks, dtype=jnp.int32)
    block_start = jnp.minimum(block_id, n_used[0] - 1) * tm
    block_expert = jnp.minimum(jnp.sum((pad_end[None, :] <= block_start[:, None]).astype(jnp.int32), axis=1),
                               N_EXPERTS - 1)
    last_row = jnp.sum(jnp.where(block_expert[:, None] == jnp.arange(N_EXPERTS, dtype=jnp.int32),
                                 pad_start + count, 0), axis=1)
    block_valid = jnp.where(block_id < n_used[0], jnp.clip(last_row - block_start, 0, tm), 0)
    out = _expert_ffn(h[row_token], block_expert, block_valid, n_used, wg, wu, wd, layer)
    return _moe_combine(x, mod, meta, out[dest[:, 0]], out[dest[:, 1]], split_out)


def _rope_tables():
    half = ROPE_AXIS // 2
    pos = jnp.arange(DEC_SEQ)
    row = (pos // GRID_W).astype(F32)
    col = (pos % GRID_W).astype(F32)
    inv = ROPE_THETA ** (-jnp.arange(0, ROPE_AXIS, 2, dtype=F32) / ROPE_AXIS)
    ang_r = row[:, None] * inv
    ang_c = col[:, None] * inv
    pad = jnp.zeros((DEC_SEQ, LANES - QK_ROPE), F32)
    cos = jnp.concatenate([jnp.cos(ang_r), jnp.cos(ang_r), jnp.cos(ang_c), jnp.cos(ang_c), pad + 1.0], axis=1)
    sin = jnp.concatenate([-jnp.sin(ang_r), jnp.sin(ang_r), -jnp.sin(ang_c), jnp.sin(ang_c), pad], axis=1)
    prompt = jnp.zeros((N_PROMPT, LANES), F32)
    tile = lambda t: jnp.tile(t, (DEC_BATCH, 1))
    return (jnp.concatenate([prompt + 1.0, tile(cos)], axis=0),
            jnp.concatenate([prompt, tile(sin)], axis=0))


def _rope_blocks(a):
    half = ROPE_AXIS // 2
    partner = jnp.concatenate([a[..., half:2 * half], a[..., 0:half], a[..., 3 * half:4 * half],
                               a[..., 2 * half:3 * half]], axis=-1)
    return jnp.concatenate([_pad_lanes(a, LANES), _pad_lanes(partner, LANES)], axis=-1)


def _pad_lanes(a, width):
    return jnp.pad(a, [(0, 0)] * (a.ndim - 1) + [(0, width - a.shape[-1])])


def _ssd_constants():
    r = jnp.arange(CHUNK)
    tri = (r[:, None] >= r[None, :]).astype(BF16)
    head_of = jnp.arange(D_SSD) // SSD_HEADDIM
    e_f = (r[:, None] == head_of[None, :]).astype(BF16)
    e_b = (r[:, None] == head_of[None, :] + SSD_HEADS).astype(BF16)
    return tri, e_f, e_b


def kernel(x_prompt, x_sample, c, state_ssm, cache_ckv, cache_kpe, c_ctx, w_mod, b_mod, w_in, conv_w, conv_b, dt_bias, a_log, d_skip, ssd_norm, sgu_norm, w_sp, b_sp, w_out, ffn_w_gate, ffn_w_up, ffn_w_down, w_dq, q_a_norm, w_uq, w_dkv, kv_a_norm, w_ukv, q_norm, k_norm, w_o, router, moe_w_gate, moe_w_up, moe_w_down):
    x = jnp.concatenate([x_prompt.reshape(N_PROMPT, D_MODEL), x_sample.reshape(N_SAMPLE, D_MODEL)], axis=0)
    cond = jnp.concatenate([c_ctx[None, :], c, jnp.zeros((SEG_PAD - N_SEG, D_MODEL), F32)], axis=0)
    mods = _modulation_tables(cond, w_mod, b_mod).reshape(DEPTH, SEG_PAD, N_MOD, D_MODEL)
    rope_tabs = _rope_tables()
    tri, e_f, e_b = _ssd_constants()
    blk = jnp.arange(2 * LANES) // LANES
    pair_ones = (blk[:, None] == blk[None, :]).astype(BF16)
    i1 = D_SSD
    i2 = i1 + CONV_CH
    i3 = i2 + 2 * SSD_HEADS
    i4 = i3 + D_SGU
    new_ssm, new_ckv, new_kpe = [], [], []
    for i in range(DEPTH):
        j = i // 2
        mod = mods[i]
        if i % 2 == 0:
            w = w_in[j]
            wx = _col_blocks(w[:, i1:i2].astype(BF16), XBC_BLOCK)
            wgate = _col_blocks(jnp.concatenate([w[:, 0:i1], w[:, i3:]], axis=1).astype(BF16), GATE_BLOCK)
            wdt = _pad_lanes(w[:, i2:i3], LANES).astype(BF16)
            a_log_row = _pad_lanes(a_log[j].reshape(1, -1), LANES)
            xbc, gates, dt, cs = _in_projection(
                x, mod, (wx, wgate, wdt),
                (_col_blocks(conv_w[j], XBC_BLOCK), _col_blocks(conv_b[j][None, :], XBC_BLOCK),
                 _pad_lanes(dt_bias[j].reshape(1, -1), LANES), a_log_row, tri))
            consts = (a_log_row, jnp.repeat(d_skip[j], SSD_HEADDIM)[None, :], e_f, e_b)
            y_ctx, st_p = _ssd_scan(xbc, dt, cs, None, consts, BATCH, SEQ, 0)
            init = state_ssm[:, j].reshape(DEC_BATCH, 2, D_SSD, SSD_STATE)
            y_lat, _ = _ssd_scan(xbc, dt, cs, init, consts, DEC_BATCH, DEC_SEQ, N_PROMPT // DEC_SEQ)
            new_ssm.append(st_p.reshape(BATCH, 2, SSD_HEADS, SSD_HEADDIM, SSD_STATE))
            b_sp_e = jnp.repeat(b_sp[j].T, SGU_GDIM, axis=1)
            x = _mixer_output(x, mod, y_ctx, y_lat, gates, ssd_norm[j][None, :], sgu_norm[j][None, :],
                              w_sp[j].astype(BF16), b_sp_e, w_out[j].astype(BF16))
            x = _dense_ffn(x, mod, ffn_w_gate[j].astype(BF16), ffn_w_up[j].astype(BF16),
                           ffn_w_down[j].astype(BF16))
        else:
            split = lambda a: jnp.concatenate([a[..., :QK_NOPE], _rope_blocks(a[..., QK_NOPE:])], axis=-1)
            wuq = split(w_uq[j].reshape(Q_RANK, MLA_HEADS, QK_DIM)).reshape(Q_RANK, MLA_HEADS * HEAD_PROJ)
            wdkv = jnp.concatenate([w_dkv[j][:, :KV_RANK], _rope_blocks(w_dkv[j][:, KV_RANK:])], axis=-1)
            wukv = w_ukv[j].reshape(KV_RANK, MLA_HEADS, QK_NOPE + V_HEAD)
            wuk = wukv[:, :, :QK_NOPE].reshape(KV_RANK, -1).astype(BF16)
            wuv = wukv[:, :, QK_NOPE:].reshape(KV_RANK, -1).astype(BF16)
            qn = split(q_norm[j][None, :])
            kn = split(k_norm[j][None, :])
            q, k, v, ckv, kpe = _mla_projection(
                x, mod, rope_tabs, w_dq[j].astype(BF16), wdkv.astype(BF16), q_a_norm[j][None, :],
                kv_a_norm[j][None, :], wuq.astype(BF16), wuk, wuv, qn, kn, pair_ones)
            new_ckv.append(ckv[:N_PROMPT].reshape(BATCH, SEQ, KV_RANK))
            new_kpe.append(kpe[:N_PROMPT, :QK_ROPE].reshape(BATCH, SEQ, QK_ROPE))
            kc, vc = _cache_keys(cache_ckv[:, j].reshape(DEC_BATCH * PAST_LEN, KV_RANK),
                                 _pad_lanes(cache_kpe[:, j].reshape(DEC_BATCH * PAST_LEN, QK_ROPE), LANES),
                                 wuk, wuv, kn, pair_ones)
            o_ctx = _attention(q, k, v, None, None, BATCH, SEQ, 0)
            o_lat = _attention(q, k, v, kc, vc, DEC_BATCH, DEC_SEQ, N_PROMPT // DEC_SEQ)
            x, h, meta, counts = _attn_out_router(x, mod, o_ctx, o_lat, w_o[j].astype(BF16),
                                                  _pad_lanes(router[j], LANES))
            x = _moe(x, mod, h, meta, counts, moe_w_gate, moe_w_up, moe_w_down, j, split_out=(i == DEPTH - 1))
    assert DEPTH % 2 == 0
    x_ctx, x_lat = x
    return (x_ctx.reshape(BATCH, SEQ, D_MODEL),
            x_lat.reshape(DEC_BATCH, DEC_SEQ, D_MODEL),
            jnp.stack(new_ssm, axis=1),
            jnp.stack(new_ckv, axis=1),
            jnp.stack(new_kpe, axis=1))
```

```python
import functools
import math

import jax
import jax.numpy as jnp
from jax import lax
from jax.experimental import pallas as pl
from jax.experimental.pallas import tpu as pltpu

F32 = jnp.float32
BF16 = jnp.bfloat16

D_MODEL = 1024
BATCH = 16
SEQ = 256
DEPTH = 4
DEC_BATCH = 4
DEC_SEQ = 2048
PAST_LEN = 512
GRID_W = 64
N_MOD = 6
EPS = 1e-6

SSD_HEADDIM = 64
SSD_HEADS = 16
D_SSD = 1024
SSD_GROUPS = 4
SSD_STATE = 128
CHUNK = 128
CONV_W = 5
CONV_CH = D_SSD + 2 * SSD_GROUPS * SSD_STATE
D_SGU = 1024
SGU_GROUPS = 4
SGU_GDIM = D_SGU // SGU_GROUPS

MLA_HEADS = 8
Q_RANK = 384
KV_RANK = 256
QK_NOPE = 128
QK_ROPE = 64
V_HEAD = 128
QK_DIM = QK_NOPE + QK_ROPE
ROPE_AXIS = QK_ROPE // 2
ROPE_THETA = 10000.0
HEAD_PAD = 256

D_FF = 2816
N_EXPERTS = 8
TOP_K = 2

N_PROMPT = BATCH * SEQ
N_SAMPLE = DEC_BATCH * DEC_SEQ
N_TOK = N_PROMPT + N_SAMPLE
N_SEG = 1 + DEC_BATCH
SEG_PAD = 8

LANES = 128
VMEM_LIMIT = 56 * 1024 * 1024

TM_ROWS = 256
TM_FFN = 512
FF_CHUNK = 256
TM_MOE = 1024
TQ = 256


def _cparams(*sem):
    return pltpu.CompilerParams(dimension_semantics=sem, vmem_limit_bytes=VMEM_LIMIT)


def _dot(a, b):
    return jnp.dot(a, b, preferred_element_type=F32)


def _dot_nt(a, b):
    return lax.dot_general(a, b, (((1,), (1,)), ((), ())), preferred_element_type=F32)


def _split(x):
    hi = x.astype(BF16)
    lo = (x - hi.astype(F32)).astype(BF16)
    return hi, lo


def _dot3(a, b):
    ah, al = _split(a)
    bh, bl = _split(b)
    return _dot(ah, bh) + _dot(ah, bl) + _dot(al, bh)


def _silu(x):
    return x / (1.0 + jnp.exp(-x))


def _rms(x):
    return x * lax.rsqrt(jnp.mean(x * x, axis=-1, keepdims=True) + EPS)


def _modulated(x_ref, mod_ref, first):
    shift = mod_ref[0, first:first + 1, :]
    scale = mod_ref[0, first + 1:first + 2, :]
    return _rms(x_ref[...]) * (1.0 + scale) + shift


def _seg_map(tm):
    def index_map(i, *_):
        r = i * tm
        return (jnp.where(r < N_PROMPT, 0, 1 + (r - N_PROMPT) // DEC_SEQ), 0, 0)
    return index_map


def _row_spec(tm, width):
    return pl.BlockSpec((tm, width), lambda i, *_: (i, 0))


def _const_spec(shape):
    zeros = (0,) * len(shape)
    return pl.BlockSpec(shape, lambda i, *_: zeros)


def _mod_spec(tm):
    return pl.BlockSpec((1, N_MOD, D_MODEL), _seg_map(tm))


def _mod_kernel(c_ref, w_ref, b_ref, o_ref):
    o_ref[0] = _dot3(_silu(c_ref[...]), w_ref[0]) + b_ref[0]


def _modulation_tables(cond, w_mod, b_mod):
    tn = 1536
    return pl.pallas_call(
        _mod_kernel,
        grid=(DEPTH, N_MOD * D_MODEL // tn),
        in_specs=[pl.BlockSpec((SEG_PAD, D_MODEL), lambda l, j: (0, 0)),
                  pl.BlockSpec((1, D_MODEL, tn), lambda l, j: (l, 0, j)),
                  pl.BlockSpec((1, 1, tn), lambda l, j: (l, 0, j))],
        out_specs=pl.BlockSpec((1, SEG_PAD, tn), lambda l, j: (l, 0, j)),
        out_shape=jax.ShapeDtypeStruct((DEPTH, SEG_PAD, N_MOD * D_MODEL), F32),
        compiler_params=_cparams("parallel", "parallel"),
        name="modulation",
    )(cond, w_mod, b_mod.reshape(DEPTH, 1, N_MOD * D_MODEL))


HALO = 8


XBC_BLOCK = 512
N_XBC = CONV_CH // XBC_BLOCK
GATE_BLOCK = 256
N_GATE = (D_SSD + 2 * D_SGU) // GATE_BLOCK
GATE_PER_STAGE = N_GATE // N_XBC


def _inproj_kernel(x_ref, xp_ref, xn_ref, mod_ref, wx_ref, wg_ref, wdt, convw_ref, convb_ref, dtb_ref, alog_ref,
                   tri_ref, xbc_ref, gate_ref, dt_ref, cs_ref, h_ref, xa_ref, xb_ref):
    tm = x_ref.shape[0]
    r0 = pl.program_id(0) * tm
    shift = mod_ref[0, 0:1, :]
    scale = mod_ref[0, 1:2, :]
    x_ext = jnp.concatenate([x_ref[...], xp_ref[...], xn_ref[...]], axis=0)
    h_ref[...] = (_rms(x_ext) * (1.0 + scale) + shift).astype(BF16)

    rel = r0 - N_PROMPT
    in_latent = r0 >= N_PROMPT
    keep_prev = jnp.where(jnp.logical_and(in_latent, lax.rem(rel, DEC_SEQ) != 0), 1.0, 0.0)
    keep_next = jnp.where(jnp.logical_and(in_latent, lax.rem(rel + tm, DEC_SEQ) != 0), 1.0, 0.0)
    reach = CONV_W // 2

    bufs = (xa_ref, xb_ref)
    bufs[0][...] = _dot(h_ref[...], wx_ref[0])

    def stage(s, cur_ref, nxt_ref):
        if s + 1 < N_XBC:
            nxt_ref[...] = _dot(h_ref[...], wx_ref[s + 1])
        for t in range(GATE_PER_STAGE):
            g = s * GATE_PER_STAGE + t
            gate_ref[g] = _dot(h_ref[0:tm, :], wg_ref[g]).astype(BF16)
        for half in range(XBC_BLOCK // 256):
            cols = slice(half * 256, (half + 1) * 256)
            win = jnp.concatenate([cur_ref[tm:tm + HALO, cols] * keep_prev, cur_ref[0:tm, cols],
                                   cur_ref[tm + HALO:tm + 2 * HALO, cols] * keep_next], axis=0)
            acc = jnp.zeros((tm, 256), F32) + convb_ref[s, :, cols]
            for k in range(CONV_W):
                start = HALO - reach + k
                acc = acc + win[start:start + tm, :] * convw_ref[s, k:k + 1, cols]
            xbc_ref[s, :, cols] = _silu(acc).astype(BF16)

    for s in range(N_XBC):
        stage(s, bufs[s % 2], bufs[1 - s % 2])

    h = h_ref[0:tm, :]
    lane = lax.broadcasted_iota(jnp.int32, (1, LANES), 1)
    raw = _dot(h, wdt[...]) + dtb_ref[...]
    dt = jnp.where(lane < 2 * SSD_HEADS, jnp.maximum(raw, 0.0) + jnp.log(1.0 + jnp.exp(-jnp.abs(raw))), 0.0)
    dt_ref[...] = dt
    ac = dt * -jnp.exp(alog_ref[...])
    hi = ac.astype(BF16)
    rest = ac - hi.astype(F32)
    mid = rest.astype(BF16)
    lo = (rest - mid.astype(F32)).astype(BF16)
    tri = tri_ref[...]
    for k in range(tm // CHUNK):
        rows = slice(k * CHUNK, (k + 1) * CHUNK)
        cs_ref[rows, :] = _dot(tri, hi[rows, :]) + _dot(tri, mid[rows, :]) + _dot(tri, lo[rows, :])


def _col_blocks(w, block):
    return w.reshape(w.shape[0], -1, block).transpose(1, 0, 2)


def _in_projection(x, mod, weights, consts):
    tm = TM_ROWS
    per_tile = tm // HALO
    last = N_TOK // HALO - 1
    halo_prev = pl.BlockSpec((HALO, D_MODEL), lambda i: (jnp.maximum(i * per_tile - 1, 0), 0))
    halo_next = pl.BlockSpec((HALO, D_MODEL), lambda i: (jnp.minimum((i + 1) * per_tile, last), 0))
    blocked = lambda n, w: pl.BlockSpec((n, tm, w), lambda i: (0, i, 0))
    return pl.pallas_call(
        _inproj_kernel,
        grid=(N_TOK // tm,),
        in_specs=[_row_spec(tm, D_MODEL), halo_prev, halo_next, _mod_spec(tm)]
                 + [_const_spec(a.shape) for a in weights + consts],
        out_specs=[blocked(N_XBC, XBC_BLOCK), blocked(N_GATE, GATE_BLOCK), _row_spec(tm, LANES),
                   _row_spec(tm, LANES)],
        out_shape=[jax.ShapeDtypeStruct((N_XBC, N_TOK, XBC_BLOCK), BF16),
                   jax.ShapeDtypeStruct((N_GATE, N_TOK, GATE_BLOCK), BF16),
                   jax.ShapeDtypeStruct((N_TOK, LANES), F32), jax.ShapeDtypeStruct((N_TOK, LANES), F32)],
        scratch_shapes=[pltpu.VMEM((tm + 2 * HALO, D_MODEL), BF16),
                        pltpu.VMEM((tm + 2 * HALO, XBC_BLOCK), F32),
                        pltpu.VMEM((tm + 2 * HALO, XBC_BLOCK), F32)],
        compiler_params=_cparams("parallel"),
        name="in_projection",
    )(x, x, x, mod, *weights, *consts)


def _ssd_kernel(has_init, n_chunks, *refs):
    refs = list(refs)
    cv_ref, dtv_ref, cs_ref = refs[0:3]
    del refs[0:3]
    init_ref = refs.pop(0) if has_init else None
    alog_ref, dskip_ref, ef_ref, eb_ref, y_ref, fin_ref, st_ref = refs
    a_neg = -jnp.exp(alog_ref[...])
    rows_i = lax.broadcasted_iota(jnp.int32, (CHUNK, CHUNK), 0)
    cols_i = lax.broadcasted_iota(jnp.int32, (CHUNK, CHUNK), 1)
    lower = rows_i >= cols_i
    upper = cols_i >= rows_i
    even_head = jnp.bitwise_and(lax.broadcasted_iota(jnp.int32, (CHUNK, D_SSD), 1), LANES - 1) < SSD_HEADDIM

    assert SSD_GROUPS * SSD_STATE == XBC_BLOCK
    b_block = D_SSD // XBC_BLOCK

    def row0(c):
        return pl.multiple_of(c * CHUNK, CHUNK)

    def chunk_pass(c, forward):
        r0 = row0(c)
        rows = pl.ds(r0, CHUNK)
        e_ref = ef_ref if forward else eb_ref
        off = 0 if forward else SSD_HEADS
        dt = dtv_ref[rows, :]
        cs = cs_ref[rows, :]
        total = cs_ref[pl.ds(r0 + CHUNK - 1, 1), :]
        if forward:
            pos = cs
            to_edge = jnp.exp(total - cs)
            from_edge = jnp.exp(cs)
            mask = lower
        else:
            pos = cs - dt * a_neg
            to_edge = jnp.exp(pos)
            from_edge = jnp.exp(total - pos)
            mask = upper
        pos_t = pos.T
        x = jnp.concatenate([cv_ref[b, rows, :] for b in range(D_SSD // XBC_BLOCK)], axis=1).astype(F32)
        stacked = jnp.concatenate([dt, dt * to_edge, from_edge], axis=0).astype(BF16)
        spread = _dot(stacked, e_ref[...])
        xc = x * spread[0:CHUNK]
        xc_sub = (jnp.where(even_head, xc, 0.0).astype(BF16), jnp.where(even_head, 0.0, xc).astype(BF16))
        xd_b = (x * spread[CHUNK:2 * CHUNK]).astype(BF16)
        from_e = spread[2 * CHUNK:3 * CHUNK]
        tot_hi, tot_lo = _split(jnp.exp(jnp.broadcast_to(total, (8, LANES))))
        tot_e = (_dot(tot_hi, e_ref[...]) + _dot(tot_lo, e_ref[...]))[0:1, :]
        st = st_ref[0 if forward else 1]
        st_b = st.astype(BF16)
        y_parts = []
        st_parts = []
        for g in range(SSD_GROUPS):
            bm = cv_ref[b_block, rows, g * SSD_STATE:(g + 1) * SSD_STATE]
            cm = cv_ref[b_block + 1, rows, g * SSD_STATE:(g + 1) * SSD_STATE]
            bm_t = bm.astype(F32).T.astype(BF16)
            cb = _dot(cm, bm_t)
            gcols = slice(g * 256, (g + 1) * 256)
            y_off = _dot(cm, st_b[:, gcols])
            st_parts.append(_dot(bm_t, xd_b[:, gcols]))
            diag = []
            for pair in range(2):
                pcols = slice(g * 256 + pair * LANES, g * 256 + (pair + 1) * LANES)
                decayed = []
                for sub in range(2):
                    h = g * 4 + pair * 2 + sub
                    col = pos[:, off + h:off + h + 1]
                    row = pos_t[off + h:off + h + 1, :]
                    diff = (col - row) if forward else (row - col)
                    decayed.append((jnp.where(mask, jnp.exp(diff), 0.0) * cb).astype(BF16))
                diag.append(_dot(jnp.concatenate(decayed, axis=1),
                                 jnp.concatenate([xc_sub[0][:, pcols], xc_sub[1][:, pcols]], axis=0)))
            y_parts.append(jnp.concatenate(diag, axis=1) + y_off * from_e[:, gcols])
        y_new = jnp.concatenate(y_parts, axis=1)
        st_ref[0 if forward else 1] = st * tot_e + jnp.concatenate(st_parts, axis=1)
        if forward:
            y_new = y_new + x * dskip_ref[...]
        return rows, y_new

    if has_init:
        st_ref[0] = init_ref[0, 0].T
        st_ref[1] = init_ref[0, 1].T
    else:
        st_ref[...] = jnp.zeros_like(st_ref)

    def first_half(i, carry):
        for rows, y_new in (chunk_pass(i, True), chunk_pass(n_chunks - 1 - i, False)):
            y_ref[rows, :] = y_new
        return carry

    def second_half(i, carry):
        for rows, y_new in (chunk_pass(i, True), chunk_pass(n_chunks - 1 - i, False)):
            y_ref[rows, :] = y_ref[rows, :] + y_new
        return carry

    lax.fori_loop(0, n_chunks // 2, first_half, 0)
    lax.fori_loop(n_chunks // 2, n_chunks, second_half, 0)
    fin_ref[0, 0] = st_ref[0].T
    fin_ref[0, 1] = st_ref[1].T


def _ssd_scan(xbc, dt, cs, init, consts, n_seq, seq, row_block0):
    n_chunks = seq // CHUNK
    assert n_chunks % 2 == 0
    has_init = init is not None
    once = pl.Buffered(1) if seq * CONV_CH * 2 * 2 > VMEM_LIMIT // 4 else None
    seq_spec = lambda w: pl.BlockSpec((seq, w), lambda b: (row_block0 + b, 0), pipeline_mode=once)
    xbc_spec = pl.BlockSpec((N_XBC, seq, XBC_BLOCK), lambda b: (0, row_block0 + b, 0), pipeline_mode=once)
    in_specs = [xbc_spec, seq_spec(LANES), seq_spec(LANES)]
    args = [xbc, dt, cs]
    if has_init:
        in_specs.append(pl.BlockSpec((1, 2, D_SSD, SSD_STATE), lambda b: (b, 0, 0, 0), pipeline_mode=once))
        args.append(init)
    for cst in consts:
        in_specs.append(pl.BlockSpec(cst.shape, lambda b, nd=cst.ndim: (0,) * nd))
        args.append(cst)
    return pl.pallas_call(
        functools.partial(_ssd_kernel, has_init, n_chunks),
        grid=(n_seq,),
        in_specs=in_specs,
        out_specs=[pl.BlockSpec((seq, D_SSD), lambda b: (b, 0)),
                   pl.BlockSpec((1, 2, D_SSD, SSD_STATE), lambda b: (b, 0, 0, 0))],
        out_shape=[jax.ShapeDtypeStruct((n_seq * seq, D_SSD), F32),
                   jax.ShapeDtypeStruct((n_seq, 2, D_SSD, SSD_STATE), F32)],
        scratch_shapes=[pltpu.VMEM((2, SSD_STATE, D_SSD), F32)],
        compiler_params=_cparams("parallel"),
        name="ssd_scan_%d" % seq,
    )(*args)


def _mixer_out_kernel(x_ref, mod_ref, yc_ref, yl_ref, z_ref, u_ref, v_ref, ssdn_ref, sgun_ref, wsp_ref, bsp_ref,
                      wo_ref, o_ref):
    tm = x_ref.shape[0]
    wide = lambda ref: jnp.concatenate([ref[b] for b in range(ref.shape[0])], axis=1).astype(F32)
    gated = _select_rows(yc_ref, yl_ref) * _silu(wide(z_ref))
    a = (_rms(gated) * ssdn_ref[...]).astype(BF16)
    vb = (_rms(wide(v_ref)) * sgun_ref[...]).astype(BF16)
    chunks = []
    for k in range(tm // CHUNK):
        rows = slice(k * CHUNK, (k + 1) * CHUNK)
        groups = [_dot(wsp_ref[g], vb[rows, g * SGU_GDIM:(g + 1) * SGU_GDIM]) for g in range(SGU_GROUPS)]
        chunks.append(jnp.concatenate(groups, axis=1) + bsp_ref[...])
    s = (wide(u_ref) * jnp.concatenate(chunks, axis=0)).astype(BF16)
    out = _dot(a, wo_ref[0:D_SSD, :]) + _dot(s, wo_ref[D_SSD:D_SSD + D_SGU, :])
    o_ref[...] = x_ref[...] + mod_ref[0, 2:3, :] * out


def _mixer_output(x, mod, y_ctx, y_lat, gates, ssd_norm, sgu_norm, w_sp, b_sp_e, w_out):
    tm = TM_FFN
    per = D_SSD // GATE_BLOCK
    part = lambda k: pl.BlockSpec((per, tm, GATE_BLOCK), lambda i: (k, i, 0))
    return pl.pallas_call(
        _mixer_out_kernel,
        grid=(N_TOK // tm,),
        in_specs=[_row_spec(tm, D_MODEL), _mod_spec(tm)] + _split_row_specs(tm, D_SSD) + [part(0), part(1), part(2),
                  _const_spec((1, D_SSD)), _const_spec((1, D_SGU)),
                  _const_spec((SGU_GROUPS, CHUNK, CHUNK)), _const_spec((CHUNK, D_SGU)),
                  _const_spec((D_SSD + D_SGU, D_MODEL))],
        out_specs=_row_spec(tm, D_MODEL),
        out_shape=jax.ShapeDtypeStruct((N_TOK, D_MODEL), F32),
        compiler_params=_cparams("parallel"),
        name="mixer_output",
    )(x, mod, y_ctx, y_lat, gates, gates, gates, ssd_norm, sgu_norm, w_sp, b_sp_e, w_out)


def _ffn_kernel(x_ref, mod_ref, wg_ref, wu_ref, wd_ref, o_ref):
    h = _modulated(x_ref, mod_ref, 3).astype(BF16)
    acc = jnp.zeros(o_ref.shape, F32)
    for f in range(D_FF // FF_CHUNK):
        cols = slice(f * FF_CHUNK, (f + 1) * FF_CHUNK)
        act = (_silu(_dot(h, wg_ref[:, cols])) * _dot(h, wu_ref[:, cols])).astype(BF16)
        acc = acc + _dot(act, wd_ref[cols, :])
    o_ref[...] = x_ref[...] + mod_ref[0, 5:6, :] * acc


def _dense_ffn(x, mod, wg, wu, wd):
    tm = TM_FFN
    once = pl.Buffered(1)
    return pl.pallas_call(
        _ffn_kernel,
        grid=(N_TOK // tm,),
        in_specs=[_row_spec(tm, D_MODEL), _mod_spec(tm),
                  pl.BlockSpec((D_MODEL, D_FF), lambda i: (0, 0), pipeline_mode=once),
                  pl.BlockSpec((D_MODEL, D_FF), lambda i: (0, 0), pipeline_mode=once),
                  pl.BlockSpec((D_FF, D_MODEL), lambda i: (0, 0), pipeline_mode=once)],
        out_specs=_row_spec(tm, D_MODEL),
        out_shape=jax.ShapeDtypeStruct((N_TOK, D_MODEL), F32),
        compiler_params=_cparams("parallel"),
        name="dense_ffn",
    )(x, mod, wg, wu, wd)


HEAD_PROJ = 3 * LANES


def _pair_sums(sq_a, sq_b, ones_ref):
    return _dot(jnp.concatenate([sq_a, sq_b], axis=1).astype(BF16), ones_ref[...])


def _expand_keys(ckv_b, pe_sq, pe_rot, wuk_ref, wuv_ref, kn_ref, ones_ref, k_ref, v_ref):
    kn_nope = kn_ref[:, 0:QK_NOPE]
    v_ref[...] = _dot(ckv_b, wuv_ref[...]).astype(BF16)
    k_nope = _dot(ckv_b, wuk_ref[...])
    for pair in range(MLA_HEADS // 2):
        kh = [k_nope[:, h * QK_NOPE:(h + 1) * QK_NOPE] for h in (2 * pair, 2 * pair + 1)]
        ss = _pair_sums(kh[0] * kh[0] + pe_sq, kh[1] * kh[1] + pe_sq, ones_ref)
        r = lax.rsqrt(ss * (1.0 / QK_DIM) + EPS)
        for idx in range(2):
            h = 2 * pair + idx
            rh = r[:, idx * LANES:(idx + 1) * LANES]
            k_ref[:, h * HEAD_PAD:h * HEAD_PAD + QK_NOPE] = (kh[idx] * rh * kn_nope).astype(BF16)
            k_ref[:, h * HEAD_PAD + QK_NOPE:(h + 1) * HEAD_PAD] = (pe_rot * rh).astype(BF16)


def _mla_proj_kernel(x_ref, mod_ref, cos_ref, sin_ref, wdq_ref, wdkv_ref, qan_ref, kvan_ref,
                     wuq_ref, wuk_ref, wuv_ref, qn_ref, kn_ref, ones_ref, q_ref, k_ref, v_ref, ckv_ref, kpe_ref):
    h = _modulated(x_ref, mod_ref, 0).astype(BF16)
    cos, sin = cos_ref[...], sin_ref[...]
    qa = (_rms(_dot(h, wdq_ref[...])) * qan_ref[...]).astype(BF16)
    q = _dot(qa, wuq_ref[...])
    qn_nope = qn_ref[:, 0:LANES]
    qn_rope = qn_ref[:, LANES:2 * LANES] * cos
    qn_part = qn_ref[:, 2 * LANES:3 * LANES] * sin
    scale = QK_DIM ** -0.5 * math.log2(math.e)
    for pair in range(MLA_HEADS // 2):
        blocks = []
        for hd in (2 * pair, 2 * pair + 1):
            base = hd * HEAD_PROJ
            blocks.append((q[:, base:base + LANES], q[:, base + LANES:base + 2 * LANES],
                           q[:, base + 2 * LANES:base + 3 * LANES]))
        ss = _pair_sums(*[qh * qh + qr * qr for qh, qr, _ in blocks], ones_ref)
        r = lax.rsqrt(ss * (1.0 / QK_DIM) + EPS) * scale
        for idx, (qh, qr, qp) in enumerate(blocks):
            hd = 2 * pair + idx
            rh = r[:, idx * LANES:(idx + 1) * LANES]
            q_ref[:, hd * HEAD_PAD:hd * HEAD_PAD + QK_NOPE] = (qh * rh * qn_nope).astype(BF16)
            q_ref[:, hd * HEAD_PAD + QK_NOPE:(hd + 1) * HEAD_PAD] = ((qr * qn_rope + qp * qn_part) * rh).astype(BF16)
    kva = _dot(h, wdkv_ref[...])
    ckv = _rms(kva[:, 0:KV_RANK]) * kvan_ref[...]
    kpe = kva[:, KV_RANK:KV_RANK + LANES]
    kpe_part = kva[:, KV_RANK + LANES:KV_RANK + 2 * LANES]
    ckv_ref[...] = ckv
    kpe_ref[...] = kpe
    pe_rot = kpe * (kn_ref[:, LANES:2 * LANES] * cos) + kpe_part * (kn_ref[:, 2 * LANES:3 * LANES] * sin)
    _expand_keys(ckv.astype(BF16), kpe * kpe, pe_rot, wuk_ref, wuv_ref, kn_ref, ones_ref, k_ref, v_ref)


def _mla_projection(x, mod, rope_tabs, wdq, wdkv, qan, kvan, wuq, wuk, wuv, qn, kn, ones):
    tm = TM_FFN
    hw = MLA_HEADS * HEAD_PAD
    out_w = (hw, hw, MLA_HEADS * V_HEAD, KV_RANK, LANES)
    out_dt = (BF16, BF16, BF16, F32, F32)
    consts = (wdq, wdkv, qan, kvan, wuq, wuk, wuv, qn, kn, ones)
    return pl.pallas_call(
        _mla_proj_kernel,
        grid=(N_TOK // tm,),
        in_specs=[_row_spec(tm, D_MODEL), _mod_spec(tm)] + [_row_spec(tm, LANES)] * 2
                 + [_const_spec(cst.shape) for cst in consts],
        out_specs=[_row_spec(tm, w) for w in out_w],
        out_shape=[jax.ShapeDtypeStruct((N_TOK, w), dt) for w, dt in zip(out_w, out_dt)],
        compiler_params=_cparams("parallel"),
        name="mla_projection",
    )(x, mod, *rope_tabs, *consts)


def _cache_kv_kernel(ckv_ref, kpe_ref, wuk_ref, wuv_ref, kn_ref, ones_ref, k_ref, v_ref):
    kpe = kpe_ref[...]
    _expand_keys(ckv_ref[...].astype(BF16), kpe * kpe, kpe * kn_ref[:, LANES:2 * LANES],
                 wuk_ref, wuv_ref, kn_ref, ones_ref, k_ref, v_ref)


def _cache_keys(ckv, kpe, wuk, wuv, kn, ones):
    rows = ckv.shape[0]
    tm = TM_ROWS
    consts = (wuk, wuv, kn, ones)
    return pl.pallas_call(
        _cache_kv_kernel,
        grid=(rows // tm,),
        in_specs=[_row_spec(tm, KV_RANK), _row_spec(tm, LANES)] + [_const_spec(cst.shape) for cst in consts],
        out_specs=[_row_spec(tm, MLA_HEADS * HEAD_PAD), _row_spec(tm, MLA_HEADS * V_HEAD)],
        out_shape=[jax.ShapeDtypeStruct((rows, MLA_HEADS * HEAD_PAD), BF16),
                   jax.ShapeDtypeStruct((rows, MLA_HEADS * V_HEAD), BF16)],
        compiler_params=_cparams("parallel"),
        name="cache_keys",
    )(ckv, kpe, *consts)


def _attn_kernel(has_cache, tq, n_par, heads, *refs):
    if has_cache:
        q_ref, k_ref, v_ref, kc_ref, vc_ref, o_ref = refs
    else:
        q_ref, k_ref, v_ref, o_ref = refs

    def tile(r0, hd):
        qk = slice(hd * HEAD_PAD, (hd + 1) * HEAD_PAD)
        vo = slice(hd * V_HEAD, (hd + 1) * V_HEAD)
        q = q_ref[pl.ds(r0, tq), qk]
        s = _dot_nt(q, k_ref[:, qk])
        m = jnp.max(s, axis=-1, keepdims=True)
        if has_cache:
            sc = _dot_nt(q, kc_ref[:, qk])
            m = jnp.maximum(m, jnp.max(sc, axis=-1, keepdims=True))
            pc = jnp.exp2(sc - m)
        p = jnp.exp2(s - m)
        den = jnp.sum(p, axis=-1, keepdims=True)
        num = _dot(p.astype(BF16), v_ref[:, vo])
        if has_cache:
            den = den + jnp.sum(pc, axis=-1, keepdims=True)
            num = num + _dot(pc.astype(BF16), vc_ref[:, vo])
        o_ref[pl.ds(r0, tq), vo] = (num / den).astype(BF16)

    group = n_par * tq
    n_groups = q_ref.shape[0] // group
    if n_groups == 1:
        for hd in range(heads):
            for t in range(n_par):
                tile(t * tq, hd)
    else:
        def body(i, carry):
            r0 = pl.multiple_of(i * group, group)
            for hd in range(heads):
                for t in range(n_par):
                    tile(r0 + t * tq, hd)
            return carry
        lax.fori_loop(0, n_groups, body, 0)


def _attention(q, k, v, kc, vc, n_seq, seq, row_block0):
    has_cache = kc is not None
    if seq >= 2 * TQ:
        tq, n_par, heads = TQ, 8, 1
    else:
        tq, n_par, heads = seq, 1, MLA_HEADS
    seq_spec = lambda w: pl.BlockSpec((seq, heads * w), lambda b, h: (row_block0 + b, h))
    in_specs = [seq_spec(HEAD_PAD), seq_spec(HEAD_PAD), seq_spec(V_HEAD)]
    args = [q, k, v]
    if has_cache:
        past = kc.shape[0] // n_seq
        in_specs += [pl.BlockSpec((past, heads * HEAD_PAD), lambda b, h: (b, h)),
                     pl.BlockSpec((past, heads * V_HEAD), lambda b, h: (b, h))]
        args += [kc, vc]
    return pl.pallas_call(
        functools.partial(_attn_kernel, has_cache, tq, n_par, heads),
        grid=(n_seq, MLA_HEADS // heads),
        in_specs=in_specs,
        out_specs=pl.BlockSpec((seq, heads * V_HEAD), lambda b, h: (b, h)),
        out_shape=jax.ShapeDtypeStruct((n_seq * seq, MLA_HEADS * V_HEAD), BF16),
        compiler_params=_cparams("parallel", "parallel"),
        name="attention_%d" % seq,
    )(*args)


def _split_row_specs(tm, width):
    n_ctx = N_PROMPT // tm
    return [pl.BlockSpec((tm, width), lambda i, *_: (jnp.minimum(i, n_ctx - 1), 0)),
            pl.BlockSpec((tm, width), lambda i, *_: (jnp.maximum(i - n_ctx, 0), 0))]


def _select_rows(ctx_ref, lat_ref):
    tm = ctx_ref.shape[0]
    return jnp.where(pl.program_id(0) < N_PROMPT // tm, ctx_ref[...], lat_ref[...])


def _attn_out_router_kernel(x_ref, mod_ref, ac_ref, al_ref, wo_ref, wr_ref, tri_ref,
                            xo_ref, h_ref, meta_ref, cnt_ref, run_ref):
    @pl.when(pl.program_id(0) == 0)
    def _():
        run_ref[...] = jnp.zeros_like(run_ref)

    sub = tri_ref.shape[0]
    attn = _select_rows(ac_ref, al_ref)
    lane = lax.broadcasted_iota(jnp.int32, (sub, LANES), 1)
    lane_f = lane.astype(F32)
    run = run_ref[...]
    for s in range(x_ref.shape[0] // sub):
        rows = slice(s * sub, (s + 1) * sub)
        x_new = x_ref[rows, :] + mod_ref[0, 2:3, :] * _dot(attn[rows, :], wo_ref[...])
        xo_ref[rows, :] = x_new
        h = _rms(x_new) * (1.0 + mod_ref[0, 4:5, :]) + mod_ref[0, 3:4, :]
        h_ref[rows, :] = h.astype(BF16)
        logits = jnp.where(lane < N_EXPERTS, _dot3(h, wr_ref[...]), -jnp.inf)
        m1 = jnp.max(logits, axis=-1, keepdims=True)
        e1 = jnp.min(jnp.where(logits == m1, lane_f, float(LANES)), axis=-1, keepdims=True)
        rest = jnp.where(lane_f == e1, -jnp.inf, logits)
        m2 = jnp.max(rest, axis=-1, keepdims=True)
        e2 = jnp.min(jnp.where(rest == m2, lane_f, float(LANES)), axis=-1, keepdims=True)
        t = jnp.exp(m2 - m1)
        g1 = 1.0 / (1.0 + t)
        pick1 = lane_f == e1
        pick2 = lane_f == e2
        onehot = jnp.where(pick1, 1.0, jnp.where(pick2, 1.0, 0.0))
        before = run + _dot(tri_ref[...], onehot.astype(BF16)) - onehot
        rank1 = jnp.sum(jnp.where(pick1, before, 0.0), axis=-1, keepdims=True)
        rank2 = jnp.sum(jnp.where(pick2, before, 0.0), axis=-1, keepdims=True)
        run = run + jnp.sum(onehot, axis=0, keepdims=True)
        cols = (e1, e2, g1, 1.0 - g1, rank1, rank2)
        meta = jnp.zeros((sub, LANES), F32)
        for idx, val in enumerate(cols):
            meta = jnp.where(lane == idx, val, meta)
        meta_ref[rows, :] = meta
    run_ref[...] = run
    cnt_ref[...] = jnp.broadcast_to(run, cnt_ref.shape)


META_EXPERT, META_GATE, META_RANK = 0, 2, 4


def _attn_out_router(x, mod, a_ctx, a_lat, w_o, wr):
    tm = TM_FFN
    sub = TM_ROWS
    k = a_ctx.shape[1]
    r = jnp.arange(sub)
    tri = (r[:, None] >= r[None, :]).astype(BF16)
    return pl.pallas_call(
        _attn_out_router_kernel,
        grid=(N_TOK // tm,),
        in_specs=[_row_spec(tm, D_MODEL), _mod_spec(tm)] + _split_row_specs(tm, k)
                 + [_const_spec((k, D_MODEL)), _const_spec((D_MODEL, LANES)), _const_spec((sub, sub))],
        out_specs=[_row_spec(tm, D_MODEL), _row_spec(tm, D_MODEL), _row_spec(tm, LANES),
                   _const_spec((SEG_PAD, LANES))],
        out_shape=[jax.ShapeDtypeStruct((N_TOK, D_MODEL), F32), jax.ShapeDtypeStruct((N_TOK, D_MODEL), BF16),
                   jax.ShapeDtypeStruct((N_TOK, LANES), F32), jax.ShapeDtypeStruct((SEG_PAD, LANES), F32)],
        scratch_shapes=[pltpu.VMEM((1, LANES), F32)],
        compiler_params=_cparams("arbitrary"),
        name="attn_out_router",
    )(x, mod, a_ctx, a_lat, w_o, wr, tri)


N_FF_CHUNKS = D_FF // FF_CHUNK
MOE_STEPS = 2
CHUNKS_PER_STEP = -(-N_FF_CHUNKS // MOE_STEPS)
TF_MOE = CHUNKS_PER_STEP * FF_CHUNK


def _experts_kernel(be_ref, bv_ref, nu_ref, rows_ref, wg_ref, wu_ref, wd_ref, o_ref, acc_ref):
    i = pl.program_id(0)
    f = pl.program_id(1)

    def chunk(c, rows):
        cols = slice(c * FF_CHUNK, (c + 1) * FF_CHUNK)
        gate = _dot(rows, wg_ref[0, 0, :, cols].astype(BF16))
        up = _dot(rows, wu_ref[0, 0, :, cols].astype(BF16))
        return _dot((_silu(gate) * up).astype(BF16), wd_ref[0, 0, cols, :].astype(BF16))

    @pl.when(bv_ref[i] > 0)
    def _():
        rows = rows_ref[...]
        acc = chunk(0, rows)
        for c in range(1, CHUNKS_PER_STEP - 1):
            acc = acc + chunk(c, rows)

        @pl.when(f == 0)
        def _():
            acc_ref[...] = acc

        @pl.when(f > 0)
        def _():
            acc_ref[...] += acc

        @pl.when((f * CHUNKS_PER_STEP + CHUNKS_PER_STEP) * FF_CHUNK <= D_FF)
        def _():
            acc_ref[...] += chunk(CHUNKS_PER_STEP - 1, rows_ref[...])

    @pl.when(f == pl.num_programs(1) - 1)
    def _():
        o_ref[...] = acc_ref[...].astype(BF16)


def _expert_ffn(rows, block_expert, block_valid, n_used, wg, wu, wd, layer):
    n_rows = rows.shape[0]
    tm = TM_MOE
    tf = TF_MOE
    n_f = MOE_STEPS

    def f_idx(i, f, nu):
        return jnp.where(i < nu[0], f, n_f - 1)

    grid_spec = pltpu.PrefetchScalarGridSpec(
        num_scalar_prefetch=3,
        grid=(n_rows // tm, n_f),
        in_specs=[pl.BlockSpec((tm, D_MODEL), lambda i, f, be, bv, nu: (i, 0)),
                  pl.BlockSpec((1, 1, D_MODEL, tf), lambda i, f, be, bv, nu: (layer, be[i], 0, f_idx(i, f, nu))),
                  pl.BlockSpec((1, 1, D_MODEL, tf), lambda i, f, be, bv, nu: (layer, be[i], 0, f_idx(i, f, nu))),
                  pl.BlockSpec((1, 1, tf, D_MODEL), lambda i, f, be, bv, nu: (layer, be[i], f_idx(i, f, nu), 0))],
        out_specs=pl.BlockSpec((tm, D_MODEL), lambda i, f, be, bv, nu: (i, 0)),
        scratch_shapes=[pltpu.VMEM((tm, D_MODEL), F32)])
    return pl.pallas_call(
        _experts_kernel,
        grid_spec=grid_spec,
        out_shape=jax.ShapeDtypeStruct((n_rows, D_MODEL), BF16),
        compiler_params=_cparams("arbitrary", "arbitrary"),
        name="expert_ffn",
    )(block_expert, block_valid, n_used, rows, wg, wu, wd)


def _combine_kernel(split, x_ref, mod_ref, meta_ref, a_ref, b_ref, *o_refs):
    g1 = meta_ref[:, META_GATE:META_GATE + 1]
    g2 = meta_ref[:, META_GATE + 1:META_GATE + 2]
    y = a_ref[...].astype(F32) * g1 + b_ref[...].astype(F32) * g2
    res = x_ref[...] + mod_ref[0, 5:6, :] * y
    if not split:
        o_refs[0][...] = res
        return
    is_ctx = pl.program_id(0) < N_PROMPT // x_ref.shape[0]

    @pl.when(is_ctx)
    def _():
        o_refs[0][...] = res

    @pl.when(jnp.logical_not(is_ctx))
    def _():
        o_refs[1][...] = res


def _moe_combine(x, mod, meta, a, b, split):
    tm = TM_FFN
    if split:
        out_specs = _split_row_specs(tm, D_MODEL)
        out_shape = [jax.ShapeDtypeStruct((N_PROMPT, D_MODEL), F32), jax.ShapeDtypeStruct((N_SAMPLE, D_MODEL), F32)]
    else:
        out_specs = _row_spec(tm, D_MODEL)
        out_shape = jax.ShapeDtypeStruct((N_TOK, D_MODEL), F32)
    return pl.pallas_call(
        functools.partial(_combine_kernel, split),
        grid=(N_TOK // tm,),
        in_specs=[_row_spec(tm, D_MODEL), _mod_spec(tm), _row_spec(tm, LANES), _row_spec(tm, D_MODEL),
                  _row_spec(tm, D_MODEL)],
        out_specs=out_specs,
        out_shape=out_shape,
        compiler_params=_cparams("arbitrary"),
        name="moe_combine",
    )(x, mod, meta, a, b)


def _moe(x, mod, h, meta, counts, wg, wu, wd, layer, split_out):
    tm = TM_MOE
    n_assign = N_TOK * TOP_K
    experts = meta[:, META_EXPERT:META_EXPERT + TOP_K].astype(jnp.int32)
    rank = meta[:, META_RANK:META_RANK + TOP_K].astype(jnp.int32)
    count = counts[0, :N_EXPERTS].astype(jnp.int32)
    padded = (count + tm - 1) // tm * tm
    pad_end = jnp.cumsum(padded)
    pad_start = pad_end - padded
    onehot = experts[:, :, None] == jnp.arange(N_EXPERTS, dtype=jnp.int32)
    dest = jnp.sum(jnp.where(onehot, pad_start, 0), axis=-1) + rank
    n_rows = n_assign + N_EXPERTS * tm
    n_blocks = n_rows // tm
    row_token = jnp.zeros((n_rows,), jnp.int32).at[dest.reshape(-1)].set(
        jnp.arange(n_assign, dtype=jnp.int32) // TOP_K)
    n_used = (pad_end[-1] // tm).astype(jnp.int32).reshape(1)
    block_id = jnp.arange(n_blocks, dtype=jnp.int32)
    block_start = jnp.minimum(block_id, n_used[0] - 1) * tm
    block_expert = jnp.minimum(jnp.sum((pad_end[None, :] <= block_start[:, None]).astype(jnp.int32), axis=1),
                               N_EXPERTS - 1)
    last_row = jnp.sum(jnp.where(block_expert[:, None] == jnp.arange(N_EXPERTS, dtype=jnp.int32),
                                 pad_start + count, 0), axis=1)
    block_valid = jnp.where(block_id < n_used[0], jnp.clip(last_row - block_start, 0, tm), 0)
    out = _expert_ffn(h[row_token], block_expert, block_valid, n_used, wg, wu, wd, layer)
    return _moe_combine(x, mod, meta, out[dest[:, 0]], out[dest[:, 1]], split_out)


def _rope_tables():
    half = ROPE_AXIS // 2
    pos = jnp.arange(DEC_SEQ)
    row = (pos // GRID_W).astype(F32)
    col = (pos % GRID_W).astype(F32)
    inv = ROPE_THETA ** (-jnp.arange(0, ROPE_AXIS, 2, dtype=F32) / ROPE_AXIS)
    ang_r = row[:, None] * inv
    ang_c = col[:, None] * inv
    pad = jnp.zeros((DEC_SEQ, LANES - QK_ROPE), F32)
    cos = jnp.concatenate([jnp.cos(ang_r), jnp.cos(ang_r), jnp.cos(ang_c), jnp.cos(ang_c), pad + 1.0], axis=1)
    sin = jnp.concatenate([-jnp.sin(ang_r), jnp.sin(ang_r), -jnp.sin(ang_c), jnp.sin(ang_c), pad], axis=1)
    prompt = jnp.zeros((N_PROMPT, LANES), F32)
    tile = lambda t: jnp.tile(t, (DEC_BATCH, 1))
    return (jnp.concatenate([prompt + 1.0, tile(cos)], axis=0),
            jnp.concatenate([prompt, tile(sin)], axis=0))


def _rope_blocks(a):
    half = ROPE_AXIS // 2
    partner = jnp.concatenate([a[..., half:2 * half], a[..., 0:half], a[..., 3 * half:4 * half],
                               a[..., 2 * half:3 * half]], axis=-1)
    return jnp.concatenate([_pad_lanes(a, LANES), _pad_lanes(partner, LANES)], axis=-1)


def _pad_lanes(a, width):
    return jnp.pad(a, [(0, 0)] * (a.ndim - 1) + [(0, width - a.shape[-1])])


def _ssd_constants():
    r = jnp.arange(CHUNK)
    tri = (r[:, None] >= r[None, :]).astype(BF16)
    head_of = jnp.arange(D_SSD) // SSD_HEADDIM
    e_f = (r[:, None] == head_of[None, :]).astype(BF16)
    e_b = (r[:, None] == head_of[None, :] + SSD_HEADS).astype(BF16)
    return tri, e_f, e_b


def kernel(x_prompt, x_sample, c, state_ssm, cache_ckv, cache_kpe, c_ctx, w_mod, b_mod, w_in, conv_w, conv_b, dt_bias, a_log, d_skip, ssd_norm, sgu_norm, w_sp, b_sp, w_out, ffn_w_gate, ffn_w_up, ffn_w_down, w_dq, q_a_norm, w_uq, w_dkv, kv_a_norm, w_ukv, q_norm, k_norm, w_o, router, moe_w_gate, moe_w_up, moe_w_down):
    x = jnp.concatenate([x_prompt.reshape(N_PROMPT, D_MODEL), x_sample.reshape(N_SAMPLE, D_MODEL)], axis=0)
    cond = jnp.concatenate([c_ctx[None, :], c, jnp.zeros((SEG_PAD - N_SEG, D_MODEL), F32)], axis=0)
    mods = _modulation_tables(cond, w_mod, b_mod).reshape(DEPTH, SEG_PAD, N_MOD, D_MODEL)
    rope_tabs = _rope_tables()
    tri, e_f, e_b = _ssd_constants()
    blk = jnp.arange(2 * LANES) // LANES
    pair_ones = (blk[:, None] == blk[None, :]).astype(BF16)
    i1 = D_SSD
    i2 = i1 + CONV_CH
    i3 = i2 + 2 * SSD_HEADS
    i4 = i3 + D_SGU
    new_ssm, new_ckv, new_kpe = [], [], []
    for i in range(DEPTH):
        j = i // 2
        mod = mods[i]
        if i % 2 == 0:
            w = w_in[j]
            wx = _col_blocks(w[:, i1:i2].astype(BF16), XBC_BLOCK)
            wgate = _col_blocks(jnp.concatenate([w[:, 0:i1], w[:, i3:]], axis=1).astype(BF16), GATE_BLOCK)
            wdt = _pad_lanes(w[:, i2:i3], LANES).astype(BF16)
            a_log_row = _pad_lanes(a_log[j].reshape(1, -1), LANES)
            xbc, gates, dt, cs = _in_projection(
                x, mod, (wx, wgate, wdt),
                (_col_blocks(conv_w[j], XBC_BLOCK), _col_blocks(conv_b[j][None, :], XBC_BLOCK),
                 _pad_lanes(dt_bias[j].reshape(1, -1), LANES), a_log_row, tri))
            consts = (a_log_row, jnp.repeat(d_skip[j], SSD_HEADDIM)[None, :], e_f, e_b)
            y_ctx, st_p = _ssd_scan(xbc, dt, cs, None, consts, BATCH, SEQ, 0)
            init = state_ssm[:, j].reshape(DEC_BATCH, 2, D_SSD, SSD_STATE)
            y_lat, _ = _ssd_scan(xbc, dt, cs, init, consts, DEC_BATCH, DEC_SEQ, N_PROMPT // DEC_SEQ)
            new_ssm.append(st_p.reshape(BATCH, 2, SSD_HEADS, SSD_HEADDIM, SSD_STATE))
            b_sp_e = jnp.repeat(b_sp[j].T, SGU_GDIM, axis=1)
            x = _mixer_output(x, mod, y_ctx, y_lat, gates, ssd_norm[j][None, :], sgu_norm[j][None, :],
                              w_sp[j].astype(BF16), b_sp_e, w_out[j].astype(BF16))
            x = _dense_ffn(x, mod, ffn_w_gate[j].astype(BF16), ffn_w_up[j].astype(BF16),
                           ffn_w_down[j].astype(BF16))
        else:
            split = lambda a: jnp.concatenate([a[..., :QK_NOPE], _rope_blocks(a[..., QK_NOPE:])], axis=-1)
            wuq = split(w_uq[j].reshape(Q_RANK, MLA_HEADS, QK_DIM)).reshape(Q_RANK, MLA_HEADS * HEAD_PROJ)
            wdkv = jnp.concatenate([w_dkv[j][:, :KV_RANK], _rope_blocks(w_dkv[j][:, KV_RANK:])], axis=-1)
            wukv = w_ukv[j].reshape(KV_RANK, MLA_HEADS, QK_NOPE + V_HEAD)
            wuk = wukv[:, :, :QK_NOPE].reshape(KV_RANK, -1).astype(BF16)
            wuv = wukv[:, :, QK_NOPE:].reshape(KV_RANK, -1).astype(BF16)
            qn = split(q_norm[j][None, :])
            kn = split(k_norm[j][None, :])
            q, k, v, ckv, kpe = _mla_projection(
                x, mod, rope_tabs, w_dq[j].astype(BF16), wdkv.astype(BF16), q_a_norm[j][None, :],
                kv_a_norm[j][None, :], wuq.astype(BF16), wuk, wuv, qn, kn, pair_ones)
            new_ckv.append(ckv[:N_PROMPT].reshape(BATCH, SEQ, KV_RANK))
            new_kpe.append(kpe[:N_PROMPT, :QK_ROPE].reshape(BATCH, SEQ, QK_ROPE))
            kc, vc = _cache_keys(cache_ckv[:, j].reshape(DEC_BATCH * PAST_LEN, KV_RANK),
                                 _pad_lanes(cache_kpe[:, j].reshape(DEC_BATCH * PAST_LEN, QK_ROPE), LANES),
                                 wuk, wuv, kn, pair_ones)
            o_ctx = _attention(q, k, v, None, None, BATCH, SEQ, 0)
            o_lat = _attention(q, k, v, kc, vc, DEC_BATCH, DEC_SEQ, N_PROMPT // DEC_SEQ)
            x, h, meta, counts = _attn_out_router(x, mod, o_ctx, o_lat, w_o[j].astype(BF16),
                                                  _pad_lanes(router[j], LANES))
            x = _moe(x, mod, h, meta, counts, moe_w_gate, moe_w_up, moe_w_down, j, split_out=(i == DEPTH - 1))
    assert DEPTH % 2 == 0
    x_ctx, x_lat = x
    return (x_ctx.reshape(BATCH, SEQ, D_MODEL),
            x_lat.reshape(DEC_BATCH, DEC_SEQ, D_MODEL),
            jnp.stack(new_ssm, axis=1),
            jnp.stack(new_ckv, axis=1),
            jnp.stack(new_kpe, axis=1))
```

```python
import functools
import math

import jax
import jax.numpy as jnp
from jax import lax
from jax.experimental import pallas as pl
from jax.experimental.pallas import tpu as pltpu

F32 = jnp.float32
BF16 = jnp.bfloat16

D_MODEL = 1024
BATCH = 16
SEQ = 256
DEPTH = 4
DEC_BATCH = 4
DEC_SEQ = 2048
PAST_LEN = 512
GRID_W = 64
N_MOD = 6
EPS = 1e-6

SSD_HEADDIM = 64
SSD_HEADS = 16
D_SSD = 1024
SSD_GROUPS = 4
SSD_STATE = 128
CHUNK = 128
CONV_W = 5
CONV_CH = D_SSD + 2 * SSD_GROUPS * SSD_STATE
D_SGU = 1024
SGU_GROUPS = 4
SGU_GDIM = D_SGU // SGU_GROUPS

MLA_HEADS = 8
Q_RANK = 384
KV_RANK = 256
QK_NOPE = 128
QK_ROPE = 64
V_HEAD = 128
QK_DIM = QK_NOPE + QK_ROPE
ROPE_AXIS = QK_ROPE // 2
ROPE_THETA = 10000.0
HEAD_PAD = 256

D_FF = 2816
N_EXPERTS = 8
TOP_K = 2

N_PROMPT = BATCH * SEQ
N_SAMPLE = DEC_BATCH * DEC_SEQ
N_TOK = N_PROMPT + N_SAMPLE
N_SEG = 1 + DEC_BATCH
SEG_PAD = 8

LANES = 128
VMEM_LIMIT = 56 * 1024 * 1024

TM_ROWS = 256
TM_FFN = 512
FF_CHUNK = 256
TM_MOE = 1024
TQ = 256


def _cparams(*sem):
    return pltpu.CompilerParams(dimension_semantics=sem, vmem_limit_bytes=VMEM_LIMIT)


def _dot(a, b):
    return jnp.dot(a, b, preferred_element_type=F32)


def _dot_nt(a, b):
    return lax.dot_general(a, b, (((1,), (1,)), ((), ())), preferred_element_type=F32)


def _split(x):
    hi = x.astype(BF16)
    lo = (x - hi.astype(F32)).astype(BF16)
    return hi, lo


def _dot3(a, b):
    ah, al = _split(a)
    bh, bl = _split(b)
    return _dot(ah, bh) + _dot(ah, bl) + _dot(al, bh)


def _silu(x):
    return x / (1.0 + jnp.exp(-x))


def _rms(x):
    return x * lax.rsqrt(jnp.mean(x * x, axis=-1, keepdims=True) + EPS)


def _modulated(x_ref, mod_ref, first):
    shift = mod_ref[0, first:first + 1, :]
    scale = mod_ref[0, first + 1:first + 2, :]
    return _rms(x_ref[...]) * (1.0 + scale) + shift


def _seg_map(tm):
    def index_map(i, *_):
        r = i * tm
        return (jnp.where(r < N_PROMPT, 0, 1 + (r - N_PROMPT) // DEC_SEQ), 0, 0)
    return index_map


def _row_spec(tm, width):
    return pl.BlockSpec((tm, width), lambda i, *_: (i, 0))


def _const_spec(shape):
    zeros = (0,) * len(shape)
    return pl.BlockSpec(shape, lambda i, *_: zeros)


def _mod_spec(tm):
    return pl.BlockSpec((1, N_MOD, D_MODEL), _seg_map(tm))


def _mod_kernel(c_ref, w_ref, b_ref, o_ref):
    o_ref[0] = _dot3(_silu(c_ref[...]), w_ref[0]) + b_ref[0]


def _modulation_tables(cond, w_mod, b_mod):
    tn = 1536
    return pl.pallas_call(
        _mod_kernel,
        grid=(DEPTH, N_MOD * D_MODEL // tn),
        in_specs=[pl.BlockSpec((SEG_PAD, D_MODEL), lambda l, j: (0, 0)),
                  pl.BlockSpec((1, D_MODEL, tn), lambda l, j: (l, 0, j)),
                  pl.BlockSpec((1, 1, tn), lambda l, j: (l, 0, j))],
        out_specs=pl.BlockSpec((1, SEG_PAD, tn), lambda l, j: (l, 0, j)),
        out_shape=jax.ShapeDtypeStruct((DEPTH, SEG_PAD, N_MOD * D_MODEL), F32),
        compiler_params=_cparams("parallel", "parallel"),
        name="modulation",
    )(cond, w_mod, b_mod.reshape(DEPTH, 1, N_MOD * D_MODEL))


HALO = 8


XBC_BLOCK = 512
N_XBC = CONV_CH // XBC_BLOCK
GATE_BLOCK = 256
N_GATE = (D_SSD + 2 * D_SGU) // GATE_BLOCK
GATE_PER_STAGE = N_GATE // N_XBC


def _inproj_kernel(x_ref, xp_ref, xn_ref, mod_ref, wx_ref, wg_ref, wdt, convw_ref, convb_ref, dtb_ref, alog_ref,
                   tri_ref, xbc_ref, gate_ref, dt_ref, cs_ref, h_ref, xa_ref, xb_ref):
    tm = x_ref.shape[0]
    r0 = pl.program_id(0) * tm
    shift = mod_ref[0, 0:1, :]
    scale = mod_ref[0, 1:2, :]
    x_ext = jnp.concatenate([x_ref[...], xp_ref[...], xn_ref[...]], axis=0)
    h_ref[...] = (_rms(x_ext) * (1.0 + scale) + shift).astype(BF16)

    rel = r0 - N_PROMPT
    in_latent = r0 >= N_PROMPT
    keep_prev = jnp.where(jnp.logical_and(in_latent, lax.rem(rel, DEC_SEQ) != 0), 1.0, 0.0)
    keep_next = jnp.where(jnp.logical_and(in_latent, lax.rem(rel + tm, DEC_SEQ) != 0), 1.0, 0.0)
    reach = CONV_W // 2

    bufs = (xa_ref, xb_ref)
    bufs[0][...] = _dot(h_ref[...], wx_ref[0])

    def stage(s, cur_ref, nxt_ref):
        if s + 1 < N_XBC:
            nxt_ref[...] = _dot(h_ref[...], wx_ref[s + 1])
        for t in range(GATE_PER_STAGE):
            g = s * GATE_PER_STAGE + t
            gate_ref[g] = _dot(h_ref[0:tm, :], wg_ref[g]).astype(BF16)
        for half in range(XBC_BLOCK // 256):
            cols = slice(half * 256, (half + 1) * 256)
            win = jnp.concatenate([cur_ref[tm:tm + HALO, cols] * keep_prev, cur_ref[0:tm, cols],
                                   cur_ref[tm + HALO:tm + 2 * HALO, cols] * keep_next], axis=0)
            acc = jnp.zeros((tm, 256), F32) + convb_ref[s, :, cols]
            for k in range(CONV_W):
                start = HALO - reach + k
                acc = acc + win[start:start + tm, :] * convw_ref[s, k:k + 1, cols]
            xbc_ref[s, :, cols] = _silu(acc).astype(BF16)

    for s in range(N_XBC):
        stage(s, bufs[s % 2], bufs[1 - s % 2])

    h = h_ref[0:tm, :]
    lane = lax.broadcasted_iota(jnp.int32, (1, LANES), 1)
    raw = _dot(h, wdt[...]) + dtb_ref[...]
    dt = jnp.where(lane < 2 * SSD_HEADS, jnp.maximum(raw, 0.0) + jnp.log(1.0 + jnp.exp(-jnp.abs(raw))), 0.0)
    dt_ref[...] = dt
    ac = dt * -jnp.exp(alog_ref[...])
    hi = ac.astype(BF16)
    rest = ac - hi.astype(F32)
    mid = rest.astype(BF16)
    lo = (rest - mid.astype(F32)).astype(BF16)
    tri = tri_ref[...]
    for k in range(tm // CHUNK):
        rows = slice(k * CHUNK, (k + 1) * CHUNK)
        cs_ref[rows, :] = _dot(tri, hi[rows, :]) + _dot(tri, mid[rows, :]) + _dot(tri, lo[rows, :])


def _col_blocks(w, block):
    return w.reshape(w.shape[0], -1, block).transpose(1, 0, 2)


def _in_projection(x, mod, weights, consts):
    tm = TM_ROWS
    per_tile = tm // HALO
    last = N_TOK // HALO - 1
    halo_prev = pl.BlockSpec((HALO, D_MODEL), lambda i: (jnp.maximum(i * per_tile - 1, 0), 0))
    halo_next = pl.BlockSpec((HALO, D_MODEL), lambda i: (jnp.minimum((i + 1) * per_tile, last), 0))
    blocked = lambda n, w: pl.BlockSpec((n, tm, w), lambda i: (0, i, 0))
    return pl.pallas_call(
        _inproj_kernel,
        grid=(N_TOK // tm,),
        in_specs=[_row_spec(tm, D_MODEL), halo_prev, halo_next, _mod_spec(tm)]
                 + [_const_spec(a.shape) for a in weights + consts],
        out_specs=[blocked(N_XBC, XBC_BLOCK), blocked(N_GATE, GATE_BLOCK), _row_spec(tm, LANES),
                   _row_spec(tm, LANES)],
        out_shape=[jax.ShapeDtypeStruct((N_XBC, N_TOK, XBC_BLOCK), BF16),
                   jax.ShapeDtypeStruct((N_GATE, N_TOK, GATE_BLOCK), BF16),
                   jax.ShapeDtypeStruct((N_TOK, LANES), F32), jax.ShapeDtypeStruct((N_TOK, LANES), F32)],
        scratch_shapes=[pltpu.VMEM((tm + 2 * HALO, D_MODEL), BF16),
                        pltpu.VMEM((tm + 2 * HALO, XBC_BLOCK), F32),
                        pltpu.VMEM((tm + 2 * HALO, XBC_BLOCK), F32)],
        compiler_params=_cparams("parallel"),
        name="in_projection",
    )(x, x, x, mod, *weights, *consts)


def _ssd_kernel(has_init, n_chunks, *refs):
    refs = list(refs)
    cv_ref, dtv_ref, cs_ref = refs[0:3]
    del refs[0:3]
    init_ref = refs.pop(0) if has_init else None
    alog_ref, dskip_ref, ef_ref, eb_ref, y_ref, fin_ref, st_ref = refs
    a_neg = -jnp.exp(alog_ref[...])
    rows_i = lax.broadcasted_iota(jnp.int32, (CHUNK, CHUNK), 0)
    cols_i = lax.broadcasted_iota(jnp.int32, (CHUNK, CHUNK), 1)
    lower = rows_i >= cols_i
    upper = cols_i >= rows_i
    even_head = jnp.bitwise_and(lax.broadcasted_iota(jnp.int32, (CHUNK, D_SSD), 1), LANES - 1) < SSD_HEADDIM

    assert SSD_GROUPS * SSD_STATE == XBC_BLOCK
    b_block = D_SSD // XBC_BLOCK

    def row0(c):
        return pl.multiple_of(c * CHUNK, CHUNK)

    def chunk_pass(c, forward):
        r0 = row0(c)
        rows = pl.ds(r0, CHUNK)
        e_ref = ef_ref if forward else eb_ref
        off = 0 if forward else SSD_HEADS
        dt = dtv_ref[rows, :]
        cs = cs_ref[rows, :]
        total = cs_ref[pl.ds(r0 + CHUNK - 1, 1), :]
        if forward:
            pos = cs
            to_edge = jnp.exp(total - cs)
            from_edge = jnp.exp(cs)
            mask = lower
        else:
            pos = cs - dt * a_neg
            to_edge = jnp.exp(pos)
            from_edge = jnp.exp(total - pos)
            mask = upper
        pos_t = pos.T
        x = jnp.concatenate([cv_ref[b, rows, :] for b in range(D_SSD // XBC_BLOCK)], axis=1).astype(F32)
        stacked = jnp.concatenate([dt, dt * to_edge, from_edge], axis=0).astype(BF16)
        spread = _dot(stacked, e_ref[...])
        xc = x * spread[0:CHUNK]
        xc_sub = (jnp.where(even_head, xc, 0.0).astype(BF16), jnp.where(even_head, 0.0, xc).astype(BF16))
        xd_b = (x * spread[CHUNK:2 * CHUNK]).astype(BF16)
        from_e = spread[2 * CHUNK:3 * CHUNK]
        tot_hi, tot_lo = _split(jnp.exp(jnp.broadcast_to(total, (8, LANES))))
        tot_e = (_dot(tot_hi, e_ref[...]) + _dot(tot_lo, e_ref[...]))[0:1, :]
        st = st_ref[0 if forward else 1]
        st_b = st.astype(BF16)
        y_parts = []
        st_parts = []
        for g in range(SSD_GROUPS):
            bm = cv_ref[b_block, rows, g * SSD_STATE:(g + 1) * SSD_STATE]
            cm = cv_ref[b_block + 1, rows, g * SSD_STATE:(g + 1) * SSD_STATE]
            bm_t = bm.astype(F32).T.astype(BF16)
            cb = _dot(cm, bm_t)
            gcols = slice(g * 256, (g + 1) * 256)
            y_off = _dot(cm, st_b[:, gcols])
            st_parts.append(_dot(bm_t, xd_b[:, gcols]))
            diag = []
            for pair in range(2):
                pcols = slice(g * 256 + pair * LANES, g * 256 + (pair + 1) * LANES)
                decayed = []
                for sub in range(2):
                    h = g * 4 + pair * 2 + sub
                    col = pos[:, off + h:off + h + 1]
                    row = pos_t[off + h:off + h + 1, :]
                    diff = (col - row) if forward else (row - col)
                    decayed.append((jnp.where(mask, jnp.exp(diff), 0.0) * cb).astype(BF16))
                diag.append(_dot(jnp.concatenate(decayed, axis=1),
                                 jnp.concatenate([xc_sub[0][:, pcols], xc_sub[1][:, pcols]], axis=0)))
            y_parts.append(jnp.concatenate(diag, axis=1) + y_off * from_e[:, gcols])
        y_new = jnp.concatenate(y_parts, axis=1)
        st_ref[0 if forward else 1] = st * tot_e + jnp.concatenate(st_parts, axis=1)
        if forward:
            y_new = y_new + x * dskip_ref[...]
        return rows, y_new

    if has_init:
        st_ref[0] = init_ref[0, 0].T
        st_ref[1] = init_ref[0, 1].T
    else:
        st_ref[...] = jnp.zeros_like(st_ref)

    def first_half(i, carry):
        for rows, y_new in (chunk_pass(i, True), chunk_pass(n_chunks - 1 - i, False)):
            y_ref[rows, :] = y_new
        return carry

    def second_half(i, carry):
        for rows, y_new in (chunk_pass(i, True), chunk_pass(n_chunks - 1 - i, False)):
            y_ref[rows, :] = y_ref[rows, :] + y_new
        return carry

    lax.fori_loop(0, n_chunks // 2, first_half, 0)
    lax.fori_loop(n_chunks // 2, n_chunks, second_half, 0)
    fin_ref[0, 0] = st_ref[0].T
    fin_ref[0, 1] = st_ref[1].T


def _ssd_scan(xbc, dt, cs, init, consts, n_seq, seq, row_block0):
    n_chunks = seq // CHUNK
    assert n_chunks % 2 == 0
    has_init = init is not None
    seq_spec = lambda w: pl.BlockSpec((seq, w), lambda b: (row_block0 + b, 0))
    xbc_spec = pl.BlockSpec((N_XBC, seq, XBC_BLOCK), lambda b: (0, row_block0 + b, 0))
    in_specs = [xbc_spec, seq_spec(LANES), seq_spec(LANES)]
    args = [xbc, dt, cs]
    if has_init:
        in_specs.append(pl.BlockSpec((1, 2, D_SSD, SSD_STATE), lambda b: (b, 0, 0, 0)))
        args.append(init)
    for cst in consts:
        in_specs.append(pl.BlockSpec(cst.shape, lambda b, nd=cst.ndim: (0,) * nd))
        args.append(cst)
    return pl.pallas_call(
        functools.partial(_ssd_kernel, has_init, n_chunks),
        grid=(n_seq,),
        in_specs=in_specs,
        out_specs=[pl.BlockSpec((seq, D_SSD), lambda b: (b, 0)),
                   pl.BlockSpec((1, 2, D_SSD, SSD_STATE), lambda b: (b, 0, 0, 0))],
        out_shape=[jax.ShapeDtypeStruct((n_seq * seq, D_SSD), F32),
                   jax.ShapeDtypeStruct((n_seq, 2, D_SSD, SSD_STATE), F32)],
        scratch_shapes=[pltpu.VMEM((2, SSD_STATE, D_SSD), F32)],
        compiler_params=_cparams("parallel"),
        name="ssd_scan_%d" % seq,
    )(*args)


def _mixer_out_kernel(x_ref, mod_ref, yc_ref, yl_ref, z_ref, u_ref, v_ref, ssdn_ref, sgun_ref, wsp_ref, bsp_ref,
                      wo_ref, o_ref):
    tm = x_ref.shape[0]
    wide = lambda ref: jnp.concatenate([ref[b] for b in range(ref.shape[0])], axis=1).astype(F32)
    gated = _select_rows(yc_ref, yl_ref) * _silu(wide(z_ref))
    a = (_rms(gated) * ssdn_ref[...]).astype(BF16)
    vb = (_rms(wide(v_ref)) * sgun_ref[...]).astype(BF16)
    chunks = []
    for k in range(tm // CHUNK):
        rows = slice(k * CHUNK, (k + 1) * CHUNK)
        groups = [_dot(wsp_ref[g], vb[rows, g * SGU_GDIM:(g + 1) * SGU_GDIM]) for g in range(SGU_GROUPS)]
        chunks.append(jnp.concatenate(groups, axis=1) + bsp_ref[...])
    s = (wide(u_ref) * jnp.concatenate(chunks, axis=0)).astype(BF16)
    out = _dot(a, wo_ref[0:D_SSD, :]) + _dot(s, wo_ref[D_SSD:D_SSD + D_SGU, :])
    o_ref[...] = x_ref[...] + mod_ref[0, 2:3, :] * out


def _mixer_output(x, mod, y_ctx, y_lat, gates, ssd_norm, sgu_norm, w_sp, b_sp_e, w_out):
    tm = TM_FFN
    per = D_SSD // GATE_BLOCK
    part = lambda k: pl.BlockSpec((per, tm, GATE_BLOCK), lambda i: (k, i, 0))
    return pl.pallas_call(
        _mixer_out_kernel,
        grid=(N_TOK // tm,),
        in_specs=[_row_spec(tm, D_MODEL), _mod_spec(tm)] + _split_row_specs(tm, D_SSD) + [part(0), part(1), part(2),
                  _const_spec((1, D_SSD)), _const_spec((1, D_SGU)),
                  _const_spec((SGU_GROUPS, CHUNK, CHUNK)), _const_spec((CHUNK, D_SGU)),
                  _const_spec((D_SSD + D_SGU, D_MODEL))],
        out_specs=_row_spec(tm, D_MODEL),
        out_shape=jax.ShapeDtypeStruct((N_TOK, D_MODEL), F32),
        compiler_params=_cparams("parallel"),
        name="mixer_output",
    )(x, mod, y_ctx, y_lat, gates, gates, gates, ssd_norm, sgu_norm, w_sp, b_sp_e, w_out)


def _ffn_kernel(x_ref, mod_ref, wg_ref, wu_ref, wd_ref, o_ref):
    h = _modulated(x_ref, mod_ref, 3).astype(BF16)
    acc = jnp.zeros(o_ref.shape, F32)
    for f in range(D_FF // FF_CHUNK):
        cols = slice(f * FF_CHUNK, (f + 1) * FF_CHUNK)
        act = (_silu(_dot(h, wg_ref[:, cols])) * _dot(h, wu_ref[:, cols])).astype(BF16)
        acc = acc + _dot(act, wd_ref[cols, :])
    o_ref[...] = x_ref[...] + mod_ref[0, 5:6, :] * acc


def _dense_ffn(x, mod, wg, wu, wd):
    tm = TM_FFN
    once = pl.Buffered(1)
    return pl.pallas_call(
        _ffn_kernel,
        grid=(N_TOK // tm,),
        in_specs=[_row_spec(tm, D_MODEL), _mod_spec(tm),
                  pl.BlockSpec((D_MODEL, D_FF), lambda i: (0, 0), pipeline_mode=once),
                  pl.BlockSpec((D_MODEL, D_FF), lambda i: (0, 0), pipeline_mode=once),
                  pl.BlockSpec((D_FF, D_MODEL), lambda i: (0, 0), pipeline_mode=once)],
        out_specs=_row_spec(tm, D_MODEL),
        out_shape=jax.ShapeDtypeStruct((N_TOK, D_MODEL), F32),
        compiler_params=_cparams("parallel"),
        name="dense_ffn",
    )(x, mod, wg, wu, wd)


HEAD_PROJ = 3 * LANES


def _pair_sums(sq_a, sq_b, ones_ref):
    return _dot(jnp.concatenate([sq_a, sq_b], axis=1).astype(BF16), ones_ref[...])


def _expand_keys(ckv_b, pe_sq, pe_rot, wuk_ref, wuv_ref, kn_ref, ones_ref, k_ref, v_ref):
    kn_nope = kn_ref[:, 0:QK_NOPE]
    v_ref[...] = _dot(ckv_b, wuv_ref[...]).astype(BF16)
    k_nope = _dot(ckv_b, wuk_ref[...])
    for pair in range(MLA_HEADS // 2):
        kh = [k_nope[:, h * QK_NOPE:(h + 1) * QK_NOPE] for h in (2 * pair, 2 * pair + 1)]
        ss = _pair_sums(kh[0] * kh[0] + pe_sq, kh[1] * kh[1] + pe_sq, ones_ref)
        r = lax.rsqrt(ss * (1.0 / QK_DIM) + EPS)
        for idx in range(2):
            h = 2 * pair + idx
            rh = r[:, idx * LANES:(idx + 1) * LANES]
            k_ref[:, h * HEAD_PAD:h * HEAD_PAD + QK_NOPE] = (kh[idx] * rh * kn_nope).astype(BF16)
            k_ref[:, h * HEAD_PAD + QK_NOPE:(h + 1) * HEAD_PAD] = (pe_rot * rh).astype(BF16)


def _mla_proj_kernel(x_ref, mod_ref, cos_ref, sin_ref, wdq_ref, wdkv_ref, qan_ref, kvan_ref,
                     wuq_ref, wuk_ref, wuv_ref, qn_ref, kn_ref, ones_ref, q_ref, k_ref, v_ref, ckv_ref, kpe_ref):
    h = _modulated(x_ref, mod_ref, 0).astype(BF16)
    cos, sin = cos_ref[...], sin_ref[...]
    qa = (_rms(_dot(h, wdq_ref[...])) * qan_ref[...]).astype(BF16)
    q = _dot(qa, wuq_ref[...])
    qn_nope = qn_ref[:, 0:LANES]
    qn_rope = qn_ref[:, LANES:2 * LANES] * cos
    qn_part = qn_ref[:, 2 * LANES:3 * LANES] * sin
    scale = QK_DIM ** -0.5 * math.log2(math.e)
    for pair in range(MLA_HEADS // 2):
        blocks = []
        for hd in (2 * pair, 2 * pair + 1):
            base = hd * HEAD_PROJ
            blocks.append((q[:, base:base + LANES], q[:, base + LANES:base + 2 * LANES],
                           q[:, base + 2 * LANES:base + 3 * LANES]))
        ss = _pair_sums(*[qh * qh + qr * qr for qh, qr, _ in blocks], ones_ref)
        r = lax.rsqrt(ss * (1.0 / QK_DIM) + EPS) * scale
        for idx, (qh, qr, qp) in enumerate(blocks):
            hd = 2 * pair + idx
            rh = r[:, idx * LANES:(idx + 1) * LANES]
            q_ref[:, hd * HEAD_PAD:hd * HEAD_PAD + QK_NOPE] = (qh * rh * qn_nope).astype(BF16)
            q_ref[:, hd * HEAD_PAD + QK_NOPE:(hd + 1) * HEAD_PAD] = ((qr * qn_rope + qp * qn_part) * rh).astype(BF16)
    kva = _dot(h, wdkv_ref[...])
    ckv = _rms(kva[:, 0:KV_RANK]) * kvan_ref[...]
    kpe = kva[:, KV_RANK:KV_RANK + LANES]
    kpe_part = kva[:, KV_RANK + LANES:KV_RANK + 2 * LANES]
    ckv_ref[...] = ckv
    kpe_ref[...] = kpe
    pe_rot = kpe * (kn_ref[:, LANES:2 * LANES] * cos) + kpe_part * (kn_ref[:, 2 * LANES:3 * LANES] * sin)
    _expand_keys(ckv.astype(BF16), kpe * kpe, pe_rot, wuk_ref, wuv_ref, kn_ref, ones_ref, k_ref, v_ref)


def _mla_projection(x, mod, rope_tabs, wdq, wdkv, qan, kvan, wuq, wuk, wuv, qn, kn, ones):
    tm = TM_FFN
    hw = MLA_HEADS * HEAD_PAD
    out_w = (hw, hw, MLA_HEADS * V_HEAD, KV_RANK, LANES)
    out_dt = (BF16, BF16, BF16, F32, F32)
    consts = (wdq, wdkv, qan, kvan, wuq, wuk, wuv, qn, kn, ones)
    return pl.pallas_call(
        _mla_proj_kernel,
        grid=(N_TOK // tm,),
        in_specs=[_row_spec(tm, D_MODEL), _mod_spec(tm)] + [_row_spec(tm, LANES)] * 2
                 + [_const_spec(cst.shape) for cst in consts],
        out_specs=[_row_spec(tm, w) for w in out_w],
        out_shape=[jax.ShapeDtypeStruct((N_TOK, w), dt) for w, dt in zip(out_w, out_dt)],
        compiler_params=_cparams("parallel"),
        name="mla_projection",
    )(x, mod, *rope_tabs, *consts)


def _cache_kv_kernel(ckv_ref, kpe_ref, wuk_ref, wuv_ref, kn_ref, ones_ref, k_ref, v_ref):
    kpe = kpe_ref[...]
    _expand_keys(ckv_ref[...].astype(BF16), kpe * kpe, kpe * kn_ref[:, LANES:2 * LANES],
                 wuk_ref, wuv_ref, kn_ref, ones_ref, k_ref, v_ref)


def _cache_keys(ckv, kpe, wuk, wuv, kn, ones):
    rows = ckv.shape[0]
    tm = TM_ROWS
    consts = (wuk, wuv, kn, ones)
    return pl.pallas_call(
        _cache_kv_kernel,
        grid=(rows // tm,),
        in_specs=[_row_spec(tm, KV_RANK), _row_spec(tm, LANES)] + [_const_spec(cst.shape) for cst in consts],
        out_specs=[_row_spec(tm, MLA_HEADS * HEAD_PAD), _row_spec(tm, MLA_HEADS * V_HEAD)],
        out_shape=[jax.ShapeDtypeStruct((rows, MLA_HEADS * HEAD_PAD), BF16),
                   jax.ShapeDtypeStruct((rows, MLA_HEADS * V_HEAD), BF16)],
        compiler_params=_cparams("parallel"),
        name="cache_keys",
    )(ckv, kpe, *consts)


def _attn_kernel(has_cache, tq, n_par, heads, *refs):
    if has_cache:
        q_ref, k_ref, v_ref, kc_ref, vc_ref, o_ref = refs
    else:
        q_ref, k_ref, v_ref, o_ref = refs

    def tile(r0, hd):
        qk = slice(hd * HEAD_PAD, (hd + 1) * HEAD_PAD)
        vo = slice(hd * V_HEAD, (hd + 1) * V_HEAD)
        q = q_ref[pl.ds(r0, tq), qk]
        s = _dot_nt(q, k_ref[:, qk])
        m = jnp.max(s, axis=-1, keepdims=True)
        if has_cache:
            sc = _dot_nt(q, kc_ref[:, qk])
            m = jnp.maximum(m, jnp.max(sc, axis=-1, keepdims=True))
            pc = jnp.exp2(sc - m)
        p = jnp.exp2(s - m)
        den = jnp.sum(p, axis=-1, keepdims=True)
        num = _dot(p.astype(BF16), v_ref[:, vo])
        if has_cache:
            den = den + jnp.sum(pc, axis=-1, keepdims=True)
            num = num + _dot(pc.astype(BF16), vc_ref[:, vo])
        o_ref[pl.ds(r0, tq), vo] = (num / den).astype(BF16)

    group = n_par * tq
    n_groups = q_ref.shape[0] // group
    if n_groups == 1:
        for hd in range(heads):
            for t in range(n_par):
                tile(t * tq, hd)
    else:
        def body(i, carry):
            r0 = pl.multiple_of(i * group, group)
            for hd in range(heads):
                for t in range(n_par):
                    tile(r0 + t * tq, hd)
            return carry
        lax.fori_loop(0, n_groups, body, 0)


def _attention(q, k, v, kc, vc, n_seq, seq, row_block0):
    has_cache = kc is not None
    if seq >= 2 * TQ:
        tq, n_par, heads = TQ, 8, 1
    else:
        tq, n_par, heads = seq, 1, MLA_HEADS
    seq_spec = lambda w: pl.BlockSpec((seq, heads * w), lambda b, h: (row_block0 + b, h))
    in_specs = [seq_spec(HEAD_PAD), seq_spec(HEAD_PAD), seq_spec(V_HEAD)]
    args = [q, k, v]
    if has_cache:
        past = kc.shape[0] // n_seq
        in_specs += [pl.BlockSpec((past, heads * HEAD_PAD), lambda b, h: (b, h)),
                     pl.BlockSpec((past, heads * V_HEAD), lambda b, h: (b, h))]
        args += [kc, vc]
    return pl.pallas_call(
        functools.partial(_attn_kernel, has_cache, tq, n_par, heads),
        grid=(n_seq, MLA_HEADS // heads),
        in_specs=in_specs,
        out_specs=pl.BlockSpec((seq, heads * V_HEAD), lambda b, h: (b, h)),
        out_shape=jax.ShapeDtypeStruct((n_seq * seq, MLA_HEADS * V_HEAD), BF16),
        compiler_params=_cparams("parallel", "parallel"),
        name="attention_%d" % seq,
    )(*args)


def _split_row_specs(tm, width):
    n_ctx = N_PROMPT // tm
    return [pl.BlockSpec((tm, width), lambda i, *_: (jnp.minimum(i, n_ctx - 1), 0)),
            pl.BlockSpec((tm, width), lambda i, *_: (jnp.maximum(i - n_ctx, 0), 0))]


def _select_rows(ctx_ref, lat_ref):
    tm = ctx_ref.shape[0]
    return jnp.where(pl.program_id(0) < N_PROMPT // tm, ctx_ref[...], lat_ref[...])


def _attn_out_router_kernel(x_ref, mod_ref, ac_ref, al_ref, wo_ref, wr_ref, tri_ref,
                            xo_ref, h_ref, meta_ref, cnt_ref, run_ref):
    @pl.when(pl.program_id(0) == 0)
    def _():
        run_ref[...] = jnp.zeros_like(run_ref)

    sub = tri_ref.shape[0]
    attn = _select_rows(ac_ref, al_ref)
    lane = lax.broadcasted_iota(jnp.int32, (sub, LANES), 1)
    lane_f = lane.astype(F32)
    run = run_ref[...]
    for s in range(x_ref.shape[0] // sub):
        rows = slice(s * sub, (s + 1) * sub)
        x_new = x_ref[rows, :] + mod_ref[0, 2:3, :] * _dot(attn[rows, :], wo_ref[...])
        xo_ref[rows, :] = x_new
        h = _rms(x_new) * (1.0 + mod_ref[0, 4:5, :]) + mod_ref[0, 3:4, :]
        h_ref[rows, :] = h.astype(BF16)
        logits = jnp.where(lane < N_EXPERTS, _dot3(h, wr_ref[...]), -jnp.inf)
        m1 = jnp.max(logits, axis=-1, keepdims=True)
        e1 = jnp.min(jnp.where(logits == m1, lane_f, float(LANES)), axis=-1, keepdims=True)
        rest = jnp.where(lane_f == e1, -jnp.inf, logits)
        m2 = jnp.max(rest, axis=-1, keepdims=True)
        e2 = jnp.min(jnp.where(rest == m2, lane_f, float(LANES)), axis=-1, keepdims=True)
        t = jnp.exp(m2 - m1)
        g1 = 1.0 / (1.0 + t)
        pick1 = lane_f == e1
        pick2 = lane_f == e2
        onehot = jnp.where(pick1, 1.0, jnp.where(pick2, 1.0, 0.0))
        before = run + _dot(tri_ref[...], onehot.astype(BF16)) - onehot
        rank1 = jnp.sum(jnp.where(pick1, before, 0.0), axis=-1, keepdims=True)
        rank2 = jnp.sum(jnp.where(pick2, before, 0.0), axis=-1, keepdims=True)
        run = run + jnp.sum(onehot, axis=0, keepdims=True)
        cols = (e1, e2, g1, 1.0 - g1, rank1, rank2)
        meta = jnp.zeros((sub, LANES), F32)
        for idx, val in enumerate(cols):
            meta = jnp.where(lane == idx, val, meta)
        meta_ref[rows, :] = meta
    run_ref[...] = run
    cnt_ref[...] = jnp.broadcast_to(run, cnt_ref.shape)


META_EXPERT, META_GATE, META_RANK = 0, 2, 4


def _attn_out_router(x, mod, a_ctx, a_lat, w_o, wr):
    tm = TM_FFN
    sub = TM_ROWS
    k = a_ctx.shape[1]
    r = jnp.arange(sub)
    tri = (r[:, None] >= r[None, :]).astype(BF16)
    return pl.pallas_call(
        _attn_out_router_kernel,
        grid=(N_TOK // tm,),
        in_specs=[_row_spec(tm, D_MODEL), _mod_spec(tm)] + _split_row_specs(tm, k)
                 + [_const_spec((k, D_MODEL)), _const_spec((D_MODEL, LANES)), _const_spec((sub, sub))],
        out_specs=[_row_spec(tm, D_MODEL), _row_spec(tm, D_MODEL), _row_spec(tm, LANES),
                   _const_spec((SEG_PAD, LANES))],
        out_shape=[jax.ShapeDtypeStruct((N_TOK, D_MODEL), F32), jax.ShapeDtypeStruct((N_TOK, D_MODEL), BF16),
                   jax.ShapeDtypeStruct((N_TOK, LANES), F32), jax.ShapeDtypeStruct((SEG_PAD, LANES), F32)],
        scratch_shapes=[pltpu.VMEM((1, LANES), F32)],
        compiler_params=_cparams("arbitrary"),
        name="attn_out_router",
    )(x, mod, a_ctx, a_lat, w_o, wr, tri)


N_FF_CHUNKS = D_FF // FF_CHUNK
MOE_STEPS = 2
CHUNKS_PER_STEP = -(-N_FF_CHUNKS // MOE_STEPS)
TF_MOE = CHUNKS_PER_STEP * FF_CHUNK


def _experts_kernel(be_ref, bv_ref, nu_ref, rows_ref, wg_ref, wu_ref, wd_ref, o_ref, acc_ref):
    i = pl.program_id(0)
    f = pl.program_id(1)

    def chunk(c, rows):
        cols = slice(c * FF_CHUNK, (c + 1) * FF_CHUNK)
        gate = _dot(rows, wg_ref[0, 0, :, cols].astype(BF16))
        up = _dot(rows, wu_ref[0, 0, :, cols].astype(BF16))
        return _dot((_silu(gate) * up).astype(BF16), wd_ref[0, 0, cols, :].astype(BF16))

    @pl.when(bv_ref[i] > 0)
    def _():
        rows = rows_ref[...]
        acc = chunk(0, rows)
        for c in range(1, CHUNKS_PER_STEP - 1):
            acc = acc + chunk(c, rows)

        @pl.when(f == 0)
        def _():
            acc_ref[...] = acc

        @pl.when(f > 0)
        def _():
            acc_ref[...] += acc

        @pl.when((f * CHUNKS_PER_STEP + CHUNKS_PER_STEP) * FF_CHUNK <= D_FF)
        def _():
            acc_ref[...] += chunk(CHUNKS_PER_STEP - 1, rows_ref[...])

    @pl.when(f == pl.num_programs(1) - 1)
    def _():
        o_ref[...] = acc_ref[...].astype(BF16)


def _expert_ffn(rows, block_expert, block_valid, n_used, wg, wu, wd, layer):
    n_rows = rows.shape[0]
    tm = TM_MOE
    tf = TF_MOE
    n_f = MOE_STEPS

    def f_idx(i, f, nu):
        return jnp.where(i < nu[0], f, n_f - 1)

    grid_spec = pltpu.PrefetchScalarGridSpec(
        num_scalar_prefetch=3,
        grid=(n_rows // tm, n_f),
        in_specs=[pl.BlockSpec((tm, D_MODEL), lambda i, f, be, bv, nu: (i, 0)),
                  pl.BlockSpec((1, 1, D_MODEL, tf), lambda i, f, be, bv, nu: (layer, be[i], 0, f_idx(i, f, nu))),
                  pl.BlockSpec((1, 1, D_MODEL, tf), lambda i, f, be, bv, nu: (layer, be[i], 0, f_idx(i, f, nu))),
                  pl.BlockSpec((1, 1, tf, D_MODEL), lambda i, f, be, bv, nu: (layer, be[i], f_idx(i, f, nu), 0))],
        out_specs=pl.BlockSpec((tm, D_MODEL), lambda i, f, be, bv, nu: (i, 0)),
        scratch_shapes=[pltpu.VMEM((tm, D_MODEL), F32)])
    return pl.pallas_call(
        _experts_kernel,
        grid_spec=grid_spec,
        out_shape=jax.ShapeDtypeStruct((n_rows, D_MODEL), BF16),
        compiler_params=_cparams("arbitrary", "arbitrary"),
        name="expert_ffn",
    )(block_expert, block_valid, n_used, rows, wg, wu, wd)


def _combine_kernel(split, x_ref, mod_ref, meta_ref, a_ref, b_ref, *o_refs):
    g1 = meta_ref[:, META_GATE:META_GATE + 1]
    g2 = meta_ref[:, META_GATE + 1:META_GATE + 2]
    y = a_ref[...].astype(F32) * g1 + b_ref[...].astype(F32) * g2
    res = x_ref[...] + mod_ref[0, 5:6, :] * y
    if not split:
        o_refs[0][...] = res
        return
    is_ctx = pl.program_id(0) < N_PROMPT // x_ref.shape[0]

    @pl.when(is_ctx)
    def _():
        o_refs[0][...] = res

    @pl.when(jnp.logical_not(is_ctx))
    def _():
        o_refs[1][...] = res


def _moe_combine(x, mod, meta, a, b, split):
    tm = TM_FFN
    if split:
        out_specs = _split_row_specs(tm, D_MODEL)
        out_shape = [jax.ShapeDtypeStruct((N_PROMPT, D_MODEL), F32), jax.ShapeDtypeStruct((N_SAMPLE, D_MODEL), F32)]
    else:
        out_specs = _row_spec(tm, D_MODEL)
        out_shape = jax.ShapeDtypeStruct((N_TOK, D_MODEL), F32)
    return pl.pallas_call(
        functools.partial(_combine_kernel, split),
        grid=(N_TOK // tm,),
        in_specs=[_row_spec(tm, D_MODEL), _mod_spec(tm), _row_spec(tm, LANES), _row_spec(tm, D_MODEL),
                  _row_spec(tm, D_MODEL)],
        out_specs=out_specs,
        out_shape=out_shape,
        compiler_params=_cparams("arbitrary"),
        name="moe_combine",
    )(x, mod, meta, a, b)


def _moe(x, mod, h, meta, counts, wg, wu, wd, layer, split_out):
    tm = TM_MOE
    n_assign = N_TOK * TOP_K
    experts = meta[:, META_EXPERT:META_EXPERT + TOP_K].astype(jnp.int32)
    rank = meta[:, META_RANK:META_RANK + TOP_K].astype(jnp.int32)
    count = counts[0, :N_EXPERTS].astype(jnp.int32)
    padded = (count + tm - 1) // tm * tm
    pad_end = jnp.cumsum(padded)
    pad_start = pad_end - padded
    onehot = experts[:, :, None] == jnp.arange(N_EXPERTS, dtype=jnp.int32)
    dest = jnp.sum(jnp.where(onehot, pad_start, 0), axis=-1) + rank
    n_rows = n_assign + N_EXPERTS * tm
    n_blocks = n_rows // tm
    row_token = jnp.zeros((n_rows,), jnp.int32).at[dest.reshape(-1)].set(
        jnp.arange(n_assign, dtype=jnp.int32) // TOP_K)
    n_used = (pad_end[-1] // tm).astype(jnp.int32).reshape(1)
    block_id = jnp.arange(n_blocks, dtype=jnp.int32)
    block_start = jnp.minimum(block_id, n_used[0] - 1) * tm
    block_expert = jnp.minimum(jnp.sum((pad_end[None, :] <= block_start[:, None]).astype(jnp.int32), axis=1),
                               N_EXPERTS - 1)
    last_row = jnp.sum(jnp.where(block_expert[:, None] == jnp.arange(N_EXPERTS, dtype=jnp.int32),
                                 pad_start + count, 0), axis=1)
    block_valid = jnp.where(block_id < n_used[0], jnp.clip(last_row - block_start, 0, tm), 0)
    out = _expert_ffn(h[row_token], block_expert, block_valid, n_used, wg, wu, wd, layer)
    return _moe_combine(x, mod, meta, out[dest[:, 0]], out[dest[:, 1]], split_out)


def _rope_tables():
    half = ROPE_AXIS // 2
    pos = jnp.arange(DEC_SEQ)
    row = (pos // GRID_W).astype(F32)
    col = (pos % GRID_W).astype(F32)
    inv = ROPE_THETA ** (-jnp.arange(0, ROPE_AXIS, 2, dtype=F32) / ROPE_AXIS)
    ang_r = row[:, None] * inv
    ang_c = col[:, None] * inv
    pad = jnp.zeros((DEC_SEQ, LANES - QK_ROPE), F32)
    cos = jnp.concatenate([jnp.cos(ang_r), jnp.cos(ang_r), jnp.cos(ang_c), jnp.cos(ang_c), pad + 1.0], axis=1)
    sin = jnp.concatenate([-jnp.sin(ang_r), jnp.sin(ang_r), -jnp.sin(ang_c), jnp.sin(ang_c), pad], axis=1)
    prompt = jnp.zeros((N_PROMPT, LANES), F32)
    tile = lambda t: jnp.tile(t, (DEC_BATCH, 1))
    return (jnp.concatenate([prompt + 1.0, tile(cos)], axis=0),
            jnp.concatenate([prompt, tile(sin)], axis=0))


def _rope_blocks(a):
    half = ROPE_AXIS // 2
    partner = jnp.concatenate([a[..., half:2 * half], a[..., 0:half], a[..., 3 * half:4 * half],
                               a[..., 2 * half:3 * half]], axis=-1)
    return jnp.concatenate([_pad_lanes(a, LANES), _pad_lanes(partner, LANES)], axis=-1)


def _pad_lanes(a, width):
    return jnp.pad(a, [(0, 0)] * (a.ndim - 1) + [(0, width - a.shape[-1])])


def _ssd_constants():
    r = jnp.arange(CHUNK)
    tri = (r[:, None] >= r[None, :]).astype(BF16)
    head_of = jnp.arange(D_SSD) // SSD_HEADDIM
    e_f = (r[:, None] == head_of[None, :]).astype(BF16)
    e_b = (r[:, None] == head_of[None, :] + SSD_HEADS).astype(BF16)
    return tri, e_f, e_b


def kernel(x_prompt, x_sample, c, state_ssm, cache_ckv, cache_kpe, c_ctx, w_mod, b_mod, w_in, conv_w, conv_b, dt_bias, a_log, d_skip, ssd_norm, sgu_norm, w_sp, b_sp, w_out, ffn_w_gate, ffn_w_up, ffn_w_down, w_dq, q_a_norm, w_uq, w_dkv, kv_a_norm, w_ukv, q_norm, k_norm, w_o, router, moe_w_gate, moe_w_up, moe_w_down):
    x = jnp.concatenate([x_prompt.reshape(N_PROMPT, D_MODEL), x_sample.reshape(N_SAMPLE, D_MODEL)], axis=0)
    cond = jnp.concatenate([c_ctx[None, :], c, jnp.zeros((SEG_PAD - N_SEG, D_MODEL), F32)], axis=0)
    mods = _modulation_tables(cond, w_mod, b_mod).reshape(DEPTH, SEG_PAD, N_MOD, D_MODEL)
    rope_tabs = _rope_tables()
    tri, e_f, e_b = _ssd_constants()
    blk = jnp.arange(2 * LANES) // LANES
    pair_ones = (blk[:, None] == blk[None, :]).astype(BF16)
    i1 = D_SSD
    i2 = i1 + CONV_CH
    i3 = i2 + 2 * SSD_HEADS
    i4 = i3 + D_SGU
    new_ssm, new_ckv, new_kpe = [], [], []
    for i in range(DEPTH):
        j = i // 2
        mod = mods[i]
        if i % 2 == 0:
            w = w_in[j]
            wx = _col_blocks(w[:, i1:i2].astype(BF16), XBC_BLOCK)
            wgate = _col_blocks(jnp.concatenate([w[:, 0:i1], w[:, i3:]], axis=1).astype(BF16), GATE_BLOCK)
            wdt = _pad_lanes(w[:, i2:i3], LANES).astype(BF16)
            a_log_row = _pad_lanes(a_log[j].reshape(1, -1), LANES)
            xbc, gates, dt, cs = _in_projection(
                x, mod, (wx, wgate, wdt),
                (_col_blocks(conv_w[j], XBC_BLOCK), _col_blocks(conv_b[j][None, :], XBC_BLOCK),
                 _pad_lanes(dt_bias[j].reshape(1, -1), LANES), a_log_row, tri))
            consts = (a_log_row, jnp.repeat(d_skip[j], SSD_HEADDIM)[None, :], e_f, e_b)
            y_ctx, st_p = _ssd_scan(xbc, dt, cs, None, consts, BATCH, SEQ, 0)
            init = state_ssm[:, j].reshape(DEC_BATCH, 2, D_SSD, SSD_STATE)
            y_lat, _ = _ssd_scan(xbc, dt, cs, init, consts, DEC_BATCH, DEC_SEQ, N_PROMPT // DEC_SEQ)
            new_ssm.append(st_p.reshape(BATCH, 2, SSD_HEADS, SSD_HEADDIM, SSD_STATE))
            b_sp_e = jnp.repeat(b_sp[j].T, SGU_GDIM, axis=1)
            x = _mixer_output(x, mod, y_ctx, y_lat, gates, ssd_norm[j][None, :], sgu_norm[j][None, :],
                              w_sp[j].astype(BF16), b_sp_e, w_out[j].astype(BF16))
            x = _dense_ffn(x, mod, ffn_w_gate[j].astype(BF16), ffn_w_up[j].astype(BF16),
                           ffn_w_down[j].astype(BF16))
        else:
            split = lambda a: jnp.concatenate([a[..., :QK_NOPE], _rope_blocks(a[..., QK_NOPE:])], axis=-1)
            wuq = split(w_uq[j].reshape(Q_RANK, MLA_HEADS, QK_DIM)).reshape(Q_RANK, MLA_HEADS * HEAD_PROJ)
            wdkv = jnp.concatenate([w_dkv[j][:, :KV_RANK], _rope_blocks(w_dkv[j][:, KV_RANK:])], axis=-1)
            wukv = w_ukv[j].reshape(KV_RANK, MLA_HEADS, QK_NOPE + V_HEAD)
            wuk = wukv[:, :, :QK_NOPE].reshape(KV_RANK, -1).astype(BF16)
            wuv = wukv[:, :, QK_NOPE:].reshape(KV_RANK, -1).astype(BF16)
            qn = split(q_norm[j][None, :])
            kn = split(k_norm[j][None, :])
            q, k, v, ckv, kpe = _mla_projection(
                x, mod, rope_tabs, w_dq[j].astype(BF16), wdkv.astype(BF16), q_a_norm[j][None, :],
                kv_a_norm[j][None, :], wuq.astype(BF16), wuk, wuv, qn, kn, pair_ones)
            new_ckv.append(ckv[:N_PROMPT].reshape(BATCH, SEQ, KV_RANK))
            new_kpe.append(kpe[:N_PROMPT, :QK_ROPE].reshape(BATCH, SEQ, QK_ROPE))
            kc, vc = _cache_keys(cache_ckv[:, j].reshape(DEC_BATCH * PAST_LEN, KV_RANK),
                                 _pad_lanes(cache_kpe[:, j].reshape(DEC_BATCH * PAST_LEN, QK_ROPE), LANES),
                                 wuk, wuv, kn, pair_ones)
            o_ctx = _attention(q, k, v, None, None, BATCH, SEQ, 0)
            o_lat = _attention(q, k, v, kc, vc, DEC_BATCH, DEC_SEQ, N_PROMPT // DEC_SEQ)
            x, h, meta, counts = _attn_out_router(x, mod, o_ctx, o_lat, w_o[j].astype(BF16),
                                                  _pad_lanes(router[j], LANES))
            x = _moe(x, mod, h, meta, counts, moe_w_gate, moe_w_up, moe_w_down, j, split_out=(i == DEPTH - 1))
    assert DEPTH % 2 == 0
    x_ctx, x_lat = x
    return (x_ctx.reshape(BATCH, SEQ, D_MODEL),
            x_lat.reshape(DEC_BATCH, DEC_SEQ, D_MODEL),
            jnp.stack(new_ssm, axis=1),
            jnp.stack(new_ckv, axis=1),
            jnp.stack(new_kpe, axis=1))
```

```python
import functools
import math

import jax
import jax.numpy as jnp
from jax import lax
from jax.experimental import pallas as pl
from jax.experimental.pallas import tpu as pltpu

F32 = jnp.float32
BF16 = jnp.bfloat16

D_MODEL = 1024
BATCH = 16
SEQ = 256
DEPTH = 4
DEC_BATCH = 4
DEC_SEQ = 2048
PAST_LEN = 512
GRID_W = 64
N_MOD = 6
EPS = 1e-6

SSD_HEADDIM = 64
SSD_HEADS = 16
D_SSD = 1024
SSD_GROUPS = 4
SSD_STATE = 128
CHUNK = 128
CONV_W = 5
CONV_CH = D_SSD + 2 * SSD_GROUPS * SSD_STATE
D_SGU = 1024
SGU_GROUPS = 4
SGU_GDIM = D_SGU // SGU_GROUPS

MLA_HEADS = 8
Q_RANK = 384
KV_RANK = 256
QK_NOPE = 128
QK_ROPE = 64
V_HEAD = 128
QK_DIM = QK_NOPE + QK_ROPE
ROPE_AXIS = QK_ROPE // 2
ROPE_THETA = 10000.0
HEAD_PAD = 256

D_FF = 2816
N_EXPERTS = 8
TOP_K = 2

N_PROMPT = BATCH * SEQ
N_SAMPLE = DEC_BATCH * DEC_SEQ
N_TOK = N_PROMPT + N_SAMPLE
N_SEG = 1 + DEC_BATCH
SEG_PAD = 8

LANES = 128
VMEM_LIMIT = 58 * 1024 * 1024

TM_ROWS = 256
TM_FFN = 512
FF_CHUNK = 256
TM_MOE = 1024
TQ = 256


def _cparams(*sem):
    return pltpu.CompilerParams(dimension_semantics=sem, vmem_limit_bytes=VMEM_LIMIT)


def _dot(a, b):
    return jnp.dot(a, b, preferred_element_type=F32)


def _dot_nt(a, b):
    return lax.dot_general(a, b, (((1,), (1,)), ((), ())), preferred_element_type=F32)


def _split(x):
    hi = x.astype(BF16)
    lo = (x - hi.astype(F32)).astype(BF16)
    return hi, lo


def _dot3(a, b):
    ah, al = _split(a)
    bh, bl = _split(b)
    return _dot(ah, bh) + _dot(ah, bl) + _dot(al, bh)


def _silu(x):
    return x / (1.0 + jnp.exp(-x))


def _rms(x):
    return x * lax.rsqrt(jnp.mean(x * x, axis=-1, keepdims=True) + EPS)


def _modulated(x_ref, mod_ref, first):
    shift = mod_ref[0, first:first + 1, :]
    scale = mod_ref[0, first + 1:first + 2, :]
    return _rms(x_ref[...]) * (1.0 + scale) + shift


def _seg_map(tm):
    def index_map(i, *_):
        r = i * tm
        return (jnp.where(r < N_PROMPT, 0, 1 + (r - N_PROMPT) // DEC_SEQ), 0, 0)
    return index_map


def _row_spec(tm, width):
    return pl.BlockSpec((tm, width), lambda i, *_: (i, 0))


def _const_spec(shape):
    zeros = (0,) * len(shape)
    return pl.BlockSpec(shape, lambda i, *_: zeros)


def _mod_spec(tm):
    return pl.BlockSpec((1, N_MOD, D_MODEL), _seg_map(tm))


def _mod_kernel(c_ref, w_ref, b_ref, o_ref):
    o_ref[0] = _dot3(_silu(c_ref[...]), w_ref[0]) + b_ref[0]


def _modulation_tables(cond, w_mod, b_mod):
    tn = 1536
    return pl.pallas_call(
        _mod_kernel,
        grid=(DEPTH, N_MOD * D_MODEL // tn),
        in_specs=[pl.BlockSpec((SEG_PAD, D_MODEL), lambda l, j: (0, 0)),
                  pl.BlockSpec((1, D_MODEL, tn), lambda l, j: (l, 0, j)),
                  pl.BlockSpec((1, 1, tn), lambda l, j: (l, 0, j))],
        out_specs=pl.BlockSpec((1, SEG_PAD, tn), lambda l, j: (l, 0, j)),
        out_shape=jax.ShapeDtypeStruct((DEPTH, SEG_PAD, N_MOD * D_MODEL), F32),
        compiler_params=_cparams("parallel", "parallel"),
        name="modulation",
    )(cond, w_mod, b_mod.reshape(DEPTH, 1, N_MOD * D_MODEL))


HALO = 8


XBC_BLOCK = 512
N_XBC = CONV_CH // XBC_BLOCK
GATE_BLOCK = 256
N_GATE = (D_SSD + 2 * D_SGU) // GATE_BLOCK
GATE_PER_STAGE = N_GATE // N_XBC


def _inproj_kernel(x_ref, xp_ref, xn_ref, mod_ref, wx_ref, wg_ref, wdt, convw_ref, convb_ref, dtb_ref, alog_ref,
                   tri_ref, xbc_ref, gate_ref, dt_ref, cs_ref, h_ref, xa_ref, xb_ref):
    tm = x_ref.shape[0]
    r0 = pl.program_id(0) * tm
    shift = mod_ref[0, 0:1, :]
    scale = mod_ref[0, 1:2, :]
    x_ext = jnp.concatenate([x_ref[...], xp_ref[...], xn_ref[...]], axis=0)
    h_ref[...] = (_rms(x_ext) * (1.0 + scale) + shift).astype(BF16)

    rel = r0 - N_PROMPT
    in_latent = r0 >= N_PROMPT
    keep_prev = jnp.where(jnp.logical_and(in_latent, lax.rem(rel, DEC_SEQ) != 0), 1.0, 0.0)
    keep_next = jnp.where(jnp.logical_and(in_latent, lax.rem(rel + tm, DEC_SEQ) != 0), 1.0, 0.0)
    reach = CONV_W // 2

    bufs = (xa_ref, xb_ref)
    bufs[0][...] = _dot(h_ref[...], wx_ref[0])

    def stage(s, cur_ref, nxt_ref):
        if s + 1 < N_XBC:
            nxt_ref[...] = _dot(h_ref[...], wx_ref[s + 1])
        for t in range(GATE_PER_STAGE):
            g = s * GATE_PER_STAGE + t
            gate_ref[g] = _dot(h_ref[0:tm, :], wg_ref[g]).astype(BF16)
        for half in range(XBC_BLOCK // 256):
            cols = slice(half * 256, (half + 1) * 256)
            win = jnp.concatenate([cur_ref[tm:tm + HALO, cols] * keep_prev, cur_ref[0:tm, cols],
                                   cur_ref[tm + HALO:tm + 2 * HALO, cols] * keep_next], axis=0)
            acc = jnp.zeros((tm, 256), F32) + convb_ref[s, :, cols]
            for k in range(CONV_W):
                start = HALO - reach + k
                acc = acc + win[start:start + tm, :] * convw_ref[s, k:k + 1, cols]
            xbc_ref[s, :, cols] = _silu(acc).astype(BF16)

    for s in range(N_XBC):
        stage(s, bufs[s % 2], bufs[1 - s % 2])

    h = h_ref[0:tm, :]
    lane = lax.broadcasted_iota(jnp.int32, (1, LANES), 1)
    raw = _dot(h, wdt[...]) + dtb_ref[...]
    dt = jnp.where(lane < 2 * SSD_HEADS, jnp.maximum(raw, 0.0) + jnp.log(1.0 + jnp.exp(-jnp.abs(raw))), 0.0)
    dt_ref[...] = dt
    ac = dt * -jnp.exp(alog_ref[...])
    hi = ac.astype(BF16)
    rest = ac - hi.astype(F32)
    mid = rest.astype(BF16)
    lo = (rest - mid.astype(F32)).astype(BF16)
    tri = tri_ref[...]
    for k in range(tm // CHUNK):
        rows = slice(k * CHUNK, (k + 1) * CHUNK)
        cs_ref[rows, :] = _dot(tri, hi[rows, :]) + _dot(tri, mid[rows, :]) + _dot(tri, lo[rows, :])


def _col_blocks(w, block):
    return w.reshape(w.shape[0], -1, block).transpose(1, 0, 2)


def _in_projection(x, mod, weights, consts):
    tm = TM_ROWS
    per_tile = tm // HALO
    last = N_TOK // HALO - 1
    halo_prev = pl.BlockSpec((HALO, D_MODEL), lambda i: (jnp.maximum(i * per_tile - 1, 0), 0))
    halo_next = pl.BlockSpec((HALO, D_MODEL), lambda i: (jnp.minimum((i + 1) * per_tile, last), 0))
    blocked = lambda n, w: pl.BlockSpec((n, tm, w), lambda i: (0, i, 0))
    return pl.pallas_call(
        _inproj_kernel,
        grid=(N_TOK // tm,),
        in_specs=[_row_spec(tm, D_MODEL), halo_prev, halo_next, _mod_spec(tm)]
                 + [_const_spec(a.shape) for a in weights + consts],
        out_specs=[blocked(N_XBC, XBC_BLOCK), blocked(N_GATE, GATE_BLOCK), _row_spec(tm, LANES),
                   _row_spec(tm, LANES)],
        out_shape=[jax.ShapeDtypeStruct((N_XBC, N_TOK, XBC_BLOCK), BF16),
                   jax.ShapeDtypeStruct((N_GATE, N_TOK, GATE_BLOCK), BF16),
                   jax.ShapeDtypeStruct((N_TOK, LANES), F32), jax.ShapeDtypeStruct((N_TOK, LANES), F32)],
        scratch_shapes=[pltpu.VMEM((tm + 2 * HALO, D_MODEL), BF16),
                        pltpu.VMEM((tm + 2 * HALO, XBC_BLOCK), F32),
                        pltpu.VMEM((tm + 2 * HALO, XBC_BLOCK), F32)],
        compiler_params=_cparams("parallel"),
        name="in_projection",
    )(x, x, x, mod, *weights, *consts)


def _ssd_kernel(has_init, n_chunks, *refs):
    refs = list(refs)
    cv_ref, dtv_ref, cs_ref = refs[0:3]
    del refs[0:3]
    init_ref = refs.pop(0) if has_init else None
    alog_ref, dskip_ref, ef_ref, eb_ref, y_ref, fin_ref, st_ref = refs
    a_neg = -jnp.exp(alog_ref[...])
    rows_i = lax.broadcasted_iota(jnp.int32, (CHUNK, CHUNK), 0)
    cols_i = lax.broadcasted_iota(jnp.int32, (CHUNK, CHUNK), 1)
    lower = rows_i >= cols_i
    upper = cols_i >= rows_i
    even_head = jnp.bitwise_and(lax.broadcasted_iota(jnp.int32, (CHUNK, D_SSD), 1), LANES - 1) < SSD_HEADDIM

    assert SSD_GROUPS * SSD_STATE == XBC_BLOCK
    b_block = D_SSD // XBC_BLOCK

    def row0(c):
        return pl.multiple_of(c * CHUNK, CHUNK)

    def chunk_pass(c, forward):
        r0 = row0(c)
        rows = pl.ds(r0, CHUNK)
        e_ref = ef_ref if forward else eb_ref
        off = 0 if forward else SSD_HEADS
        dt = dtv_ref[rows, :]
        cs = cs_ref[rows, :]
        total = cs_ref[pl.ds(r0 + CHUNK - 1, 1), :]
        if forward:
            pos = cs
            to_edge = jnp.exp(total - cs)
            from_edge = jnp.exp(cs)
            mask = lower
        else:
            pos = cs - dt * a_neg
            to_edge = jnp.exp(pos)
            from_edge = jnp.exp(total - pos)
            mask = upper
        pos_t = pos.T
        x = jnp.concatenate([cv_ref[b, rows, :] for b in range(D_SSD // XBC_BLOCK)], axis=1).astype(F32)
        stacked = jnp.concatenate([dt, dt * to_edge, from_edge], axis=0).astype(BF16)
        spread = _dot(stacked, e_ref[...])
        xc = x * spread[0:CHUNK]
        xc_sub = (jnp.where(even_head, xc, 0.0).astype(BF16), jnp.where(even_head, 0.0, xc).astype(BF16))
        xd_b = (x * spread[CHUNK:2 * CHUNK]).astype(BF16)
        from_e = spread[2 * CHUNK:3 * CHUNK]
        tot_hi, tot_lo = _split(jnp.exp(jnp.broadcast_to(total, (8, LANES))))
        tot_e = (_dot(tot_hi, e_ref[...]) + _dot(tot_lo, e_ref[...]))[0:1, :]
        st = st_ref[0 if forward else 1]
        st_b = st.astype(BF16)
        y_parts = []
        st_parts = []
        for g in range(SSD_GROUPS):
            bm = cv_ref[b_block, rows, g * SSD_STATE:(g + 1) * SSD_STATE]
            cm = cv_ref[b_block + 1, rows, g * SSD_STATE:(g + 1) * SSD_STATE]
            bm_t = bm.astype(F32).T.astype(BF16)
            cb = _dot(cm, bm_t)
            gcols = slice(g * 256, (g + 1) * 256)
            y_off = _dot(cm, st_b[:, gcols])
            st_parts.append(_dot(bm_t, xd_b[:, gcols]))
            diag = []
            for pair in range(2):
                pcols = slice(g * 256 + pair * LANES, g * 256 + (pair + 1) * LANES)
                decayed = []
                for sub in range(2):
                    h = g * 4 + pair * 2 + sub
                    col = pos[:, off + h:off + h + 1]
                    row = pos_t[off + h:off + h + 1, :]
                    diff = (col - row) if forward else (row - col)
                    decayed.append((jnp.where(mask, jnp.exp(diff), 0.0) * cb).astype(BF16))
                diag.append(_dot(jnp.concatenate(decayed, axis=1),
                                 jnp.concatenate([xc_sub[0][:, pcols], xc_sub[1][:, pcols]], axis=0)))
            y_parts.append(jnp.concatenate(diag, axis=1) + y_off * from_e[:, gcols])
        y_new = jnp.concatenate(y_parts, axis=1)
        st_ref[0 if forward else 1] = st * tot_e + jnp.concatenate(st_parts, axis=1)
        if forward:
            y_new = y_new + x * dskip_ref[...]
        return rows, y_new

    if has_init:
        st_ref[0] = init_ref[0, 0].T
        st_ref[1] = init_ref[0, 1].T
    else:
        st_ref[...] = jnp.zeros_like(st_ref)

    def first_half(i, carry):
        for rows, y_new in (chunk_pass(i, True), chunk_pass(n_chunks - 1 - i, False)):
            y_ref[rows, :] = y_new
        return carry

    def second_half(i, carry):
        for rows, y_new in (chunk_pass(i, True), chunk_pass(n_chunks - 1 - i, False)):
            y_ref[rows, :] = y_ref[rows, :] + y_new
        return carry

    lax.fori_loop(0, n_chunks // 2, first_half, 0)
    lax.fori_loop(n_chunks // 2, n_chunks, second_half, 0)
    fin_ref[0, 0] = st_ref[0].T
    fin_ref[0, 1] = st_ref[1].T


def _ssd_scan(xbc, dt, cs, init, consts, n_seq, seq, row_block0):
    n_chunks = seq // CHUNK
    assert n_chunks % 2 == 0
    has_init = init is not None
    seq_spec = lambda w: pl.BlockSpec((seq, w), lambda b: (row_block0 + b, 0))
    xbc_spec = pl.BlockSpec((N_XBC, seq, XBC_BLOCK), lambda b: (0, row_block0 + b, 0))
    in_specs = [xbc_spec, seq_spec(LANES), seq_spec(LANES)]
    args = [xbc, dt, cs]
    if has_init:
        in_specs.append(pl.BlockSpec((1, 2, D_SSD, SSD_STATE), lambda b: (b, 0, 0, 0)))
        args.append(init)
    for cst in consts:
        in_specs.append(pl.BlockSpec(cst.shape, lambda b, nd=cst.ndim: (0,) * nd))
        args.append(cst)
    return pl.pallas_call(
        functools.partial(_ssd_kernel, has_init, n_chunks),
        grid=(n_seq,),
        in_specs=in_specs,
        out_specs=[pl.BlockSpec((seq, D_SSD), lambda b: (b, 0)),
                   pl.BlockSpec((1, 2, D_SSD, SSD_STATE), lambda b: (b, 0, 0, 0))],
        out_shape=[jax.ShapeDtypeStruct((n_seq * seq, D_SSD), F32),
                   jax.ShapeDtypeStruct((n_seq, 2, D_SSD, SSD_STATE), F32)],
        scratch_shapes=[pltpu.VMEM((2, SSD_STATE, D_SSD), F32)],
        compiler_params=_cparams("parallel"),
        name="ssd_scan_%d" % seq,
    )(*args)


def _mixer_out_kernel(x_ref, mod_ref, yc_ref, yl_ref, z_ref, u_ref, v_ref, ssdn_ref, sgun_ref, wsp_ref, bsp_ref,
                      wo_ref, o_ref):
    tm = x_ref.shape[0]
    wide = lambda ref: jnp.concatenate([ref[b] for b in range(ref.shape[0])], axis=1).astype(F32)
    gated = _select_rows(yc_ref, yl_ref) * _silu(wide(z_ref))
    a = (_rms(gated) * ssdn_ref[...]).astype(BF16)
    vb = (_rms(wide(v_ref)) * sgun_ref[...]).astype(BF16)
    chunks = []
    for k in range(tm // CHUNK):
        rows = slice(k * CHUNK, (k + 1) * CHUNK)
        groups = [_dot(wsp_ref[g], vb[rows, g * SGU_GDIM:(g + 1) * SGU_GDIM]) for g in range(SGU_GROUPS)]
        chunks.append(jnp.concatenate(groups, axis=1) + bsp_ref[...])
    s = (wide(u_ref) * jnp.concatenate(chunks, axis=0)).astype(BF16)
    out = _dot(a, wo_ref[0:D_SSD, :]) + _dot(s, wo_ref[D_SSD:D_SSD + D_SGU, :])
    o_ref[...] = x_ref[...] + mod_ref[0, 2:3, :] * out


def _mixer_output(x, mod, y_ctx, y_lat, gates, ssd_norm, sgu_norm, w_sp, b_sp_e, w_out):
    tm = TM_FFN
    per = D_SSD // GATE_BLOCK
    part = lambda k: pl.BlockSpec((per, tm, GATE_BLOCK), lambda i: (k, i, 0))
    return pl.pallas_call(
        _mixer_out_kernel,
        grid=(N_TOK // tm,),
        in_specs=[_row_spec(tm, D_MODEL), _mod_spec(tm)] + _split_row_specs(tm, D_SSD) + [part(0), part(1), part(2),
                  _const_spec((1, D_SSD)), _const_spec((1, D_SGU)),
                  _const_spec((SGU_GROUPS, CHUNK, CHUNK)), _const_spec((CHUNK, D_SGU)),
                  _const_spec((D_SSD + D_SGU, D_MODEL))],
        out_specs=_row_spec(tm, D_MODEL),
        out_shape=jax.ShapeDtypeStruct((N_TOK, D_MODEL), F32),
        compiler_params=_cparams("parallel"),
        name="mixer_output",
    )(x, mod, y_ctx, y_lat, gates, gates, gates, ssd_norm, sgu_norm, w_sp, b_sp_e, w_out)


def _ffn_kernel(x_ref, mod_ref, wg_ref, wu_ref, wd_ref, o_ref):
    h = _modulated(x_ref, mod_ref, 3).astype(BF16)
    acc = jnp.zeros(o_ref.shape, F32)
    for f in range(D_FF // FF_CHUNK):
        cols = slice(f * FF_CHUNK, (f + 1) * FF_CHUNK)
        act = (_silu(_dot(h, wg_ref[:, cols])) * _dot(h, wu_ref[:, cols])).astype(BF16)
        acc = acc + _dot(act, wd_ref[cols, :])
    o_ref[...] = x_ref[...] + mod_ref[0, 5:6, :] * acc


def _dense_ffn(x, mod, wg, wu, wd):
    tm = TM_FFN
    once = pl.Buffered(1)
    return pl.pallas_call(
        _ffn_kernel,
        grid=(N_TOK // tm,),
        in_specs=[_row_spec(tm, D_MODEL), _mod_spec(tm),
                  pl.BlockSpec((D_MODEL, D_FF), lambda i: (0, 0), pipeline_mode=once),
                  pl.BlockSpec((D_MODEL, D_FF), lambda i: (0, 0), pipeline_mode=once),
                  pl.BlockSpec((D_FF, D_MODEL), lambda i: (0, 0), pipeline_mode=once)],
        out_specs=_row_spec(tm, D_MODEL),
        out_shape=jax.ShapeDtypeStruct((N_TOK, D_MODEL), F32),
        compiler_params=_cparams("parallel"),
        name="dense_ffn",
    )(x, mod, wg, wu, wd)


HEAD_PROJ = 3 * LANES


def _pair_sums(sq_a, sq_b, ones_ref):
    return _dot(jnp.concatenate([sq_a, sq_b], axis=1).astype(BF16), ones_ref[...])


def _expand_keys(ckv_b, pe_sq, pe_rot, wuk_ref, wuv_ref, kn_ref, ones_ref, k_ref, v_ref):
    kn_nope = kn_ref[:, 0:QK_NOPE]
    v_ref[...] = _dot(ckv_b, wuv_ref[...]).astype(BF16)
    k_nope = _dot(ckv_b, wuk_ref[...])
    for pair in range(MLA_HEADS // 2):
        kh = [k_nope[:, h * QK_NOPE:(h + 1) * QK_NOPE] for h in (2 * pair, 2 * pair + 1)]
        ss = _pair_sums(kh[0] * kh[0] + pe_sq, kh[1] * kh[1] + pe_sq, ones_ref)
        r = lax.rsqrt(ss * (1.0 / QK_DIM) + EPS)
        for idx in range(2):
            h = 2 * pair + idx
            rh = r[:, idx * LANES:(idx + 1) * LANES]
            k_ref[:, h * HEAD_PAD:h * HEAD_PAD + QK_NOPE] = (kh[idx] * rh * kn_nope).astype(BF16)
            k_ref[:, h * HEAD_PAD + QK_NOPE:(h + 1) * HEAD_PAD] = (pe_rot * rh).astype(BF16)


def _mla_proj_kernel(x_ref, mod_ref, cos_ref, sin_ref, wdq_ref, wdkv_ref, qan_ref, kvan_ref,
                     wuq_ref, wuk_ref, wuv_ref, qn_ref, kn_ref, ones_ref, q_ref, k_ref, v_ref, ckv_ref, kpe_ref):
    h = _modulated(x_ref, mod_ref, 0).astype(BF16)
    cos, sin = cos_ref[...], sin_ref[...]
    qa = (_rms(_dot(h, wdq_ref[...])) * qan_ref[...]).astype(BF16)
    q = _dot(qa, wuq_ref[...])
    qn_nope = qn_ref[:, 0:LANES]
    qn_rope = qn_ref[:, LANES:2 * LANES] * cos
    qn_part = qn_ref[:, 2 * LANES:3 * LANES] * sin
    scale = QK_DIM ** -0.5 * math.log2(math.e)
    for pair in range(MLA_HEADS // 2):
        blocks = []
        for hd in (2 * pair, 2 * pair + 1):
            base = hd * HEAD_PROJ
            blocks.append((q[:, base:base + LANES], q[:, base + LANES:base + 2 * LANES],
                           q[:, base + 2 * LANES:base + 3 * LANES]))
        ss = _pair_sums(*[qh * qh + qr * qr for qh, qr, _ in blocks], ones_ref)
        r = lax.rsqrt(ss * (1.0 / QK_DIM) + EPS) * scale
        for idx, (qh, qr, qp) in enumerate(blocks):
            hd = 2 * pair + idx
            rh = r[:, idx * LANES:(idx + 1) * LANES]
            q_ref[:, hd * HEAD_PAD:hd * HEAD_PAD + QK_NOPE] = (qh * rh * qn_nope).astype(BF16)
            q_ref[:, hd * HEAD_PAD + QK_NOPE:(hd + 1) * HEAD_PAD] = ((qr * qn_rope + qp * qn_part) * rh).astype(BF16)
    kva = _dot(h, wdkv_ref[...])
    ckv = _rms(kva[:, 0:KV_RANK]) * kvan_ref[...]
    kpe = kva[:, KV_RANK:KV_RANK + LANES]
    kpe_part = kva[:, KV_RANK + LANES:KV_RANK + 2 * LANES]
    ckv_ref[...] = ckv
    kpe_ref[...] = kpe
    pe_rot = kpe * (kn_ref[:, LANES:2 * LANES] * cos) + kpe_part * (kn_ref[:, 2 * LANES:3 * LANES] * sin)
    _expand_keys(ckv.astype(BF16), kpe * kpe, pe_rot, wuk_ref, wuv_ref, kn_ref, ones_ref, k_ref, v_ref)


def _mla_projection(x, mod, rope_tabs, wdq, wdkv, qan, kvan, wuq, wuk, wuv, qn, kn, ones):
    tm = TM_FFN
    hw = MLA_HEADS * HEAD_PAD
    out_w = (hw, hw, MLA_HEADS * V_HEAD, KV_RANK, LANES)
    out_dt = (BF16, BF16, BF16, F32, F32)
    consts = (wdq, wdkv, qan, kvan, wuq, wuk, wuv, qn, kn, ones)
    return pl.pallas_call(
        _mla_proj_kernel,
        grid=(N_TOK // tm,),
        in_specs=[_row_spec(tm, D_MODEL), _mod_spec(tm)] + [_row_spec(tm, LANES)] * 2
                 + [_const_spec(cst.shape) for cst in consts],
        out_specs=[_row_spec(tm, w) for w in out_w],
        out_shape=[jax.ShapeDtypeStruct((N_TOK, w), dt) for w, dt in zip(out_w, out_dt)],
        compiler_params=_cparams("parallel"),
        name="mla_projection",
    )(x, mod, *rope_tabs, *consts)


def _cache_kv_kernel(ckv_ref, kpe_ref, wuk_ref, wuv_ref, kn_ref, ones_ref, k_ref, v_ref):
    kpe = kpe_ref[...]
    _expand_keys(ckv_ref[...].astype(BF16), kpe * kpe, kpe * kn_ref[:, LANES:2 * LANES],
                 wuk_ref, wuv_ref, kn_ref, ones_ref, k_ref, v_ref)


def _cache_keys(ckv, kpe, wuk, wuv, kn, ones):
    rows = ckv.shape[0]
    tm = TM_ROWS
    consts = (wuk, wuv, kn, ones)
    return pl.pallas_call(
        _cache_kv_kernel,
        grid=(rows // tm,),
        in_specs=[_row_spec(tm, KV_RANK), _row_spec(tm, LANES)] + [_const_spec(cst.shape) for cst in consts],
        out_specs=[_row_spec(tm, MLA_HEADS * HEAD_PAD), _row_spec(tm, MLA_HEADS * V_HEAD)],
        out_shape=[jax.ShapeDtypeStruct((rows, MLA_HEADS * HEAD_PAD), BF16),
                   jax.ShapeDtypeStruct((rows, MLA_HEADS * V_HEAD), BF16)],
        compiler_params=_cparams("parallel"),
        name="cache_keys",
    )(ckv, kpe, *consts)


def _attn_kernel(has_cache, tq, n_par, heads, *refs):
    if has_cache:
        q_ref, k_ref, v_ref, kc_ref, vc_ref, o_ref = refs
    else:
        q_ref, k_ref, v_ref, o_ref = refs

    def tile(r0, hd):
        qk = slice(hd * HEAD_PAD, (hd + 1) * HEAD_PAD)
        vo = slice(hd * V_HEAD, (hd + 1) * V_HEAD)
        q = q_ref[pl.ds(r0, tq), qk]
        s = _dot_nt(q, k_ref[:, qk])
        m = jnp.max(s, axis=-1, keepdims=True)
        if has_cache:
            sc = _dot_nt(q, kc_ref[:, qk])
            m = jnp.maximum(m, jnp.max(sc, axis=-1, keepdims=True))
            pc = jnp.exp2(sc - m)
        p = jnp.exp2(s - m)
        den = jnp.sum(p, axis=-1, keepdims=True)
        num = _dot(p.astype(BF16), v_ref[:, vo])
        if has_cache:
            den = den + jnp.sum(pc, axis=-1, keepdims=True)
            num = num + _dot(pc.astype(BF16), vc_ref[:, vo])
        o_ref[pl.ds(r0, tq), vo] = (num / den).astype(BF16)

    group = n_par * tq
    n_groups = q_ref.shape[0] // group
    if n_groups == 1:
        for hd in range(heads):
            for t in range(n_par):
                tile(t * tq, hd)
    else:
        def body(i, carry):
            r0 = pl.multiple_of(i * group, group)
            for hd in range(heads):
                for t in range(n_par):
                    tile(r0 + t * tq, hd)
            return carry
        lax.fori_loop(0, n_groups, body, 0)


def _attention(q, k, v, kc, vc, n_seq, seq, row_block0):
    has_cache = kc is not None
    if seq >= 2 * TQ:
        tq, n_par, heads = TQ, 8, 1
    else:
        tq, n_par, heads = seq, 1, MLA_HEADS
    seq_spec = lambda w: pl.BlockSpec((seq, heads * w), lambda b, h: (row_block0 + b, h))
    in_specs = [seq_spec(HEAD_PAD), seq_spec(HEAD_PAD), seq_spec(V_HEAD)]
    args = [q, k, v]
    if has_cache:
        past = kc.shape[0] // n_seq
        in_specs += [pl.BlockSpec((past, heads * HEAD_PAD), lambda b, h: (b, h)),
                     pl.BlockSpec((past, heads * V_HEAD), lambda b, h: (b, h))]
        args += [kc, vc]
    return pl.pallas_call(
        functools.partial(_attn_kernel, has_cache, tq, n_par, heads),
        grid=(n_seq, MLA_HEADS // heads),
        in_specs=in_specs,
        out_specs=pl.BlockSpec((seq, heads * V_HEAD), lambda b, h: (b, h)),
        out_shape=jax.ShapeDtypeStruct((n_seq * seq, MLA_HEADS * V_HEAD), BF16),
        compiler_params=_cparams("parallel", "parallel"),
        name="attention_%d" % seq,
    )(*args)


def _split_row_specs(tm, width):
    n_ctx = N_PROMPT // tm
    return [pl.BlockSpec((tm, width), lambda i, *_: (jnp.minimum(i, n_ctx - 1), 0)),
            pl.BlockSpec((tm, width), lambda i, *_: (jnp.maximum(i - n_ctx, 0), 0))]


def _select_rows(ctx_ref, lat_ref):
    tm = ctx_ref.shape[0]
    return jnp.where(pl.program_id(0) < N_PROMPT // tm, ctx_ref[...], lat_ref[...])


def _attn_out_router_kernel(x_ref, mod_ref, ac_ref, al_ref, wo_ref, wr_ref, tri_ref,
                            xo_ref, h_ref, meta_ref, cnt_ref, run_ref):
    @pl.when(pl.program_id(0) == 0)
    def _():
        run_ref[...] = jnp.zeros_like(run_ref)

    sub = tri_ref.shape[0]
    attn = _select_rows(ac_ref, al_ref)
    lane = lax.broadcasted_iota(jnp.int32, (sub, LANES), 1)
    lane_f = lane.astype(F32)
    run = run_ref[...]
    for s in range(x_ref.shape[0] // sub):
        rows = slice(s * sub, (s + 1) * sub)
        x_new = x_ref[rows, :] + mod_ref[0, 2:3, :] * _dot(attn[rows, :], wo_ref[...])
        xo_ref[rows, :] = x_new
        h = _rms(x_new) * (1.0 + mod_ref[0, 4:5, :]) + mod_ref[0, 3:4, :]
        h_ref[rows, :] = h.astype(BF16)
        logits = jnp.where(lane < N_EXPERTS, _dot3(h, wr_ref[...]), -jnp.inf)
        m1 = jnp.max(logits, axis=-1, keepdims=True)
        e1 = jnp.min(jnp.where(logits == m1, lane_f, float(LANES)), axis=-1, keepdims=True)
        rest = jnp.where(lane_f == e1, -jnp.inf, logits)
        m2 = jnp.max(rest, axis=-1, keepdims=True)
        e2 = jnp.min(jnp.where(rest == m2, lane_f, float(LANES)), axis=-1, keepdims=True)
        t = jnp.exp(m2 - m1)
        g1 = 1.0 / (1.0 + t)
        pick1 = lane_f == e1
        pick2 = lane_f == e2
        onehot = jnp.where(pick1, 1.0, jnp.where(pick2, 1.0, 0.0))
        before = run + _dot(tri_ref[...], onehot.astype(BF16)) - onehot
        rank1 = jnp.sum(jnp.where(pick1, before, 0.0), axis=-1, keepdims=True)
        rank2 = jnp.sum(jnp.where(pick2, before, 0.0), axis=-1, keepdims=True)
        run = run + jnp.sum(onehot, axis=0, keepdims=True)
        cols = (e1, e2, g1, 1.0 - g1, rank1, rank2)
        meta = jnp.zeros((sub, LANES), F32)
        for idx, val in enumerate(cols):
            meta = jnp.where(lane == idx, val, meta)
        meta_ref[rows, :] = meta
    run_ref[...] = run
    cnt_ref[...] = jnp.broadcast_to(run, cnt_ref.shape)


META_EXPERT, META_GATE, META_RANK = 0, 2, 4


def _attn_out_router(x, mod, a_ctx, a_lat, w_o, wr):
    tm = TM_FFN
    sub = TM_ROWS
    k = a_ctx.shape[1]
    r = jnp.arange(sub)
    tri = (r[:, None] >= r[None, :]).astype(BF16)
    return pl.pallas_call(
        _attn_out_router_kernel,
        grid=(N_TOK // tm,),
        in_specs=[_row_spec(tm, D_MODEL), _mod_spec(tm)] + _split_row_specs(tm, k)
                 + [_const_spec((k, D_MODEL)), _const_spec((D_MODEL, LANES)), _const_spec((sub, sub))],
        out_specs=[_row_spec(tm, D_MODEL), _row_spec(tm, D_MODEL), _row_spec(tm, LANES),
                   _const_spec((SEG_PAD, LANES))],
        out_shape=[jax.ShapeDtypeStruct((N_TOK, D_MODEL), F32), jax.ShapeDtypeStruct((N_TOK, D_MODEL), BF16),
                   jax.ShapeDtypeStruct((N_TOK, LANES), F32), jax.ShapeDtypeStruct((SEG_PAD, LANES), F32)],
        scratch_shapes=[pltpu.VMEM((1, LANES), F32)],
        compiler_params=_cparams("arbitrary"),
        name="attn_out_router",
    )(x, mod, a_ctx, a_lat, w_o, wr, tri)


N_FF_CHUNKS = D_FF // FF_CHUNK
MOE_STEPS = 2
CHUNKS_PER_STEP = -(-N_FF_CHUNKS // MOE_STEPS)
TF_MOE = CHUNKS_PER_STEP * FF_CHUNK
assert MOE_STEPS * CHUNKS_PER_STEP - 1 == N_FF_CHUNKS


def _experts_kernel(be_ref, bv_ref, nu_ref, rows_ref, wg_ref, wu_ref, wd_ref, o_ref, acc_ref):
    i = pl.program_id(0)
    f = pl.program_id(1)

    def chunk(c, rows):
        cols = slice(c * FF_CHUNK, (c + 1) * FF_CHUNK)
        gate = _dot(rows, wg_ref[0, 0, :, cols].astype(BF16))
        up = _dot(rows, wu_ref[0, 0, :, cols].astype(BF16))
        return _dot((_silu(gate) * up).astype(BF16), wd_ref[0, 0, cols, :].astype(BF16))

    @pl.when(jnp.logical_and(i == 0, f == 0))
    def _():
        acc_ref[...] = jnp.zeros_like(acc_ref)

    @pl.when(bv_ref[i] > 0)
    def _():
        rows = rows_ref[...]
        acc = jnp.where(f == 0, 0.0, acc_ref[...])
        for c in range(CHUNKS_PER_STEP - 1):
            acc = acc + chunk(c, rows)
        acc_ref[...] = acc
        o_ref[...] = acc.astype(BF16)

        @pl.when((f * CHUNKS_PER_STEP + CHUNKS_PER_STEP) * FF_CHUNK <= D_FF)
        def _():
            acc_ref[...] += chunk(CHUNKS_PER_STEP - 1, rows_ref[...])

    @pl.when(bv_ref[i] == 0)
    def _():
        o_ref[...] = jnp.zeros_like(o_ref)


def _expert_ffn(rows, block_expert, block_valid, n_used, wg, wu, wd, layer):
    n_rows = rows.shape[0]
    tm = TM_MOE
    tf = TF_MOE
    n_f = MOE_STEPS

    def f_idx(i, f, nu):
        return jnp.where(i < nu[0], f, n_f - 1)

    grid_spec = pltpu.PrefetchScalarGridSpec(
        num_scalar_prefetch=3,
        grid=(n_rows // tm, n_f),
        in_specs=[pl.BlockSpec((tm, D_MODEL), lambda i, f, be, bv, nu: (i, 0)),
                  pl.BlockSpec((1, 1, D_MODEL, tf), lambda i, f, be, bv, nu: (layer, be[i], 0, f_idx(i, f, nu))),
                  pl.BlockSpec((1, 1, D_MODEL, tf), lambda i, f, be, bv, nu: (layer, be[i], 0, f_idx(i, f, nu))),
                  pl.BlockSpec((1, 1, tf, D_MODEL), lambda i, f, be, bv, nu: (layer, be[i], f_idx(i, f, nu), 0))],
        out_specs=pl.BlockSpec((tm, D_MODEL), lambda i, f, be, bv, nu: (i, 0)),
        scratch_shapes=[pltpu.VMEM((tm, D_MODEL), F32)])
    return pl.pallas_call(
        _experts_kernel,
        grid_spec=grid_spec,
        out_shape=jax.ShapeDtypeStruct((n_rows, D_MODEL), BF16),
        compiler_params=_cparams("arbitrary", "arbitrary"),
        name="expert_ffn",
    )(block_expert, block_valid, n_used, rows, wg, wu, wd)


def _combine_kernel(split, x_ref, mod_ref, meta_ref, a_ref, b_ref, *o_refs):
    g1 = meta_ref[:, META_GATE:META_GATE + 1]
    g2 = meta_ref[:, META_GATE + 1:META_GATE + 2]
    y = a_ref[...].astype(F32) * g1 + b_ref[...].astype(F32) * g2
    res = x_ref[...] + mod_ref[0, 5:6, :] * y
    if not split:
        o_refs[0][...] = res
        return
    is_ctx = pl.program_id(0) < N_PROMPT // x_ref.shape[0]

    @pl.when(is_ctx)
    def _():
        o_refs[0][...] = res

    @pl.when(jnp.logical_not(is_ctx))
    def _():
        o_refs[1][...] = res


def _moe_combine(x, mod, meta, a, b, split):
    tm = TM_FFN
    if split:
        out_specs = _split_row_specs(tm, D_MODEL)
        out_shape = [jax.ShapeDtypeStruct((N_PROMPT, D_MODEL), F32), jax.ShapeDtypeStruct((N_SAMPLE, D_MODEL), F32)]
    else:
        out_specs = _row_spec(tm, D_MODEL)
        out_shape = jax.ShapeDtypeStruct((N_TOK, D_MODEL), F32)
    return pl.pallas_call(
        functools.partial(_combine_kernel, split),
        grid=(N_TOK // tm,),
        in_specs=[_row_spec(tm, D_MODEL), _mod_spec(tm), _row_spec(tm, LANES), _row_spec(tm, D_MODEL),
                  _row_spec(tm, D_MODEL)],
        out_specs=out_specs,
        out_shape=out_shape,
        compiler_params=_cparams("arbitrary"),
        name="moe_combine",
    )(x, mod, meta, a, b)


def _moe(x, mod, h, meta, counts, wg, wu, wd, layer, split_out):
    tm = TM_MOE
    n_assign = N_TOK * TOP_K
    experts = meta[:, META_EXPERT:META_EXPERT + TOP_K].astype(jnp.int32)
    rank = meta[:, META_RANK:META_RANK + TOP_K].astype(jnp.int32)
    count = counts[0, :N_EXPERTS].astype(jnp.int32)
    padded = (count + tm - 1) // tm * tm
    pad_end = jnp.cumsum(padded)
    pad_start = pad_end - padded
    onehot = experts[:, :, None] == jnp.arange(N_EXPERTS, dtype=jnp.int32)
    dest = jnp.sum(jnp.where(onehot, pad_start, 0), axis=-1) + rank
    n_rows = n_assign + N_EXPERTS * tm
    n_blocks = n_rows // tm
    row_token = jnp.zeros((n_rows,), jnp.int32).at[dest.reshape(-1)].set(
        jnp.arange(n_assign, dtype=jnp.int32) // TOP_K)
    n_used = (pad_end[-1] // tm).astype(jnp.int32).reshape(1)
    block_id = jnp.arange(n_blocks, dtype=jnp.int32)
    block_start = jnp.minimum(block_id, n_used[0] - 1) * tm
    block_expert = jnp.minimum(jnp.sum((pad_end[None, :] <= block_start[:, None]).astype(jnp.int32), axis=1),
                               N_EXPERTS - 1)
    last_row = jnp.sum(jnp.where(block_expert[:, None] == jnp.arange(N_EXPERTS, dtype=jnp.int32),
                                 pad_start + count, 0), axis=1)
    block_valid = jnp.where(block_id < n_used[0], jnp.clip(last_row - block_start, 0, tm), 0)
    out = _expert_ffn(h[row_token], block_expert, block_valid, n_used, wg, wu, wd, layer)
    return _moe_combine(x, mod, meta, out[dest[:, 0]], out[dest[:, 1]], split_out)


def _rope_tables():
    half = ROPE_AXIS // 2
    pos = jnp.arange(DEC_SEQ)
    row = (pos // GRID_W).astype(F32)
    col = (pos % GRID_W).astype(F32)
    inv = ROPE_THETA ** (-jnp.arange(0, ROPE_AXIS, 2, dtype=F32) / ROPE_AXIS)
    ang_r = row[:, None] * inv
    ang_c = col[:, None] * inv
    pad = jnp.zeros((DEC_SEQ, LANES - QK_ROPE), F32)
    cos = jnp.concatenate([jnp.cos(ang_r), jnp.cos(ang_r), jnp.cos(ang_c), jnp.cos(ang_c), pad + 1.0], axis=1)
    sin = jnp.concatenate([-jnp.sin(ang_r), jnp.sin(ang_r), -jnp.sin(ang_c), jnp.sin(ang_c), pad], axis=1)
    prompt = jnp.zeros((N_PROMPT, LANES), F32)
    tile = lambda t: jnp.tile(t, (DEC_BATCH, 1))
    return (jnp.concatenate([prompt + 1.0, tile(cos)], axis=0),
            jnp.concatenate([prompt, tile(sin)], axis=0))


def _rope_blocks(a):
    half = ROPE_AXIS // 2
    partner = jnp.concatenate([a[..., half:2 * half], a[..., 0:half], a[..., 3 * half:4 * half],
                               a[..., 2 * half:3 * half]], axis=-1)
    return jnp.concatenate([_pad_lanes(a, LANES), _pad_lanes(partner, LANES)], axis=-1)


def _pad_lanes(a, width):
    return jnp.pad(a, [(0, 0)] * (a.ndim - 1) + [(0, width - a.shape[-1])])


def _ssd_constants():
    r = jnp.arange(CHUNK)
    tri = (r[:, None] >= r[None, :]).astype(BF16)
    head_of = jnp.arange(D_SSD) // SSD_HEADDIM
    e_f = (r[:, None] == head_of[None, :]).astype(BF16)
    e_b = (r[:, None] == head_of[None, :] + SSD_HEADS).astype(BF16)
    return tri, e_f, e_b


def kernel(x_prompt, x_sample, c, state_ssm, cache_ckv, cache_kpe, c_ctx, w_mod, b_mod, w_in, conv_w, conv_b, dt_bias, a_log, d_skip, ssd_norm, sgu_norm, w_sp, b_sp, w_out, ffn_w_gate, ffn_w_up, ffn_w_down, w_dq, q_a_norm, w_uq, w_dkv, kv_a_norm, w_ukv, q_norm, k_norm, w_o, router, moe_w_gate, moe_w_up, moe_w_down):
    x = jnp.concatenate([x_prompt.reshape(N_PROMPT, D_MODEL), x_sample.reshape(N_SAMPLE, D_MODEL)], axis=0)
    cond = jnp.concatenate([c_ctx[None, :], c, jnp.zeros((SEG_PAD - N_SEG, D_MODEL), F32)], axis=0)
    mods = _modulation_tables(cond, w_mod, b_mod).reshape(DEPTH, SEG_PAD, N_MOD, D_MODEL)
    rope_tabs = _rope_tables()
    tri, e_f, e_b = _ssd_constants()
    blk = jnp.arange(2 * LANES) // LANES
    pair_ones = (blk[:, None] == blk[None, :]).astype(BF16)
    i1 = D_SSD
    i2 = i1 + CONV_CH
    i3 = i2 + 2 * SSD_HEADS
    i4 = i3 + D_SGU
    new_ssm, new_ckv, new_kpe = [], [], []
    for i in range(DEPTH):
        j = i // 2
        mod = mods[i]
        if i % 2 == 0:
            w = w_in[j]
            wx = _col_blocks(w[:, i1:i2].astype(BF16), XBC_BLOCK)
            wgate = _col_blocks(jnp.concatenate([w[:, 0:i1], w[:, i3:]], axis=1).astype(BF16), GATE_BLOCK)
            wdt = _pad_lanes(w[:, i2:i3], LANES).astype(BF16)
            a_log_row = _pad_lanes(a_log[j].reshape(1, -1), LANES)
            xbc, gates, dt, cs = _in_projection(
                x, mod, (wx, wgate, wdt),
                (_col_blocks(conv_w[j], XBC_BLOCK), _col_blocks(conv_b[j][None, :], XBC_BLOCK),
                 _pad_lanes(dt_bias[j].reshape(1, -1), LANES), a_log_row, tri))
            consts = (a_log_row, jnp.repeat(d_skip[j], SSD_HEADDIM)[None, :], e_f, e_b)
            y_ctx, st_p = _ssd_scan(xbc, dt, cs, None, consts, BATCH, SEQ, 0)
            init = state_ssm[:, j].reshape(DEC_BATCH, 2, D_SSD, SSD_STATE)
            y_lat, _ = _ssd_scan(xbc, dt, cs, init, consts, DEC_BATCH, DEC_SEQ, N_PROMPT // DEC_SEQ)
            new_ssm.append(st_p.reshape(BATCH, 2, SSD_HEADS, SSD_HEADDIM, SSD_STATE))
            b_sp_e = jnp.repeat(b_sp[j].T, SGU_GDIM, axis=1)
            x = _mixer_output(x, mod, y_ctx, y_lat, gates, ssd_norm[j][None, :], sgu_norm[j][None, :],
                              w_sp[j].astype(BF16), b_sp_e, w_out[j].astype(BF16))
            x = _dense_ffn(x, mod, ffn_w_gate[j].astype(BF16), ffn_w_up[j].astype(BF16),
                           ffn_w_down[j].astype(BF16))
        else:
            split = lambda a: jnp.concatenate([a[..., :QK_NOPE], _rope_blocks(a[..., QK_NOPE:])], axis=-1)
            wuq = split(w_uq[j].reshape(Q_RANK, MLA_HEADS, QK_DIM)).reshape(Q_RANK, MLA_HEADS * HEAD_PROJ)
            wdkv = jnp.concatenate([w_dkv[j][:, :KV_RANK], _rope_blocks(w_dkv[j][:, KV_RANK:])], axis=-1)
            wukv = w_ukv[j].reshape(KV_RANK, MLA_HEADS, QK_NOPE + V_HEAD)
            wuk = wukv[:, :, :QK_NOPE].reshape(KV_RANK, -1).astype(BF16)
            wuv = wukv[:, :, QK_NOPE:].reshape(KV_RANK, -1).astype(BF16)
            qn = split(q_norm[j][None, :])
            kn = split(k_norm[j][None, :])
            q, k, v, ckv, kpe = _mla_projection(
                x, mod, rope_tabs, w_dq[j].astype(BF16), wdkv.astype(BF16), q_a_norm[j][None, :],
                kv_a_norm[j][None, :], wuq.astype(BF16), wuk, wuv, qn, kn, pair_ones)
            new_ckv.append(ckv[:N_PROMPT].reshape(BATCH, SEQ, KV_RANK))
            new_kpe.append(kpe[:N_PROMPT, :QK_ROPE].reshape(BATCH, SEQ, QK_ROPE))
            kc, vc = _cache_keys(cache_ckv[:, j].reshape(DEC_BATCH * PAST_LEN, KV_RANK),
                                 _pad_lanes(cache_kpe[:, j].reshape(DEC_BATCH * PAST_LEN, QK_ROPE), LANES),
                                 wuk, wuv, kn, pair_ones)
            o_ctx = _attention(q, k, v, None, None, BATCH, SEQ, 0)
            o_lat = _attention(q, k, v, kc, vc, DEC_BATCH, DEC_SEQ, N_PROMPT // DEC_SEQ)
            x, h, meta, counts = _attn_out_router(x, mod, o_ctx, o_lat, w_o[j].astype(BF16),
                                                  _pad_lanes(router[j], LANES))
            x = _moe(x, mod, h, meta, counts, moe_w_gate, moe_w_up, moe_w_down, j, split_out=(i == DEPTH - 1))
    assert DEPTH % 2 == 0
    x_ctx, x_lat = x
    return (x_ctx.reshape(BATCH, SEQ, D_MODEL),
            x_lat.reshape(DEC_BATCH, DEC_SEQ, D_MODEL),
            jnp.stack(new_ssm, axis=1),
            jnp.stack(new_ckv, axis=1),
            jnp.stack(new_kpe, axis=1))
```

```python
import functools
import math

import jax
import jax.numpy as jnp
from jax import lax
from jax.experimental import pallas as pl
from jax.experimental.pallas import tpu as pltpu

F32 = jnp.float32
BF16 = jnp.bfloat16

D_MODEL = 1024
BATCH = 16
SEQ = 256
DEPTH = 4
DEC_BATCH = 4
DEC_SEQ = 2048
PAST_LEN = 512
GRID_W = 64
N_MOD = 6
EPS = 1e-6

SSD_HEADDIM = 64
SSD_HEADS = 16
D_SSD = 1024
SSD_GROUPS = 4
SSD_STATE = 128
CHUNK = 128
CONV_W = 5
CONV_CH = D_SSD + 2 * SSD_GROUPS * SSD_STATE
D_SGU = 1024
SGU_GROUPS = 4
SGU_GDIM = D_SGU // SGU_GROUPS

MLA_HEADS = 8
Q_RANK = 384
KV_RANK = 256
QK_NOPE = 128
QK_ROPE = 64
V_HEAD = 128
QK_DIM = QK_NOPE + QK_ROPE
ROPE_AXIS = QK_ROPE // 2
ROPE_THETA = 10000.0
HEAD_PAD = 256

D_FF = 2816
N_EXPERTS = 8
TOP_K = 2

N_PROMPT = BATCH * SEQ
N_SAMPLE = DEC_BATCH * DEC_SEQ
N_TOK = N_PROMPT + N_SAMPLE
N_SEG = 1 + DEC_BATCH
SEG_PAD = 8

LANES = 128
VMEM_LIMIT = 58 * 1024 * 1024

TM_ROWS = 256
TM_FFN = 512
FF_CHUNK = 256
TM_MOE = 1024
TQ = 256


def _cparams(*sem):
    return pltpu.CompilerParams(dimension_semantics=sem, vmem_limit_bytes=VMEM_LIMIT)


def _dot(a, b):
    return jnp.dot(a, b, preferred_element_type=F32)


def _dot_nt(a, b):
    return lax.dot_general(a, b, (((1,), (1,)), ((), ())), preferred_element_type=F32)


def _split(x):
    hi = x.astype(BF16)
    lo = (x - hi.astype(F32)).astype(BF16)
    return hi, lo


def _dot3(a, b):
    ah, al = _split(a)
    bh, bl = _split(b)
    return _dot(ah, bh) + _dot(ah, bl) + _dot(al, bh)


def _silu(x):
    return x / (1.0 + jnp.exp(-x))


def _rms(x):
    return x * lax.rsqrt(jnp.mean(x * x, axis=-1, keepdims=True) + EPS)


def _modulated(x_ref, mod_ref, first):
    shift = mod_ref[0, first:first + 1, :]
    scale = mod_ref[0, first + 1:first + 2, :]
    return _rms(x_ref[...]) * (1.0 + scale) + shift


def _seg_map(tm):
    def index_map(i, *_):
        r = i * tm
        return (jnp.where(r < N_PROMPT, 0, 1 + (r - N_PROMPT) // DEC_SEQ), 0, 0)
    return index_map


def _row_spec(tm, width):
    return pl.BlockSpec((tm, width), lambda i, *_: (i, 0))


def _const_spec(shape):
    zeros = (0,) * len(shape)
    return pl.BlockSpec(shape, lambda i, *_: zeros)


def _mod_spec(tm):
    return pl.BlockSpec((1, N_MOD, D_MODEL), _seg_map(tm))


def _mod_kernel(c_ref, w_ref, b_ref, o_ref):
    o_ref[0] = _dot3(_silu(c_ref[...]), w_ref[0]) + b_ref[0]


def _modulation_tables(cond, w_mod, b_mod):
    tn = 1536
    return pl.pallas_call(
        _mod_kernel,
        grid=(DEPTH, N_MOD * D_MODEL // tn),
        in_specs=[pl.BlockSpec((SEG_PAD, D_MODEL), lambda l, j: (0, 0)),
                  pl.BlockSpec((1, D_MODEL, tn), lambda l, j: (l, 0, j)),
                  pl.BlockSpec((1, 1, tn), lambda l, j: (l, 0, j))],
        out_specs=pl.BlockSpec((1, SEG_PAD, tn), lambda l, j: (l, 0, j)),
        out_shape=jax.ShapeDtypeStruct((DEPTH, SEG_PAD, N_MOD * D_MODEL), F32),
        compiler_params=_cparams("parallel", "parallel"),
        name="modulation",
    )(cond, w_mod, b_mod.reshape(DEPTH, 1, N_MOD * D_MODEL))


HALO = 8


XBC_BLOCK = 512
N_XBC = CONV_CH // XBC_BLOCK
GATE_BLOCK = 256
N_GATE = (D_SSD + 2 * D_SGU) // GATE_BLOCK
GATE_PER_STAGE = N_GATE // N_XBC


def _inproj_kernel(split_x, *refs):
    n_x = 2 if split_x else 1
    x_tile = _select_rows(*refs[:2]) if split_x else refs[0][...]
    (xp_ref, xn_ref, mod_ref, wx_ref, wg_ref, wdt, convw_ref, convb_ref, dtb_ref, alog_ref,
     tri_ref, xbc_ref, gate_ref, dt_ref, cs_ref, h_ref, xa_ref, xb_ref) = refs[n_x:]
    tm = x_tile.shape[0]
    r0 = pl.program_id(0) * tm
    shift = mod_ref[0, 0:1, :]
    scale = mod_ref[0, 1:2, :]
    x_ext = jnp.concatenate([x_tile, xp_ref[...], xn_ref[...]], axis=0)
    h_ref[...] = (_rms(x_ext) * (1.0 + scale) + shift).astype(BF16)

    rel = r0 - N_PROMPT
    in_latent = r0 >= N_PROMPT
    keep_prev = jnp.where(jnp.logical_and(in_latent, lax.rem(rel, DEC_SEQ) != 0), 1.0, 0.0)
    keep_next = jnp.where(jnp.logical_and(in_latent, lax.rem(rel + tm, DEC_SEQ) != 0), 1.0, 0.0)
    reach = CONV_W // 2

    bufs = (xa_ref, xb_ref)
    bufs[0][...] = _dot(h_ref[...], wx_ref[0])

    def stage(s, cur_ref, nxt_ref):
        if s + 1 < N_XBC:
            nxt_ref[...] = _dot(h_ref[...], wx_ref[s + 1])
        for t in range(GATE_PER_STAGE):
            g = s * GATE_PER_STAGE + t
            gate_ref[g] = _dot(h_ref[0:tm, :], wg_ref[g]).astype(BF16)
        for half in range(XBC_BLOCK // 256):
            cols = slice(half * 256, (half + 1) * 256)
            win = jnp.concatenate([cur_ref[tm:tm + HALO, cols] * keep_prev, cur_ref[0:tm, cols],
                                   cur_ref[tm + HALO:tm + 2 * HALO, cols] * keep_next], axis=0)
            acc = jnp.zeros((tm, 256), F32) + convb_ref[s, :, cols]
            for k in range(CONV_W):
                start = HALO - reach + k
                acc = acc + win[start:start + tm, :] * convw_ref[s, k:k + 1, cols]
            xbc_ref[s, :, cols] = _silu(acc).astype(BF16)

    for s in range(N_XBC):
        stage(s, bufs[s % 2], bufs[1 - s % 2])

    h = h_ref[0:tm, :]
    lane = lax.broadcasted_iota(jnp.int32, (1, LANES), 1)
    raw = _dot(h, wdt[...]) + dtb_ref[...]
    dt = jnp.where(lane < 2 * SSD_HEADS, jnp.maximum(raw, 0.0) + jnp.log(1.0 + jnp.exp(-jnp.abs(raw))), 0.0)
    dt_ref[...] = dt
    ac = dt * -jnp.exp(alog_ref[...])
    hi = ac.astype(BF16)
    rest = ac - hi.astype(F32)
    mid = rest.astype(BF16)
    lo = (rest - mid.astype(F32)).astype(BF16)
    tri = tri_ref[...]
    for k in range(tm // CHUNK):
        rows = slice(k * CHUNK, (k + 1) * CHUNK)
        cs_ref[rows, :] = _dot(tri, hi[rows, :]) + _dot(tri, mid[rows, :]) + _dot(tri, lo[rows, :])


def _col_blocks(w, block):
    return w.reshape(w.shape[0], -1, block).transpose(1, 0, 2)


def _in_projection(x, mod, weights, consts):
    tm = TM_ROWS
    per_tile = tm // HALO
    split_x = isinstance(x, tuple)
    if split_x:
        n_ctx = N_PROMPT // tm
        last = N_SAMPLE // HALO - 1
        x_specs = _split_row_specs(tm, D_MODEL) + [
            pl.BlockSpec((HALO, D_MODEL), lambda i: (jnp.clip((i - n_ctx) * per_tile - 1, 0, last), 0)),
            pl.BlockSpec((HALO, D_MODEL), lambda i: (jnp.clip((i - n_ctx + 1) * per_tile, 0, last), 0))]
        x_args = (x[0], x[1], x[1], x[1])
    else:
        last = N_TOK // HALO - 1
        x_specs = [_row_spec(tm, D_MODEL),
                   pl.BlockSpec((HALO, D_MODEL), lambda i: (jnp.maximum(i * per_tile - 1, 0), 0)),
                   pl.BlockSpec((HALO, D_MODEL), lambda i: (jnp.minimum((i + 1) * per_tile, last), 0))]
        x_args = (x, x, x)
    blocked = lambda n, w: pl.BlockSpec((n, tm, w), lambda i: (0, i, 0))
    return pl.pallas_call(
        functools.partial(_inproj_kernel, split_x),
        grid=(N_TOK // tm,),
        in_specs=x_specs + [_mod_spec(tm)] + [_const_spec(a.shape) for a in weights + consts],
        out_specs=[blocked(N_XBC, XBC_BLOCK), blocked(N_GATE, GATE_BLOCK), _row_spec(tm, LANES),
                   _row_spec(tm, LANES)],
        out_shape=[jax.ShapeDtypeStruct((N_XBC, N_TOK, XBC_BLOCK), BF16),
                   jax.ShapeDtypeStruct((N_GATE, N_TOK, GATE_BLOCK), BF16),
                   jax.ShapeDtypeStruct((N_TOK, LANES), F32), jax.ShapeDtypeStruct((N_TOK, LANES), F32)],
        scratch_shapes=[pltpu.VMEM((tm + 2 * HALO, D_MODEL), BF16),
                        pltpu.VMEM((tm + 2 * HALO, XBC_BLOCK), F32),
                        pltpu.VMEM((tm + 2 * HALO, XBC_BLOCK), F32)],
        compiler_params=_cparams("parallel"),
        name="in_projection",
    )(*x_args, mod, *weights, *consts)


def _ssd_kernel(has_init, n_chunks, *refs):
    refs = list(refs)
    cv_ref, dtv_ref, cs_ref = refs[0:3]
    del refs[0:3]
    init_ref = refs.pop(0) if has_init else None
    alog_ref, dskip_ref, ef_ref, eb_ref, y_ref, fin_ref, st_ref = refs
    a_neg = -jnp.exp(alog_ref[...])
    rows_i = lax.broadcasted_iota(jnp.int32, (CHUNK, CHUNK), 0)
    cols_i = lax.broadcasted_iota(jnp.int32, (CHUNK, CHUNK), 1)
    lower = rows_i >= cols_i
    upper = cols_i >= rows_i
    even_head = jnp.bitwise_and(lax.broadcasted_iota(jnp.int32, (CHUNK, D_SSD), 1), LANES - 1) < SSD_HEADDIM

    assert SSD_GROUPS * SSD_STATE == XBC_BLOCK
    b_block = D_SSD // XBC_BLOCK

    def row0(c):
        return pl.multiple_of(c * CHUNK, CHUNK)

    def chunk_pass(c, forward):
        r0 = row0(c)
        rows = pl.ds(r0, CHUNK)
        e_ref = ef_ref if forward else eb_ref
        off = 0 if forward else SSD_HEADS
        dt = dtv_ref[rows, :]
        cs = cs_ref[rows, :]
        total = cs_ref[pl.ds(r0 + CHUNK - 1, 1), :]
        if forward:
            pos = cs
            to_edge = jnp.exp(total - cs)
            from_edge = jnp.exp(cs)
            mask = lower
        else:
            pos = cs - dt * a_neg
            to_edge = jnp.exp(pos)
            from_edge = jnp.exp(total - pos)
            mask = upper
        pos_t = pos.T
        x = jnp.concatenate([cv_ref[b, rows, :] for b in range(D_SSD // XBC_BLOCK)], axis=1).astype(F32)
        stacked = jnp.concatenate([dt, dt * to_edge, from_edge], axis=0).astype(BF16)
        spread = _dot(stacked, e_ref[...])
        xc = x * spread[0:CHUNK]
        xc_sub = (jnp.where(even_head, xc, 0.0).astype(BF16), jnp.where(even_head, 0.0, xc).astype(BF16))
        xd_b = (x * spread[CHUNK:2 * CHUNK]).astype(BF16)
        from_e = spread[2 * CHUNK:3 * CHUNK]
        tot_hi, tot_lo = _split(jnp.exp(jnp.broadcast_to(total, (8, LANES))))
        tot_e = (_dot(tot_hi, e_ref[...]) + _dot(tot_lo, e_ref[...]))[0:1, :]
        st = st_ref[0 if forward else 1]
        st_b = st.astype(BF16)
        y_parts = []
        st_parts = []
        for g in range(SSD_GROUPS):
            bm = cv_ref[b_block, rows, g * SSD_STATE:(g + 1) * SSD_STATE]
            cm = cv_ref[b_block + 1, rows, g * SSD_STATE:(g + 1) * SSD_STATE]
            bm_t = bm.astype(F32).T.astype(BF16)
            cb = _dot(cm, bm_t)
            gcols = slice(g * 256, (g + 1) * 256)
            y_off = _dot(cm, st_b[:, gcols])
            st_parts.append(_dot(bm_t, xd_b[:, gcols]))
            diag = []
            for pair in range(2):
                pcols = slice(g * 256 + pair * LANES, g * 256 + (pair + 1) * LANES)
                decayed = []
                for sub in range(2):
                    h = g * 4 + pair * 2 + sub
                    col = pos[:, off + h:off + h + 1]
                    row = pos_t[off + h:off + h + 1, :]
                    diff = (col - row) if forward else (row - col)
                    decayed.append((jnp.where(mask, jnp.exp(diff), 0.0) * cb).astype(BF16))
                diag.append(_dot(jnp.concatenate(decayed, axis=1),
                                 jnp.concatenate([xc_sub[0][:, pcols], xc_sub[1][:, pcols]], axis=0)))
            y_parts.append(jnp.concatenate(diag, axis=1) + y_off * from_e[:, gcols])
        y_new = jnp.concatenate(y_parts, axis=1)
        st_ref[0 if forward else 1] = st * tot_e + jnp.concatenate(st_parts, axis=1)
        if forward:
            y_new = y_new + x * dskip_ref[...]
        return rows, y_new

    if has_init:
        st_ref[0] = init_ref[0, 0].T
        st_ref[1] = init_ref[0, 1].T
    else:
        st_ref[...] = jnp.zeros_like(st_ref)

    def first_half(i, carry):
        for rows, y_new in (chunk_pass(i, True), chunk_pass(n_chunks - 1 - i, False)):
            y_ref[rows, :] = y_new
        return carry

    def second_half(i, carry):
        for rows, y_new in (chunk_pass(i, True), chunk_pass(n_chunks - 1 - i, False)):
            y_ref[rows, :] = y_ref[rows, :] + y_new
        return carry

    lax.fori_loop(0, n_chunks // 2, first_half, 0)
    lax.fori_loop(n_chunks // 2, n_chunks, second_half, 0)
    fin_ref[0, 0] = st_ref[0].T
    fin_ref[0, 1] = st_ref[1].T


def _ssd_scan(xbc, dt, cs, init, consts, n_seq, seq, row_block0):
    n_chunks = seq // CHUNK
    assert n_chunks % 2 == 0
    has_init = init is not None
    seq_spec = lambda w: pl.BlockSpec((seq, w), lambda b: (row_block0 + b, 0))
    xbc_spec = pl.BlockSpec((N_XBC, seq, XBC_BLOCK), lambda b: (0, row_block0 + b, 0))
    in_specs = [xbc_spec, seq_spec(LANES), seq_spec(LANES)]
    args = [xbc, dt, cs]
    if has_init:
        in_specs.append(pl.BlockSpec((1, 2, D_SSD, SSD_STATE), lambda b: (b, 0, 0, 0)))
        args.append(init)
    for cst in consts:
        in_specs.append(pl.BlockSpec(cst.shape, lambda b, nd=cst.ndim: (0,) * nd))
        args.append(cst)
    return pl.pallas_call(
        functools.partial(_ssd_kernel, has_init, n_chunks),
        grid=(n_seq,),
        in_specs=in_specs,
        out_specs=[pl.BlockSpec((seq, D_SSD), lambda b: (b, 0)),
                   pl.BlockSpec((1, 2, D_SSD, SSD_STATE), lambda b: (b, 0, 0, 0))],
        out_shape=[jax.ShapeDtypeStruct((n_seq * seq, D_SSD), F32),
                   jax.ShapeDtypeStruct((n_seq, 2, D_SSD, SSD_STATE), F32)],
        scratch_shapes=[pltpu.VMEM((2, SSD_STATE, D_SSD), F32)],
        compiler_params=_cparams("parallel"),
        name="ssd_scan_%d" % seq,
    )(*args)


def _mixer_out_kernel(split_x, *refs):
    n_x = 2 if split_x else 1
    x_tile = _select_rows(*refs[:2]) if split_x else refs[0][...]
    mod_ref, yc_ref, yl_ref, z_ref, u_ref, v_ref, ssdn_ref, sgun_ref, wsp_ref, bsp_ref, wo_ref, o_ref = refs[n_x:]
    tm = x_tile.shape[0]
    wide = lambda ref: jnp.concatenate([ref[b] for b in range(ref.shape[0])], axis=1).astype(F32)
    gated = _select_rows(yc_ref, yl_ref) * _silu(wide(z_ref))
    a = (_rms(gated) * ssdn_ref[...]).astype(BF16)
    vb = (_rms(wide(v_ref)) * sgun_ref[...]).astype(BF16)
    chunks = []
    for k in range(tm // CHUNK):
        rows = slice(k * CHUNK, (k + 1) * CHUNK)
        groups = [_dot(wsp_ref[g], vb[rows, g * SGU_GDIM:(g + 1) * SGU_GDIM]) for g in range(SGU_GROUPS)]
        chunks.append(jnp.concatenate(groups, axis=1) + bsp_ref[...])
    s = (wide(u_ref) * jnp.concatenate(chunks, axis=0)).astype(BF16)
    out = _dot(a, wo_ref[0:D_SSD, :]) + _dot(s, wo_ref[D_SSD:D_SSD + D_SGU, :])
    o_ref[...] = x_tile + mod_ref[0, 2:3, :] * out


def _mixer_output(x, mod, y_ctx, y_lat, gates, ssd_norm, sgu_norm, w_sp, b_sp_e, w_out):
    tm = TM_FFN
    per = D_SSD // GATE_BLOCK
    part = lambda k: pl.BlockSpec((per, tm, GATE_BLOCK), lambda i: (k, i, 0))
    split_x = isinstance(x, tuple)
    x_specs = _split_row_specs(tm, D_MODEL) if split_x else [_row_spec(tm, D_MODEL)]
    x_args = x if split_x else (x,)
    return pl.pallas_call(
        functools.partial(_mixer_out_kernel, split_x),
        grid=(N_TOK // tm,),
        in_specs=x_specs + [_mod_spec(tm)] + _split_row_specs(tm, D_SSD) + [part(0), part(1), part(2),
                  _const_spec((1, D_SSD)), _const_spec((1, D_SGU)),
                  _const_spec((SGU_GROUPS, CHUNK, CHUNK)), _const_spec((CHUNK, D_SGU)),
                  _const_spec((D_SSD + D_SGU, D_MODEL))],
        out_specs=_row_spec(tm, D_MODEL),
        out_shape=jax.ShapeDtypeStruct((N_TOK, D_MODEL), F32),
        compiler_params=_cparams("parallel"),
        name="mixer_output",
    )(*x_args, mod, y_ctx, y_lat, gates, gates, gates, ssd_norm, sgu_norm, w_sp, b_sp_e, w_out)


def _ffn_kernel(x_ref, mod_ref, wg_ref, wu_ref, wd_ref, o_ref):
    h = _modulated(x_ref, mod_ref, 3).astype(BF16)
    acc = jnp.zeros(o_ref.shape, F32)
    for f in range(D_FF // FF_CHUNK):
        cols = slice(f * FF_CHUNK, (f + 1) * FF_CHUNK)
        act = (_silu(_dot(h, wg_ref[:, cols])) * _dot(h, wu_ref[:, cols])).astype(BF16)
        acc = acc + _dot(act, wd_ref[cols, :])
    o_ref[...] = x_ref[...] + mod_ref[0, 5:6, :] * acc


def _dense_ffn(x, mod, wg, wu, wd):
    tm = TM_FFN
    once = pl.Buffered(1)
    return pl.pallas_call(
        _ffn_kernel,
        grid=(N_TOK // tm,),
        in_specs=[_row_spec(tm, D_MODEL), _mod_spec(tm),
                  pl.BlockSpec((D_MODEL, D_FF), lambda i: (0, 0), pipeline_mode=once),
                  pl.BlockSpec((D_MODEL, D_FF), lambda i: (0, 0), pipeline_mode=once),
                  pl.BlockSpec((D_FF, D_MODEL), lambda i: (0, 0), pipeline_mode=once)],
        out_specs=_row_spec(tm, D_MODEL),
        out_shape=jax.ShapeDtypeStruct((N_TOK, D_MODEL), F32),
        compiler_params=_cparams("parallel"),
        name="dense_ffn",
    )(x, mod, wg, wu, wd)


HEAD_PROJ = 3 * LANES


def _pair_sums(sq_a, sq_b, ones_ref):
    return _dot(jnp.concatenate([sq_a, sq_b], axis=1).astype(BF16), ones_ref[...])


def _expand_keys(ckv_b, pe_sq, pe_rot, wuk_ref, wuv_ref, kn_ref, ones_ref, k_ref, v_ref):
    kn_nope = kn_ref[:, 0:QK_NOPE]
    v_ref[...] = _dot(ckv_b, wuv_ref[...]).astype(BF16)
    k_nope = _dot(ckv_b, wuk_ref[...])
    for pair in range(MLA_HEADS // 2):
        kh = [k_nope[:, h * QK_NOPE:(h + 1) * QK_NOPE] for h in (2 * pair, 2 * pair + 1)]
        ss = _pair_sums(kh[0] * kh[0] + pe_sq, kh[1] * kh[1] + pe_sq, ones_ref)
        r = lax.rsqrt(ss * (1.0 / QK_DIM) + EPS)
        for idx in range(2):
            h = 2 * pair + idx
            rh = r[:, idx * LANES:(idx + 1) * LANES]
            k_ref[:, h * HEAD_PAD:h * HEAD_PAD + QK_NOPE] = (kh[idx] * rh * kn_nope).astype(BF16)
            k_ref[:, h * HEAD_PAD + QK_NOPE:(h + 1) * HEAD_PAD] = (pe_rot * rh).astype(BF16)


def _mla_proj_kernel(x_ref, mod_ref, cos_ref, sin_ref, wdq_ref, wdkv_ref, qan_ref, kvan_ref,
                     wuq_ref, wuk_ref, wuv_ref, qn_ref, kn_ref, ones_ref, q_ref, k_ref, v_ref, ckv_ref, kpe_ref):
    h = _modulated(x_ref, mod_ref, 0).astype(BF16)
    cos, sin = cos_ref[...], sin_ref[...]
    qa = (_rms(_dot(h, wdq_ref[...])) * qan_ref[...]).astype(BF16)
    q = _dot(qa, wuq_ref[...])
    qn_nope = qn_ref[:, 0:LANES]
    qn_rope = qn_ref[:, LANES:2 * LANES] * cos
    qn_part = qn_ref[:, 2 * LANES:3 * LANES] * sin
    scale = QK_DIM ** -0.5 * math.log2(math.e)
    for pair in range(MLA_HEADS // 2):
        blocks = []
        for hd in (2 * pair, 2 * pair + 1):
            base = hd * HEAD_PROJ
            blocks.append((q[:, base:base + LANES], q[:, base + LANES:base + 2 * LANES],
                           q[:, base + 2 * LANES:base + 3 * LANES]))
        ss = _pair_sums(*[qh * qh + qr * qr for qh, qr, _ in blocks], ones_ref)
        r = lax.rsqrt(ss * (1.0 / QK_DIM) + EPS) * scale
        for idx, (qh, qr, qp) in enumerate(blocks):
            hd = 2 * pair + idx
            rh = r[:, idx * LANES:(idx + 1) * LANES]
            q_ref[:, hd * HEAD_PAD:hd * HEAD_PAD + QK_NOPE] = (qh * rh * qn_nope).astype(BF16)
            q_ref[:, hd * HEAD_PAD + QK_NOPE:(hd + 1) * HEAD_PAD] = ((qr * qn_rope + qp * qn_part) * rh).astype(BF16)
    kva = _dot(h, wdkv_ref[...])
    ckv = _rms(kva[:, 0:KV_RANK]) * kvan_ref[...]
    kpe = kva[:, KV_RANK:KV_RANK + LANES]
    kpe_part = kva[:, KV_RANK + LANES:KV_RANK + 2 * LANES]
    ckv_ref[...] = ckv
    kpe_ref[...] = kpe
    pe_rot = kpe * (kn_ref[:, LANES:2 * LANES] * cos) + kpe_part * (kn_ref[:, 2 * LANES:3 * LANES] * sin)
    _expand_keys(ckv.astype(BF16), kpe * kpe, pe_rot, wuk_ref, wuv_ref, kn_ref, ones_ref, k_ref, v_ref)


def _mla_projection(x, mod, rope_tabs, wdq, wdkv, qan, kvan, wuq, wuk, wuv, qn, kn, ones):
    tm = TM_FFN
    hw = MLA_HEADS * HEAD_PAD
    out_w = (hw, hw, MLA_HEADS * V_HEAD, KV_RANK, LANES)
    out_dt = (BF16, BF16, BF16, F32, F32)
    consts = (wdq, wdkv, qan, kvan, wuq, wuk, wuv, qn, kn, ones)
    return pl.pallas_call(
        _mla_proj_kernel,
        grid=(N_TOK // tm,),
        in_specs=[_row_spec(tm, D_MODEL), _mod_spec(tm)] + [_row_spec(tm, LANES)] * 2
                 + [_const_spec(cst.shape) for cst in consts],
        out_specs=[_row_spec(tm, w) for w in out_w],
        out_shape=[jax.ShapeDtypeStruct((N_TOK, w), dt) for w, dt in zip(out_w, out_dt)],
        compiler_params=_cparams("parallel"),
        name="mla_projection",
    )(x, mod, *rope_tabs, *consts)


def _cache_kv_kernel(ckv_ref, kpe_ref, wuk_ref, wuv_ref, kn_ref, ones_ref, k_ref, v_ref):
    kpe = kpe_ref[...]
    _expand_keys(ckv_ref[...].astype(BF16), kpe * kpe, kpe * kn_ref[:, LANES:2 * LANES],
                 wuk_ref, wuv_ref, kn_ref, ones_ref, k_ref, v_ref)


def _cache_keys(ckv, kpe, wuk, wuv, kn, ones):
    rows = ckv.shape[0]
    tm = TM_ROWS
    consts = (wuk, wuv, kn, ones)
    return pl.pallas_call(
        _cache_kv_kernel,
        grid=(rows // tm,),
        in_specs=[_row_spec(tm, KV_RANK), _row_spec(tm, LANES)] + [_const_spec(cst.shape) for cst in consts],
        out_specs=[_row_spec(tm, MLA_HEADS * HEAD_PAD), _row_spec(tm, MLA_HEADS * V_HEAD)],
        out_shape=[jax.ShapeDtypeStruct((rows, MLA_HEADS * HEAD_PAD), BF16),
                   jax.ShapeDtypeStruct((rows, MLA_HEADS * V_HEAD), BF16)],
        compiler_params=_cparams("parallel"),
        name="cache_keys",
    )(ckv, kpe, *consts)


def _attn_kernel(has_cache, tq, n_par, heads, *refs):
    if has_cache:
        q_ref, k_ref, v_ref, kc_ref, vc_ref, o_ref = refs
    else:
        q_ref, k_ref, v_ref, o_ref = refs

    def tile(r0, hd):
        qk = slice(hd * HEAD_PAD, (hd + 1) * HEAD_PAD)
        vo = slice(hd * V_HEAD, (hd + 1) * V_HEAD)
        q = q_ref[pl.ds(r0, tq), qk]
        s = _dot_nt(q, k_ref[:, qk])
        m = jnp.max(s, axis=-1, keepdims=True)
        if has_cache:
            sc = _dot_nt(q, kc_ref[:, qk])
            m = jnp.maximum(m, jnp.max(sc, axis=-1, keepdims=True))
            pc = jnp.exp2(sc - m)
        p = jnp.exp2(s - m)
        den = jnp.sum(p, axis=-1, keepdims=True)
        num = _dot(p.astype(BF16), v_ref[:, vo])
        if has_cache:
            den = den + jnp.sum(pc, axis=-1, keepdims=True)
            num = num + _dot(pc.astype(BF16), vc_ref[:, vo])
        o_ref[pl.ds(r0, tq), vo] = (num / den).astype(BF16)

    group = n_par * tq
    n_groups = q_ref.shape[0] // group
    if n_groups == 1:
        for hd in range(heads):
            for t in range(n_par):
                tile(t * tq, hd)
    else:
        def body(i, carry):
            r0 = pl.multiple_of(i * group, group)
            for hd in range(heads):
                for t in range(n_par):
                    tile(r0 + t * tq, hd)
            return carry
        lax.fori_loop(0, n_groups, body, 0)


def _attention(q, k, v, kc, vc, n_seq, seq, row_block0):
    has_cache = kc is not None
    if seq >= 2 * TQ:
        tq, n_par, heads = TQ, 8, 1
    else:
        tq, n_par, heads = seq, 1, MLA_HEADS
    seq_spec = lambda w: pl.BlockSpec((seq, heads * w), lambda b, h: (row_block0 + b, h))
    in_specs = [seq_spec(HEAD_PAD), seq_spec(HEAD_PAD), seq_spec(V_HEAD)]
    args = [q, k, v]
    if has_cache:
        past = kc.shape[0] // n_seq
        in_specs += [pl.BlockSpec((past, heads * HEAD_PAD), lambda b, h: (b, h)),
                     pl.BlockSpec((past, heads * V_HEAD), lambda b, h: (b, h))]
        args += [kc, vc]
    return pl.pallas_call(
        functools.partial(_attn_kernel, has_cache, tq, n_par, heads),
        grid=(n_seq, MLA_HEADS // heads),
        in_specs=in_specs,
        out_specs=pl.BlockSpec((seq, heads * V_HEAD), lambda b, h: (b, h)),
        out_shape=jax.ShapeDtypeStruct((n_seq * seq, MLA_HEADS * V_HEAD), BF16),
        compiler_params=_cparams("parallel", "parallel"),
        name="attention_%d" % seq,
    )(*args)


def _split_row_specs(tm, width):
    n_ctx = N_PROMPT // tm
    return [pl.BlockSpec((tm, width), lambda i, *_: (jnp.minimum(i, n_ctx - 1), 0)),
            pl.BlockSpec((tm, width), lambda i, *_: (jnp.maximum(i - n_ctx, 0), 0))]


def _select_rows(ctx_ref, lat_ref):
    tm = ctx_ref.shape[0]
    return jnp.where(pl.program_id(0) < N_PROMPT // tm, ctx_ref[...], lat_ref[...])


def _attn_out_router_kernel(x_ref, mod_ref, ac_ref, al_ref, wo_ref, wr_ref, tri_ref,
                            xo_ref, h_ref, meta_ref, cnt_ref, run_ref):
    @pl.when(pl.program_id(0) == 0)
    def _():
        run_ref[...] = jnp.zeros_like(run_ref)

    sub = tri_ref.shape[0]
    attn = _select_rows(ac_ref, al_ref)
    lane = lax.broadcasted_iota(jnp.int32, (sub, LANES), 1)
    lane_f = lane.astype(F32)
    run = run_ref[...]
    for s in range(x_ref.shape[0] // sub):
        rows = slice(s * sub, (s + 1) * sub)
        x_new = x_ref[rows, :] + mod_ref[0, 2:3, :] * _dot(attn[rows, :], wo_ref[...])
        xo_ref[rows, :] = x_new
        h = _rms(x_new) * (1.0 + mod_ref[0, 4:5, :]) + mod_ref[0, 3:4, :]
        h_ref[rows, :] = h.astype(BF16)
        logits = jnp.where(lane < N_EXPERTS, _dot3(h, wr_ref[...]), -jnp.inf)
        m1 = jnp.max(logits, axis=-1, keepdims=True)
        e1 = jnp.min(jnp.where(logits == m1, lane_f, float(LANES)), axis=-1, keepdims=True)
        rest = jnp.where(lane_f == e1, -jnp.inf, logits)
        m2 = jnp.max(rest, axis=-1, keepdims=True)
        e2 = jnp.min(jnp.where(rest == m2, lane_f, float(LANES)), axis=-1, keepdims=True)
        t = jnp.exp(m2 - m1)
        g1 = 1.0 / (1.0 + t)
        pick1 = lane_f == e1
        pick2 = lane_f == e2
        onehot = jnp.where(pick1, 1.0, jnp.where(pick2, 1.0, 0.0))
        before = run + _dot(tri_ref[...], onehot.astype(BF16)) - onehot
        rank1 = jnp.sum(jnp.where(pick1, before, 0.0), axis=-1, keepdims=True)
        rank2 = jnp.sum(jnp.where(pick2, before, 0.0), axis=-1, keepdims=True)
        run = run + jnp.sum(onehot, axis=0, keepdims=True)
        cols = (e1, e2, g1, 1.0 - g1, rank1, rank2)
        meta = jnp.zeros((sub, LANES), F32)
        for idx, val in enumerate(cols):
            meta = jnp.where(lane == idx, val, meta)
        meta_ref[rows, :] = meta
    run_ref[...] = run
    cnt_ref[...] = jnp.broadcast_to(run, cnt_ref.shape)


META_EXPERT, META_GATE, META_RANK = 0, 2, 4


def _attn_out_router(x, mod, a_ctx, a_lat, w_o, wr):
    tm = TM_FFN
    sub = TM_ROWS
    k = a_ctx.shape[1]
    r = jnp.arange(sub)
    tri = (r[:, None] >= r[None, :]).astype(BF16)
    return pl.pallas_call(
        _attn_out_router_kernel,
        grid=(N_TOK // tm,),
        in_specs=[_row_spec(tm, D_MODEL), _mod_spec(tm)] + _split_row_specs(tm, k)
                 + [_const_spec((k, D_MODEL)), _const_spec((D_MODEL, LANES)), _const_spec((sub, sub))],
        out_specs=[_row_spec(tm, D_MODEL), _row_spec(tm, D_MODEL), _row_spec(tm, LANES),
                   _const_spec((SEG_PAD, LANES))],
        out_shape=[jax.ShapeDtypeStruct((N_TOK, D_MODEL), F32), jax.ShapeDtypeStruct((N_TOK, D_MODEL), BF16),
                   jax.ShapeDtypeStruct((N_TOK, LANES), F32), jax.ShapeDtypeStruct((SEG_PAD, LANES), F32)],
        scratch_shapes=[pltpu.VMEM((1, LANES), F32)],
        compiler_params=_cparams("arbitrary"),
        name="attn_out_router",
    )(x, mod, a_ctx, a_lat, w_o, wr, tri)


N_FF_CHUNKS = D_FF // FF_CHUNK
MOE_STEPS = 2
CHUNKS_PER_STEP = -(-N_FF_CHUNKS // MOE_STEPS)
TF_MOE = CHUNKS_PER_STEP * FF_CHUNK
assert MOE_STEPS * CHUNKS_PER_STEP - 1 == N_FF_CHUNKS


def _experts_kernel(be_ref, bv_ref, nu_ref, rows_ref, wg_ref, wu_ref, wd_ref, o_ref, acc_ref):
    i = pl.program_id(0)
    f = pl.program_id(1)

    def chunk(c, rows):
        cols = slice(c * FF_CHUNK, (c + 1) * FF_CHUNK)
        gate = _dot(rows, wg_ref[0, 0, :, cols].astype(BF16))
        up = _dot(rows, wu_ref[0, 0, :, cols].astype(BF16))
        return _dot((_silu(gate) * up).astype(BF16), wd_ref[0, 0, cols, :].astype(BF16))

    @pl.when(jnp.logical_and(i == 0, f == 0))
    def _():
        acc_ref[...] = jnp.zeros_like(acc_ref)

    @pl.when(bv_ref[i] > 0)
    def _():
        rows = rows_ref[...]
        acc = jnp.where(f == 0, 0.0, acc_ref[...])
        for c in range(CHUNKS_PER_STEP - 1):
            acc = acc + chunk(c, rows)
        acc_ref[...] = acc
        o_ref[...] = acc.astype(BF16)

        @pl.when((f * CHUNKS_PER_STEP + CHUNKS_PER_STEP) * FF_CHUNK <= D_FF)
        def _():
            acc_ref[...] += chunk(CHUNKS_PER_STEP - 1, rows_ref[...])

    @pl.when(bv_ref[i] == 0)
    def _():
        o_ref[...] = jnp.zeros_like(o_ref)


def _expert_ffn(rows, block_expert, block_valid, n_used, wg, wu, wd, layer):
    n_rows = rows.shape[0]
    tm = TM_MOE
    tf = TF_MOE
    n_f = MOE_STEPS

    def f_idx(i, f, nu):
        return jnp.where(i < nu[0], f, n_f - 1)

    grid_spec = pltpu.PrefetchScalarGridSpec(
        num_scalar_prefetch=3,
        grid=(n_rows // tm, n_f),
        in_specs=[pl.BlockSpec((tm, D_MODEL), lambda i, f, be, bv, nu: (i, 0)),
                  pl.BlockSpec((1, 1, D_MODEL, tf), lambda i, f, be, bv, nu: (layer, be[i], 0, f_idx(i, f, nu))),
                  pl.BlockSpec((1, 1, D_MODEL, tf), lambda i, f, be, bv, nu: (layer, be[i], 0, f_idx(i, f, nu))),
                  pl.BlockSpec((1, 1, tf, D_MODEL), lambda i, f, be, bv, nu: (layer, be[i], f_idx(i, f, nu), 0))],
        out_specs=pl.BlockSpec((tm, D_MODEL), lambda i, f, be, bv, nu: (i, 0)),
        scratch_shapes=[pltpu.VMEM((tm, D_MODEL), F32)])
    return pl.pallas_call(
        _experts_kernel,
        grid_spec=grid_spec,
        out_shape=jax.ShapeDtypeStruct((n_rows, D_MODEL), BF16),
        compiler_params=_cparams("arbitrary", "arbitrary"),
        name="expert_ffn",
    )(block_expert, block_valid, n_used, rows, wg, wu, wd)


def _combine_kernel(split, x_ref, mod_ref, meta_ref, a_ref, b_ref, *o_refs):
    g1 = meta_ref[:, META_GATE:META_GATE + 1]
    g2 = meta_ref[:, META_GATE + 1:META_GATE + 2]
    y = a_ref[...].astype(F32) * g1 + b_ref[...].astype(F32) * g2
    res = x_ref[...] + mod_ref[0, 5:6, :] * y
    if not split:
        o_refs[0][...] = res
        return
    is_ctx = pl.program_id(0) < N_PROMPT // x_ref.shape[0]

    @pl.when(is_ctx)
    def _():
        o_refs[0][...] = res

    @pl.when(jnp.logical_not(is_ctx))
    def _():
        o_refs[1][...] = res


def _moe_combine(x, mod, meta, a, b, split):
    tm = TM_FFN
    if split:
        out_specs = _split_row_specs(tm, D_MODEL)
        out_shape = [jax.ShapeDtypeStruct((N_PROMPT, D_MODEL), F32), jax.ShapeDtypeStruct((N_SAMPLE, D_MODEL), F32)]
    else:
        out_specs = _row_spec(tm, D_MODEL)
        out_shape = jax.ShapeDtypeStruct((N_TOK, D_MODEL), F32)
    return pl.pallas_call(
        functools.partial(_combine_kernel, split),
        grid=(N_TOK // tm,),
        in_specs=[_row_spec(tm, D_MODEL), _mod_spec(tm), _row_spec(tm, LANES), _row_spec(tm, D_MODEL),
                  _row_spec(tm, D_MODEL)],
        out_specs=out_specs,
        out_shape=out_shape,
        compiler_params=_cparams("arbitrary"),
        name="moe_combine",
    )(x, mod, meta, a, b)


def _moe(x, mod, h, meta, counts, wg, wu, wd, layer, split_out):
    tm = TM_MOE
    n_assign = N_TOK * TOP_K
    experts = meta[:, META_EXPERT:META_EXPERT + TOP_K].astype(jnp.int32)
    rank = meta[:, META_RANK:META_RANK + TOP_K].astype(jnp.int32)
    count = counts[0, :N_EXPERTS].astype(jnp.int32)
    padded = (count + tm - 1) // tm * tm
    pad_end = jnp.cumsum(padded)
    pad_start = pad_end - padded
    onehot = experts[:, :, None] == jnp.arange(N_EXPERTS, dtype=jnp.int32)
    dest = jnp.sum(jnp.where(onehot, pad_start, 0), axis=-1) + rank
    n_rows = n_assign + N_EXPERTS * tm
    n_blocks = n_rows // tm
    row_token = jnp.zeros((n_rows,), jnp.int32).at[dest.reshape(-1)].set(
        jnp.arange(n_assign, dtype=jnp.int32) // TOP_K)
    n_used = (pad_end[-1] // tm).astype(jnp.int32).reshape(1)
    block_id = jnp.arange(n_blocks, dtype=jnp.int32)
    block_start = jnp.minimum(block_id, n_used[0] - 1) * tm
    block_expert = jnp.minimum(jnp.sum((pad_end[None, :] <= block_start[:, None]).astype(jnp.int32), axis=1),
                               N_EXPERTS - 1)
    last_row = jnp.sum(jnp.where(block_expert[:, None] == jnp.arange(N_EXPERTS, dtype=jnp.int32),
                                 pad_start + count, 0), axis=1)
    block_valid = jnp.where(block_id < n_used[0], jnp.clip(last_row - block_start, 0, tm), 0)
    out = _expert_ffn(h[row_token], block_expert, block_valid, n_used, wg, wu, wd, layer)
    return _moe_combine(x, mod, meta, out[dest[:, 0]], out[dest[:, 1]], split_out)


def _rope_tables():
    half = ROPE_AXIS // 2
    pos = jnp.arange(DEC_SEQ)
    row = (pos // GRID_W).astype(F32)
    col = (pos % GRID_W).astype(F32)
    inv = ROPE_THETA ** (-jnp.arange(0, ROPE_AXIS, 2, dtype=F32) / ROPE_AXIS)
    ang_r = row[:, None] * inv
    ang_c = col[:, None] * inv
    pad = jnp.zeros((DEC_SEQ, LANES - QK_ROPE), F32)
    cos = jnp.concatenate([jnp.cos(ang_r), jnp.cos(ang_r), jnp.cos(ang_c), jnp.cos(ang_c), pad + 1.0], axis=1)
    sin = jnp.concatenate([-jnp.sin(ang_r), jnp.sin(ang_r), -jnp.sin(ang_c), jnp.sin(ang_c), pad], axis=1)
    prompt = jnp.zeros((N_PROMPT, LANES), F32)
    tile = lambda t: jnp.tile(t, (DEC_BATCH, 1))
    return (jnp.concatenate([prompt + 1.0, tile(cos)], axis=0),
            jnp.concatenate([prompt, tile(sin)], axis=0))


def _rope_blocks(a):
    half = ROPE_AXIS // 2
    partner = jnp.concatenate([a[..., half:2 * half], a[..., 0:half], a[..., 3 * half:4 * half],
                               a[..., 2 * half:3 * half]], axis=-1)
    return jnp.concatenate([_pad_lanes(a, LANES), _pad_lanes(partner, LANES)], axis=-1)


def _pad_lanes(a, width):
    return jnp.pad(a, [(0, 0)] * (a.ndim - 1) + [(0, width - a.shape[-1])])


def _ssd_constants():
    r = jnp.arange(CHUNK)
    tri = (r[:, None] >= r[None, :]).astype(BF16)
    head_of = jnp.arange(D_SSD) // SSD_HEADDIM
    e_f = (r[:, None] == head_of[None, :]).astype(BF16)
    e_b = (r[:, None] == head_of[None, :] + SSD_HEADS).astype(BF16)
    return tri, e_f, e_b


def kernel(x_prompt, x_sample, c, state_ssm, cache_ckv, cache_kpe, c_ctx, w_mod, b_mod, w_in, conv_w, conv_b, dt_bias, a_log, d_skip, ssd_norm, sgu_norm, w_sp, b_sp, w_out, ffn_w_gate, ffn_w_up, ffn_w_down, w_dq, q_a_norm, w_uq, w_dkv, kv_a_norm, w_ukv, q_norm, k_norm, w_o, router, moe_w_gate, moe_w_up, moe_w_down):
    x = (x_prompt.reshape(N_PROMPT, D_MODEL), x_sample.reshape(N_SAMPLE, D_MODEL))
    cond = jnp.concatenate([c_ctx[None, :], c, jnp.zeros((SEG_PAD - N_SEG, D_MODEL), F32)], axis=0)
    mods = _modulation_tables(cond, w_mod, b_mod).reshape(DEPTH, SEG_PAD, N_MOD, D_MODEL)
    rope_tabs = _rope_tables()
    tri, e_f, e_b = _ssd_constants()
    blk = jnp.arange(2 * LANES) // LANES
    pair_ones = (blk[:, None] == blk[None, :]).astype(BF16)
    i1 = D_SSD
    i2 = i1 + CONV_CH
    i3 = i2 + 2 * SSD_HEADS
    i4 = i3 + D_SGU
    new_ssm, new_ckv, new_kpe = [], [], []
    for i in range(DEPTH):
        j = i // 2
        mod = mods[i]
        if i % 2 == 0:
            w = w_in[j]
            wx = _col_blocks(w[:, i1:i2].astype(BF16), XBC_BLOCK)
            wgate = _col_blocks(jnp.concatenate([w[:, 0:i1], w[:, i3:]], axis=1).astype(BF16), GATE_BLOCK)
            wdt = _pad_lanes(w[:, i2:i3], LANES).astype(BF16)
            a_log_row = _pad_lanes(a_log[j].reshape(1, -1), LANES)
            xbc, gates, dt, cs = _in_projection(
                x, mod, (wx, wgate, wdt),
                (_col_blocks(conv_w[j], XBC_BLOCK), _col_blocks(conv_b[j][None, :], XBC_BLOCK),
                 _pad_lanes(dt_bias[j].reshape(1, -1), LANES), a_log_row, tri))
            consts = (a_log_row, jnp.repeat(d_skip[j], SSD_HEADDIM)[None, :], e_f, e_b)
            y_ctx, st_p = _ssd_scan(xbc, dt, cs, None, consts, BATCH, SEQ, 0)
            init = state_ssm[:, j].reshape(DEC_BATCH, 2, D_SSD, SSD_STATE)
            y_lat, _ = _ssd_scan(xbc, dt, cs, init, consts, DEC_BATCH, DEC_SEQ, N_PROMPT // DEC_SEQ)
            new_ssm.append(st_p.reshape(BATCH, 2, SSD_HEADS, SSD_HEADDIM, SSD_STATE))
            b_sp_e = jnp.repeat(b_sp[j].T, SGU_GDIM, axis=1)
            x = _mixer_output(x, mod, y_ctx, y_lat, gates, ssd_norm[j][None, :], sgu_norm[j][None, :],
                              w_sp[j].astype(BF16), b_sp_e, w_out[j].astype(BF16))
            x = _dense_ffn(x, mod, ffn_w_gate[j].astype(BF16), ffn_w_up[j].astype(BF16),
                           ffn_w_down[j].astype(BF16))
        else:
            split = lambda a: jnp.concatenate([a[..., :QK_NOPE], _rope_blocks(a[..., QK_NOPE:])], axis=-1)
            wuq = split(w_uq[j].reshape(Q_RANK, MLA_HEADS, QK_DIM)).reshape(Q_RANK, MLA_HEADS * HEAD_PROJ)
            wdkv = jnp.concatenate([w_dkv[j][:, :KV_RANK], _rope_blocks(w_dkv[j][:, KV_RANK:])], axis=-1)
            wukv = w_ukv[j].reshape(KV_RANK, MLA_HEADS, QK_NOPE + V_HEAD)
            wuk = wukv[:, :, :QK_NOPE].reshape(KV_RANK, -1).astype(BF16)
            wuv = wukv[:, :, QK_NOPE:].reshape(KV_RANK, -1).astype(BF16)
            qn = split(q_norm[j][None, :])
            kn = split(k_norm[j][None, :])
            q, k, v, ckv, kpe = _mla_projection(
                x, mod, rope_tabs, w_dq[j].astype(BF16), wdkv.astype(BF16), q_a_norm[j][None, :],
                kv_a_norm[j][None, :], wuq.astype(BF16), wuk, wuv, qn, kn, pair_ones)
            new_ckv.append(ckv[:N_PROMPT].reshape(BATCH, SEQ, KV_RANK))
            new_kpe.append(kpe[:N_PROMPT, :QK_ROPE].reshape(BATCH, SEQ, QK_ROPE))
            kc, vc = _cache_keys(cache_ckv[:, j].reshape(DEC_BATCH * PAST_LEN, KV_RANK),
                                 _pad_lanes(cache_kpe[:, j].reshape(DEC_BATCH * PAST_LEN, QK_ROPE), LANES),
                                 wuk, wuv, kn, pair_ones)
            o_ctx = _attention(q, k, v, None, None, BATCH, SEQ, 0)
            o_lat = _attention(q, k, v, kc, vc, DEC_BATCH, DEC_SEQ, N_PROMPT // DEC_SEQ)
            x, h, meta, counts = _attn_out_router(x, mod, o_ctx, o_lat, w_o[j].astype(BF16),
                                                  _pad_lanes(router[j], LANES))
            x = _moe(x, mod, h, meta, counts, moe_w_gate, moe_w_up, moe_w_down, j, split_out=(i == DEPTH - 1))
    assert DEPTH % 2 == 0
    x_ctx, x_lat = x
    return (x_ctx.reshape(BATCH, SEQ, D_MODEL),
            x_lat.reshape(DEC_BATCH, DEC_SEQ, D_MODEL),
            jnp.stack(new_ssm, axis=1),
            jnp.stack(new_ckv, axis=1),
            jnp.stack(new_kpe, axis=1))
```
